```python
import math
import jax, jax.numpy as jnp
from jax import lax
import numpy as np


D_MODEL = 2048
BATCH = 8
SEQ = 8192
DEPTH = 4

MEM_LEN = 256
GLA_HEADS = 4
GLA_DK = 128
GLA_DV = 256
GLA_GATE_RANK = 16
GLA_GATE_NORMALIZER = 16.0
GLA_CHUNK = 64
DIL_HEADS = 4
DIL_HEAD_DIM = 128
DIL_CONFIGS = ((128, 1), (512, 4), (2048, 16))
MEM_HEADS = 4
MEM_HEAD_DIM = 128
REL_BUCKETS = 32
REL_MAX_DISTANCE = 1024
D_FF = 4 * D_MODEL
EPS = 1e-6
NEG_INF = -1e30

GLA_QK_WIDTH = GLA_HEADS * GLA_DK
GLA_V_WIDTH = GLA_HEADS * GLA_DV
DIL_WIDTH = DIL_HEADS * DIL_HEAD_DIM
MEM_WIDTH = MEM_HEADS * MEM_HEAD_DIM
MIX_WIDTH = GLA_V_WIDTH + DIL_WIDTH + MEM_WIDTH
IN_SPLITS = (GLA_QK_WIDTH, GLA_QK_WIDTH, GLA_V_WIDTH, GLA_V_WIDTH, GLA_GATE_RANK, GLA_GATE_RANK,
             DIL_WIDTH, DIL_WIDTH, DIL_WIDTH, MEM_WIDTH)
IN_WIDTH = 2 * GLA_QK_WIDTH + 2 * GLA_V_WIDTH + 2 * GLA_GATE_RANK + 3 * DIL_WIDTH + MEM_WIDTH

kernel_name = 'hymba_gla_dilated_memxattn_encoder'


def rms_norm(x, gain):
    xf = x.astype(jnp.float32)
    y = xf * lax.rsqrt(jnp.mean(xf * xf, axis=-1, keepdims=True) + EPS)
    return (y * gain.astype(jnp.float32)).astype(x.dtype)


def head_rms_norm(x, gain):
    h, e = x.shape[-2:]
    xf = x.astype(jnp.float32)
    y = xf * lax.rsqrt(jnp.mean(xf * xf, axis=-1, keepdims=True) + EPS)
    return y * gain.reshape(h, e).astype(jnp.float32)


def split_columns(t, sizes):
    outs, start = [], 0
    for s in sizes:
        outs.append(t[..., start:start + s])
        start += s
    return outs


def t5_bucket(rel):
    half = REL_BUCKETS // 2
    max_exact = half // 2
    ret = jnp.where(rel > 0, half, 0)
    n = jnp.abs(rel)
    nf = jnp.maximum(n, 1).astype(jnp.float32)
    large = max_exact + (jnp.log(nf / max_exact) / math.log(REL_MAX_DISTANCE / max_exact)
                         * (half - max_exact)).astype(jnp.int32)
    large = jnp.minimum(large, half - 1)
    return ret + jnp.where(n < max_exact, n, large)


def gla_direction(q, k, v, g):
    b_, h_, s_, dk = q.shape
    dv = v.shape[-1]
    c = GLA_CHUNK
    n = s_ // c
    q, k, g = [t.reshape(b_, h_, n, c, dk) for t in (q, k, g)]
    v = v.reshape(b_, h_, n, c, dv)
    b = jnp.cumsum(g, axis=3)
    b_last = b[:, :, :, -1:, :]
    q_dec = q * jnp.exp(b)
    k_inv = k * jnp.exp(-b)
    k_end = k * jnp.exp(b_last - b)
    causal = jnp.tril(jnp.ones((c, c), dtype=bool))
    a = jnp.where(causal, jnp.einsum('bhncd,bhnsd->bhncs', q_dec, k_inv), 0.0)
    o_intra = jnp.einsum('bhncs,bhnsv->bhncv', a, v)
    d_state = jnp.einsum('bhncd,bhncv->nbhdv', k_end, v)
    decay = jnp.moveaxis(jnp.exp(b_last[:, :, :, 0, :]), 2, 0)
    q_chunks = jnp.moveaxis(q_dec, 2, 0)

    def step(state, inp):
        ds, dec, qc = inp
        out = jnp.einsum('bhcd,bhdv->bhcv', qc, state)
        return dec[..., None] * state + ds, out

    state0 = jnp.zeros((b_, h_, dk, dv), dtype=q.dtype)
    _, o_inter = lax.scan(step, state0, (d_state, decay, q_chunks))
    o = o_intra + jnp.moveaxis(o_inter, 0, 2)
    return o.reshape(b_, h_, s_, dv)


def gla_mixer(q, k, v, r, lr_f, lr_b, up_f, bias_f, up_b, bias_b, norm_gain):
    b_, s_ = q.shape[:2]

    def heads(t, e):
        return t.reshape(b_, s_, GLA_HEADS, e).transpose(0, 2, 1, 3).astype(jnp.float32)

    def log_gate(lr, up, bias):
        logits = (jnp.einsum('bsr,rk->bsk', lr, up) + bias).astype(jnp.float32)
        return heads(jax.nn.log_sigmoid(logits) / GLA_GATE_NORMALIZER, GLA_DK)

    qh = heads(q, GLA_DK) * (GLA_DK ** -0.5)
    kh = heads(k, GLA_DK)
    vh = heads(v, GLA_DV)
    g_f = log_gate(lr_f, up_f, bias_f)
    g_b = log_gate(lr_b, up_b, bias_b)
    o_f = gla_direction(qh, kh, vh, g_f)
    flip = lambda t: jnp.flip(t, axis=2)
    o_b = flip(gla_direction(flip(qh), flip(kh), flip(vh), flip(g_b)))
    o = (o_f + o_b).transpose(0, 2, 1, 3)
    gate = jax.nn.silu(r.reshape(b_, s_, GLA_HEADS, GLA_DV).astype(jnp.float32))
    o = head_rms_norm(o, norm_gain) * gate
    return o.reshape(b_, s_, GLA_V_WIDTH)


def dilated_branch(q, k, v, rel_table, window, dilation):
    b_, s_, h_, e = q.shape
    w = window // (2 * dilation)
    l = s_ // dilation
    nb = -(-l // w)
    lp = nb * w

    def stride(t):
        return t.reshape(b_, l, dilation, h_, e).transpose(0, 2, 3, 1, 4)

    qs = jnp.pad(stride(q), ((0, 0), (0, 0), (0, 0), (0, lp - l), (0, 0))).reshape(b_, dilation, h_, nb, w, e)

    def windows(t):
        tp = jnp.pad(stride(t), ((0, 0), (0, 0), (0, 0), (w, lp - l + w), (0, 0)))
        tp = tp.reshape(b_, dilation, h_, nb + 2, w, e)
        return jnp.concatenate([tp[:, :, :, :-2], tp[:, :, :, 1:-1], tp[:, :, :, 2:]], axis=4)

    kw = windows(k)
    vw = windows(v)
    rel_sub = jnp.arange(3 * w)[None, :] - w - jnp.arange(w)[:, None]
    bias = jnp.transpose(rel_table[t5_bucket(rel_sub * dilation)], (2, 0, 1)).astype(jnp.float32)
    key_pos = jnp.arange(nb)[:, None] * w - w + jnp.arange(3 * w)[None, :]
    mask = (jnp.abs(rel_sub) <= w)[None] & ((key_pos >= 0) & (key_pos < l))[:, None, :]
    s = jnp.einsum('bdhnqe,bdhnke->bdhnqk', qs, kw, preferred_element_type=jnp.float32) * (e ** -0.5)
    s = jnp.where(mask, s + bias[None, None, :, None], NEG_INF)
    m = jnp.max(s, axis=-1, keepdims=True)
    p = jnp.exp(s - m)
    den = jnp.sum(p, axis=-1, keepdims=True)
    o = jnp.einsum('bdhnqk,bdhnke->bdhnqe', p, vw.astype(jnp.float32)) / den
    lse = m + jnp.log(den)

    def unstride(t):
        t = t.reshape(b_, dilation, h_, lp, t.shape[-1])[:, :, :, :l]
        return t.transpose(0, 3, 1, 2, 4).reshape(b_, s_, h_, t.shape[-1])

    return unstride(o), unstride(lse)[..., 0]


def dilated_mixer(q, k, v, rel_table, norm_gain):
    b_, s_ = q.shape[:2]
    qh, kh, vh = [t.reshape(b_, s_, DIL_HEADS, DIL_HEAD_DIM) for t in (q, k, v)]
    outs, lses = [], []
    for window, dilation in DIL_CONFIGS:
        o, lse = dilated_branch(qh, kh, vh, rel_table, window, dilation)
        outs.append(o)
        lses.append(lse)
    weights = jax.nn.softmax(jnp.stack(lses, axis=0), axis=0)
    o = jnp.einsum('rbsh,rbshe->bshe', weights, jnp.stack(outs, axis=0))
    return head_rms_norm(o, norm_gain).reshape(b_, s_, DIL_WIDTH)


def memory_mixer(q, mem, mem_gain, w_mem_kv, norm_gain):
    b_, s_ = q.shape[:2]
    qh = q.reshape(b_, s_, MEM_HEADS, MEM_HEAD_DIM)
    kv = jnp.einsum('bmd,dk->bmk', rms_norm(mem, mem_gain), w_mem_kv)
    km, vm = [t.reshape(b_, mem.shape[1], MEM_HEADS, MEM_HEAD_DIM) for t in split_columns(kv, (MEM_WIDTH, MEM_WIDTH))]
    s = jnp.einsum('bshe,bmhe->bhsm', qh, km, preferred_element_type=jnp.float32) * (MEM_HEAD_DIM ** -0.5)
    p = jax.nn.softmax(s, axis=-1)
    o = jnp.einsum('bhsm,bmhe->bshe', p, vm.astype(jnp.float32))
    return head_rms_norm(o, norm_gain).reshape(b_, s_, MEM_WIDTH)


def _fwd_setup_inputs(seed: int = 0) -> dict:
    key = jax.random.key(seed)
    ks = jax.random.split(key, 24)
    f32 = jnp.float32
    nrm = lambda k, shape, scale: jax.random.normal(k, shape, f32) * scale
    gain = lambda k, shape: 1.0 + 0.02 * jax.random.normal(k, shape, f32)
    return {
        'x': nrm(ks[0], (BATCH, SEQ, D_MODEL), 1.0),
        'mem': nrm(ks[1], (BATCH, MEM_LEN, D_MODEL), 1.0),
        'norm_mix': gain(ks[2], (DEPTH, D_MODEL)),
        'w_in': nrm(ks[3], (DEPTH, D_MODEL, IN_WIDTH), D_MODEL ** -0.5),
        'gla_gate_up_fwd': nrm(ks[4], (DEPTH, GLA_GATE_RANK, GLA_QK_WIDTH), GLA_GATE_RANK ** -0.5),
        'gla_gate_bias_fwd': nrm(ks[5], (DEPTH, GLA_QK_WIDTH), 0.1),
        'gla_gate_up_bwd': nrm(ks[6], (DEPTH, GLA_GATE_RANK, GLA_QK_WIDTH), GLA_GATE_RANK ** -0.5),
        'gla_gate_bias_bwd': nrm(ks[7], (DEPTH, GLA_QK_WIDTH), 0.1),
        'gla_norm': gain(ks[8], (DEPTH, GLA_V_WIDTH)),
        'rel_bias': nrm(ks[9], (REL_BUCKETS, DIL_HEADS), 0.5),
        'dil_norm': gain(ks[10], (DEPTH, DIL_WIDTH)),
        'mem_norm': gain(ks[11], (DEPTH, D_MODEL)),
        'w_mem_kv': nrm(ks[12], (DEPTH, D_MODEL, 2 * MEM_WIDTH), D_MODEL ** -0.5),
        'mem_out_norm': gain(ks[13], (DEPTH, MEM_WIDTH)),
        'w_out': nrm(ks[14], (DEPTH, MIX_WIDTH, D_MODEL), MIX_WIDTH ** -0.5),
        'norm_mlp': gain(ks[15], (DEPTH, D_MODEL)),
        'w_up': nrm(ks[16], (DEPTH, D_MODEL, D_FF), D_MODEL ** -0.5),
        'w_down': nrm(ks[17], (DEPTH, D_FF, D_MODEL), D_FF ** -0.5),
        'norm_final': gain(ks[18], (D_MODEL,)),
    }


def _fwd_reference(x, mem, norm_mix, w_in, gla_gate_up_fwd, gla_gate_bias_fwd, gla_gate_up_bwd, gla_gate_bias_bwd,
              gla_norm, rel_bias, dil_norm, mem_norm, w_mem_kv, mem_out_norm, w_out, norm_mlp, w_up, w_down,
              norm_final):
    for l in range(DEPTH):
        h = rms_norm(x, norm_mix[l])
        proj = jnp.einsum('bsd,dk->bsk', h, w_in[l])
        (g_q, g_k, g_v, g_r, lr_f, lr_b, d_q, d_k, d_v, m_q) = split_columns(proj, IN_SPLITS)
        gla_out = gla_mixer(g_q, g_k, g_v, g_r, lr_f, lr_b, gla_gate_up_fwd[l], gla_gate_bias_fwd[l],
                            gla_gate_up_bwd[l], gla_gate_bias_bwd[l], gla_norm[l])
        dil_out = dilated_mixer(d_q, d_k, d_v, rel_bias, dil_norm[l])
        mem_out = memory_mixer(m_q, mem, mem_norm[l], w_mem_kv[l], mem_out_norm[l])
        mixed = jnp.concatenate([gla_out, dil_out, mem_out], axis=-1).astype(x.dtype)
        x = x + jnp.einsum('bsk,kd->bsd', mixed, w_out[l])
        h = rms_norm(x, norm_mlp[l])
        u = jnp.square(jax.nn.relu(jnp.einsum('bsd,df->bsf', h, w_up[l])))
        x = x + jnp.einsum('bsf,fd->bsd', u, w_down[l])
    return rms_norm(x, norm_final)


import jax as _jax
import jax.numpy as _jnp

TWIN_FORMAT = 'train_step'
FWD_PARAMS = ['x', 'mem', 'norm_mix', 'w_in', 'gla_gate_up_fwd', 'gla_gate_bias_fwd', 'gla_gate_up_bwd', 'gla_gate_bias_bwd', 'gla_norm', 'rel_bias', 'dil_norm', 'mem_norm', 'w_mem_kv', 'mem_out_norm', 'w_out', 'norm_mlp', 'w_up', 'w_down', 'norm_final']
TWIN_WEIGHTS = ['norm_mix', 'w_in', 'gla_gate_up_fwd', 'gla_gate_bias_fwd', 'gla_gate_up_bwd', 'gla_gate_bias_bwd', 'gla_norm', 'rel_bias', 'dil_norm', 'mem_norm', 'w_mem_kv', 'mem_out_norm', 'w_out', 'norm_mlp', 'w_up', 'w_down', 'norm_final']
TWIN_DIFF_INPUT = 'x'
TWIN_INPUTS = ['x', 'mem', 'norm_mix', 'w_in', 'gla_gate_up_fwd', 'gla_gate_bias_fwd', 'gla_gate_up_bwd', 'gla_gate_bias_bwd', 'gla_norm', 'rel_bias', 'dil_norm', 'mem_norm', 'w_mem_kv', 'mem_out_norm', 'w_out', 'norm_mlp', 'w_up', 'w_down', 'norm_final', 'loss_target', 'm_norm_mix', 'm_w_in', 'm_gla_gate_up_fwd', 'm_gla_gate_bias_fwd', 'm_gla_gate_up_bwd', 'm_gla_gate_bias_bwd', 'm_gla_norm', 'm_rel_bias', 'm_dil_norm', 'm_mem_norm', 'm_w_mem_kv', 'm_mem_out_norm', 'm_w_out', 'm_norm_mlp', 'm_w_up', 'm_w_down', 'm_norm_final', 'v_norm_mix', 'v_w_in', 'v_gla_gate_up_fwd', 'v_gla_gate_bias_fwd', 'v_gla_gate_up_bwd', 'v_gla_gate_bias_bwd', 'v_gla_norm', 'v_rel_bias', 'v_dil_norm', 'v_mem_norm', 'v_w_mem_kv', 'v_mem_out_norm', 'v_w_out', 'v_norm_mlp', 'v_w_up', 'v_w_down', 'v_norm_final']
TWIN_OUTPUTS = ['loss', 'grad_x', 'grad_norm_mix', 'grad_w_in', 'grad_gla_gate_up_fwd', 'grad_gla_gate_bias_fwd', 'grad_gla_gate_up_bwd', 'grad_gla_gate_bias_bwd', 'grad_gla_norm', 'grad_rel_bias', 'grad_dil_norm', 'grad_mem_norm', 'grad_w_mem_kv', 'grad_mem_out_norm', 'grad_w_out', 'grad_norm_mlp', 'grad_w_up', 'grad_w_down', 'grad_norm_final', 'delta_norm_mix', 'delta_w_in', 'delta_gla_gate_up_fwd', 'delta_gla_gate_bias_fwd', 'delta_gla_gate_up_bwd', 'delta_gla_gate_bias_bwd', 'delta_gla_norm', 'delta_rel_bias', 'delta_dil_norm', 'delta_mem_norm', 'delta_w_mem_kv', 'delta_mem_out_norm', 'delta_w_out', 'delta_norm_mlp', 'delta_w_up', 'delta_w_down', 'delta_norm_final', 'new_m_norm_mix', 'new_m_w_in', 'new_m_gla_gate_up_fwd', 'new_m_gla_gate_bias_fwd', 'new_m_gla_gate_up_bwd', 'new_m_gla_gate_bias_bwd', 'new_m_gla_norm', 'new_m_rel_bias', 'new_m_dil_norm', 'new_m_mem_norm', 'new_m_w_mem_kv', 'new_m_mem_out_norm', 'new_m_w_out', 'new_m_norm_mlp', 'new_m_w_up', 'new_m_w_down', 'new_m_norm_final', 'new_v_norm_mix', 'new_v_w_in', 'new_v_gla_gate_up_fwd', 'new_v_gla_gate_bias_fwd', 'new_v_gla_gate_up_bwd', 'new_v_gla_gate_bias_bwd', 'new_v_gla_norm', 'new_v_rel_bias', 'new_v_dil_norm', 'new_v_mem_norm', 'new_v_w_mem_kv', 'new_v_mem_out_norm', 'new_v_w_out', 'new_v_norm_mlp', 'new_v_w_up', 'new_v_w_down', 'new_v_norm_final']
TWIN_LEAF_KINDS = {'loss': 'loss', 'grad_x': 'grad_x', 'grad_norm_mix': 'grad_w', 'grad_w_in': 'grad_w', 'grad_gla_gate_up_fwd': 'grad_w', 'grad_gla_gate_bias_fwd': 'grad_w', 'grad_gla_gate_up_bwd': 'grad_w', 'grad_gla_gate_bias_bwd': 'grad_w', 'grad_gla_norm': 'grad_w', 'grad_rel_bias': 'grad_w', 'grad_dil_norm': 'grad_w', 'grad_mem_norm': 'grad_w', 'grad_w_mem_kv': 'grad_w', 'grad_mem_out_norm': 'grad_w', 'grad_w_out': 'grad_w', 'grad_norm_mlp': 'grad_w', 'grad_w_up': 'grad_w', 'grad_w_down': 'grad_w', 'grad_norm_final': 'grad_w', 'delta_norm_mix': 'delta_w', 'delta_w_in': 'delta_w', 'delta_gla_gate_up_fwd': 'delta_w', 'delta_gla_gate_bias_fwd': 'delta_w', 'delta_gla_gate_up_bwd': 'delta_w', 'delta_gla_gate_bias_bwd': 'delta_w', 'delta_gla_norm': 'delta_w', 'delta_rel_bias': 'delta_w', 'delta_dil_norm': 'delta_w', 'delta_mem_norm': 'delta_w', 'delta_w_mem_kv': 'delta_w', 'delta_mem_out_norm': 'delta_w', 'delta_w_out': 'delta_w', 'delta_norm_mlp': 'delta_w', 'delta_w_up': 'delta_w', 'delta_w_down': 'delta_w', 'delta_norm_final': 'delta_w', 'new_m_norm_mix': 'new_m', 'new_m_w_in': 'new_m', 'new_m_gla_gate_up_fwd': 'new_m', 'new_m_gla_gate_bias_fwd': 'new_m', 'new_m_gla_gate_up_bwd': 'new_m', 'new_m_gla_gate_bias_bwd': 'new_m', 'new_m_gla_norm': 'new_m', 'new_m_rel_bias': 'new_m', 'new_m_dil_norm': 'new_m', 'new_m_mem_norm': 'new_m', 'new_m_w_mem_kv': 'new_m', 'new_m_mem_out_norm': 'new_m', 'new_m_w_out': 'new_m', 'new_m_norm_mlp': 'new_m', 'new_m_w_up': 'new_m', 'new_m_w_down': 'new_m', 'new_m_norm_final': 'new_m', 'new_v_norm_mix': 'new_v', 'new_v_w_in': 'new_v', 'new_v_gla_gate_up_fwd': 'new_v', 'new_v_gla_gate_bias_fwd': 'new_v', 'new_v_gla_gate_up_bwd': 'new_v', 'new_v_gla_gate_bias_bwd': 'new_v', 'new_v_gla_norm': 'new_v', 'new_v_rel_bias': 'new_v', 'new_v_dil_norm': 'new_v', 'new_v_mem_norm': 'new_v', 'new_v_w_mem_kv': 'new_v', 'new_v_mem_out_norm': 'new_v', 'new_v_w_out': 'new_v', 'new_v_norm_mlp': 'new_v', 'new_v_w_up': 'new_v', 'new_v_w_down': 'new_v', 'new_v_norm_final': 'new_v'}


def _forward(args):
    return _fwd_reference(*[args[k] for k in FWD_PARAMS])


def _output_shape():
    def fwd():
        inp = _fwd_setup_inputs(0)
        return _fwd_reference(*[inp[k] for k in FWD_PARAMS])
    out = _jax.eval_shape(fwd)
    return out.shape, out.dtype

N_MICROBATCH = 1
ADAM_LR = 0.001
ADAM_B1 = 0.9
ADAM_B2 = 0.999
ADAM_EPS = 1e-08
ADAM_WD = 0.01
ADAM_STEP = 10
PER_EXAMPLE_BATCH_AXIS = {'x': 0, 'mem': 0, 'loss_target': 0}
SHARED_INPUTS = []
_WEIGHT_DTYPES = {'norm_mix': _jnp.float32, 'w_in': _jnp.float32, 'gla_gate_up_fwd': _jnp.float32, 'gla_gate_bias_fwd': _jnp.float32, 'gla_gate_up_bwd': _jnp.float32, 'gla_gate_bias_bwd': _jnp.float32, 'gla_norm': _jnp.float32, 'rel_bias': _jnp.float32, 'dil_norm': _jnp.float32, 'mem_norm': _jnp.float32, 'w_mem_kv': _jnp.float32, 'mem_out_norm': _jnp.float32, 'w_out': _jnp.float32, 'norm_mlp': _jnp.float32, 'w_up': _jnp.float32, 'w_down': _jnp.float32, 'norm_final': _jnp.float32}
MOMENT_SCALE = {'norm_mix': 1.109426e-01, 'w_in': 6.990536e-02, 'gla_gate_up_fwd': 9.963619e-03, 'gla_gate_bias_fwd': 2.440408e-02, 'gla_gate_up_bwd': 8.638776e-03, 'gla_gate_bias_bwd': 2.394924e-02, 'gla_norm': 4.818325e-02, 'rel_bias': 2.360981e-01, 'dil_norm': 1.203910e-01, 'mem_norm': 6.680946e-02, 'w_mem_kv': 9.022164e-02, 'mem_out_norm': 1.008415e-01, 'w_out': 8.427347e-02, 'norm_mlp': 9.169387e-02, 'w_up': 4.612488e-02, 'w_down': 1.071211e-01, 'norm_final': 3.331609e+01}


def _to_microbatches(a, axis):
    t = _jnp.moveaxis(a, axis, 0)
    t = t.reshape((N_MICROBATCH, t.shape[0] // N_MICROBATCH) + t.shape[1:])
    return _jnp.moveaxis(t, 1, axis + 1)


def setup_inputs(seed: int = 0) -> dict:
    inp = _fwd_setup_inputs(seed)
    key = _jax.random.fold_in(_jax.random.key(seed), 7919)
    shape, _ = _output_shape()
    out = dict(inp)
    out["loss_target"] = _jax.random.normal(_jax.random.fold_in(key, 0), shape, _jnp.float32)
    for i, name in enumerate(TWIN_WEIGHTS):
        w = inp[name].astype(_jnp.float32)
        if MOMENT_SCALE is None:
            s = _jnp.sqrt(_jnp.mean(_jnp.square(w)) + 1e-30)
        else:
            s = MOMENT_SCALE[name]
        km, kv = _jax.random.split(_jax.random.fold_in(key, i + 1))
        out[name] = w
        out["m_" + name] = s * _jax.random.normal(km, w.shape, _jnp.float32)
        out["v_" + name] = (s * s) * _jax.random.uniform(kv, w.shape, _jnp.float32, 0.5, 1.5)
    if N_MICROBATCH > 1:
        for name, axis in PER_EXAMPLE_BATCH_AXIS.items():
            out[name] = _to_microbatches(out[name], axis)
    return {'x': out['x'], 'mem': out['mem'], 'norm_mix': out['norm_mix'], 'w_in': out['w_in'], 'gla_gate_up_fwd': out['gla_gate_up_fwd'], 'gla_gate_bias_fwd': out['gla_gate_bias_fwd'], 'gla_gate_up_bwd': out['gla_gate_up_bwd'], 'gla_gate_bias_bwd': out['gla_gate_bias_bwd'], 'gla_norm': out['gla_norm'], 'rel_bias': out['rel_bias'], 'dil_norm': out['dil_norm'], 'mem_norm': out['mem_norm'], 'w_mem_kv': out['w_mem_kv'], 'mem_out_norm': out['mem_out_norm'], 'w_out': out['w_out'], 'norm_mlp': out['norm_mlp'], 'w_up': out['w_up'], 'w_down': out['w_down'], 'norm_final': out['norm_final'], 'loss_target': out['loss_target'], 'm_norm_mix': out['m_norm_mix'], 'm_w_in': out['m_w_in'], 'm_gla_gate_up_fwd': out['m_gla_gate_up_fwd'], 'm_gla_gate_bias_fwd': out['m_gla_gate_bias_fwd'], 'm_gla_gate_up_bwd': out['m_gla_gate_up_bwd'], 'm_gla_gate_bias_bwd': out['m_gla_gate_bias_bwd'], 'm_gla_norm': out['m_gla_norm'], 'm_rel_bias': out['m_rel_bias'], 'm_dil_norm': out['m_dil_norm'], 'm_mem_norm': out['m_mem_norm'], 'm_w_mem_kv': out['m_w_mem_kv'], 'm_mem_out_norm': out['m_mem_out_norm'], 'm_w_out': out['m_w_out'], 'm_norm_mlp': out['m_norm_mlp'], 'm_w_up': out['m_w_up'], 'm_w_down': out['m_w_down'], 'm_norm_final': out['m_norm_final'], 'v_norm_mix': out['v_norm_mix'], 'v_w_in': out['v_w_in'], 'v_gla_gate_up_fwd': out['v_gla_gate_up_fwd'], 'v_gla_gate_bias_fwd': out['v_gla_gate_bias_fwd'], 'v_gla_gate_up_bwd': out['v_gla_gate_up_bwd'], 'v_gla_gate_bias_bwd': out['v_gla_gate_bias_bwd'], 'v_gla_norm': out['v_gla_norm'], 'v_rel_bias': out['v_rel_bias'], 'v_dil_norm': out['v_dil_norm'], 'v_mem_norm': out['v_mem_norm'], 'v_w_mem_kv': out['v_w_mem_kv'], 'v_mem_out_norm': out['v_mem_out_norm'], 'v_w_out': out['v_w_out'], 'v_norm_mlp': out['v_norm_mlp'], 'v_w_up': out['v_w_up'], 'v_w_down': out['v_w_down'], 'v_norm_final': out['v_norm_final']}


def _loss(weights, diff, rest, loss_target):
    with _jax.named_scope("forward"):
        args = {**rest, TWIN_DIFF_INPUT: diff, **{k: w.astype(_WEIGHT_DTYPES[k]) for k, w in weights.items()}}
        y = _forward(args)
    with _jax.named_scope("loss_head"):
        err = _jnp.square(y.astype(_jnp.float32) - loss_target)
        return 0.5 * _jnp.sum(_jnp.mean(err, axis=-1)) if err.ndim else 0.5 * err


def _adamw(w, g, m, v):
    m = ADAM_B1 * m + (1.0 - ADAM_B1) * g
    v = ADAM_B2 * v + (1.0 - ADAM_B2) * _jnp.square(g)
    m_hat = m / (1.0 - ADAM_B1 ** ADAM_STEP)
    v_hat = v / (1.0 - ADAM_B2 ** ADAM_STEP)
    delta = -ADAM_LR * (m_hat / (_jnp.sqrt(v_hat) + ADAM_EPS) + ADAM_WD * w)
    return delta, m, v


def reference(x, mem, norm_mix, w_in, gla_gate_up_fwd, gla_gate_bias_fwd, gla_gate_up_bwd, gla_gate_bias_bwd, gla_norm, rel_bias, dil_norm, mem_norm, w_mem_kv, mem_out_norm, w_out, norm_mlp, w_up, w_down, norm_final, loss_target, m_norm_mix, m_w_in, m_gla_gate_up_fwd, m_gla_gate_bias_fwd, m_gla_gate_up_bwd, m_gla_gate_bias_bwd, m_gla_norm, m_rel_bias, m_dil_norm, m_mem_norm, m_w_mem_kv, m_mem_out_norm, m_w_out, m_norm_mlp, m_w_up, m_w_down, m_norm_final, v_norm_mix, v_w_in, v_gla_gate_up_fwd, v_gla_gate_bias_fwd, v_gla_gate_up_bwd, v_gla_gate_bias_bwd, v_gla_norm, v_rel_bias, v_dil_norm, v_mem_norm, v_w_mem_kv, v_mem_out_norm, v_w_out, v_norm_mlp, v_w_up, v_w_down, v_norm_final):
    given = dict(x=x, mem=mem, norm_mix=norm_mix, w_in=w_in, gla_gate_up_fwd=gla_gate_up_fwd, gla_gate_bias_fwd=gla_gate_bias_fwd, gla_gate_up_bwd=gla_gate_up_bwd, gla_gate_bias_bwd=gla_gate_bias_bwd, gla_norm=gla_norm, rel_bias=rel_bias, dil_norm=dil_norm, mem_norm=mem_norm, w_mem_kv=w_mem_kv, mem_out_norm=mem_out_norm, w_out=w_out, norm_mlp=norm_mlp, w_up=w_up, w_down=w_down, norm_final=norm_final, loss_target=loss_target, m_norm_mix=m_norm_mix, m_w_in=m_w_in, m_gla_gate_up_fwd=m_gla_gate_up_fwd, m_gla_gate_bias_fwd=m_gla_gate_bias_fwd, m_gla_gate_up_bwd=m_gla_gate_up_bwd, m_gla_gate_bias_bwd=m_gla_gate_bias_bwd, m_gla_norm=m_gla_norm, m_rel_bias=m_rel_bias, m_dil_norm=m_dil_norm, m_mem_norm=m_mem_norm, m_w_mem_kv=m_w_mem_kv, m_mem_out_norm=m_mem_out_norm, m_w_out=m_w_out, m_norm_mlp=m_norm_mlp, m_w_up=m_w_up, m_w_down=m_w_down, m_norm_final=m_norm_final, v_norm_mix=v_norm_mix, v_w_in=v_w_in, v_gla_gate_up_fwd=v_gla_gate_up_fwd, v_gla_gate_bias_fwd=v_gla_gate_bias_fwd, v_gla_gate_up_bwd=v_gla_gate_up_bwd, v_gla_gate_bias_bwd=v_gla_gate_bias_bwd, v_gla_norm=v_gla_norm, v_rel_bias=v_rel_bias, v_dil_norm=v_dil_norm, v_mem_norm=v_mem_norm, v_w_mem_kv=v_w_mem_kv, v_mem_out_norm=v_mem_out_norm, v_w_out=v_w_out, v_norm_mlp=v_norm_mlp, v_w_up=v_w_up, v_w_down=v_w_down, v_norm_final=v_norm_final)
    weights = {n: given[n] for n in TWIN_WEIGHTS}
    shared = {n: given[n] for n in SHARED_INPUTS}
    per_example = {n: given[n] for n in ['x', 'mem']}
    grad_fn = _jax.value_and_grad(_loss, argnums=(0, 1))

    def one_microbatch(ex, loss_target):
        ex = dict(ex)
        diff = ex.pop(TWIN_DIFF_INPUT)
        return grad_fn(weights, diff, {**shared, **ex}, loss_target)

    if N_MICROBATCH == 1:
        loss, (grad_w, grad_x) = one_microbatch(per_example, given["loss_target"])
    else:
        def body(carry, xs):
            loss_sum, grad_sum = carry
            l_k, (gw_k, gx_k) = one_microbatch(xs[0], xs[1])
            with _jax.named_scope("update"):
                return (loss_sum + l_k, _jax.tree.map(_jnp.add, grad_sum, gw_k)), gx_k

        init = (_jnp.zeros((), _jnp.float32), _jax.tree.map(_jnp.zeros_like, weights))
        (loss, grad_w), grad_x = _jax.lax.scan(body, init, (per_example, given["loss_target"]))
    with _jax.named_scope("update"):
        delta_w, new_m, new_v = {}, {}, {}
        for n in TWIN_WEIGHTS:
            delta_w[n], new_m[n], new_v[n] = _adamw(weights[n], grad_w[n], given["m_" + n], given["v_" + n])
    return (loss, grad_x, *[grad_w[n] for n in TWIN_WEIGHTS], *[delta_w[n] for n in TWIN_WEIGHTS],
            *[new_m[n] for n in TWIN_WEIGHTS], *[new_v[n] for n in TWIN_WEIGHTS])
```

```python
import functools
import math

import numpy as np
import jax
import jax.numpy as jnp
from jax import lax
from jax.experimental import pallas as pl
from jax.experimental.pallas import tpu as pltpu

F32, BF16 = jnp.float32, jnp.bfloat16
SDS = jax.ShapeDtypeStruct

DEPTH = 4
GLA_HEADS, GLA_DK, GLA_DV = 4, 128, 256
GLA_GATE_RANK = 16
GLA_GATE_NORMALIZER = 16.0
GLA_CHUNK = 64
DIL_HEADS, DIL_HEAD_DIM = 4, 128
DIL_DILATIONS = (1, 4, 16)
DIL_W = 64
MEM_HEADS, MEM_HEAD_DIM = 4, 128
REL_BUCKETS, REL_MAX_DISTANCE = 32, 1024
EPS = 1e-6
NEG_INF = -1e30
QK_W, V_W, DIL_WIDTH, MEM_WIDTH = 512, 1024, 512, 512
IN_WIDTH = 5152
LR_COL = 3072
IN_PAD = 5376
C_GQ, C_GK, C_GV, C_GR, C_DQ, C_MQ, C_LR = 0, 512, 1024, 2048, 3072, 4608, 5120
IN_TILE = 768

ADAM_LR, ADAM_B1, ADAM_B2, ADAM_EPS, ADAM_WD, ADAM_STEP = 0.001, 0.9, 0.999, 1e-08, 0.01, 10

V7X_VMEM_BYTES = 64 * 1024 * 1024
VMEM_LIMIT = V7X_VMEM_BYTES * 7 // 8
LANES = 128

TM_FWD = 512
TN = 1024
TK = 2048
TM_NORM_BWD = 256
TK_NORM_BWD = 1024
TM_W, TN_W, TK_W = 1024, 1024, 1024
TM_LOSS = 256
ELEMWISE_BLOCK_BYTES = 24 * 1024 * 1024

NT_DIMS = (((1,), (1,)), ((), ()))
TN_DIMS = (((0,), (0,)), ((), ()))


def _nt(a, b):
    return lax.dot_general(a, b, NT_DIMS, preferred_element_type=F32)


def _tn(a, b):
    return lax.dot_general(a, b, TN_DIMS, preferred_element_type=F32)


def _nn(a, b):
    return jnp.dot(a, b, preferred_element_type=F32)


def _cp(*sem):
    return pltpu.CompilerParams(dimension_semantics=sem, vmem_limit_bytes=VMEM_LIMIT)


def _tile(n, t):
    t = min(n, t)
    assert n % t == 0, (n, t)
    return t


def norm_mm(x, gain, w, *, tm, tn, name):
    M, D = x.shape
    N = w.shape[1]
    tm, tn = _tile(M, tm), _tile(N, tn)

    def body(x_ref, g_ref, w_ref, o_ref, hn_ref):
        @pl.when(pl.program_id(1) == 0)
        def _():
            xf = x_ref[...]
            r = lax.rsqrt(jnp.mean(xf * xf, axis=-1, keepdims=True) + EPS)
            hn_ref[...] = (xf * r * g_ref[...]).astype(BF16)
        o_ref[...] = _nn(hn_ref[...], w_ref[...]).astype(BF16)

    return pl.pallas_call(
        body, name=name, grid=(M // tm, N // tn),
        in_specs=[pl.BlockSpec((tm, D), lambda i, j: (i, 0)), pl.BlockSpec((1, D), lambda i, j: (0, 0)),
                  pl.BlockSpec((D, tn), lambda i, j: (0, j))],
        out_specs=[pl.BlockSpec((tm, tn), lambda i, j: (i, j)), pl.BlockSpec((tm, D), lambda i, j: (i, 0))],
        out_shape=[SDS((M, N), BF16), SDS((M, D), BF16)],
        compiler_params=_cp("parallel", "arbitrary"))(x, gain, w)


def mm_res(a, w, res, *, relu2, tm, tn, tk, name):
    M, K = a.shape
    N = w.shape[1]
    tm, tn, tk = _tile(M, tm), _tile(N, tn), _tile(K, tk)

    def body(a_ref, w_ref, r_ref, o_ref):
        @pl.when(pl.program_id(2) == 0)
        def _():
            o_ref[...] = r_ref[...]
        av = a_ref[...]
        if relu2:
            af = jnp.maximum(av.astype(F32), 0.0)
            av = (af * af).astype(BF16)
        o_ref[...] += _nn(av, w_ref[...])

    return pl.pallas_call(
        body, name=name, grid=(M // tm, N // tn, K // tk),
        in_specs=[pl.BlockSpec((tm, tk), lambda i, j, k: (i, k)), pl.BlockSpec((tk, tn), lambda i, j, k: (k, j)),
                  pl.BlockSpec((tm, tn), lambda i, j, k: (i, j))],
        out_specs=pl.BlockSpec((tm, tn), lambda i, j, k: (i, j)),
        out_shape=SDS((M, N), F32),
        compiler_params=_cp("parallel", "parallel", "arbitrary"))(a, w, res)


def mm_nt(dy, w, a, *, tm, tn, name):
    M, K = dy.shape
    N = w.shape[0]
    tm, tn = _tile(M, tm), _tile(N, tn)
    with_a = a is not None

    def body(*refs):
        if with_a:
            dy_ref, w_ref, a_ref, o_ref, dyb = refs
        else:
            dy_ref, w_ref, o_ref, dyb = refs

        @pl.when(pl.program_id(1) == 0)
        def _():
            dyb[...] = dy_ref[...].astype(BF16)
        r = _nt(dyb[...], w_ref[...])
        if with_a:
            r = r * (2.0 * jnp.maximum(a_ref[...].astype(F32), 0.0))
        o_ref[...] = r.astype(BF16)

    in_specs = [pl.BlockSpec((tm, K), lambda i, j: (i, 0)), pl.BlockSpec((tn, K), lambda i, j: (j, 0))]
    args = [dy, w]
    if with_a:
        in_specs.append(pl.BlockSpec((tm, tn), lambda i, j: (i, j)))
        args.append(a)
    return pl.pallas_call(
        body, name=name, grid=(M // tm, N // tn), in_specs=in_specs,
        out_specs=pl.BlockSpec((tm, tn), lambda i, j: (i, j)), out_shape=SDS((M, N), BF16),
        scratch_shapes=[pltpu.VMEM((tm, K), BF16)],
        compiler_params=_cp("parallel", "arbitrary"))(*args)


def mm_nt_normbwd(dy, w, x, gain, dx_in, *, tm, tk, name):
    M, K = dy.shape
    D = w.shape[0]
    tm, tk = _tile(M, tm), _tile(K, tk)
    nk = K // tk

    def body(dy_ref, w_ref, x_ref, g_ref, dxin_ref, dx_ref, gg_ref, acc):
        i, k = pl.program_id(0), pl.program_id(1)

        @pl.when(k == 0)
        def _():
            acc[...] = jnp.zeros_like(acc)
        acc[...] += _nt(dy_ref[...], w_ref[...])

        @pl.when(k == nk - 1)
        def _():
            dh = acc[...]
            xf = x_ref[...]
            r = lax.rsqrt(jnp.mean(xf * xf, axis=-1, keepdims=True) + EPS)
            xhat = xf * r
            u = dh * g_ref[...]
            dx_ref[...] = dxin_ref[...] + r * (u - xhat * jnp.mean(u * xhat, axis=-1, keepdims=True))
            part = jnp.sum(dh * xhat, axis=0, keepdims=True)

            @pl.when(i == 0)
            def _():
                gg_ref[...] = part

            @pl.when(i > 0)
            def _():
                gg_ref[...] += part

    return pl.pallas_call(
        body, name=name, grid=(M // tm, nk),
        in_specs=[pl.BlockSpec((tm, tk), lambda i, k: (i, k)), pl.BlockSpec((D, tk), lambda i, k: (0, k)),
                  pl.BlockSpec((tm, D), lambda i, k: (i, 0)), pl.BlockSpec((1, D), lambda i, k: (0, 0)),
                  pl.BlockSpec((tm, D), lambda i, k: (i, 0))],
        out_specs=[pl.BlockSpec((tm, D), lambda i, k: (i, 0)), pl.BlockSpec((1, D), lambda i, k: (0, 0))],
        out_shape=[SDS((M, D), F32), SDS((1, D), F32)],
        scratch_shapes=[pltpu.VMEM((tm, D), F32)],
        compiler_params=_cp("arbitrary", "arbitrary"))(dy, w, x, gain, dx_in)


def mm_tn(a, dy, *, relu2, tm, tn, tk, name):
    S_, Ka = a.shape
    N = dy.shape[1]
    tm, tn, tk = _tile(Ka, tm), _tile(N, tn), _tile(S_, tk)

    def body(a_ref, dy_ref, o_ref):
        @pl.when(pl.program_id(2) == 0)
        def _():
            o_ref[...] = jnp.zeros_like(o_ref)
        av = a_ref[...]
        if relu2:
            af = jnp.maximum(av.astype(F32), 0.0)
            av = (af * af).astype(BF16)
        o_ref[...] += _tn(av, dy_ref[...].astype(BF16))

    return pl.pallas_call(
        body, name=name, grid=(Ka // tm, N // tn, S_ // tk),
        in_specs=[pl.BlockSpec((tk, tm), lambda i, j, k: (k, i)), pl.BlockSpec((tk, tn), lambda i, j, k: (k, j))],
        out_specs=pl.BlockSpec((tm, tn), lambda i, j, k: (i, j)), out_shape=SDS((Ka, N), F32),
        compiler_params=_cp("parallel", "parallel", "arbitrary"))(a, dy)


def loss_head(x, gain, target, *, tm, name):
    M, D = x.shape
    tm = _tile(M, tm)

    def body(x_ref, g_ref, t_ref, dx_ref, gg_ref, l_ref):
        i = pl.program_id(0)
        xf = x_ref[...]
        r = lax.rsqrt(jnp.mean(xf * xf, axis=-1, keepdims=True) + EPS)
        xhat = xf * r
        e = xhat * g_ref[...] - t_ref[...]
        lpart = 0.5 * jnp.sum(jnp.mean(e * e, axis=-1, keepdims=True), axis=0, keepdims=True)
        dy = e * (1.0 / D)
        u = dy * g_ref[...]
        dx_ref[...] = r * (u - xhat * jnp.mean(u * xhat, axis=-1, keepdims=True))
        gpart = jnp.sum(dy * xhat, axis=0, keepdims=True)
        lrow = jnp.broadcast_to(lpart, (1, LANES))

        @pl.when(i == 0)
        def _():
            gg_ref[...] = gpart
            l_ref[...] = lrow

        @pl.when(i > 0)
        def _():
            gg_ref[...] += gpart
            l_ref[...] += lrow

    return pl.pallas_call(
        body, name=name, grid=(M // tm,),
        in_specs=[pl.BlockSpec((tm, D), lambda i: (i, 0)), pl.BlockSpec((1, D), lambda i: (0, 0)),
                  pl.BlockSpec((tm, D), lambda i: (i, 0))],
        out_specs=[pl.BlockSpec((tm, D), lambda i: (i, 0)), pl.BlockSpec((1, D), lambda i: (0, 0)),
                   pl.BlockSpec((1, LANES), lambda i: (0, 0))],
        out_shape=[SDS((M, D), F32), SDS((1, D), F32), SDS((1, LANES), F32)],
        compiler_params=_cp("arbitrary"))(x, gain, target)


def _log_sigmoid(z):
    return jnp.minimum(z, 0.0) - jnp.log(1.0 + jnp.exp(-jnp.abs(z)))


def _chunk_scan(x, row, reverse):
    n = x.shape[0]
    s = 1
    while s < GLA_CHUNK:
        if reverse:
            x = x + jnp.where(row < GLA_CHUNK - s, pltpu.roll(x, n - s, 0), 0.0)
        else:
            x = x + jnp.where(row >= s, pltpu.roll(x, s, 0), 0.0)
        s *= 2
    return x


def _chunk_edge(b, reverse):
    t = b.shape[0]
    nc = t // GLA_CHUNK
    b3 = b.reshape(nc, GLA_CHUNK, b.shape[1])
    e = b3[:, 0:1, :] if reverse else b3[:, GLA_CHUNK - 1:GLA_CHUNK, :]
    return jnp.broadcast_to(e, b3.shape).reshape(b.shape), e.reshape(nc, b.shape[1])


def _gla_gates(lr, up_ref, bias_ref, cs, row, reverse):
    z = _nn(lr, up_ref[:, cs]) + bias_ref[:, cs]
    g = _log_sigmoid(z) * (1.0 / GLA_GATE_NORMALIZER)
    b = _chunk_scan(g, row, reverse)
    bl, blc = _chunk_edge(b, reverse)
    return z, b, bl, blc


def _gla_mask(reverse):
    r = lax.broadcasted_iota(jnp.int32, (GLA_CHUNK, GLA_CHUNK), 0)
    c = lax.broadcasted_iota(jnp.int32, (GLA_CHUNK, GLA_CHUNK), 1)
    return (r <= c) if reverse else (r >= c)


def gla_fwd(proj, upad, bias, *, reverse, name):
    S_ = proj.shape[0]
    T = _tile(S_, 512)
    nt, nc = S_ // T, T // GLA_CHUNK
    H, DK, DV, C = GLA_HEADS, GLA_DK, GLA_DV, GLA_CHUNK
    tix = (lambda i: nt - 1 - i) if reverse else (lambda i: i)
    qscale = DK ** -0.5

    def body(q_ref, k_ref, v_ref, lr_ref, up_ref, b_ref, o_ref, st_ref, st_s, qd_s, ki_s, ke_s, dec_s):
        @pl.when(pl.program_id(0) == 0)
        def _():
            st_s[...] = jnp.zeros_like(st_s)
        row = lax.broadcasted_iota(jnp.int32, (T, DK), 0) % C
        mask = _gla_mask(reverse)
        lr = lr_ref[...]
        for h in range(H):
            cs = slice(h * DK, (h + 1) * DK)
            vs = slice(h * DV, (h + 1) * DV)
            _, b, bl, _ = _gla_gates(lr, up_ref, b_ref, cs, row, reverse)
            q = q_ref[:, cs].astype(F32) * qscale
            k = k_ref[:, cs].astype(F32)
            qd_s[...] = (q * jnp.exp(b)).astype(BF16)
            ki_s[...] = (k * jnp.exp(-b)).astype(BF16)
            ke_s[...] = (k * jnp.exp(bl - b)).astype(BF16)
            dec_s[...] = jnp.exp(bl)

            def chunk(ci, carry):
                c = (nc - 1 - ci) if reverse else ci
                r0 = pl.multiple_of(c * C, C)
                s0 = pl.multiple_of(c * DV, DV)
                qd, ki, ke = qd_s[pl.ds(r0, C), :], ki_s[pl.ds(r0, C), :], ke_s[pl.ds(r0, C), :]
                v = v_ref[pl.ds(r0, C), vs]
                st = st_s[h]
                stb = st.astype(BF16)
                st_ref[h, pl.ds(s0, DV), :] = stb
                a = jnp.where(mask, _nt(qd, ki), 0.0).astype(BF16)
                o_ref[pl.ds(r0, C), vs] = _nn(a, v) + _nt(qd, stb)
                st_s[h] = dec_s[pl.ds(r0, 1), :] * st + _tn(v, ke)
                return carry

            lax.fori_loop(0, nc, chunk, 0)

    return pl.pallas_call(
        body, name=name, grid=(nt,),
        in_specs=[pl.BlockSpec((T, QK_W), lambda i: (tix(i), C_GQ // QK_W)),
                  pl.BlockSpec((T, QK_W), lambda i: (tix(i), C_GK // QK_W)),
                  pl.BlockSpec((T, V_W), lambda i: (tix(i), C_GV // V_W)),
                  pl.BlockSpec((T, LANES), lambda i: (tix(i), C_LR // LANES)),
                  pl.BlockSpec((LANES, QK_W), lambda i: (0, 0)), pl.BlockSpec((1, QK_W), lambda i: (0, 0))],
        out_specs=[pl.BlockSpec((T, V_W), lambda i: (tix(i), 0)),
                   pl.BlockSpec((H, nc * DV, DK), lambda i: (0, tix(i), 0))],
        out_shape=[SDS((S_, V_W), F32), SDS((H, S_ // C * DV, DK), BF16)],
        scratch_shapes=[pltpu.VMEM((H, DV, DK), F32), pltpu.VMEM((T, DK), BF16), pltpu.VMEM((T, DK), BF16),
                        pltpu.VMEM((T, DK), BF16), pltpu.VMEM((T, DK), F32)],
        compiler_params=_cp("arbitrary"))(proj, proj, proj, proj, upad, bias)


def gla_bwd(proj, upad, bias, states, do, prev, *, reverse, name):
    S_ = proj.shape[0]
    T = _tile(S_, 512)
    nt, nc = S_ // T, T // GLA_CHUNK
    H, DK, DV, C = GLA_HEADS, GLA_DK, GLA_DV, GLA_CHUNK
    tix = (lambda i: i) if reverse else (lambda i: nt - 1 - i)
    qscale = DK ** -0.5
    with_prev = prev is not None
    NCP = max(8, nc)

    def body(*refs):
        (q_ref, k_ref, v_ref, lr_ref, up_ref, b_ref, st_ref, do_ref) = refs[:8]
        n_in = 12 if with_prev else 8
        pq_ref, pk_ref, pv_ref, plr_ref = refs[8:12] if with_prev else (None,) * 4
        dq_ref, dk_ref, dv_ref, dlr_ref, dup_ref, dbias_ref = refs[n_in:n_in + 6]
        dst_s, qd_s, ki_s, ke_s, dec_s, dqd_s, dki_s, dke_s, ddec_s = refs[n_in + 6:]
        first = pl.program_id(0) == 0

        @pl.when(first)
        def _():
            dst_s[...] = jnp.zeros_like(dst_s)
            dup_ref[...] = jnp.zeros_like(dup_ref)
            dbias_ref[...] = jnp.zeros_like(dbias_ref)

        row = lax.broadcasted_iota(jnp.int32, (T, DK), 0) % C
        edge_row = 0 if reverse else C - 1
        mask = _gla_mask(reverse)
        lr = lr_ref[...]
        dlr = plr_ref[...].astype(F32) if with_prev else jnp.zeros((T, LANES), F32)
        for h in range(H):
            cs = slice(h * DK, (h + 1) * DK)
            vs = slice(h * DV, (h + 1) * DV)
            z, b, bl, blc = _gla_gates(lr, up_ref, b_ref, cs, row, reverse)
            q = q_ref[:, cs].astype(F32) * qscale
            k = k_ref[:, cs].astype(F32)
            eb, einv, eend = jnp.exp(b), jnp.exp(-b), jnp.exp(bl - b)
            qd, ki, ke = q * eb, k * einv, k * eend
            qd_s[...] = qd.astype(BF16)
            ki_s[...] = ki.astype(BF16)
            ke_s[...] = ke.astype(BF16)
            dec_s[...] = jnp.exp(bl)
            ddec_s[...] = jnp.zeros_like(ddec_s)

            def chunk(ci, carry):
                c = ci if reverse else (nc - 1 - ci)
                r0 = pl.multiple_of(c * C, C)
                s0 = pl.multiple_of(c * DV, DV)
                qdc, kic, kec = qd_s[pl.ds(r0, C), :], ki_s[pl.ds(r0, C), :], ke_s[pl.ds(r0, C), :]
                v = v_ref[pl.ds(r0, C), vs]
                doc = do_ref[pl.ds(r0, C), vs]
                st = st_ref[h, pl.ds(s0, DV), :]
                dst = dst_s[h]
                dstb = dst.astype(BF16)
                a = jnp.where(mask, _nt(qdc, kic), 0.0).astype(BF16)
                da = jnp.where(mask, _nt(doc, v), 0.0).astype(BF16)
                dvc = _tn(a, doc) + _nt(kec, dstb)
                if with_prev:
                    dvc = dvc + pv_ref[pl.ds(r0, C), vs].astype(F32)
                dv_ref[pl.ds(r0, C), vs] = dvc.astype(BF16)
                dqd_s[pl.ds(r0, C), :] = _nn(da, kic) + _nn(doc, st)
                dki_s[pl.ds(r0, C), :] = _tn(da, qdc)
                dke_s[pl.ds(r0, C), :] = _nn(v, dstb)
                ddec_s[pl.ds(c, 1), :] = jnp.sum(dst * st.astype(F32), axis=0, keepdims=True)
                dst_s[h] = dec_s[pl.ds(r0, 1), :] * dst + _tn(doc, qdc)
                return carry

            lax.fori_loop(0, nc, chunk, 0)

            dqd, dki, dke = dqd_s[...], dki_s[...], dke_s[...]
            dq = dqd * eb * qscale
            dk = dki * einv + dke * eend
            if with_prev:
                dq = dq + pq_ref[:, cs].astype(F32)
                dk = dk + pk_ref[:, cs].astype(F32)
            dq_ref[:, cs] = dq.astype(BF16)
            dk_ref[:, cs] = dk.astype(BF16)
            wke = dke * ke
            db = dqd * qd - dki * ki - wke
            dbl = jnp.sum(wke.reshape(nc, C, DK), axis=1) + ddec_s[0:nc, :] * jnp.exp(blc)
            dbl_b = jnp.broadcast_to(dbl.reshape(nc, 1, DK), (nc, C, DK)).reshape(T, DK)
            db = db + jnp.where(row == edge_row, dbl_b, 0.0)
            dg = _chunk_scan(db, row, not reverse)
            dz = dg * (1.0 / GLA_GATE_NORMALIZER) * (1.0 / (1.0 + jnp.exp(z)))
            dzb = dz.astype(BF16)
            dbias_ref[:, cs] += jnp.sum(dz, axis=0, keepdims=True)
            dup_ref[:, cs] += _tn(lr, dzb)
            dlr = dlr + _nt(dzb, up_ref[:, cs])
        dlr_ref[...] = dlr.astype(BF16)

    tile = lambda w, cb: pl.BlockSpec((T, w), lambda i: (tix(i), cb))
    in_specs = [tile(QK_W, C_GQ // QK_W), tile(QK_W, C_GK // QK_W), tile(V_W, C_GV // V_W), tile(LANES, C_LR // LANES),
                pl.BlockSpec((LANES, QK_W), lambda i: (0, 0)), pl.BlockSpec((1, QK_W), lambda i: (0, 0)),
                pl.BlockSpec((H, nc * DV, DK), lambda i: (0, tix(i), 0)), tile(V_W, 0)]
    args = [proj, proj, proj, proj, upad, bias, states, do]
    if with_prev:
        in_specs += [tile(QK_W, 0), tile(QK_W, 0), tile(V_W, 0), tile(LANES, 0)]
        args += list(prev)
    return pl.pallas_call(
        body, name=name, grid=(nt,), in_specs=in_specs,
        out_specs=[tile(QK_W, 0), tile(QK_W, 0), tile(V_W, 0), tile(LANES, 0),
                   pl.BlockSpec((LANES, QK_W), lambda i: (0, 0)), pl.BlockSpec((1, QK_W), lambda i: (0, 0))],
        out_shape=[SDS((S_, QK_W), BF16), SDS((S_, QK_W), BF16), SDS((S_, V_W), BF16), SDS((S_, LANES), BF16),
                   SDS((LANES, QK_W), F32), SDS((1, QK_W), F32)],
        scratch_shapes=[pltpu.VMEM((H, DV, DK), F32), pltpu.VMEM((T, DK), BF16), pltpu.VMEM((T, DK), BF16),
                        pltpu.VMEM((T, DK), BF16), pltpu.VMEM((T, DK), F32), pltpu.VMEM((T, DK), F32),
                        pltpu.VMEM((T, DK), F32), pltpu.VMEM((T, DK), F32), pltpu.VMEM((NCP, DK), F32)],
        compiler_params=_cp("arbitrary"))(*args)


def _head_norm_fwd(o, width):
    out = []
    for h in range(o.shape[1] // width):
        oh = o[:, h * width:(h + 1) * width]
        rs = lax.rsqrt(jnp.mean(oh * oh, axis=-1, keepdims=True) + EPS)
        out.append((oh * rs, rs))
    return out


def gla_post(o_f, o_b, proj, gain, *, name):
    S_ = proj.shape[0]
    T = _tile(S_, 512)

    def body(of_ref, ob_ref, r_ref, g_ref, y_ref):
        o = of_ref[...] + ob_ref[...]
        for h, (ohat, _) in enumerate(_head_norm_fwd(o, GLA_DV)):
            vs = slice(h * GLA_DV, (h + 1) * GLA_DV)
            r = r_ref[:, vs].astype(F32)
            y_ref[:, vs] = (ohat * g_ref[:, vs] * (r / (1.0 + jnp.exp(-r)))).astype(BF16)

    blk = pl.BlockSpec((T, V_W), lambda i: (i, 0))
    return pl.pallas_call(
        body, name=name, grid=(S_ // T,),
        in_specs=[blk, blk, pl.BlockSpec((T, V_W), lambda i: (i, C_GR // V_W)), pl.BlockSpec((1, V_W), lambda i: (0, 0))],
        out_specs=blk, out_shape=SDS((S_, V_W), BF16), compiler_params=_cp("parallel"))(o_f, o_b, proj, gain)


def gla_post_bwd(dmix, o_f, o_b, proj, gain, *, name):
    S_ = proj.shape[0]
    T = _tile(S_, 512)

    def body(dy_ref, of_ref, ob_ref, r_ref, g_ref, do_ref, dr_ref, gg_ref):
        @pl.when(pl.program_id(0) == 0)
        def _():
            gg_ref[...] = jnp.zeros_like(gg_ref)
        o = of_ref[...] + ob_ref[...]
        for h, (ohat, rs) in enumerate(_head_norm_fwd(o, GLA_DV)):
            vs = slice(h * GLA_DV, (h + 1) * GLA_DV)
            r = r_ref[:, vs].astype(F32)
            dy = dy_ref[:, vs].astype(F32)
            sg = 1.0 / (1.0 + jnp.exp(-r))
            gate = r * sg
            n = ohat * g_ref[:, vs]
            dr_ref[:, vs] = (dy * n * (sg * (1.0 + r * (1.0 - sg)))).astype(BF16)
            dn = dy * gate
            u = dn * g_ref[:, vs]
            do_ref[:, vs] = (rs * (u - ohat * jnp.mean(u * ohat, axis=-1, keepdims=True))).astype(BF16)
            gg_ref[:, vs] += jnp.sum(dn * ohat, axis=0, keepdims=True)

    blk = pl.BlockSpec((T, V_W), lambda i: (i, 0))
    vec = pl.BlockSpec((1, V_W), lambda i: (0, 0))
    return pl.pallas_call(
        body, name=name, grid=(S_ // T,),
        in_specs=[blk, blk, blk, pl.BlockSpec((T, V_W), lambda i: (i, C_GR // V_W)), vec],
        out_specs=[blk, blk, vec], out_shape=[SDS((S_, V_W), BF16), SDS((S_, V_W), BF16), SDS((1, V_W), F32)],
        compiler_params=_cp("arbitrary"))(dmix, o_f, o_b, proj, gain)


def _t5_bucket_np(rel):
    half = REL_BUCKETS // 2
    max_exact = half // 2
    ret = np.where(rel > 0, half, 0)
    n = np.abs(rel)
    nf = np.maximum(n, 1).astype(np.float32)
    large = max_exact + (np.log(nf / np.float32(max_exact)) / np.float32(math.log(REL_MAX_DISTANCE / max_exact))
                         * np.float32(half - max_exact)).astype(np.int32)
    large = np.minimum(large, half - 1)
    return ret + np.where(n < max_exact, n, large)


def _rel_onehot(dilation):
    rel = np.arange(-DIL_W, DIL_W + 1)
    return np.eye(REL_BUCKETS, dtype=np.float32)[_t5_bucket_np(rel * dilation)]


def _toeplitz(vec, tq, flip):
    H = vec.shape[0]
    wlen = tq + 2 * DIL_W
    lu = wlen + tq
    if flip:
        vec = vec[:, ::-1]
    u = jnp.full((H, lu), NEG_INF, F32)
    u = lax.dynamic_update_slice(u, vec, (0, 0))
    t = jnp.tile(u, (1, tq))[:, :tq * (lu - 1)].reshape(H, tq, lu - 1)
    return t[:, :, :wlen]


def _untoeplitz(ds, tq):
    H = ds.shape[0]
    wlen = tq + 2 * DIL_W
    lu = wlen + tq
    p = jnp.pad(ds, ((0, 0), (0, 0), (0, lu - 1 - wlen))).reshape(H, tq * (lu - 1))
    p = jnp.pad(p, ((0, 0), (0, tq))).reshape(H, tq, lu)
    return jnp.sum(p, axis=1)[:, :2 * DIL_W + 1]


def _dil_window(p_ref, o_ref, n_ref, tq):
    return jnp.concatenate([p_ref[tq - DIL_W:, :], o_ref[...], n_ref[:DIL_W, :]], axis=0)


def _dil_specs(tq, nt, col0, ncols_per_r):
    cb = lambda h, r: r * ncols_per_r + col0 + h
    own = pl.BlockSpec((tq, LANES), lambda h, r, i: (i, cb(h, r)))
    prev = pl.BlockSpec((tq, LANES), lambda h, r, i: (jnp.maximum(i - 1, 0), cb(h, r)))
    nxt = pl.BlockSpec((tq, LANES), lambda h, r, i: (jnp.minimum(i + 1, nt - 1), cb(h, r)))
    return prev, own, nxt


def dil_fwd(qkv, bias, *, d, name):
    l = qkv.shape[0]
    tq = _tile(l, 512)
    nt = l // tq
    wlen = tq + 2 * DIL_W
    H = DIL_HEADS
    scale = DIL_HEAD_DIM ** -0.5

    def body(q_ref, kp, ko, kn, vp, vo, vn, b_ref, o_ref, l_ref):
        i = pl.program_id(2)
        kw = _dil_window(kp, ko, kn, tq)
        vw = _dil_window(vp, vo, vn, tq)
        s = _nt(q_ref[...], kw) * scale + b_ref[0]
        kpos = i * tq - DIL_W + lax.broadcasted_iota(jnp.int32, (tq, wlen), 1)
        s = jnp.where((kpos >= 0) & (kpos < l), s, NEG_INF)
        m = jnp.max(s, axis=-1, keepdims=True)
        p = jnp.exp(s - m)
        den = jnp.sum(p, axis=-1, keepdims=True)
        o_ref[...] = _nn(p.astype(BF16), vw) / den
        l_ref[...] = jnp.broadcast_to(m + jnp.log(den), (tq, LANES))

    _, q_spec, _ = _dil_specs(tq, nt, 0, 12)
    out_spec = pl.BlockSpec((tq, LANES), lambda h, r, i: (i, r * H + h))
    return pl.pallas_call(
        body, name=name, grid=(H, d, nt),
        in_specs=[q_spec, *_dil_specs(tq, nt, 4, 12), *_dil_specs(tq, nt, 8, 12),
                  pl.BlockSpec((1, tq, wlen), lambda h, r, i: (h, 0, 0))],
        out_specs=[out_spec, out_spec], out_shape=[SDS((l, d * DIL_WIDTH), F32), SDS((l, d * DIL_WIDTH), F32)],
        compiler_params=_cp("parallel", "parallel", "arbitrary"))(qkv, qkv, qkv, qkv, qkv, qkv, qkv, bias)


def dil_bwd_q(qkv, bias, do, lse, corr, *, d, name):
    l = qkv.shape[0]
    tq = _tile(l, 512)
    nt = l // tq
    wlen = tq + 2 * DIL_W
    H = DIL_HEADS
    scale = DIL_HEAD_DIM ** -0.5

    def body(q_ref, kp, ko, kn, vp, vo, vn, b_ref, do_ref, l_ref, c_ref, dq_ref, ds_ref):
        r, i = pl.program_id(1), pl.program_id(2)
        kw = _dil_window(kp, ko, kn, tq)
        vw = _dil_window(vp, vo, vn, tq)
        s = _nt(q_ref[...], kw) * scale + b_ref[0]
        kpos = i * tq - DIL_W + lax.broadcasted_iota(jnp.int32, (tq, wlen), 1)
        p = jnp.where((kpos >= 0) & (kpos < l), jnp.exp(s - l_ref[:, 0:1]), 0.0)
        ds = p * (_nt(do_ref[...], vw) - c_ref[:, 0:1])
        dq_ref[...] = _nn(ds.astype(BF16), kw) * scale

        @pl.when((r == 0) & (i == 0))
        def _():
            ds_ref[0] = ds

        @pl.when((r > 0) | (i > 0))
        def _():
            ds_ref[0] += ds

    _, q_spec, _ = _dil_specs(tq, nt, 0, 12)
    t_spec = pl.BlockSpec((tq, LANES), lambda h, r, i: (i, r * H + h))
    b_spec = pl.BlockSpec((1, tq, wlen), lambda h, r, i: (h, 0, 0))
    return pl.pallas_call(
        body, name=name, grid=(H, d, nt),
        in_specs=[q_spec, *_dil_specs(tq, nt, 4, 12), *_dil_specs(tq, nt, 8, 12), b_spec, t_spec, t_spec, t_spec],
        out_specs=[t_spec, b_spec], out_shape=[SDS((l, d * DIL_WIDTH), F32), SDS((H, tq, wlen), F32)],
        compiler_params=_cp("parallel", "arbitrary", "arbitrary"))(qkv, qkv, qkv, qkv, qkv, qkv, qkv, bias, do, lse, corr)


def dil_bwd_kv(qkv, bias_t, do, lse, corr, *, d, name):
    l = qkv.shape[0]
    tq = _tile(l, 512)
    nt = l // tq
    wlen = tq + 2 * DIL_W
    H = DIL_HEADS
    scale = DIL_HEAD_DIM ** -0.5

    def body(k_ref, v_ref, qp, qo, qn, dp, do_, dn, lp, lo, ln, cp, co, cn, b_ref, dk_ref, dv_ref):
        j = pl.program_id(2)
        qw = _dil_window(qp, qo, qn, tq)
        dow = _dil_window(dp, do_, dn, tq)
        lrow = _dil_window(lp, lo, ln, tq).T[0:1, :]
        crow = _dil_window(cp, co, cn, tq).T[0:1, :]
        st = _nt(k_ref[...], qw) * scale + b_ref[0]
        qpos = j * tq - DIL_W + lax.broadcasted_iota(jnp.int32, (tq, wlen), 1)
        pt = jnp.where((qpos >= 0) & (qpos < l), jnp.exp(st - lrow), 0.0)
        dv_ref[...] = _nn(pt.astype(BF16), dow)
        dst = pt * (_nt(v_ref[...], dow) - crow)
        dk_ref[...] = _nn(dst.astype(BF16), qw) * scale

    _, k_spec, _ = _dil_specs(tq, nt, 4, 12)
    _, v_spec, _ = _dil_specs(tq, nt, 8, 12)
    t_spec = pl.BlockSpec((tq, LANES), lambda h, r, i: (i, r * H + h))
    return pl.pallas_call(
        body, name=name, grid=(H, d, nt),
        in_specs=[k_spec, v_spec, *_dil_specs(tq, nt, 0, 12), *_dil_specs(tq, nt, 0, 4), *_dil_specs(tq, nt, 0, 4),
                  *_dil_specs(tq, nt, 0, 4), pl.BlockSpec((1, tq, wlen), lambda h, r, i: (h, 0, 0))],
        out_specs=[t_spec, t_spec], out_shape=[SDS((l, d * DIL_WIDTH), F32), SDS((l, d * DIL_WIDTH), F32)],
        compiler_params=_cp("parallel", "parallel", "arbitrary"))(
            qkv, qkv, qkv, qkv, qkv, do, do, do, lse, lse, lse, corr, corr, corr, bias_t)


def _dil_weights(l1, l2, l3):
    m = jnp.maximum(jnp.maximum(l1, l2), l3)
    e = [jnp.exp(x - m) for x in (l1, l2, l3)]
    tot = e[0] + e[1] + e[2]
    return [x / tot for x in e]


def dil_combine(os_, ls_, gain, *, name):
    S_ = os_[0].shape[0]
    T = _tile(S_, 512)

    def body(o1, o2, o3, l1, l2, l3, g_ref, y_ref):
        w = _dil_weights(l1[...], l2[...], l3[...])
        o = w[0] * o1[...] + w[1] * o2[...] + w[2] * o3[...]
        for h, (ohat, _) in enumerate(_head_norm_fwd(o, DIL_HEAD_DIM)):
            cs = slice(h * DIL_HEAD_DIM, (h + 1) * DIL_HEAD_DIM)
            y_ref[:, cs] = (ohat * g_ref[:, cs]).astype(BF16)

    blk = pl.BlockSpec((T, DIL_WIDTH), lambda i: (i, 0))
    return pl.pallas_call(
        body, name=name, grid=(S_ // T,), in_specs=[blk] * 6 + [pl.BlockSpec((1, DIL_WIDTH), lambda i: (0, 0))],
        out_specs=blk, out_shape=SDS((S_, DIL_WIDTH), BF16), compiler_params=_cp("parallel"))(*os_, *ls_, gain)


def dil_combine_bwd(dmix, os_, ls_, gain, *, name):
    S_ = os_[0].shape[0]
    T = _tile(S_, 512)
    E = DIL_HEAD_DIM

    def body(dy_ref, o1, o2, o3, l1, l2, l3, g_ref, d1, d2, d3, c1, c2, c3, gg_ref):
        @pl.when(pl.program_id(0) == 0)
        def _():
            gg_ref[...] = jnp.zeros_like(gg_ref)
        w = _dil_weights(l1[...], l2[...], l3[...])
        o = w[0] * o1[...] + w[1] * o2[...] + w[2] * o3[...]
        for h, (ohat, rs) in enumerate(_head_norm_fwd(o, E)):
            cs = slice(h * E, (h + 1) * E)
            dy = dy_ref[:, cs].astype(F32)
            u = dy * g_ref[:, cs]
            do = rs * (u - ohat * jnp.mean(u * ohat, axis=-1, keepdims=True))
            gg_ref[:, cs] += jnp.sum(dy * ohat, axis=0, keepdims=True)
            tot = jnp.sum(do * o[:, cs], axis=-1, keepdims=True)
            for wr, d_ref, c_ref in zip(w, (d1, d2, d3), (c1, c2, c3)):
                d_ref[:, cs] = (wr[:, cs] * do).astype(BF16)
                c_ref[:, cs] = wr[:, cs] * tot

    blk = pl.BlockSpec((T, DIL_WIDTH), lambda i: (i, 0))
    vec = pl.BlockSpec((1, DIL_WIDTH), lambda i: (0, 0))
    return pl.pallas_call(
        body, name=name, grid=(S_ // T,),
        in_specs=[pl.BlockSpec((T, DIL_WIDTH), lambda i: (i, V_W // DIL_WIDTH))] + [blk] * 6 + [vec],
        out_specs=[blk] * 6 + [vec],
        out_shape=[SDS((S_, DIL_WIDTH), BF16)] * 3 + [SDS((S_, DIL_WIDTH), F32)] * 3 + [SDS((1, DIL_WIDTH), F32)],
        compiler_params=_cp("arbitrary"))(dmix, *os_, *ls_, gain)


def _mem_softmax(q, k, scale):
    s = _nt(q, k) * scale
    e = jnp.exp(s - jnp.max(s, axis=-1, keepdims=True))
    return e / jnp.sum(e, axis=-1, keepdims=True)


def mem_fwd(proj, kv, gain, *, name):
    S_ = proj.shape[0]
    M = kv.shape[0]
    T = _tile(S_, 1024)
    E, H = MEM_HEAD_DIM, MEM_HEADS
    scale = E ** -0.5

    def body(q_ref, k_ref, v_ref, g_ref, y_ref):
        p = _mem_softmax(q_ref[...], k_ref[...], scale)
        o = _nn(p.astype(BF16), v_ref[...])
        rs = lax.rsqrt(jnp.mean(o * o, axis=-1, keepdims=True) + EPS)
        y_ref[...] = (o * rs * g_ref[...]).astype(BF16)

    return pl.pallas_call(
        body, name=name, grid=(H, S_ // T),
        in_specs=[pl.BlockSpec((T, E), lambda h, i: (i, C_MQ // E + h)), pl.BlockSpec((M, E), lambda h, i: (0, h)),
                  pl.BlockSpec((M, E), lambda h, i: (0, H + h)), pl.BlockSpec((1, E), lambda h, i: (0, h))],
        out_specs=pl.BlockSpec((T, E), lambda h, i: (i, h)), out_shape=SDS((S_, MEM_WIDTH), BF16),
        compiler_params=_cp("parallel", "parallel"))(proj, kv, kv, gain)


def mem_bwd(dmix, proj, kv, gain, *, name):
    S_ = proj.shape[0]
    M = kv.shape[0]
    T = _tile(S_, 1024)
    E, H = MEM_HEAD_DIM, MEM_HEADS
    scale = E ** -0.5

    def body(dy_ref, q_ref, k_ref, v_ref, g_ref, dq_ref, dk_ref, dv_ref, gg_ref):
        @pl.when(pl.program_id(1) == 0)
        def _():
            dk_ref[...] = jnp.zeros_like(dk_ref)
            dv_ref[...] = jnp.zeros_like(dv_ref)
            gg_ref[...] = jnp.zeros_like(gg_ref)
        q, k, v = q_ref[...], k_ref[...], v_ref[...]
        p = _mem_softmax(q, k, scale)
        pb = p.astype(BF16)
        o = _nn(pb, v)
        rs = lax.rsqrt(jnp.mean(o * o, axis=-1, keepdims=True) + EPS)
        ohat = o * rs
        dy = dy_ref[...].astype(F32)
        u = dy * g_ref[...]
        do = (rs * (u - ohat * jnp.mean(u * ohat, axis=-1, keepdims=True))).astype(BF16)
        gg_ref[...] += jnp.sum(dy * ohat, axis=0, keepdims=True)
        dv_ref[...] += _tn(pb, do)
        dp = _nt(do, v)
        ds = (p * (dp - jnp.sum(p * dp, axis=-1, keepdims=True))).astype(BF16)
        dq_ref[...] = (_nn(ds, k) * scale).astype(BF16)
        dk_ref[...] += _tn(ds, q) * scale

    return pl.pallas_call(
        body, name=name, grid=(H, S_ // T),
        in_specs=[pl.BlockSpec((T, E), lambda h, i: (i, (V_W + DIL_WIDTH) // E + h)),
                  pl.BlockSpec((T, E), lambda h, i: (i, C_MQ // E + h)), pl.BlockSpec((M, E), lambda h, i: (0, h)),
                  pl.BlockSpec((M, E), lambda h, i: (0, H + h)), pl.BlockSpec((1, E), lambda h, i: (0, h))],
        out_specs=[pl.BlockSpec((T, E), lambda h, i: (i, h)), pl.BlockSpec((M, E), lambda h, i: (0, h)),
                   pl.BlockSpec((M, E), lambda h, i: (0, h)), pl.BlockSpec((1, E), lambda h, i: (0, h))],
        out_shape=[SDS((S_, MEM_WIDTH), BF16), SDS((M, MEM_WIDTH), F32), SDS((M, MEM_WIDTH), F32),
                   SDS((1, MEM_WIDTH), F32)],
        compiler_params=_cp("parallel", "arbitrary"))(dmix, proj, kv, kv, gain)


def _strided(a, d):
    return a if d == 1 else a.reshape(a.shape[0] // d, d * a.shape[1])


def _unstrided(a, d):
    return a if d == 1 else a.reshape(a.shape[0] * d, a.shape[1] // d)


def _dil_tables(rel_bias, seq):
    tabs = []
    for d in DIL_DILATIONS:
        tq = min(seq // d, 512)
        vec = jnp.sum(jnp.asarray(_rel_onehot(d))[:, :, None] * rel_bias[None], axis=1).T
        tabs.append((_toeplitz(vec, tq, False), _toeplitz(vec, tq, True)))
    return tabs


def _rel_bias_grad(ds_sums, seq):
    g = jnp.zeros((REL_BUCKETS, DIL_HEADS), F32)
    for d, ds in zip(DIL_DILATIONS, ds_sums):
        gvec = _untoeplitz(ds, min(seq // d, 512))
        g = g + jnp.sum(jnp.asarray(_rel_onehot(d))[:, :, None] * gvec.T[:, None, :], axis=0)
    return g


def _pad_gate_up(up, row0):
    return jnp.zeros((LANES, QK_W), BF16).at[row0:row0 + GLA_GATE_RANK].set(up.astype(BF16))


def _layer_fwd(x, mem, p, tabs, l):
    tag = f"l{l}"
    proj, hn1 = norm_mm(x, p["norm_mix"], p["w_in"], tm=TM_FWD, tn=IN_TILE, name=f"in_proj_{tag}")
    o_f, st_f = gla_fwd(proj, p["up_f"], p["bias_f"], reverse=False, name=f"gla_fwd_f_{tag}")
    o_b, st_b = gla_fwd(proj, p["up_b"], p["bias_b"], reverse=True, name=f"gla_fwd_b_{tag}")
    gla_out = gla_post(o_f, o_b, proj, p["gla_norm"], name=f"gla_post_{tag}")
    qkv = proj[:, C_DQ:C_MQ]
    os_, ls_ = [], []
    for d, (bias, _) in zip(DIL_DILATIONS, tabs):
        o, lse = dil_fwd(_strided(qkv, d), bias, d=d, name=f"dil_fwd_d{d}_{tag}")
        os_.append(_unstrided(o, d))
        ls_.append(_unstrided(lse, d))
    dil_out = dil_combine(os_, ls_, p["dil_norm"], name=f"dil_combine_{tag}")
    kv, hmem = norm_mm(mem, p["mem_norm"], p["w_mem_kv"], tm=256, tn=1024, name=f"mem_kv_{tag}")
    mem_out = mem_fwd(proj, kv, p["mem_out_norm"], name=f"mem_fwd_{tag}")
    mixed = jnp.concatenate([gla_out, dil_out, mem_out], axis=1)
    x1 = mm_res(mixed, p["w_out"], x, relu2=False, tm=TM_FWD, tn=TN, tk=TK, name=f"out_proj_{tag}")
    a, hn2 = norm_mm(x1, p["norm_mlp"], p["w_up"], tm=TM_FWD, tn=TN, name=f"up_proj_{tag}")
    x2 = mm_res(a, p["w_down"], x1, relu2=True, tm=TM_FWD, tn=TN, tk=TK, name=f"down_proj_{tag}")
    saved = dict(x=x, proj=proj, hn1=hn1, o_f=o_f, o_b=o_b, st_f=st_f, st_b=st_b, qkv=qkv, os=os_, ls=ls_,
                 kv=kv, hmem=hmem, mixed=mixed, x1=x1, a=a, hn2=hn2)
    return x2, saved


def _layer_bwd(dx2, mem, p, tabs, s, l):
    tag = f"l{l}"
    seq = dx2.shape[0]
    g = {}
    da = mm_nt(dx2, p["w_down"], s["a"], tm=TM_FWD, tn=TN, name=f"down_bwd_x_{tag}")
    g["w_down"] = mm_tn(s["a"], dx2, relu2=True, tm=TM_W, tn=TN_W, tk=TK_W, name=f"down_bwd_w_{tag}")
    g["w_up"] = mm_tn(s["hn2"], da, relu2=False, tm=TM_W, tn=TN_W, tk=TK_W, name=f"up_bwd_w_{tag}")
    dx1, g["norm_mlp"] = mm_nt_normbwd(da, p["w_up"], s["x1"], p["norm_mlp"], dx2, tm=TM_NORM_BWD, tk=TK_NORM_BWD,
                                       name=f"up_bwd_x_{tag}")
    dmix = mm_nt(dx1, p["w_out"], None, tm=TM_FWD, tn=TN, name=f"out_bwd_x_{tag}")
    g["w_out"] = mm_tn(s["mixed"], dx1, relu2=False, tm=TM_W, tn=TN_W, tk=TK_W, name=f"out_bwd_w_{tag}")
    proj = s["proj"]
    do, dr, g["gla_norm"] = gla_post_bwd(dmix, s["o_f"], s["o_b"], proj, p["gla_norm"], name=f"gla_post_bwd_{tag}")
    r_f = gla_bwd(proj, p["up_f"], p["bias_f"], s["st_f"], do, None, reverse=False, name=f"gla_bwd_f_{tag}")
    r_b = gla_bwd(proj, p["up_b"], p["bias_b"], s["st_b"], do, r_f[:4], reverse=True, name=f"gla_bwd_b_{tag}")
    dgq, dgk, dgv, dlr = r_b[:4]
    g["up_f"], g["bias_f"] = r_f[4][0:GLA_GATE_RANK], r_f[5]
    g["up_b"], g["bias_b"] = r_b[4][GLA_GATE_RANK:2 * GLA_GATE_RANK], r_b[5]
    *branch, g["dil_norm"] = dil_combine_bwd(dmix, s["os"], s["ls"], p["dil_norm"], name=f"dil_combine_bwd_{tag}")
    dos, corrs = branch[:3], branch[3:]
    ddq = ddk = ddv = None
    ds_sums = []
    for r, (d, (bias, bias_t)) in enumerate(zip(DIL_DILATIONS, tabs)):
        qkv_d = _strided(s["qkv"], d)
        do_d, lse_d, corr_d = _strided(dos[r], d), _strided(s["ls"][r], d), _strided(corrs[r], d)
        dq_d, ds_sum = dil_bwd_q(qkv_d, bias, do_d, lse_d, corr_d, d=d, name=f"dil_bwd_q_d{d}_{tag}")
        dk_d, dv_d = dil_bwd_kv(qkv_d, bias_t, do_d, lse_d, corr_d, d=d, name=f"dil_bwd_kv_d{d}_{tag}")
        ds_sums.append(ds_sum)
        dq_d, dk_d, dv_d = _unstrided(dq_d, d), _unstrided(dk_d, d), _unstrided(dv_d, d)
        ddq, ddk, ddv = (dq_d, dk_d, dv_d) if ddq is None else (ddq + dq_d, ddk + dk_d, ddv + dv_d)
    g["rel_bias"] = _rel_bias_grad(ds_sums, seq)
    dmq, dmk, dmv, g["mem_out_norm"] = mem_bwd(dmix, proj, s["kv"], p["mem_out_norm"], name=f"mem_bwd_{tag}")
    dkv = jnp.concatenate([dmk, dmv], axis=1).astype(BF16)
    g["w_mem_kv"] = mm_tn(s["hmem"], dkv, relu2=False, tm=TM_W, tn=TN_W, tk=TK_W, name=f"mem_kv_bwd_w_{tag}")
    _, g["mem_norm"] = mm_nt_normbwd(dkv, p["w_mem_kv"], mem, p["mem_norm"], jnp.zeros_like(mem), tm=TM_NORM_BWD,
                                     tk=TK_NORM_BWD, name=f"mem_kv_bwd_x_{tag}")
    dproj = jnp.concatenate([dgq, dgk, dgv, dr, ddq.astype(BF16), ddk.astype(BF16), ddv.astype(BF16), dmq, dlr,
                             jnp.zeros((seq, IN_PAD - C_LR - LANES), BF16)], axis=1)
    g["w_in"] = mm_tn(s["hn1"], dproj, relu2=False, tm=TM_W, tn=IN_TILE, tk=TK_W, name=f"in_bwd_w_{tag}")
    dx0, g["norm_mix"] = mm_nt_normbwd(dproj, p["w_in"], s["x"], p["norm_mix"], dx1, tm=TM_NORM_BWD, tk=IN_TILE,
                                       name=f"in_bwd_x_{tag}")
    return dx0, g


def local_step(x, mem, target, layers, norm_final, rel_bias):
    tabs = _dil_tables(rel_bias, x.shape[0])
    saved = []
    for l, p in enumerate(layers):
        x, s = _layer_fwd(x, mem, p, tabs, l)
        saved.append(s)
    dx, g_final, loss = loss_head(x, norm_final, target, tm=TM_LOSS, name="loss_head")
    grads = [None] * len(layers)
    for l in reversed(range(len(layers))):
        dx, grads[l] = _layer_bwd(dx, mem, layers[l], tabs, saved[l], l)
    return loss, dx, grads, g_final


MESH = pl.DeviceIdType.MESH
HBM_SPEC = pl.BlockSpec(memory_space=pltpu.HBM)
N_CHIPS = 4
N_DEV = 8


def _mesh_pos():
    return lax.axis_index("x"), lax.axis_index("y"), lax.axis_index("c")


def chip_exchange(bufs, *, scatter, name):
    n = len(bufs)

    def body(*refs):
        ins, outs = refs[:n], refs[n:2 * n]
        send, recv, local = refs[2 * n:]
        x, y, c = _mesh_pos()
        me = 2 * x + y
        chips = [(1 - x, y), (x, 1 - y), (1 - x, 1 - y)]
        kept, sent = [], []
        for a in range(n):
            mine = pltpu.make_async_copy(ins[a].at[me] if scatter else ins[a], outs[a].at[me], local.at[a])
            mine.start()
            kept.append(mine)
            for j, (px, py) in enumerate(chips):
                cp = pltpu.make_async_remote_copy(
                    src_ref=ins[a].at[2 * px + py] if scatter else ins[a], dst_ref=outs[a].at[me],
                    send_sem=send.at[3 * a + j], recv_sem=recv.at[3 * a + j],
                    device_id=(px, py, c), device_id_type=MESH)
                cp.start()
                sent.append(cp)
        for a in range(n):
            for j, (px, py) in enumerate(chips):
                pltpu.make_async_remote_copy(
                    src_ref=ins[a].at[me] if scatter else ins[a], dst_ref=outs[a].at[2 * px + py],
                    send_sem=send.at[3 * a + j], recv_sem=recv.at[3 * a + j],
                    device_id=(px, py, c), device_id_type=MESH).wait_recv()
        for cp in sent:
            cp.wait_send()
        for cp in kept:
            cp.wait()

    out_shape = [SDS(b.shape if scatter else (N_CHIPS,) + b.shape, b.dtype) for b in bufs]
    return pl.pallas_call(
        body, name=name, in_specs=[HBM_SPEC] * n, out_specs=[HBM_SPEC] * n, out_shape=out_shape,
        scratch_shapes=[pltpu.SemaphoreType.DMA((3 * n,)), pltpu.SemaphoreType.DMA((3 * n,)),
                        pltpu.SemaphoreType.DMA((n,))])(*bufs)


def sibling_exchange(bufs, *, name):
    n = len(bufs)

    def body(*refs):
        ins, outs = refs[:n], refs[n:2 * n]
        send, recv = refs[2 * n:]
        x, y, c = _mesh_pos()
        cps = [pltpu.make_async_remote_copy(src_ref=ins[a], dst_ref=outs[a], send_sem=send.at[a], recv_sem=recv.at[a],
                                            device_id=(x, y, 1 - c), device_id_type=MESH) for a in range(n)]
        for cp in cps:
            cp.start()
        for cp in cps:
            cp.wait()

    return pl.pallas_call(
        body, name=name, in_specs=[HBM_SPEC] * n, out_specs=[HBM_SPEC] * n,
        out_shape=[SDS(b.shape, b.dtype) for b in bufs],
        scratch_shapes=[pltpu.SemaphoreType.DMA((n,)), pltpu.SemaphoreType.DMA((n,))])(*bufs)


def all_reduce_small(v, *, name):
    R = v.shape[0]

    def body(v_ref, o_ref, slots, send, recv):
        x, y, c = _mesh_pos()
        me = 4 * x + 2 * y + c
        flip = lambda p, f: 1 - p if f else p
        peers = [(flip(x, k & 4), flip(y, k & 2), flip(c, k & 1)) for k in range(1, N_DEV)]
        slots[pl.ds(me, 1)] = v_ref[...][None]
        cps = [pltpu.make_async_remote_copy(src_ref=v_ref, dst_ref=slots.at[me], send_sem=send.at[k], recv_sem=recv.at[k],
                                            device_id=peer, device_id_type=MESH) for k, peer in enumerate(peers)]
        for cp in cps:
            cp.start()
        for k, (px, py, pc) in enumerate(peers):
            pltpu.make_async_remote_copy(src_ref=v_ref, dst_ref=slots.at[4 * px + 2 * py + pc], send_sem=send.at[k],
                                         recv_sem=recv.at[k], device_id=(px, py, pc), device_id_type=MESH).wait_recv()
        for cp in cps:
            cp.wait_send()
        acc = slots[0]
        for s in range(1, N_DEV):
            acc = acc + slots[s]
        o_ref[...] = acc

    vm = pl.BlockSpec(memory_space=pltpu.VMEM)
    return pl.pallas_call(
        body, name=name, in_specs=[vm], out_specs=vm, out_shape=SDS(v.shape, F32),
        scratch_shapes=[pltpu.VMEM((N_DEV, R, LANES), F32), pltpu.SemaphoreType.DMA((N_DEV - 1,)),
                        pltpu.SemaphoreType.DMA((N_DEV - 1,))])(v)


def sum_chips(recv, *, name):
    _, R, C = recv.shape
    tr = _tile(R, 512)

    def body(r_ref, o_ref):
        acc = r_ref[0].astype(F32)
        for j in range(1, N_CHIPS):
            acc = acc + r_ref[j].astype(F32)
        o_ref[...] = acc

    return pl.pallas_call(
        body, name=name, grid=(R // tr,), in_specs=[pl.BlockSpec((N_CHIPS, tr, C), lambda i: (0, i, 0))],
        out_specs=pl.BlockSpec((tr, C), lambda i: (i, 0)), out_shape=SDS((R, C), F32),
        compiler_params=_cp("parallel"))(recv)


def adamw(parts, w, m, v, *, name):
    R, C = w.shape
    n = len(parts)
    tr = _tile(R, 512)
    while 2 * (n + 7) * tr * C * 4 > ELEMWISE_BLOCK_BYTES and tr % 16 == 0:
        tr //= 2
    c1 = 1.0 - ADAM_B1 ** ADAM_STEP
    c2 = 1.0 - ADAM_B2 ** ADAM_STEP

    def body(*refs):
        w_ref, m_ref, v_ref = refs[n:n + 3]
        g_ref, d_ref, nm_ref, nv_ref = refs[n + 3:]
        g = refs[0][...]
        for r in refs[1:n]:
            g = g + r[...]
        nm = ADAM_B1 * m_ref[...] + (1.0 - ADAM_B1) * g
        nv = ADAM_B2 * v_ref[...] + (1.0 - ADAM_B2) * (g * g)
        g_ref[...] = g
        nm_ref[...] = nm
        nv_ref[...] = nv
        d_ref[...] = -ADAM_LR * ((nm / c1) / (jnp.sqrt(nv / c2) + ADAM_EPS) + ADAM_WD * w_ref[...])

    blk = pl.BlockSpec((tr, C), lambda i: (i, 0))
    return pl.pallas_call(
        body, name=name, grid=(R // tr,), in_specs=[blk] * (n + 3), out_specs=[blk] * 4,
        out_shape=[SDS((R, C), F32)] * 4, compiler_params=_cp("parallel"))(*parts, w, m, v)


PACK_ROWS = 8


def _pack(parts):
    rows = []
    for a in parts:
        flat = a.reshape(-1).astype(F32)
        n = -(-flat.shape[0] // (PACK_ROWS * LANES)) * PACK_ROWS * LANES
        rows.append(jnp.pad(flat, (0, n - flat.shape[0])).reshape(-1, LANES))
    return jnp.concatenate(rows, axis=0)


def _unpack(slab, shapes):
    out, r = [], 0
    for shp in shapes:
        n = int(np.prod(shp))
        nr = -(-n // (PACK_ROWS * LANES)) * PACK_ROWS
        out.append(slab[r:r + nr].reshape(-1)[:n].reshape(shp))
        r += nr
    return out


SMALL = ["norm_mix", "gla_gate_bias_fwd", "gla_gate_bias_bwd", "gla_norm", "rel_bias", "dil_norm", "mem_norm",
         "mem_out_norm", "norm_mlp", "norm_final"]
GATE_UPS = ["gla_gate_up_fwd", "gla_gate_up_bwd"]
BIG = ["w_in", "w_mem_kv", "w_out", "w_up", "w_down"]
COL_SHARDED = ("w_in", "w_up")
WEIGHTS = ["norm_mix", "w_in", "gla_gate_up_fwd", "gla_gate_bias_fwd", "gla_gate_up_bwd", "gla_gate_bias_bwd", "gla_norm",
           "rel_bias", "dil_norm", "mem_norm", "w_mem_kv", "mem_out_norm", "w_out", "norm_mlp", "w_up", "w_down",
           "norm_final"]


def _full_weight(name, gathered, l):
    blocks = [gathered[j, l] for j in range(N_CHIPS)]
    full = jnp.concatenate(blocks, axis=1 if name in COL_SHARDED or name in GATE_UPS else 0)
    if name == "w_in":
        d = full.shape[0]
        full = jnp.concatenate([full[:, :LR_COL], full[:, LR_COL + 2 * GLA_GATE_RANK:],
                                full[:, LR_COL:LR_COL + 2 * GLA_GATE_RANK],
                                jnp.zeros((d, IN_PAD - IN_WIDTH), BF16)], axis=1)
    return full


def _shard_grad(name, g):
    if name == "w_in":
        g = jnp.concatenate([g[:, :LR_COL], g[:, C_LR:C_LR + 2 * GLA_GATE_RANK], g[:, LR_COL:C_LR]], axis=1)
    g = g.astype(BF16)
    if name in COL_SHARDED:
        w = g.shape[1] // N_CHIPS
        return jnp.stack([g[:, j * w:(j + 1) * w] for j in range(N_CHIPS)], axis=0)
    return g.reshape(N_CHIPS, g.shape[0] // N_CHIPS, g.shape[1])


def kernel(x, mem, norm_mix, w_in, gla_gate_up_fwd, gla_gate_bias_fwd, gla_gate_up_bwd, gla_gate_bias_bwd, gla_norm, rel_bias, dil_norm, mem_norm, w_mem_kv, mem_out_norm, w_out, norm_mlp, w_up, w_down, norm_final, loss_target, m_norm_mix, m_w_in, m_gla_gate_up_fwd, m_gla_gate_bias_fwd, m_gla_gate_up_bwd, m_gla_gate_bias_bwd, m_gla_norm, m_rel_bias, m_dil_norm, m_mem_norm, m_w_mem_kv, m_mem_out_norm, m_w_out, m_norm_mlp, m_w_up, m_w_down, m_norm_final, v_norm_mix, v_w_in, v_gla_gate_up_fwd, v_gla_gate_bias_fwd, v_gla_gate_up_bwd, v_gla_gate_bias_bwd, v_gla_norm, v_rel_bias, v_dil_norm, v_mem_norm, v_w_mem_kv, v_mem_out_norm, v_w_out, v_norm_mlp, v_w_up, v_w_down, v_norm_final):
    vals = dict(locals())
    W = {n: vals[n] for n in WEIGHTS}
    M1 = {n: vals["m_" + n] for n in WEIGHTS}
    V2 = {n: vals["v_" + n] for n in WEIGHTS}
    depth = w_in.shape[0]
    chip = 2 * lax.axis_index("x") + lax.axis_index("y")

    sharded = BIG + GATE_UPS
    gathered = dict(zip(sharded, chip_exchange([W[n].astype(BF16) for n in sharded], scatter=False, name="gather_weights")))
    row = lambda a: a.reshape(1, -1)
    layers = []
    for l in range(depth):
        p = {n: _full_weight(n, gathered[n], l) for n in BIG}
        p["up_f"] = _pad_gate_up(_full_weight("gla_gate_up_fwd", gathered["gla_gate_up_fwd"], l), 0)
        p["up_b"] = _pad_gate_up(_full_weight("gla_gate_up_bwd", gathered["gla_gate_up_bwd"], l), GLA_GATE_RANK)
        p["bias_f"], p["bias_b"] = row(gla_gate_bias_fwd[l]), row(gla_gate_bias_bwd[l])
        for n in ("norm_mix", "gla_norm", "dil_norm", "mem_norm", "mem_out_norm", "norm_mlp"):
            p[n] = row(W[n][l])
        layers.append(p)

    loss, grad_x, grads, g_final = local_step(x[0], mem[0], loss_target[0], layers, row(norm_final), rel_bias)

    send = [jnp.stack([_shard_grad(n, grads[l][n]) for l in range(depth)], axis=1) for n in BIG]
    recv = chip_exchange(send, scatter=True, name="scatter_grads")
    flat2 = lambda a: a.reshape(-1, a.shape[-1])
    part = [sum_chips(r.reshape(N_CHIPS, -1, r.shape[-1]), name=f"sum_chips_{n}") for n, r in zip(BIG, recv)]
    other = sibling_exchange(part, name="swap_partial_sums")
    out = {}
    for n, pa, ob in zip(BIG, part, other):
        res = adamw([pa, ob], flat2(W[n]), flat2(M1[n]), flat2(V2[n]), name=f"adamw_{n}")
        out[n] = [r.reshape(W[n].shape) for r in res]

    small_g = {}
    for n, key in (("norm_mix", "norm_mix"), ("gla_gate_bias_fwd", "bias_f"), ("gla_gate_bias_bwd", "bias_b"),
                   ("gla_norm", "gla_norm"), ("dil_norm", "dil_norm"), ("mem_norm", "mem_norm"),
                   ("mem_out_norm", "mem_out_norm"), ("norm_mlp", "norm_mlp"),
                   ("gla_gate_up_fwd", "up_f"), ("gla_gate_up_bwd", "up_b")):
        small_g[n] = jnp.stack([grads[l][key].reshape(W[n].shape[1:] if n not in GATE_UPS else (GLA_GATE_RANK, QK_W))
                                for l in range(depth)], axis=0)
    small_g["rel_bias"] = sum(grads[l]["rel_bias"] for l in range(depth))
    small_g["norm_final"] = g_final.reshape(-1)
    names = SMALL + GATE_UPS
    slab = all_reduce_small(_pack([small_g[n] for n in names] + [loss[:, 0]]), name="all_reduce_small")
    *summed, loss_sum = _unpack(slab, [small_g[n].shape for n in names] + [(1,)])
    summed = dict(zip(names, summed))
    for n in GATE_UPS:
        summed[n] = lax.dynamic_slice_in_dim(summed[n], chip * LANES, LANES, axis=2)
    names_s = SMALL + GATE_UPS
    shapes = [W[n].shape for n in names_s]
    res = adamw([_pack([summed[n] for n in names_s])], _pack([W[n] for n in names_s]), _pack([M1[n] for n in names_s]),
                _pack([V2[n] for n in names_s]), name="adamw_small")
    for n, *r in zip(names_s, *[_unpack(t, shapes) for t in res]):
        out[n] = r

    return (loss_sum.reshape(()), grad_x[None], *[out[n][0] for n in WEIGHTS], *[out[n][1] for n in WEIGHTS],
            *[out[n][2] for n in WEIGHTS], *[out[n][3] for n in WEIGHTS])
```

```python
import functools
import math

import numpy as np
import jax
import jax.numpy as jnp
from jax import lax
from jax.experimental import pallas as pl
from jax.experimental.pallas import tpu as pltpu

F32, BF16 = jnp.float32, jnp.bfloat16
SDS = jax.ShapeDtypeStruct

DEPTH = 4
GLA_HEADS, GLA_DK, GLA_DV = 4, 128, 256
GLA_GATE_RANK = 16
GLA_GATE_NORMALIZER = 16.0
GLA_CHUNK = 64
DIL_HEADS, DIL_HEAD_DIM = 4, 128
DIL_DILATIONS = (1, 4, 16)
DIL_W = 64
MEM_HEADS, MEM_HEAD_DIM = 4, 128
REL_BUCKETS, REL_MAX_DISTANCE = 32, 1024
EPS = 1e-6
NEG_INF = -1e30
QK_W, V_W, DIL_WIDTH, MEM_WIDTH = 512, 1024, 512, 512
IN_WIDTH = 5152
LR_COL = 3072
IN_PAD = 5376
C_GQ, C_GK, C_GV, C_GR, C_DQ, C_MQ, C_LR = 0, 512, 1024, 2048, 3072, 4608, 5120
IN_TILE = 768

ADAM_LR, ADAM_B1, ADAM_B2, ADAM_EPS, ADAM_WD, ADAM_STEP = 0.001, 0.9, 0.999, 1e-08, 0.01, 10

V7X_VMEM_BYTES = 64 * 1024 * 1024
VMEM_LIMIT = V7X_VMEM_BYTES * 7 // 8
LANES = 128

TM_FWD = 512
TN = 1024
TK = 2048
TM_NORM_BWD = 512
TK_NORM_BWD = 1024
NORM_BWD_ROWS = 64
TM_W, TN_W, TK_W = 1024, 1024, 1024
TM_LOSS = 256
ELEMWISE_BLOCK_BYTES = 24 * 1024 * 1024

NT_DIMS = (((1,), (1,)), ((), ()))
TN_DIMS = (((0,), (0,)), ((), ()))


def _nt(a, b):
    return lax.dot_general(a, b, NT_DIMS, preferred_element_type=F32)


def _tn(a, b):
    return lax.dot_general(a, b, TN_DIMS, preferred_element_type=F32)


def _nn(a, b):
    return jnp.dot(a, b, preferred_element_type=F32)


def _cp(*sem):
    return pltpu.CompilerParams(dimension_semantics=sem, vmem_limit_bytes=VMEM_LIMIT)


def _tile(n, t):
    t = min(n, t)
    assert n % t == 0, (n, t)
    return t


MESH = pl.DeviceIdType.MESH
HBM_SPEC = pl.BlockSpec(memory_space=pltpu.HBM)
N_CHIPS = 4
N_DEV = 8


def _mesh_pos():
    return lax.axis_index("x"), lax.axis_index("y"), lax.axis_index("c")


class ChipExchange:
    def __init__(self, bufs, *, scatter, layers=None):
        self.bufs = list(bufs)
        self.n = len(self.bufs)
        self.scatter = scatter
        self.layers = list(layers) if layers is not None else [None] * self.n
        shp = [b.shape if l is None else b.shape[1:] for b, l in zip(self.bufs, self.layers)]
        self.out_shape = [SDS(s if scatter else (N_CHIPS,) + s, b.dtype) for s, b in zip(shp, self.bufs)]
        self.scratch = [pltpu.SemaphoreType.DMA((3 * self.n,)), pltpu.SemaphoreType.DMA((3 * self.n,)),
                        pltpu.SemaphoreType.DMA((self.n,))]
        self.result = None

    def _plan(self, ins, outs, sems):
        send, recv, local = sems
        x, y, c = _mesh_pos()
        me = 2 * x + y
        chips = [(1 - x, y), (x, 1 - y), (1 - x, 1 - y)]
        kept, sent, landing = [], [], []
        for a in range(self.n):
            src = ins[a] if self.layers[a] is None else ins[a].at[self.layers[a]]
            pick = (lambda j, src=src: src.at[j]) if self.scatter else (lambda j, src=src: src)
            kept.append(pltpu.make_async_copy(pick(me), outs[a].at[me], local.at[a]))
            for j, (px, py) in enumerate(chips):
                sems_k = dict(send_sem=send.at[3 * a + j], recv_sem=recv.at[3 * a + j], device_id=(px, py, c),
                              device_id_type=MESH)
                sent.append(pltpu.make_async_remote_copy(src_ref=pick(2 * px + py), dst_ref=outs[a].at[me], **sems_k))
                landing.append(pltpu.make_async_remote_copy(src_ref=pick(me), dst_ref=outs[a].at[2 * px + py], **sems_k))
        return kept, sent, landing

    def start(self, ins, outs, sems):
        kept, sent, _ = self._plan(ins, outs, sems)
        for cp in kept + sent:
            cp.start()

    def finish(self, ins, outs, sems):
        kept, sent, landing = self._plan(ins, outs, sems)
        for cp in landing:
            cp.wait_recv()
        for cp in sent:
            cp.wait_send()
        for cp in kept:
            cp.wait()

    def run(self, name):
        n = self.n

        def body(*refs):
            self.start(refs[:n], refs[n:2 * n], refs[2 * n:])
            self.finish(refs[:n], refs[n:2 * n], refs[2 * n:])

        self.result = pl.pallas_call(body, name=name, in_specs=[HBM_SPEC] * n, out_specs=[HBM_SPEC] * n,
                                     out_shape=self.out_shape, scratch_shapes=self.scratch)(*self.bufs)
        return self.result


def _call(body, *, name, grid, in_specs, out_specs, out_shape, sem, args, scratch_shapes=(), comm=None):
    if comm is None:
        return pl.pallas_call(body, name=name, grid=grid, in_specs=list(in_specs), out_specs=list(out_specs),
                              out_shape=list(out_shape), scratch_shapes=list(scratch_shapes),
                              compiler_params=_cp(*sem))(*args)
    n_in, n_out, n_scr, nc = len(in_specs), len(out_shape), len(scratch_shapes), comm.n

    def wrapped(*refs):
        ins, cin = refs[:n_in], refs[n_in:n_in + nc]
        outs, cout = refs[n_in + nc:n_in + nc + n_out], refs[n_in + nc + n_out:n_in + 2 * nc + n_out]
        scr, csem = refs[n_in + 2 * nc + n_out:n_in + 2 * nc + n_out + n_scr], refs[n_in + 2 * nc + n_out + n_scr:]
        ids = [pl.program_id(d) for d in range(len(grid))]
        first = functools.reduce(jnp.logical_and, [i == 0 for i in ids])
        last = functools.reduce(jnp.logical_and, [i == g - 1 for i, g in zip(ids, grid)])

        @pl.when(first)
        def _():
            comm.start(cin, cout, csem)
        body(*ins, *outs, *scr)

        @pl.when(last)
        def _():
            comm.finish(cin, cout, csem)

    res = pl.pallas_call(
        wrapped, name=name, grid=grid, in_specs=list(in_specs) + [HBM_SPEC] * nc,
        out_specs=list(out_specs) + [HBM_SPEC] * nc, out_shape=list(out_shape) + comm.out_shape,
        scratch_shapes=list(scratch_shapes) + comm.scratch,
        compiler_params=_cp(*("arbitrary",) * len(grid)))(*args, *comm.bufs)
    comm.result = res[n_out:]
    return res[:n_out]


def _w_spec(w, rows, tn, col_of):
    if w.ndim == 2:
        tn = _tile(w.shape[1], tn)
        return w.shape[1], tn, pl.BlockSpec((rows, tn), lambda *g: (0, col_of(*g)))
    tn = _tile(w.shape[2], tn)
    per = w.shape[2] // tn
    return N_CHIPS * w.shape[2], tn, pl.BlockSpec((None, rows, tn), lambda *g: (col_of(*g) // per, 0, col_of(*g) % per))


def norm_mm(x, gain, w, *, tm, tn, name, comm=None):
    M, D = x.shape
    tm = _tile(M, tm)
    N, tn, w_spec = _w_spec(w, D, tn, lambda i, j: j)

    def body(x_ref, g_ref, w_ref, o_ref, hn_ref):
        @pl.when(pl.program_id(1) == 0)
        def _():
            xf = x_ref[...]
            r = lax.rsqrt(jnp.mean(xf * xf, axis=-1, keepdims=True) + EPS)
            hn_ref[...] = (xf * r * g_ref[...]).astype(BF16)
        o_ref[...] = _nn(hn_ref[...], w_ref[...]).astype(BF16)

    return _call(
        body, name=name, grid=(M // tm, N // tn),
        in_specs=[pl.BlockSpec((tm, D), lambda i, j: (i, 0)), pl.BlockSpec((1, D), lambda i, j: (0, 0)), w_spec],
        out_specs=[pl.BlockSpec((tm, tn), lambda i, j: (i, j)), pl.BlockSpec((tm, D), lambda i, j: (i, 0))],
        out_shape=[SDS((M, N), BF16), SDS((M, D), BF16)], sem=("parallel", "arbitrary"), args=(x, gain, w), comm=comm)


def mm_res(a, w, res, *, relu2, tm, tn, tk, name, comm=None):
    M, K = a.shape
    N = w.shape[1]
    tm, tn, tk = _tile(M, tm), _tile(N, tn), _tile(K, tk)

    def body(a_ref, w_ref, r_ref, o_ref):
        @pl.when(pl.program_id(2) == 0)
        def _():
            o_ref[...] = r_ref[...]
        av = a_ref[...]
        if relu2:
            af = jnp.maximum(av.astype(F32), 0.0)
            av = (af * af).astype(BF16)
        o_ref[...] += _nn(av, w_ref[...])

    return _call(
        body, name=name, grid=(M // tm, N // tn, K // tk),
        in_specs=[pl.BlockSpec((tm, tk), lambda i, j, k: (i, k)), pl.BlockSpec((tk, tn), lambda i, j, k: (k, j)),
                  pl.BlockSpec((tm, tn), lambda i, j, k: (i, j))],
        out_specs=[pl.BlockSpec((tm, tn), lambda i, j, k: (i, j))], out_shape=[SDS((M, N), F32)],
        sem=("parallel", "parallel", "arbitrary"), args=(a, w, res), comm=comm)[0]


def mm_nt(dy, w, a, *, tm, tn, name, comm=None):
    M, K = dy.shape
    N = w.shape[0]
    tm, tn = _tile(M, tm), _tile(N, tn)
    with_a = a is not None

    def body(*refs):
        if with_a:
            dy_ref, w_ref, a_ref, o_ref, dyb = refs
        else:
            dy_ref, w_ref, o_ref, dyb = refs

        @pl.when(pl.program_id(1) == 0)
        def _():
            dyb[...] = dy_ref[...].astype(BF16)
        r = _nt(dyb[...], w_ref[...])
        if with_a:
            r = r * (2.0 * jnp.maximum(a_ref[...].astype(F32), 0.0))
        o_ref[...] = r.astype(BF16)

    in_specs = [pl.BlockSpec((tm, K), lambda i, j: (i, 0)), pl.BlockSpec((tn, K), lambda i, j: (j, 0))]
    args = [dy, w]
    if with_a:
        in_specs.append(pl.BlockSpec((tm, tn), lambda i, j: (i, j)))
        args.append(a)
    return _call(
        body, name=name, grid=(M // tm, N // tn), in_specs=in_specs,
        out_specs=[pl.BlockSpec((tm, tn), lambda i, j: (i, j))], out_shape=[SDS((M, N), BF16)],
        scratch_shapes=[pltpu.VMEM((tm, K), BF16)], sem=("parallel", "arbitrary"), args=args, comm=comm)[0]


def mm_nt_normbwd(dy, w, x, gain, dx_in, *, tm, tk, name, comm=None):
    M, K = dy.shape
    D = x.shape[1]
    tm = _tile(M, tm)
    _, tk, w_spec = _w_spec(w, D, tk, lambda i, k: k)
    nk = K // tk
    rc = _tile(tm, NORM_BWD_ROWS)

    def body(dy_ref, w_ref, x_ref, g_ref, dxin_ref, dx_ref, gg_ref, acc):
        i, k = pl.program_id(0), pl.program_id(1)

        @pl.when(k == 0)
        def _():
            acc[...] = jnp.zeros_like(acc)
        acc[...] += _nt(dy_ref[...], w_ref[...])

        @pl.when(k == nk - 1)
        def _():
            def rows(c, part):
                rs = pl.ds(pl.multiple_of(c * rc, rc), rc)
                dh, xf = acc[rs, :], x_ref[rs, :]
                r = lax.rsqrt(jnp.mean(xf * xf, axis=-1, keepdims=True) + EPS)
                xhat = xf * r
                u = dh * g_ref[...]
                dx_ref[rs, :] = dxin_ref[rs, :] + r * (u - xhat * jnp.mean(u * xhat, axis=-1, keepdims=True))
                return part + jnp.sum(dh * xhat, axis=0, keepdims=True)

            part = lax.fori_loop(0, tm // rc, rows, jnp.zeros((1, D), F32))

            @pl.when(i == 0)
            def _():
                gg_ref[...] = part

            @pl.when(i > 0)
            def _():
                gg_ref[...] += part

    return _call(
        body, name=name, grid=(M // tm, nk),
        in_specs=[pl.BlockSpec((tm, tk), lambda i, k: (i, k)), w_spec,
                  pl.BlockSpec((tm, D), lambda i, k: (i, 0)), pl.BlockSpec((1, D), lambda i, k: (0, 0)),
                  pl.BlockSpec((tm, D), lambda i, k: (i, 0))],
        out_specs=[pl.BlockSpec((tm, D), lambda i, k: (i, 0)), pl.BlockSpec((1, D), lambda i, k: (0, 0))],
        out_shape=[SDS((M, D), F32), SDS((1, D), F32)], scratch_shapes=[pltpu.VMEM((tm, D), F32)],
        sem=("arbitrary", "arbitrary"), args=(dy, w, x, gain, dx_in), comm=comm)


def mm_tn(a, dy, *, relu2, tm, tn, tk, name, out="f32", comm=None):
    S_, Ka = a.shape
    N = dy.shape[1]
    tm, tk = _tile(Ka, tm), _tile(S_, tk)
    tn = _tile(N // N_CHIPS if out == "cols" else N, tn)
    nk = S_ // tk

    def body(a_ref, dy_ref, o_ref, acc):
        k = pl.program_id(2)

        @pl.when(k == 0)
        def _():
            acc[...] = jnp.zeros_like(acc)
        av = a_ref[...]
        if relu2:
            af = jnp.maximum(av.astype(F32), 0.0)
            av = (af * af).astype(BF16)
        acc[...] += _tn(av, dy_ref[...].astype(BF16))

        @pl.when(k == nk - 1)
        def _():
            o_ref[...] = acc[...].astype(o_ref.dtype)

    if out == "cols":
        per = N // N_CHIPS // tn
        o_spec = pl.BlockSpec((None, tm, tn), lambda i, j, k: (j // per, i, j % per))
        o_shape = SDS((N_CHIPS, Ka, N // N_CHIPS), BF16)
    else:
        o_spec = pl.BlockSpec((tm, tn), lambda i, j, k: (i, j))
        o_shape = SDS((Ka, N), F32 if out == "f32" else BF16)
    return _call(
        body, name=name, grid=(Ka // tm, N // tn, nk),
        in_specs=[pl.BlockSpec((tk, tm), lambda i, j, k: (k, i)), pl.BlockSpec((tk, tn), lambda i, j, k: (k, j))],
        out_specs=[o_spec], out_shape=[o_shape], scratch_shapes=[pltpu.VMEM((tm, tn), F32)],
        sem=("parallel", "parallel", "arbitrary"), args=(a, dy), comm=comm)[0]


def loss_head(x, gain, target, *, tm, name):
    M, D = x.shape
    tm = _tile(M, tm)

    def body(x_ref, g_ref, t_ref, dx_ref, gg_ref, l_ref):
        i = pl.program_id(0)
        xf = x_ref[...]
        r = lax.rsqrt(jnp.mean(xf * xf, axis=-1, keepdims=True) + EPS)
        xhat = xf * r
        e = xhat * g_ref[...] - t_ref[...]
        lpart = 0.5 * jnp.sum(jnp.mean(e * e, axis=-1, keepdims=True), axis=0, keepdims=True)
        dy = e * (1.0 / D)
        u = dy * g_ref[...]
        dx_ref[...] = r * (u - xhat * jnp.mean(u * xhat, axis=-1, keepdims=True))
        gpart = jnp.sum(dy * xhat, axis=0, keepdims=True)
        lrow = jnp.broadcast_to(lpart, (1, LANES))

        @pl.when(i == 0)
        def _():
            gg_ref[...] = gpart
            l_ref[...] = lrow

        @pl.when(i > 0)
        def _():
            gg_ref[...] += gpart
            l_ref[...] += lrow

    return pl.pallas_call(
        body, name=name, grid=(M // tm,),
        in_specs=[pl.BlockSpec((tm, D), lambda i: (i, 0)), pl.BlockSpec((1, D), lambda i: (0, 0)),
                  pl.BlockSpec((tm, D), lambda i: (i, 0))],
        out_specs=[pl.BlockSpec((tm, D), lambda i: (i, 0)), pl.BlockSpec((1, D), lambda i: (0, 0)),
                   pl.BlockSpec((1, LANES), lambda i: (0, 0))],
        out_shape=[SDS((M, D), F32), SDS((1, D), F32), SDS((1, LANES), F32)],
        compiler_params=_cp("arbitrary"))(x, gain, target)


def _log_sigmoid(z):
    return jnp.minimum(z, 0.0) - jnp.log(1.0 + jnp.exp(-jnp.abs(z)))


def _chunk_scan(x, row, reverse):
    n = x.shape[0]
    s = 1
    while s < GLA_CHUNK:
        if reverse:
            x = x + jnp.where(row < GLA_CHUNK - s, pltpu.roll(x, n - s, 0), 0.0)
        else:
            x = x + jnp.where(row >= s, pltpu.roll(x, s, 0), 0.0)
        s *= 2
    return x


def _chunk_edge(b, reverse):
    t = b.shape[0]
    nc = t // GLA_CHUNK
    b3 = b.reshape(nc, GLA_CHUNK, b.shape[1])
    e = b3[:, 0:1, :] if reverse else b3[:, GLA_CHUNK - 1:GLA_CHUNK, :]
    return jnp.broadcast_to(e, b3.shape).reshape(b.shape), e.reshape(nc, b.shape[1])


def _gla_gates(lr, up_ref, bias_ref, cs, row, reverse):
    z = _nn(lr, up_ref[:, cs]) + bias_ref[:, cs]
    g = _log_sigmoid(z) * (1.0 / GLA_GATE_NORMALIZER)
    b = _chunk_scan(g, row, reverse)
    bl, blc = _chunk_edge(b, reverse)
    return z, b, bl, blc


def _gla_mask(reverse):
    r = lax.broadcasted_iota(jnp.int32, (GLA_CHUNK, GLA_CHUNK), 0)
    c = lax.broadcasted_iota(jnp.int32, (GLA_CHUNK, GLA_CHUNK), 1)
    return (r <= c) if reverse else (r >= c)


def gla_fwd(proj, upad, bias, *, reverse, name):
    S_ = proj.shape[0]
    T = _tile(S_, 512)
    nt, nc = S_ // T, T // GLA_CHUNK
    H, DK, DV, C = GLA_HEADS, GLA_DK, GLA_DV, GLA_CHUNK
    tix = (lambda i: nt - 1 - i) if reverse else (lambda i: i)
    qscale = DK ** -0.5

    def body(q_ref, k_ref, v_ref, lr_ref, up_ref, b_ref, o_ref, st_ref, st_s, qd_s, ki_s, ke_s, dec_s):
        @pl.when(pl.program_id(0) == 0)
        def _():
            st_s[...] = jnp.zeros_like(st_s)
        row = lax.broadcasted_iota(jnp.int32, (T, DK), 0) % C
        mask = _gla_mask(reverse)
        lr = lr_ref[...]
        for h in range(H):
            cs = slice(h * DK, (h + 1) * DK)
            vs = slice(h * DV, (h + 1) * DV)
            _, b, bl, _ = _gla_gates(lr, up_ref, b_ref, cs, row, reverse)
            q = q_ref[:, cs].astype(F32) * qscale
            k = k_ref[:, cs].astype(F32)
            qd_s[...] = (q * jnp.exp(b)).astype(BF16)
            ki_s[...] = (k * jnp.exp(-b)).astype(BF16)
            ke_s[...] = (k * jnp.exp(bl - b)).astype(BF16)
            dec_s[...] = jnp.exp(bl)

            def chunk(ci, carry):
                c = (nc - 1 - ci) if reverse else ci
                r0 = pl.multiple_of(c * C, C)
                s0 = pl.multiple_of(c * DV, DV)
                qd, ki, ke = qd_s[pl.ds(r0, C), :], ki_s[pl.ds(r0, C), :], ke_s[pl.ds(r0, C), :]
                v = v_ref[pl.ds(r0, C), vs]
                st = st_s[h]
                stb = st.astype(BF16)
                st_ref[h, pl.ds(s0, DV), :] = stb
                a = jnp.where(mask, _nt(qd, ki), 0.0).astype(BF16)
                o_ref[pl.ds(r0, C), vs] = _nn(a, v) + _nt(qd, stb)
                st_s[h] = dec_s[pl.ds(r0, 1), :] * st + _tn(v, ke)
                return carry

            lax.fori_loop(0, nc, chunk, 0)

    return pl.pallas_call(
        body, name=name, grid=(nt,),
        in_specs=[pl.BlockSpec((T, QK_W), lambda i: (tix(i), C_GQ // QK_W)),
                  pl.BlockSpec((T, QK_W), lambda i: (tix(i), C_GK // QK_W)),
                  pl.BlockSpec((T, V_W), lambda i: (tix(i), C_GV // V_W)),
                  pl.BlockSpec((T, LANES), lambda i: (tix(i), C_LR // LANES)),
                  pl.BlockSpec((LANES, QK_W), lambda i: (0, 0)), pl.BlockSpec((1, QK_W), lambda i: (0, 0))],
        out_specs=[pl.BlockSpec((T, V_W), lambda i: (tix(i), 0)),
                   pl.BlockSpec((H, nc * DV, DK), lambda i: (0, tix(i), 0))],
        out_shape=[SDS((S_, V_W), F32), SDS((H, S_ // C * DV, DK), BF16)],
        scratch_shapes=[pltpu.VMEM((H, DV, DK), F32), pltpu.VMEM((T, DK), BF16), pltpu.VMEM((T, DK), BF16),
                        pltpu.VMEM((T, DK), BF16), pltpu.VMEM((T, DK), F32)],
        compiler_params=_cp("arbitrary"))(proj, proj, proj, proj, upad, bias)


def gla_bwd(proj, upad, bias, states, do, prev, *, reverse, name):
    S_ = proj.shape[0]
    T = _tile(S_, 512)
    nt, nc = S_ // T, T // GLA_CHUNK
    H, DK, DV, C = GLA_HEADS, GLA_DK, GLA_DV, GLA_CHUNK
    tix = (lambda i: i) if reverse else (lambda i: nt - 1 - i)
    qscale = DK ** -0.5
    with_prev = prev is not None
    NCP = max(8, nc)

    def body(*refs):
        (q_ref, k_ref, v_ref, lr_ref, up_ref, b_ref, st_ref, do_ref) = refs[:8]
        n_in = 12 if with_prev else 8
        pq_ref, pk_ref, pv_ref, plr_ref = refs[8:12] if with_prev else (None,) * 4
        dq_ref, dk_ref, dv_ref, dlr_ref, dup_ref, dbias_ref = refs[n_in:n_in + 6]
        dst_s, qd_s, ki_s, ke_s, dec_s, dqd_s, dki_s, dke_s, ddec_s = refs[n_in + 6:]
        first = pl.program_id(0) == 0

        @pl.when(first)
        def _():
            dst_s[...] = jnp.zeros_like(dst_s)
            dup_ref[...] = jnp.zeros_like(dup_ref)
            dbias_ref[...] = jnp.zeros_like(dbias_ref)

        row = lax.broadcasted_iota(jnp.int32, (T, DK), 0) % C
        edge_row = 0 if reverse else C - 1
        mask = _gla_mask(reverse)
        lr = lr_ref[...]
        dlr = plr_ref[...].astype(F32) if with_prev else jnp.zeros((T, LANES), F32)
        for h in range(H):
            cs = slice(h * DK, (h + 1) * DK)
            vs = slice(h * DV, (h + 1) * DV)
            z, b, bl, blc = _gla_gates(lr, up_ref, b_ref, cs, row, reverse)
            q = q_ref[:, cs].astype(F32) * qscale
            k = k_ref[:, cs].astype(F32)
            eb, einv, eend = jnp.exp(b), jnp.exp(-b), jnp.exp(bl - b)
            qd, ki, ke = q * eb, k * einv, k * eend
            qd_s[...] = qd.astype(BF16)
            ki_s[...] = ki.astype(BF16)
            ke_s[...] = ke.astype(BF16)
            dec_s[...] = jnp.exp(bl)
            ddec_s[...] = jnp.zeros_like(ddec_s)

            def chunk(ci, carry):
                c = ci if reverse else (nc - 1 - ci)
                r0 = pl.multiple_of(c * C, C)
                s0 = pl.multiple_of(c * DV, DV)
                qdc, kic, kec = qd_s[pl.ds(r0, C), :], ki_s[pl.ds(r0, C), :], ke_s[pl.ds(r0, C), :]
                v = v_ref[pl.ds(r0, C), vs]
                doc = do_ref[pl.ds(r0, C), vs]
                st = st_ref[h, pl.ds(s0, DV), :]
                dst = dst_s[h]
                dstb = dst.astype(BF16)
                a = jnp.where(mask, _nt(qdc, kic), 0.0).astype(BF16)
                da = jnp.where(mask, _nt(doc, v), 0.0).astype(BF16)
                dvc = _tn(a, doc) + _nt(kec, dstb)
                if with_prev:
                    dvc = dvc + pv_ref[pl.ds(r0, C), vs].astype(F32)
                dv_ref[pl.ds(r0, C), vs] = dvc.astype(BF16)
                dqd_s[pl.ds(r0, C), :] = _nn(da, kic) + _nn(doc, st)
                dki_s[pl.ds(r0, C), :] = _tn(da, qdc)
                dke_s[pl.ds(r0, C), :] = _nn(v, dstb)
                ddec_s[pl.ds(c, 1), :] = jnp.sum(dst * st.astype(F32), axis=0, keepdims=True)
                dst_s[h] = dec_s[pl.ds(r0, 1), :] * dst + _tn(doc, qdc)
                return carry

            lax.fori_loop(0, nc, chunk, 0)

            dqd, dki, dke = dqd_s[...], dki_s[...], dke_s[...]
            dq = dqd * eb * qscale
            dk = dki * einv + dke * eend
            if with_prev:
                dq = dq + pq_ref[:, cs].astype(F32)
                dk = dk + pk_ref[:, cs].astype(F32)
            dq_ref[:, cs] = dq.astype(BF16)
            dk_ref[:, cs] = dk.astype(BF16)
            wke = dke * ke
            db = dqd * qd - dki * ki - wke
            dbl = jnp.sum(wke.reshape(nc, C, DK), axis=1) + ddec_s[0:nc, :] * jnp.exp(blc)
            dbl_b = jnp.broadcast_to(dbl.reshape(nc, 1, DK), (nc, C, DK)).reshape(T, DK)
            db = db + jnp.where(row == edge_row, dbl_b, 0.0)
            dg = _chunk_scan(db, row, not reverse)
            dz = dg * (1.0 / GLA_GATE_NORMALIZER) * (1.0 / (1.0 + jnp.exp(z)))
            dzb = dz.astype(BF16)
            dbias_ref[:, cs] += jnp.sum(dz, axis=0, keepdims=True)
            dup_ref[:, cs] += _tn(lr, dzb)
            dlr = dlr + _nt(dzb, up_ref[:, cs])
        dlr_ref[...] = dlr.astype(BF16)

    tile = lambda w, cb: pl.BlockSpec((T, w), lambda i: (tix(i), cb))
    in_specs = [tile(QK_W, C_GQ // QK_W), tile(QK_W, C_GK // QK_W), tile(V_W, C_GV // V_W), tile(LANES, C_LR // LANES),
                pl.BlockSpec((LANES, QK_W), lambda i: (0, 0)), pl.BlockSpec((1, QK_W), lambda i: (0, 0)),
                pl.BlockSpec((H, nc * DV, DK), lambda i: (0, tix(i), 0)), tile(V_W, 0)]
    args = [proj, proj, proj, proj, upad, bias, states, do]
    if with_prev:
        in_specs += [tile(QK_W, 0), tile(QK_W, 0), tile(V_W, 0), tile(LANES, 0)]
        args += list(prev)
    return pl.pallas_call(
        body, name=name, grid=(nt,), in_specs=in_specs,
        out_specs=[tile(QK_W, 0), tile(QK_W, 0), tile(V_W, 0), tile(LANES, 0),
                   pl.BlockSpec((LANES, QK_W), lambda i: (0, 0)), pl.BlockSpec((1, QK_W), lambda i: (0, 0))],
        out_shape=[SDS((S_, QK_W), BF16), SDS((S_, QK_W), BF16), SDS((S_, V_W), BF16), SDS((S_, LANES), BF16),
                   SDS((LANES, QK_W), F32), SDS((1, QK_W), F32)],
        scratch_shapes=[pltpu.VMEM((H, DV, DK), F32), pltpu.VMEM((T, DK), BF16), pltpu.VMEM((T, DK), BF16),
                        pltpu.VMEM((T, DK), BF16), pltpu.VMEM((T, DK), F32), pltpu.VMEM((T, DK), F32),
                        pltpu.VMEM((T, DK), F32), pltpu.VMEM((T, DK), F32), pltpu.VMEM((NCP, DK), F32)],
        compiler_params=_cp("arbitrary"))(*args)


def _head_norm_fwd(o, width):
    out = []
    for h in range(o.shape[1] // width):
        oh = o[:, h * width:(h + 1) * width]
        rs = lax.rsqrt(jnp.mean(oh * oh, axis=-1, keepdims=True) + EPS)
        out.append((oh * rs, rs))
    return out


def gla_post(o_f, o_b, proj, gain, *, name):
    S_ = proj.shape[0]
    T = _tile(S_, 512)

    def body(of_ref, ob_ref, r_ref, g_ref, y_ref):
        o = of_ref[...] + ob_ref[...]
        for h, (ohat, _) in enumerate(_head_norm_fwd(o, GLA_DV)):
            vs = slice(h * GLA_DV, (h + 1) * GLA_DV)
            r = r_ref[:, vs].astype(F32)
            y_ref[:, vs] = (ohat * g_ref[:, vs] * (r / (1.0 + jnp.exp(-r)))).astype(BF16)

    blk = pl.BlockSpec((T, V_W), lambda i: (i, 0))
    return pl.pallas_call(
        body, name=name, grid=(S_ // T,),
        in_specs=[blk, blk, pl.BlockSpec((T, V_W), lambda i: (i, C_GR // V_W)), pl.BlockSpec((1, V_W), lambda i: (0, 0))],
        out_specs=blk, out_shape=SDS((S_, V_W), BF16), compiler_params=_cp("parallel"))(o_f, o_b, proj, gain)


def gla_post_bwd(dmix, o_f, o_b, proj, gain, *, name):
    S_ = proj.shape[0]
    T = _tile(S_, 512)

    def body(dy_ref, of_ref, ob_ref, r_ref, g_ref, do_ref, dr_ref, gg_ref):
        @pl.when(pl.program_id(0) == 0)
        def _():
            gg_ref[...] = jnp.zeros_like(gg_ref)
        o = of_ref[...] + ob_ref[...]
        for h, (ohat, rs) in enumerate(_head_norm_fwd(o, GLA_DV)):
            vs = slice(h * GLA_DV, (h + 1) * GLA_DV)
            r = r_ref[:, vs].astype(F32)
            dy = dy_ref[:, vs].astype(F32)
            sg = 1.0 / (1.0 + jnp.exp(-r))
            gate = r * sg
            n = ohat * g_ref[:, vs]
            dr_ref[:, vs] = (dy * n * (sg * (1.0 + r * (1.0 - sg)))).astype(BF16)
            dn = dy * gate
            u = dn * g_ref[:, vs]
            do_ref[:, vs] = (rs * (u - ohat * jnp.mean(u * ohat, axis=-1, keepdims=True))).astype(BF16)
            gg_ref[:, vs] += jnp.sum(dn * ohat, axis=0, keepdims=True)

    blk = pl.BlockSpec((T, V_W), lambda i: (i, 0))
    vec = pl.BlockSpec((1, V_W), lambda i: (0, 0))
    return pl.pallas_call(
        body, name=name, grid=(S_ // T,),
        in_specs=[blk, blk, blk, pl.BlockSpec((T, V_W), lambda i: (i, C_GR // V_W)), vec],
        out_specs=[blk, blk, vec], out_shape=[SDS((S_, V_W), BF16), SDS((S_, V_W), BF16), SDS((1, V_W), F32)],
        compiler_params=_cp("arbitrary"))(dmix, o_f, o_b, proj, gain)


def _t5_bucket_np(rel):
    half = REL_BUCKETS // 2
    max_exact = half // 2
    ret = np.where(rel > 0, half, 0)
    n = np.abs(rel)
    nf = np.maximum(n, 1).astype(np.float32)
    large = max_exact + (np.log(nf / np.float32(max_exact)) / np.float32(math.log(REL_MAX_DISTANCE / max_exact))
                         * np.float32(half - max_exact)).astype(np.int32)
    large = np.minimum(large, half - 1)
    return ret + np.where(n < max_exact, n, large)


def _rel_onehot(dilation):
    rel = np.arange(-DIL_W, DIL_W + 1)
    return np.eye(REL_BUCKETS, dtype=np.float32)[_t5_bucket_np(rel * dilation)]


def _toeplitz(vec, tq, flip):
    H = vec.shape[0]
    wlen = tq + 2 * DIL_W
    lu = wlen + tq
    if flip:
        vec = vec[:, ::-1]
    u = jnp.full((H, lu), NEG_INF, F32)
    u = lax.dynamic_update_slice(u, vec, (0, 0))
    t = jnp.tile(u, (1, tq))[:, :tq * (lu - 1)].reshape(H, tq, lu - 1)
    return t[:, :, :wlen]


def _untoeplitz(ds, tq):
    H = ds.shape[0]
    wlen = tq + 2 * DIL_W
    lu = wlen + tq
    p = jnp.pad(ds, ((0, 0), (0, 0), (0, lu - 1 - wlen))).reshape(H, tq * (lu - 1))
    p = jnp.pad(p, ((0, 0), (0, tq))).reshape(H, tq, lu)
    return jnp.sum(p, axis=1)[:, :2 * DIL_W + 1]


def _dil_window(p_ref, o_ref, n_ref, tq):
    return jnp.concatenate([p_ref[tq - DIL_W:, :], o_ref[...], n_ref[:DIL_W, :]], axis=0)


def _dil_specs(tq, nt, col0, ncols_per_r):
    cb = lambda h, r: r * ncols_per_r + col0 + h
    own = pl.BlockSpec((tq, LANES), lambda h, r, i: (i, cb(h, r)))
    prev = pl.BlockSpec((tq, LANES), lambda h, r, i: (jnp.maximum(i - 1, 0), cb(h, r)))
    nxt = pl.BlockSpec((tq, LANES), lambda h, r, i: (jnp.minimum(i + 1, nt - 1), cb(h, r)))
    return prev, own, nxt


def dil_fwd(qkv, bias, *, d, name):
    l = qkv.shape[0]
    tq = _tile(l, 512)
    nt = l // tq
    wlen = tq + 2 * DIL_W
    H = DIL_HEADS
    scale = DIL_HEAD_DIM ** -0.5

    def body(q_ref, kp, ko, kn, vp, vo, vn, b_ref, o_ref, l_ref):
        i = pl.program_id(2)
        kw = _dil_window(kp, ko, kn, tq)
        vw = _dil_window(vp, vo, vn, tq)
        s = _nt(q_ref[...], kw) * scale + b_ref[0]
        kpos = i * tq - DIL_W + lax.broadcasted_iota(jnp.int32, (tq, wlen), 1)
        s = jnp.where((kpos >= 0) & (kpos < l), s, NEG_INF)
        m = jnp.max(s, axis=-1, keepdims=True)
        p = jnp.exp(s - m)
        den = jnp.sum(p, axis=-1, keepdims=True)
        o_ref[...] = _nn(p.astype(BF16), vw) / den
        l_ref[...] = jnp.broadcast_to(m + jnp.log(den), (tq, LANES))

    _, q_spec, _ = _dil_specs(tq, nt, 0, 12)
    out_spec = pl.BlockSpec((tq, LANES), lambda h, r, i: (i, r * H + h))
    return pl.pallas_call(
        body, name=name, grid=(H, d, nt),
        in_specs=[q_spec, *_dil_specs(tq, nt, 4, 12), *_dil_specs(tq, nt, 8, 12),
                  pl.BlockSpec((1, tq, wlen), lambda h, r, i: (h, 0, 0))],
        out_specs=[out_spec, out_spec], out_shape=[SDS((l, d * DIL_WIDTH), F32), SDS((l, d * DIL_WIDTH), F32)],
        compiler_params=_cp("parallel", "parallel", "arbitrary"))(qkv, qkv, qkv, qkv, qkv, qkv, qkv, bias)


def dil_bwd_q(qkv, bias, do, lse, corr, *, d, name):
    l = qkv.shape[0]
    tq = _tile(l, 512)
    nt = l // tq
    wlen = tq + 2 * DIL_W
    H = DIL_HEADS
    scale = DIL_HEAD_DIM ** -0.5

    def body(q_ref, kp, ko, kn, vp, vo, vn, b_ref, do_ref, l_ref, c_ref, dq_ref, ds_ref):
        r, i = pl.program_id(1), pl.program_id(2)
        kw = _dil_window(kp, ko, kn, tq)
        vw = _dil_window(vp, vo, vn, tq)
        s = _nt(q_ref[...], kw) * scale + b_ref[0]
        kpos = i * tq - DIL_W + lax.broadcasted_iota(jnp.int32, (tq, wlen), 1)
        p = jnp.where((kpos >= 0) & (kpos < l), jnp.exp(s - l_ref[:, 0:1]), 0.0)
        ds = p * (_nt(do_ref[...], vw) - c_ref[:, 0:1])
        dq_ref[...] = _nn(ds.astype(BF16), kw) * scale

        @pl.when((r == 0) & (i == 0))
        def _():
            ds_ref[0] = ds

        @pl.when((r > 0) | (i > 0))
        def _():
            ds_ref[0] += ds

    _, q_spec, _ = _dil_specs(tq, nt, 0, 12)
    t_spec = pl.BlockSpec((tq, LANES), lambda h, r, i: (i, r * H + h))
    b_spec = pl.BlockSpec((1, tq, wlen), lambda h, r, i: (h, 0, 0))
    return pl.pallas_call(
        body, name=name, grid=(H, d, nt),
        in_specs=[q_spec, *_dil_specs(tq, nt, 4, 12), *_dil_specs(tq, nt, 8, 12), b_spec, t_spec, t_spec, t_spec],
        out_specs=[t_spec, b_spec], out_shape=[SDS((l, d * DIL_WIDTH), F32), SDS((H, tq, wlen), F32)],
        compiler_params=_cp("parallel", "arbitrary", "arbitrary"))(qkv, qkv, qkv, qkv, qkv, qkv, qkv, bias, do, lse, corr)


def dil_bwd_kv(qkv, bias_t, do, lse, corr, *, d, name):
    l = qkv.shape[0]
    tq = _tile(l, 512)
    nt = l // tq
    wlen = tq + 2 * DIL_W
    H = DIL_HEADS
    scale = DIL_HEAD_DIM ** -0.5

    def body(k_ref, v_ref, qp, qo, qn, dp, do_, dn, lp, lo, ln, cp, co, cn, b_ref, dk_ref, dv_ref):
        j = pl.program_id(2)
        qw = _dil_window(qp, qo, qn, tq)
        dow = _dil_window(dp, do_, dn, tq)
        lrow = _dil_window(lp, lo, ln, tq).T[0:1, :]
        crow = _dil_window(cp, co, cn, tq).T[0:1, :]
        st = _nt(k_ref[...], qw) * scale + b_ref[0]
        qpos = j * tq - DIL_W + lax.broadcasted_iota(jnp.int32, (tq, wlen), 1)
        pt = jnp.where((qpos >= 0) & (qpos < l), jnp.exp(st - lrow), 0.0)
        dv_ref[...] = _nn(pt.astype(BF16), dow)
        dst = pt * (_nt(v_ref[...], dow) - crow)
        dk_ref[...] = _nn(dst.astype(BF16), qw) * scale

    _, k_spec, _ = _dil_specs(tq, nt, 4, 12)
    _, v_spec, _ = _dil_specs(tq, nt, 8, 12)
    t_spec = pl.BlockSpec((tq, LANES), lambda h, r, i: (i, r * H + h))
    return pl.pallas_call(
        body, name=name, grid=(H, d, nt),
        in_specs=[k_spec, v_spec, *_dil_specs(tq, nt, 0, 12), *_dil_specs(tq, nt, 0, 4), *_dil_specs(tq, nt, 0, 4),
                  *_dil_specs(tq, nt, 0, 4), pl.BlockSpec((1, tq, wlen), lambda h, r, i: (h, 0, 0))],
        out_specs=[t_spec, t_spec], out_shape=[SDS((l, d * DIL_WIDTH), F32), SDS((l, d * DIL_WIDTH), F32)],
        compiler_params=_cp("parallel", "parallel", "arbitrary"))(
            qkv, qkv, qkv, qkv, qkv, do, do, do, lse, lse, lse, corr, corr, corr, bias_t)


def _dil_weights(l1, l2, l3):
    m = jnp.maximum(jnp.maximum(l1, l2), l3)
    e = [jnp.exp(x - m) for x in (l1, l2, l3)]
    tot = e[0] + e[1] + e[2]
    return [x / tot for x in e]


def dil_combine(os_, ls_, gain, *, name):
    S_ = os_[0].shape[0]
    T = _tile(S_, 512)

    def body(o1, o2, o3, l1, l2, l3, g_ref, y_ref):
        w = _dil_weights(l1[...], l2[...], l3[...])
        o = w[0] * o1[...] + w[1] * o2[...] + w[2] * o3[...]
        for h, (ohat, _) in enumerate(_head_norm_fwd(o, DIL_HEAD_DIM)):
            cs = slice(h * DIL_HEAD_DIM, (h + 1) * DIL_HEAD_DIM)
            y_ref[:, cs] = (ohat * g_ref[:, cs]).astype(BF16)

    blk = pl.BlockSpec((T, DIL_WIDTH), lambda i: (i, 0))
    return pl.pallas_call(
        body, name=name, grid=(S_ // T,), in_specs=[blk] * 6 + [pl.BlockSpec((1, DIL_WIDTH), lambda i: (0, 0))],
        out_specs=blk, out_shape=SDS((S_, DIL_WIDTH), BF16), compiler_params=_cp("parallel"))(*os_, *ls_, gain)


def dil_combine_bwd(dmix, os_, ls_, gain, *, name):
    S_ = os_[0].shape[0]
    T = _tile(S_, 512)
    E = DIL_HEAD_DIM

    def body(dy_ref, o1, o2, o3, l1, l2, l3, g_ref, d1, d2, d3, c1, c2, c3, gg_ref):
        @pl.when(pl.program_id(0) == 0)
        def _():
            gg_ref[...] = jnp.zeros_like(gg_ref)
        w = _dil_weights(l1[...], l2[...], l3[...])
        o = w[0] * o1[...] + w[1] * o2[...] + w[2] * o3[...]
        for h, (ohat, rs) in enumerate(_head_norm_fwd(o, E)):
            cs = slice(h * E, (h + 1) * E)
            dy = dy_ref[:, cs].astype(F32)
            u = dy * g_ref[:, cs]
            do = rs * (u - ohat * jnp.mean(u * ohat, axis=-1, keepdims=True))
            gg_ref[:, cs] += jnp.sum(dy * ohat, axis=0, keepdims=True)
            tot = jnp.sum(do * o[:, cs], axis=-1, keepdims=True)
            for wr, d_ref, c_ref in zip(w, (d1, d2, d3), (c1, c2, c3)):
                d_ref[:, cs] = (wr[:, cs] * do).astype(BF16)
                c_ref[:, cs] = wr[:, cs] * tot

    blk = pl.BlockSpec((T, DIL_WIDTH), lambda i: (i, 0))
    vec = pl.BlockSpec((1, DIL_WIDTH), lambda i: (0, 0))
    return pl.pallas_call(
        body, name=name, grid=(S_ // T,),
        in_specs=[pl.BlockSpec((T, DIL_WIDTH), lambda i: (i, V_W // DIL_WIDTH))] + [blk] * 6 + [vec],
        out_specs=[blk] * 6 + [vec],
        out_shape=[SDS((S_, DIL_WIDTH), BF16)] * 3 + [SDS((S_, DIL_WIDTH), F32)] * 3 + [SDS((1, DIL_WIDTH), F32)],
        compiler_params=_cp("arbitrary"))(dmix, *os_, *ls_, gain)


def _mem_softmax(q, k, scale):
    s = _nt(q, k) * scale
    e = jnp.exp(s - jnp.max(s, axis=-1, keepdims=True))
    return e / jnp.sum(e, axis=-1, keepdims=True)


def mem_fwd(proj, kv, gain, *, name):
    S_ = proj.shape[0]
    M = kv.shape[0]
    T = _tile(S_, 1024)
    E, H = MEM_HEAD_DIM, MEM_HEADS
    scale = E ** -0.5

    def body(q_ref, k_ref, v_ref, g_ref, y_ref):
        p = _mem_softmax(q_ref[...], k_ref[...], scale)
        o = _nn(p.astype(BF16), v_ref[...])
        rs = lax.rsqrt(jnp.mean(o * o, axis=-1, keepdims=True) + EPS)
        y_ref[...] = (o * rs * g_ref[...]).astype(BF16)

    return pl.pallas_call(
        body, name=name, grid=(H, S_ // T),
        in_specs=[pl.BlockSpec((T, E), lambda h, i: (i, C_MQ // E + h)), pl.BlockSpec((M, E), lambda h, i: (0, h)),
                  pl.BlockSpec((M, E), lambda h, i: (0, H + h)), pl.BlockSpec((1, E), lambda h, i: (0, h))],
        out_specs=pl.BlockSpec((T, E), lambda h, i: (i, h)), out_shape=SDS((S_, MEM_WIDTH), BF16),
        compiler_params=_cp("parallel", "parallel"))(proj, kv, kv, gain)


def mem_bwd(dmix, proj, kv, gain, *, name):
    S_ = proj.shape[0]
    M = kv.shape[0]
    T = _tile(S_, 1024)
    E, H = MEM_HEAD_DIM, MEM_HEADS
    scale = E ** -0.5

    def body(dy_ref, q_ref, k_ref, v_ref, g_ref, dq_ref, dk_ref, dv_ref, gg_ref):
        @pl.when(pl.program_id(1) == 0)
        def _():
            dk_ref[...] = jnp.zeros_like(dk_ref)
            dv_ref[...] = jnp.zeros_like(dv_ref)
            gg_ref[...] = jnp.zeros_like(gg_ref)
        q, k, v = q_ref[...], k_ref[...], v_ref[...]
        p = _mem_softmax(q, k, scale)
        pb = p.astype(BF16)
        o = _nn(pb, v)
        rs = lax.rsqrt(jnp.mean(o * o, axis=-1, keepdims=True) + EPS)
        ohat = o * rs
        dy = dy_ref[...].astype(F32)
        u = dy * g_ref[...]
        do = (rs * (u - ohat * jnp.mean(u * ohat, axis=-1, keepdims=True))).astype(BF16)
        gg_ref[...] += jnp.sum(dy * ohat, axis=0, keepdims=True)
        dv_ref[...] += _tn(pb, do)
        dp = _nt(do, v)
        ds = (p * (dp - jnp.sum(p * dp, axis=-1, keepdims=True))).astype(BF16)
        dq_ref[...] = (_nn(ds, k) * scale).astype(BF16)
        dk_ref[...] += _tn(ds, q) * scale

    return pl.pallas_call(
        body, name=name, grid=(H, S_ // T),
        in_specs=[pl.BlockSpec((T, E), lambda h, i: (i, (V_W + DIL_WIDTH) // E + h)),
                  pl.BlockSpec((T, E), lambda h, i: (i, C_MQ // E + h)), pl.BlockSpec((M, E), lambda h, i: (0, h)),
                  pl.BlockSpec((M, E), lambda h, i: (0, H + h)), pl.BlockSpec((1, E), lambda h, i: (0, h))],
        out_specs=[pl.BlockSpec((T, E), lambda h, i: (i, h)), pl.BlockSpec((M, E), lambda h, i: (0, h)),
                   pl.BlockSpec((M, E), lambda h, i: (0, h)), pl.BlockSpec((1, E), lambda h, i: (0, h))],
        out_shape=[SDS((S_, MEM_WIDTH), BF16), SDS((M, MEM_WIDTH), F32), SDS((M, MEM_WIDTH), F32),
                   SDS((1, MEM_WIDTH), F32)],
        compiler_params=_cp("parallel", "arbitrary"))(dmix, proj, kv, kv, gain)


def _strided(a, d):
    return a if d == 1 else a.reshape(a.shape[0] // d, d * a.shape[1])


def _unstrided(a, d):
    return a if d == 1 else a.reshape(a.shape[0] * d, a.shape[1] // d)


def _dil_tables(rel_bias, seq):
    tabs = []
    for d in DIL_DILATIONS:
        tq = min(seq // d, 512)
        vec = jnp.sum(jnp.asarray(_rel_onehot(d))[:, :, None] * rel_bias[None], axis=1).T
        tabs.append((_toeplitz(vec, tq, False), _toeplitz(vec, tq, True)))
    return tabs


def _rel_bias_grad(ds_sums, seq):
    g = jnp.zeros((REL_BUCKETS, DIL_HEADS), F32)
    for d, ds in zip(DIL_DILATIONS, ds_sums):
        gvec = _untoeplitz(ds, min(seq // d, 512))
        g = g + jnp.sum(jnp.asarray(_rel_onehot(d))[:, :, None] * gvec.T[:, None, :], axis=0)
    return g


def _pad_gate_up(up, row0):
    return jnp.zeros((LANES, QK_W), BF16).at[row0:row0 + GLA_GATE_RANK].set(up.astype(BF16))


def _layer_fwd(x, mem, p, tabs, l, carry):
    tag = f"l{l}"
    proj, hn1 = norm_mm(x, p["norm_mix"], p["w_in"], tm=TM_FWD, tn=IN_TILE, name=f"in_proj_{tag}", comm=carry.get("in"))
    o_f, st_f = gla_fwd(proj, p["up_f"], p["bias_f"], reverse=False, name=f"gla_fwd_f_{tag}")
    o_b, st_b = gla_fwd(proj, p["up_b"], p["bias_b"], reverse=True, name=f"gla_fwd_b_{tag}")
    gla_out = gla_post(o_f, o_b, proj, p["gla_norm"], name=f"gla_post_{tag}")
    qkv = proj[:, C_DQ:C_MQ]
    os_, ls_ = [], []
    for d, (bias, _) in zip(DIL_DILATIONS, tabs):
        o, lse = dil_fwd(_strided(qkv, d), bias, d=d, name=f"dil_fwd_d{d}_{tag}")
        os_.append(_unstrided(o, d))
        ls_.append(_unstrided(lse, d))
    dil_out = dil_combine(os_, ls_, p["dil_norm"], name=f"dil_combine_{tag}")
    kv, hmem = norm_mm(mem, p["mem_norm"], p["w_mem_kv"], tm=256, tn=1024, name=f"mem_kv_{tag}")
    mem_out = mem_fwd(proj, kv, p["mem_out_norm"], name=f"mem_fwd_{tag}")
    mixed = jnp.concatenate([gla_out, dil_out, mem_out], axis=1)
    x1 = mm_res(mixed, p["w_out"], x, relu2=False, tm=TM_FWD, tn=TN, tk=TK, name=f"out_proj_{tag}",
                comm=carry.get("out"))
    a, hn2 = norm_mm(x1, p["norm_mlp"], p["w_up"], tm=TM_FWD, tn=TN, name=f"up_proj_{tag}", comm=carry.get("up"))
    x2 = mm_res(a, p["w_down"], x1, relu2=True, tm=TM_FWD, tn=TN, tk=TK, name=f"down_proj_{tag}",
                comm=carry.get("down"))
    saved = dict(x=x, proj=proj, hn1=hn1, o_f=o_f, o_b=o_b, st_f=st_f, st_b=st_b, qkv=qkv, os=os_, ls=ls_,
                 kv=kv, hmem=hmem, mixed=mixed, x1=x1, a=a, hn2=hn2)
    return x2, saved


class GradScatter:
    SLOTS = {"down_x": [(1, "w_in")], "down_w": [(1, "w_out"), (1, "w_mem_kv")], "up_x": [(0, "w_down")],
             "in_x": [(0, "w_up")]}

    def __init__(self, exchange):
        self.exchange = exchange
        self.ready = {}
        self.sent = []

    def offer(self, l, name, g):
        self.ready[(l, name)] = g

    def _take(self, keys, name=None):
        keys = [k for k in keys if k in self.ready]
        if not (keys and self.exchange):
            return None
        ex = ChipExchange([self.ready.pop(k) for k in keys], scatter=True)
        self.sent.append((keys, ex))
        return ex

    def carrier(self, l, slot):
        return self._take([(l + dl, n) for dl, n in self.SLOTS[slot]])

    def flush(self, name):
        ex = self._take(sorted(self.ready))
        if ex is not None:
            ex.run(name)

    def received(self):
        return {k: r for keys, ex in self.sent for k, r in zip(keys, ex.result)}


def _quarters(g):
    return g.reshape(N_CHIPS, g.shape[0] // N_CHIPS, g.shape[1])


def _layer_bwd(dx2, mem, p, tabs, s, l, sink):
    tag = f"l{l}"
    seq = dx2.shape[0]
    g = {}
    da = mm_nt(dx2, p["w_down"], s["a"], tm=TM_FWD, tn=TN, name=f"down_bwd_x_{tag}", comm=sink.carrier(l, "down_x"))
    g_down = mm_tn(s["a"], dx2, relu2=True, tm=TM_W, tn=TN_W, tk=TK_W, out="rows", name=f"down_bwd_w_{tag}",
                   comm=sink.carrier(l, "down_w"))
    sink.offer(l, "w_down", _quarters(g_down))
    sink.offer(l, "w_up", mm_tn(s["hn2"], da, relu2=False, tm=TM_W, tn=TN_W, tk=TK_W, out="cols", name=f"up_bwd_w_{tag}"))
    dx1, g["norm_mlp"] = mm_nt_normbwd(da, p["w_up"], s["x1"], p["norm_mlp"], dx2, tm=TM_NORM_BWD, tk=TK_NORM_BWD,
                                       name=f"up_bwd_x_{tag}", comm=sink.carrier(l, "up_x"))
    dmix = mm_nt(dx1, p["w_out"], None, tm=TM_FWD, tn=TN, name=f"out_bwd_x_{tag}")
    sink.offer(l, "w_out", _quarters(mm_tn(s["mixed"], dx1, relu2=False, tm=TM_W, tn=TN_W, tk=TK_W, out="rows",
                                           name=f"out_bwd_w_{tag}")))
    proj = s["proj"]
    do, dr, g["gla_norm"] = gla_post_bwd(dmix, s["o_f"], s["o_b"], proj, p["gla_norm"], name=f"gla_post_bwd_{tag}")
    r_f = gla_bwd(proj, p["up_f"], p["bias_f"], s["st_f"], do, None, reverse=False, name=f"gla_bwd_f_{tag}")
    r_b = gla_bwd(proj, p["up_b"], p["bias_b"], s["st_b"], do, r_f[:4], reverse=True, name=f"gla_bwd_b_{tag}")
    dgq, dgk, dgv, dlr = r_b[:4]
    g["up_f"], g["bias_f"] = r_f[4][0:GLA_GATE_RANK], r_f[5]
    g["up_b"], g["bias_b"] = r_b[4][GLA_GATE_RANK:2 * GLA_GATE_RANK], r_b[5]
    *branch, g["dil_norm"] = dil_combine_bwd(dmix, s["os"], s["ls"], p["dil_norm"], name=f"dil_combine_bwd_{tag}")
    dos, corrs = branch[:3], branch[3:]
    ddq = ddk = ddv = None
    ds_sums = []
    for r, (d, (bias, bias_t)) in enumerate(zip(DIL_DILATIONS, tabs)):
        qkv_d = _strided(s["qkv"], d)
        do_d, lse_d, corr_d = _strided(dos[r], d), _strided(s["ls"][r], d), _strided(corrs[r], d)
        dq_d, ds_sum = dil_bwd_q(qkv_d, bias, do_d, lse_d, corr_d, d=d, name=f"dil_bwd_q_d{d}_{tag}")
        dk_d, dv_d = dil_bwd_kv(qkv_d, bias_t, do_d, lse_d, corr_d, d=d, name=f"dil_bwd_kv_d{d}_{tag}")
        ds_sums.append(ds_sum)
        dq_d, dk_d, dv_d = _unstrided(dq_d, d), _unstrided(dk_d, d), _unstrided(dv_d, d)
        ddq, ddk, ddv = (dq_d, dk_d, dv_d) if ddq is None else (ddq + dq_d, ddk + dk_d, ddv + dv_d)
    g["rel_bias"] = _rel_bias_grad(ds_sums, seq)
    dmq, dmk, dmv, g["mem_out_norm"] = mem_bwd(dmix, proj, s["kv"], p["mem_out_norm"], name=f"mem_bwd_{tag}")
    dkv = jnp.concatenate([dmk, dmv], axis=1).astype(BF16)
    sink.offer(l, "w_mem_kv", _quarters(mm_tn(s["hmem"], dkv, relu2=False, tm=TM_W, tn=TN_W, tk=TK_W, out="rows",
                                              name=f"mem_kv_bwd_w_{tag}")))
    _, g["mem_norm"] = mm_nt_normbwd(dkv, p["w_mem_kv"], mem, p["mem_norm"], jnp.zeros_like(mem), tm=TM_NORM_BWD,
                                     tk=TK_NORM_BWD, name=f"mem_kv_bwd_x_{tag}")
    dproj = jnp.concatenate([dgq, dgk, dgv, dr, ddq.astype(BF16), ddk.astype(BF16), ddv.astype(BF16), dmq, dlr,
                             jnp.zeros((seq, IN_PAD - C_LR - LANES), BF16)], axis=1)
    g_in = mm_tn(s["hn1"], dproj, relu2=False, tm=TM_W, tn=IN_TILE, tk=TK_W, name=f"in_bwd_w_{tag}")
    sink.offer(l, "w_in", _shard_w_in_grad(g_in))
    dx0, g["norm_mix"] = mm_nt_normbwd(dproj, p["w_in"], s["x"], p["norm_mix"], dx1, tm=TM_NORM_BWD, tk=IN_TILE,
                                       name=f"in_bwd_x_{tag}", comm=sink.carrier(l, "in_x"))
    return dx0, g


def _shard_w_in_grad(g):
    g = jnp.concatenate([g[:, :LR_COL], g[:, C_LR:C_LR + 2 * GLA_GATE_RANK], g[:, LR_COL:C_LR]], axis=1).astype(BF16)
    w = IN_WIDTH // N_CHIPS
    return jnp.stack([g[:, j * w:(j + 1) * w] for j in range(N_CHIPS)], axis=0)


def local_step(x, mem, target, depth, layer_weights, norm_final, rel_bias, fwd_carry, sink):
    tabs = _dil_tables(rel_bias, x.shape[0])
    saved, layers = [], []
    for l in range(depth):
        layers.append(layer_weights(l))
        x, s = _layer_fwd(x, mem, layers[l], tabs, l, fwd_carry(l))
        saved.append(s)
    dx, g_final, loss = loss_head(x, norm_final, target, tm=TM_LOSS, name="loss_head")
    grads = [None] * depth
    for l in reversed(range(depth)):
        dx, grads[l] = _layer_bwd(dx, mem, layers[l], tabs, saved[l], l, sink)
    return loss, dx, grads, g_final


def sibling_exchange(bufs, *, name):
    n = len(bufs)

    def body(*refs):
        ins, outs = refs[:n], refs[n:2 * n]
        send, recv = refs[2 * n:]
        x, y, c = _mesh_pos()
        cps = [pltpu.make_async_remote_copy(src_ref=ins[a], dst_ref=outs[a], send_sem=send.at[a], recv_sem=recv.at[a],
                                            device_id=(x, y, 1 - c), device_id_type=MESH) for a in range(n)]
        for cp in cps:
            cp.start()
        for cp in cps:
            cp.wait()

    return pl.pallas_call(
        body, name=name, in_specs=[HBM_SPEC] * n, out_specs=[HBM_SPEC] * n,
        out_shape=[SDS(b.shape, b.dtype) for b in bufs],
        scratch_shapes=[pltpu.SemaphoreType.DMA((n,)), pltpu.SemaphoreType.DMA((n,))])(*bufs)


def all_reduce_small(v, *, name):
    R = v.shape[0]

    def body(v_ref, o_ref, slots, send, recv):
        x, y, c = _mesh_pos()
        me = 4 * x + 2 * y + c
        flip = lambda p, f: 1 - p if f else p
        peers = [(flip(x, k & 4), flip(y, k & 2), flip(c, k & 1)) for k in range(1, N_DEV)]
        slots[pl.ds(me, 1)] = v_ref[...][None]
        cps = [pltpu.make_async_remote_copy(src_ref=v_ref, dst_ref=slots.at[me], send_sem=send.at[k], recv_sem=recv.at[k],
                                            device_id=peer, device_id_type=MESH) for k, peer in enumerate(peers)]
        for cp in cps:
            cp.start()
        for k, (px, py, pc) in enumerate(peers):
            pltpu.make_async_remote_copy(src_ref=v_ref, dst_ref=slots.at[4 * px + 2 * py + pc], send_sem=send.at[k],
                                         recv_sem=recv.at[k], device_id=(px, py, pc), device_id_type=MESH).wait_recv()
        for cp in cps:
            cp.wait_send()
        acc = slots[0]
        for s in range(1, N_DEV):
            acc = acc + slots[s]
        o_ref[...] = acc

    vm = pl.BlockSpec(memory_space=pltpu.VMEM)
    return pl.pallas_call(
        body, name=name, in_specs=[vm], out_specs=vm, out_shape=SDS(v.shape, F32),
        scratch_shapes=[pltpu.VMEM((N_DEV, R, LANES), F32), pltpu.SemaphoreType.DMA((N_DEV - 1,)),
                        pltpu.SemaphoreType.DMA((N_DEV - 1,))])(v)


def sum_chips(recv, *, name):
    _, R, C = recv.shape
    tr = _tile(R, 512)

    def body(r_ref, o_ref):
        acc = r_ref[0].astype(F32)
        for j in range(1, N_CHIPS):
            acc = acc + r_ref[j].astype(F32)
        o_ref[...] = acc

    return pl.pallas_call(
        body, name=name, grid=(R // tr,), in_specs=[pl.BlockSpec((N_CHIPS, tr, C), lambda i: (0, i, 0))],
        out_specs=pl.BlockSpec((tr, C), lambda i: (i, 0)), out_shape=SDS((R, C), F32),
        compiler_params=_cp("parallel"))(recv)


def adamw(parts, w, m, v, *, name):
    R, C = w.shape
    n = len(parts)
    tr = _tile(R, 512)
    while 2 * (n + 7) * tr * C * 4 > ELEMWISE_BLOCK_BYTES and tr % 16 == 0:
        tr //= 2
    c1 = 1.0 - ADAM_B1 ** ADAM_STEP
    c2 = 1.0 - ADAM_B2 ** ADAM_STEP

    def body(*refs):
        w_ref, m_ref, v_ref = refs[n:n + 3]
        g_ref, d_ref, nm_ref, nv_ref = refs[n + 3:]
        g = refs[0][...]
        for r in refs[1:n]:
            g = g + r[...]
        nm = ADAM_B1 * m_ref[...] + (1.0 - ADAM_B1) * g
        nv = ADAM_B2 * v_ref[...] + (1.0 - ADAM_B2) * (g * g)
        g_ref[...] = g
        nm_ref[...] = nm
        nv_ref[...] = nv
        d_ref[...] = -ADAM_LR * ((nm / c1) / (jnp.sqrt(nv / c2) + ADAM_EPS) + ADAM_WD * w_ref[...])

    blk = pl.BlockSpec((tr, C), lambda i: (i, 0))
    return pl.pallas_call(
        body, name=name, grid=(R // tr,), in_specs=[blk] * (n + 3), out_specs=[blk] * 4,
        out_shape=[SDS((R, C), F32)] * 4, compiler_params=_cp("parallel"))(*parts, w, m, v)


PACK_ROWS = 8


def _pack(parts):
    rows = []
    for a in parts:
        flat = a.reshape(-1).astype(F32)
        n = -(-flat.shape[0] // (PACK_ROWS * LANES)) * PACK_ROWS * LANES
        rows.append(jnp.pad(flat, (0, n - flat.shape[0])).reshape(-1, LANES))
    return jnp.concatenate(rows, axis=0)


def _unpack(slab, shapes):
    out, r = [], 0
    for shp in shapes:
        n = int(np.prod(shp))
        nr = -(-n // (PACK_ROWS * LANES)) * PACK_ROWS
        out.append(slab[r:r + nr].reshape(-1)[:n].reshape(shp))
        r += nr
    return out


SMALL = ["norm_mix", "gla_gate_bias_fwd", "gla_gate_bias_bwd", "gla_norm", "rel_bias", "dil_norm", "mem_norm",
         "mem_out_norm", "norm_mlp", "norm_final"]
GATE_UPS = ["gla_gate_up_fwd", "gla_gate_up_bwd"]
BIG = ["w_in", "w_mem_kv", "w_out", "w_up", "w_down"]
WEIGHTS = ["norm_mix", "w_in", "gla_gate_up_fwd", "gla_gate_bias_fwd", "gla_gate_up_bwd", "gla_gate_bias_bwd", "gla_norm",
           "rel_bias", "dil_norm", "mem_norm", "w_mem_kv", "mem_out_norm", "w_out", "norm_mlp", "w_up", "w_down",
           "norm_final"]


FWD_CARRY = {"in": ["w_in"], "out": ["w_out", "w_mem_kv"], "up": ["w_up"], "down": ["w_down"]}


def _kernel_layout(name, g):
    if name == "w_up":
        return g
    if name == "w_in":
        full = jnp.concatenate([g[j] for j in range(N_CHIPS)], axis=1)
        return jnp.concatenate([full[:, :LR_COL], full[:, LR_COL + 2 * GLA_GATE_RANK:],
                                full[:, LR_COL:LR_COL + 2 * GLA_GATE_RANK],
                                jnp.zeros((full.shape[0], IN_PAD - IN_WIDTH), BF16)], axis=1)
    return g.reshape(-1, g.shape[-1])


def kernel(x, mem, norm_mix, w_in, gla_gate_up_fwd, gla_gate_bias_fwd, gla_gate_up_bwd, gla_gate_bias_bwd, gla_norm, rel_bias, dil_norm, mem_norm, w_mem_kv, mem_out_norm, w_out, norm_mlp, w_up, w_down, norm_final, loss_target, m_norm_mix, m_w_in, m_gla_gate_up_fwd, m_gla_gate_bias_fwd, m_gla_gate_up_bwd, m_gla_gate_bias_bwd, m_gla_norm, m_rel_bias, m_dil_norm, m_mem_norm, m_w_mem_kv, m_mem_out_norm, m_w_out, m_norm_mlp, m_w_up, m_w_down, m_norm_final, v_norm_mix, v_w_in, v_gla_gate_up_fwd, v_gla_gate_bias_fwd, v_gla_gate_up_bwd, v_gla_gate_bias_bwd, v_gla_norm, v_rel_bias, v_dil_norm, v_mem_norm, v_w_mem_kv, v_mem_out_norm, v_w_out, v_norm_mlp, v_w_up, v_w_down, v_norm_final):
    vals = dict(locals())
    W = {n: vals[n] for n in WEIGHTS}
    M1 = {n: vals["m_" + n] for n in WEIGHTS}
    V2 = {n: vals["v_" + n] for n in WEIGHTS}
    depth = w_in.shape[0]
    chip = 2 * lax.axis_index("x") + lax.axis_index("y")

    wb = {n: W[n].astype(BF16) for n in BIG + GATE_UPS}
    first = ChipExchange([wb[n] for n in BIG + GATE_UPS], scatter=False, layers=[0] * len(BIG) + [None] * len(GATE_UPS))
    first.run("gather_layer0")
    gathered = {(0, n): r for n, r in zip(BIG, first.result)}
    gate_up = dict(zip(GATE_UPS, first.result[len(BIG):]))
    carried = {}
    row = lambda a: a.reshape(1, -1)

    def fwd_carry(l):
        if l + 1 == depth:
            return {}
        carried[l] = {slot: ChipExchange([wb[n] for n in names], scatter=False, layers=[l + 1] * len(names))
                      for slot, names in FWD_CARRY.items()}
        return carried[l]

    def layer_weights(l):
        if l > 0:
            for slot, names in FWD_CARRY.items():
                gathered.update({(l, n): r for n, r in zip(names, carried[l - 1][slot].result)})
        p = {n: _kernel_layout(n, gathered[(l, n)]) for n in BIG}
        full_up = lambda n: jnp.concatenate([gate_up[n][j, l] for j in range(N_CHIPS)], axis=1)
        p["up_f"] = _pad_gate_up(full_up("gla_gate_up_fwd"), 0)
        p["up_b"] = _pad_gate_up(full_up("gla_gate_up_bwd"), GLA_GATE_RANK)
        p["bias_f"], p["bias_b"] = row(gla_gate_bias_fwd[l]), row(gla_gate_bias_bwd[l])
        for n in ("norm_mix", "gla_norm", "dil_norm", "mem_norm", "mem_out_norm", "norm_mlp"):
            p[n] = row(W[n][l])
        return p

    sink = GradScatter(exchange=True)
    loss, grad_x, grads, g_final = local_step(x[0], mem[0], loss_target[0], depth, layer_weights, row(norm_final),
                                              rel_bias, fwd_carry, sink)
    sink.flush("scatter_last_grads")
    recv = sink.received()

    flat2 = lambda a: a.reshape(-1, a.shape[-1])
    part = [jnp.concatenate([sum_chips(recv[(l, n)].reshape(N_CHIPS, -1, recv[(l, n)].shape[-1]),
                                       name=f"sum_chips_{n}_l{l}") for l in range(depth)], axis=0) for n in BIG]
    other = sibling_exchange(part, name="swap_partial_sums")
    out = {}
    for n, pa, ob in zip(BIG, part, other):
        res = adamw([pa, ob], flat2(W[n]), flat2(M1[n]), flat2(V2[n]), name=f"adamw_{n}")
        out[n] = [r.reshape(W[n].shape) for r in res]

    small_g = {}
    for n, key in (("norm_mix", "norm_mix"), ("gla_gate_bias_fwd", "bias_f"), ("gla_gate_bias_bwd", "bias_b"),
                   ("gla_norm", "gla_norm"), ("dil_norm", "dil_norm"), ("mem_norm", "mem_norm"),
                   ("mem_out_norm", "mem_out_norm"), ("norm_mlp", "norm_mlp"),
                   ("gla_gate_up_fwd", "up_f"), ("gla_gate_up_bwd", "up_b")):
        small_g[n] = jnp.stack([grads[l][key].reshape(W[n].shape[1:] if n not in GATE_UPS else (GLA_GATE_RANK, QK_W))
                                for l in range(depth)], axis=0)
    small_g["rel_bias"] = sum(grads[l]["rel_bias"] for l in range(depth))
    small_g["norm_final"] = g_final.reshape(-1)
    names = SMALL + GATE_UPS
    slab = all_reduce_small(_pack([small_g[n] for n in names] + [loss[:, 0]]), name="all_reduce_small")
    *summed, loss_sum = _unpack(slab, [small_g[n].shape for n in names] + [(1,)])
    summed = dict(zip(names, summed))
    for n in GATE_UPS:
        summed[n] = lax.dynamic_slice_in_dim(summed[n], chip * LANES, LANES, axis=2)
    names_s = SMALL + GATE_UPS
    shapes = [W[n].shape for n in names_s]
    res = adamw([_pack([summed[n] for n in names_s])], _pack([W[n] for n in names_s]), _pack([M1[n] for n in names_s]),
                _pack([V2[n] for n in names_s]), name="adamw_small")
    for n, *r in zip(names_s, *[_unpack(t, shapes) for t in res]):
        out[n] = r

    return (loss_sum.reshape(()), grad_x[None], *[out[n][0] for n in WEIGHTS], *[out[n][1] for n in WEIGHTS],
            *[out[n][2] for n in WEIGHTS], *[out[n][3] for n in WEIGHTS])
```

```python
import functools
import math

import numpy as np
import jax
import jax.numpy as jnp
from jax import lax
from jax.experimental import pallas as pl
from jax.experimental.pallas import tpu as pltpu

F32, BF16 = jnp.float32, jnp.bfloat16
SDS = jax.ShapeDtypeStruct

DEPTH = 4
GLA_HEADS, GLA_DK, GLA_DV = 4, 128, 256
GLA_GATE_RANK = 16
GLA_GATE_NORMALIZER = 16.0
GLA_CHUNK = 64
DIL_HEADS, DIL_HEAD_DIM = 4, 128
DIL_DILATIONS = (1, 4, 16)
DIL_W = 64
MEM_HEADS, MEM_HEAD_DIM = 4, 128
REL_BUCKETS, REL_MAX_DISTANCE = 32, 1024
EPS = 1e-6
NEG_INF = -1e30
QK_W, V_W, DIL_WIDTH, MEM_WIDTH = 512, 1024, 512, 512
IN_WIDTH = 5152
LR_COL = 3072
IN_PAD = 5376
C_GQ, C_GK, C_GV, C_GR, C_DQ, C_MQ, C_LR = 0, 512, 1024, 2048, 3072, 4608, 5120
IN_TILE = 1792

ADAM_LR, ADAM_B1, ADAM_B2, ADAM_EPS, ADAM_WD, ADAM_STEP = 0.001, 0.9, 0.999, 1e-08, 0.01, 10

V7X_VMEM_BYTES = 64 * 1024 * 1024
VMEM_LIMIT = V7X_VMEM_BYTES * 7 // 8
LANES = 128

TM_FWD = 512
TN = 2048
TK = 2048
TM_NORM_BWD = 512
TK_NORM_BWD = 2048
NORM_BWD_ROWS = 64
TM_W, TN_W, TK_W = 1024, 1024, 2048
TM_LOSS = 256
ELEMWISE_BLOCK_BYTES = 24 * 1024 * 1024

NT_DIMS = (((1,), (1,)), ((), ()))
TN_DIMS = (((0,), (0,)), ((), ()))


def _nt(a, b):
    return lax.dot_general(a, b, NT_DIMS, preferred_element_type=F32)


def _tn(a, b):
    return lax.dot_general(a, b, TN_DIMS, preferred_element_type=F32)


def _nn(a, b):
    return jnp.dot(a, b, preferred_element_type=F32)


def _cp(*sem):
    return pltpu.CompilerParams(dimension_semantics=sem, vmem_limit_bytes=VMEM_LIMIT)


def _tile(n, t):
    t = min(n, t)
    assert n % t == 0, (n, t)
    return t


MESH = pl.DeviceIdType.MESH
HBM_SPEC = pl.BlockSpec(memory_space=pltpu.HBM)
N_CHIPS = 4
N_DEV = 8


def _mesh_pos():
    return lax.axis_index("x"), lax.axis_index("y"), lax.axis_index("c")


class ChipExchange:
    def __init__(self, bufs, *, scatter, layers=None):
        self.bufs = list(bufs)
        self.n = len(self.bufs)
        self.scatter = scatter
        self.layers = list(layers) if layers is not None else [None] * self.n
        shp = [b.shape if l is None else b.shape[1:] for b, l in zip(self.bufs, self.layers)]
        self.out_shape = [SDS(s if scatter else (N_CHIPS,) + s, b.dtype) for s, b in zip(shp, self.bufs)]
        self.scratch = [pltpu.SemaphoreType.DMA((3 * self.n,)), pltpu.SemaphoreType.DMA((3 * self.n,)),
                        pltpu.SemaphoreType.DMA((self.n,))]
        self.result = None

    def _plan(self, ins, outs, sems, with_landing):
        send, recv, local = sems
        x, y, c = _mesh_pos()
        me = 2 * x + y
        chips = [(1 - x, y), (x, 1 - y), (1 - x, 1 - y)]
        kept, sent, landing = [], [], []
        for a in range(self.n):
            src = ins[a] if self.layers[a] is None else ins[a].at[self.layers[a]]
            pick = (lambda j, src=src: src.at[j]) if self.scatter else (lambda j, src=src: src)
            kept.append(pltpu.make_async_copy(pick(me), outs[a].at[me], local.at[a]))
            for j, (px, py) in enumerate(chips):
                sems_k = dict(send_sem=send.at[3 * a + j], recv_sem=recv.at[3 * a + j], device_id=(px, py, c),
                              device_id_type=MESH)
                sent.append(pltpu.make_async_remote_copy(src_ref=pick(2 * px + py), dst_ref=outs[a].at[me], **sems_k))
                if with_landing:
                    landing.append(pltpu.make_async_remote_copy(src_ref=pick(me), dst_ref=outs[a].at[2 * px + py], **sems_k))
        return kept, sent, landing

    def start(self, ins, outs, sems):
        kept, sent, _ = self._plan(ins, outs, sems, False)
        for cp in kept + sent:
            cp.start()

    def finish(self, ins, outs, sems):
        kept, sent, landing = self._plan(ins, outs, sems, True)
        for cp in landing:
            cp.wait_recv()
        for cp in sent:
            cp.wait_send()
        for cp in kept:
            cp.wait()

    def run(self, name):
        n = self.n

        def body(*refs):
            self.start(refs[:n], refs[n:2 * n], refs[2 * n:])
            self.finish(refs[:n], refs[n:2 * n], refs[2 * n:])

        self.result = pl.pallas_call(body, name=name, in_specs=[HBM_SPEC] * n, out_specs=[HBM_SPEC] * n,
                                     out_shape=self.out_shape, scratch_shapes=self.scratch)(*self.bufs)
        return self.result


def _call(body, *, name, grid, in_specs, out_specs, out_shape, sem, args, scratch_shapes=(), comm=None):
    if comm is None:
        return pl.pallas_call(body, name=name, grid=grid, in_specs=list(in_specs), out_specs=list(out_specs),
                              out_shape=list(out_shape), scratch_shapes=list(scratch_shapes),
                              compiler_params=_cp(*sem))(*args)
    n_in, n_out, n_scr, nc = len(in_specs), len(out_shape), len(scratch_shapes), comm.n

    def wrapped(*refs):
        ins, cin = refs[:n_in], refs[n_in:n_in + nc]
        outs, cout = refs[n_in + nc:n_in + nc + n_out], refs[n_in + nc + n_out:n_in + 2 * nc + n_out]
        scr, csem = refs[n_in + 2 * nc + n_out:n_in + 2 * nc + n_out + n_scr], refs[n_in + 2 * nc + n_out + n_scr:]
        ids = [pl.program_id(d) for d in range(len(grid))]
        first = functools.reduce(jnp.logical_and, [i == 0 for i in ids])
        last = functools.reduce(jnp.logical_and, [i == g - 1 for i, g in zip(ids, grid)])

        @pl.when(first)
        def _():
            comm.start(cin, cout, csem)
        body(*ins, *outs, *scr)

        @pl.when(last)
        def _():
            comm.finish(cin, cout, csem)

    res = pl.pallas_call(
        wrapped, name=name, grid=grid, in_specs=list(in_specs) + [HBM_SPEC] * nc,
        out_specs=list(out_specs) + [HBM_SPEC] * nc, out_shape=list(out_shape) + comm.out_shape,
        scratch_shapes=list(scratch_shapes) + comm.scratch,
        compiler_params=_cp(*("arbitrary",) * len(grid)))(*args, *comm.bufs)
    comm.result = res[n_out:]
    return res[:n_out]


def _w_spec(w, rows, tn, col_of):
    if w.ndim == 2:
        tn = _tile(w.shape[1], tn)
        return w.shape[1], tn, pl.BlockSpec((rows, tn), lambda *g: (0, col_of(*g)))
    tn = _tile(w.shape[2], tn)
    per = w.shape[2] // tn
    return N_CHIPS * w.shape[2], tn, pl.BlockSpec((None, rows, tn), lambda *g: (col_of(*g) // per, 0, col_of(*g) % per))


def norm_mm(x, gain, w, *, tm, tn, name, comm=None):
    M, D = x.shape
    tm = _tile(M, tm)
    N, tn, w_spec = _w_spec(w, D, tn, lambda i, j: j)

    def body(x_ref, g_ref, w_ref, o_ref, hn_ref):
        @pl.when(pl.program_id(1) == 0)
        def _():
            xf = x_ref[...]
            r = lax.rsqrt(jnp.mean(xf * xf, axis=-1, keepdims=True) + EPS)
            hn_ref[...] = (xf * r * g_ref[...]).astype(BF16)
        o_ref[...] = _nn(hn_ref[...], w_ref[...]).astype(BF16)

    return _call(
        body, name=name, grid=(M // tm, N // tn),
        in_specs=[pl.BlockSpec((tm, D), lambda i, j: (i, 0)), pl.BlockSpec((1, D), lambda i, j: (0, 0)), w_spec],
        out_specs=[pl.BlockSpec((tm, tn), lambda i, j: (i, j)), pl.BlockSpec((tm, D), lambda i, j: (i, 0))],
        out_shape=[SDS((M, N), BF16), SDS((M, D), BF16)], sem=("parallel", "arbitrary"), args=(x, gain, w), comm=comm)


def mm_res(a, w, res, *, relu2, tm, tn, tk, name, comm=None):
    M, K = a.shape
    N = w.shape[1]
    tm, tn, tk = _tile(M, tm), _tile(N, tn), _tile(K, tk)

    def body(a_ref, w_ref, r_ref, o_ref):
        @pl.when(pl.program_id(2) == 0)
        def _():
            o_ref[...] = r_ref[...]
        av = a_ref[...]
        if relu2:
            af = jnp.maximum(av.astype(F32), 0.0)
            av = (af * af).astype(BF16)
        o_ref[...] += _nn(av, w_ref[...])

    return _call(
        body, name=name, grid=(M // tm, N // tn, K // tk),
        in_specs=[pl.BlockSpec((tm, tk), lambda i, j, k: (i, k)), pl.BlockSpec((tk, tn), lambda i, j, k: (k, j)),
                  pl.BlockSpec((tm, tn), lambda i, j, k: (i, j))],
        out_specs=[pl.BlockSpec((tm, tn), lambda i, j, k: (i, j))], out_shape=[SDS((M, N), F32)],
        sem=("parallel", "parallel", "arbitrary"), args=(a, w, res), comm=comm)[0]


def mm_nt(dy, w, a, *, tm, tn, name, comm=None):
    M, K = dy.shape
    N = w.shape[0]
    tm, tn = _tile(M, tm), _tile(N, tn)
    with_a = a is not None

    def body(*refs):
        if with_a:
            dy_ref, w_ref, a_ref, o_ref, dyb = refs
        else:
            dy_ref, w_ref, o_ref, dyb = refs

        @pl.when(pl.program_id(1) == 0)
        def _():
            dyb[...] = dy_ref[...].astype(BF16)
        r = _nt(dyb[...], w_ref[...])
        if with_a:
            r = r * (2.0 * jnp.maximum(a_ref[...].astype(F32), 0.0))
        o_ref[...] = r.astype(BF16)

    in_specs = [pl.BlockSpec((tm, K), lambda i, j: (i, 0)), pl.BlockSpec((tn, K), lambda i, j: (j, 0))]
    args = [dy, w]
    if with_a:
        in_specs.append(pl.BlockSpec((tm, tn), lambda i, j: (i, j)))
        args.append(a)
    return _call(
        body, name=name, grid=(M // tm, N // tn), in_specs=in_specs,
        out_specs=[pl.BlockSpec((tm, tn), lambda i, j: (i, j))], out_shape=[SDS((M, N), BF16)],
        scratch_shapes=[pltpu.VMEM((tm, K), BF16)], sem=("parallel", "arbitrary"), args=args, comm=comm)[0]


def mm_nt_normbwd(dy, w, x, gain, dx_in, *, tm, tk, name, comm=None):
    M, K = dy.shape
    D = x.shape[1]
    tm = _tile(M, tm)
    _, tk, w_spec = _w_spec(w, D, tk, lambda i, k: k)
    nk = K // tk
    rc = _tile(tm, NORM_BWD_ROWS)

    def body(dy_ref, w_ref, x_ref, g_ref, dxin_ref, dx_ref, gg_ref, acc):
        i, k = pl.program_id(0), pl.program_id(1)

        @pl.when(k == 0)
        def _():
            acc[...] = jnp.zeros_like(acc)
        acc[...] += _nt(dy_ref[...], w_ref[...])

        @pl.when(k == nk - 1)
        def _():
            def rows(c, part):
                rs = pl.ds(pl.multiple_of(c * rc, rc), rc)
                dh, xf = acc[rs, :], x_ref[rs, :]
                r = lax.rsqrt(jnp.mean(xf * xf, axis=-1, keepdims=True) + EPS)
                xhat = xf * r
                u = dh * g_ref[...]
                dx_ref[rs, :] = dxin_ref[rs, :] + r * (u - xhat * jnp.mean(u * xhat, axis=-1, keepdims=True))
                return part + jnp.sum(dh * xhat, axis=0, keepdims=True)

            part = lax.fori_loop(0, tm // rc, rows, jnp.zeros((1, D), F32))

            @pl.when(i == 0)
            def _():
                gg_ref[...] = part

            @pl.when(i > 0)
            def _():
                gg_ref[...] += part

    return _call(
        body, name=name, grid=(M // tm, nk),
        in_specs=[pl.BlockSpec((tm, tk), lambda i, k: (i, k)), w_spec,
                  pl.BlockSpec((tm, D), lambda i, k: (i, 0)), pl.BlockSpec((1, D), lambda i, k: (0, 0)),
                  pl.BlockSpec((tm, D), lambda i, k: (i, 0))],
        out_specs=[pl.BlockSpec((tm, D), lambda i, k: (i, 0)), pl.BlockSpec((1, D), lambda i, k: (0, 0))],
        out_shape=[SDS((M, D), F32), SDS((1, D), F32)], scratch_shapes=[pltpu.VMEM((tm, D), F32)],
        sem=("arbitrary", "arbitrary"), args=(dy, w, x, gain, dx_in), comm=comm)


def mm_tn(a, dy, *, relu2, tm, tn, tk, name, out="f32", comm=None):
    S_, Ka = a.shape
    N = dy.shape[1]
    tm, tk = _tile(Ka, tm), _tile(S_, tk)
    tn = _tile(N // N_CHIPS if out == "cols" else N, tn)
    nk = S_ // tk

    def body(a_ref, dy_ref, o_ref, acc):
        k = pl.program_id(2)

        @pl.when(k == 0)
        def _():
            acc[...] = jnp.zeros_like(acc)
        av = a_ref[...]
        if relu2:
            af = jnp.maximum(av.astype(F32), 0.0)
            av = (af * af).astype(BF16)
        acc[...] += _tn(av, dy_ref[...].astype(BF16))

        @pl.when(k == nk - 1)
        def _():
            o_ref[...] = acc[...].astype(o_ref.dtype)

    if out == "cols":
        per = N // N_CHIPS // tn
        o_spec = pl.BlockSpec((None, tm, tn), lambda i, j, k: (j // per, i, j % per))
        o_shape = SDS((N_CHIPS, Ka, N // N_CHIPS), BF16)
    else:
        o_spec = pl.BlockSpec((tm, tn), lambda i, j, k: (i, j))
        o_shape = SDS((Ka, N), F32 if out == "f32" else BF16)
    return _call(
        body, name=name, grid=(Ka // tm, N // tn, nk),
        in_specs=[pl.BlockSpec((tk, tm), lambda i, j, k: (k, i)), pl.BlockSpec((tk, tn), lambda i, j, k: (k, j))],
        out_specs=[o_spec], out_shape=[o_shape], scratch_shapes=[pltpu.VMEM((tm, tn), F32)],
        sem=("parallel", "parallel", "arbitrary"), args=(a, dy), comm=comm)[0]


def loss_head(x, gain, target, *, tm, name):
    M, D = x.shape
    tm = _tile(M, tm)

    def body(x_ref, g_ref, t_ref, dx_ref, gg_ref, l_ref):
        i = pl.program_id(0)
        xf = x_ref[...]
        r = lax.rsqrt(jnp.mean(xf * xf, axis=-1, keepdims=True) + EPS)
        xhat = xf * r
        e = xhat * g_ref[...] - t_ref[...]
        lpart = 0.5 * jnp.sum(jnp.mean(e * e, axis=-1, keepdims=True), axis=0, keepdims=True)
        dy = e * (1.0 / D)
        u = dy * g_ref[...]
        dx_ref[...] = r * (u - xhat * jnp.mean(u * xhat, axis=-1, keepdims=True))
        gpart = jnp.sum(dy * xhat, axis=0, keepdims=True)
        lrow = jnp.broadcast_to(lpart, (1, LANES))

        @pl.when(i == 0)
        def _():
            gg_ref[...] = gpart
            l_ref[...] = lrow

        @pl.when(i > 0)
        def _():
            gg_ref[...] += gpart
            l_ref[...] += lrow

    return pl.pallas_call(
        body, name=name, grid=(M // tm,),
        in_specs=[pl.BlockSpec((tm, D), lambda i: (i, 0)), pl.BlockSpec((1, D), lambda i: (0, 0)),
                  pl.BlockSpec((tm, D), lambda i: (i, 0))],
        out_specs=[pl.BlockSpec((tm, D), lambda i: (i, 0)), pl.BlockSpec((1, D), lambda i: (0, 0)),
                   pl.BlockSpec((1, LANES), lambda i: (0, 0))],
        out_shape=[SDS((M, D), F32), SDS((1, D), F32), SDS((1, LANES), F32)],
        compiler_params=_cp("arbitrary"))(x, gain, target)


def _log_sigmoid(z):
    return jnp.minimum(z, 0.0) - jnp.log(1.0 + jnp.exp(-jnp.abs(z)))


def _chunk_scan(x, row, reverse):
    n = x.shape[0]
    s = 1
    while s < GLA_CHUNK:
        if reverse:
            x = x + jnp.where(row < GLA_CHUNK - s, pltpu.roll(x, n - s, 0), 0.0)
        else:
            x = x + jnp.where(row >= s, pltpu.roll(x, s, 0), 0.0)
        s *= 2
    return x


def _chunk_edge(b, reverse):
    t = b.shape[0]
    nc = t // GLA_CHUNK
    b3 = b.reshape(nc, GLA_CHUNK, b.shape[1])
    e = b3[:, 0:1, :] if reverse else b3[:, GLA_CHUNK - 1:GLA_CHUNK, :]
    return jnp.broadcast_to(e, b3.shape).reshape(b.shape), e.reshape(nc, b.shape[1])


def _gla_gates(lr, up_ref, bias_ref, cs, row, reverse):
    z = _nn(lr, up_ref[:, cs]) + bias_ref[:, cs]
    g = _log_sigmoid(z) * (1.0 / GLA_GATE_NORMALIZER)
    b = _chunk_scan(g, row, reverse)
    bl, blc = _chunk_edge(b, reverse)
    return z, b, bl, blc


def _gla_mask(reverse):
    r = lax.broadcasted_iota(jnp.int32, (GLA_CHUNK, GLA_CHUNK), 0)
    c = lax.broadcasted_iota(jnp.int32, (GLA_CHUNK, GLA_CHUNK), 1)
    return (r <= c) if reverse else (r >= c)


def gla_fwd(proj, upad, bias, *, reverse, name):
    S_ = proj.shape[0]
    T = _tile(S_, 512)
    nt, nc = S_ // T, T // GLA_CHUNK
    H, DK, DV, C = GLA_HEADS, GLA_DK, GLA_DV, GLA_CHUNK
    tix = (lambda i: nt - 1 - i) if reverse else (lambda i: i)
    qscale = DK ** -0.5

    def body(q_ref, k_ref, v_ref, lr_ref, up_ref, b_ref, o_ref, st_ref, st_s, qd_s, ki_s, ke_s, dec_s):
        @pl.when(pl.program_id(0) == 0)
        def _():
            st_s[...] = jnp.zeros_like(st_s)
        row = lax.broadcasted_iota(jnp.int32, (T, DK), 0) % C
        mask = _gla_mask(reverse)
        lr = lr_ref[...]
        for h in range(H):
            cs = slice(h * DK, (h + 1) * DK)
            _, b, bl, _ = _gla_gates(lr, up_ref, b_ref, cs, row, reverse)
            q = q_ref[:, cs].astype(F32) * qscale
            k = k_ref[:, cs].astype(F32)
            qd_s[:, cs] = (q * jnp.exp(b)).astype(BF16)
            ki_s[:, cs] = (k * jnp.exp(-b)).astype(BF16)
            ke_s[:, cs] = (k * jnp.exp(bl - b)).astype(BF16)
            dec_s[:, cs] = jnp.exp(bl)

        def chunk(ci, carry):
            c = (nc - 1 - ci) if reverse else ci
            r0 = pl.multiple_of(c * C, C)
            s0 = pl.multiple_of(c * DV, DV)
            for h in range(H):
                cs = slice(h * DK, (h + 1) * DK)
                vs = slice(h * DV, (h + 1) * DV)
                qd, ki, ke = qd_s[pl.ds(r0, C), cs], ki_s[pl.ds(r0, C), cs], ke_s[pl.ds(r0, C), cs]
                v = v_ref[pl.ds(r0, C), vs]
                st = st_s[h]
                stb = st.astype(BF16)
                st_ref[h, pl.ds(s0, DV), :] = stb
                a = jnp.where(mask, _nt(qd, ki), 0.0).astype(BF16)
                o_ref[pl.ds(r0, C), vs] = _nn(a, v) + _nt(qd, stb)
                st_s[h] = dec_s[pl.ds(r0, 1), cs] * st + _tn(v, ke)
            return carry

        lax.fori_loop(0, nc, chunk, 0)

    return pl.pallas_call(
        body, name=name, grid=(nt,),
        in_specs=[pl.BlockSpec((T, QK_W), lambda i: (tix(i), C_GQ // QK_W)),
                  pl.BlockSpec((T, QK_W), lambda i: (tix(i), C_GK // QK_W)),
                  pl.BlockSpec((T, V_W), lambda i: (tix(i), C_GV // V_W)),
                  pl.BlockSpec((T, LANES), lambda i: (tix(i), C_LR // LANES)),
                  pl.BlockSpec((LANES, QK_W), lambda i: (0, 0)), pl.BlockSpec((1, QK_W), lambda i: (0, 0))],
        out_specs=[pl.BlockSpec((T, V_W), lambda i: (tix(i), 0)),
                   pl.BlockSpec((H, nc * DV, DK), lambda i: (0, tix(i), 0))],
        out_shape=[SDS((S_, V_W), F32), SDS((H, S_ // C * DV, DK), BF16)],
        scratch_shapes=[pltpu.VMEM((H, DV, DK), F32), pltpu.VMEM((T, QK_W), BF16), pltpu.VMEM((T, QK_W), BF16),
                        pltpu.VMEM((T, QK_W), BF16), pltpu.VMEM((T, QK_W), F32)],
        compiler_params=_cp("arbitrary"))(proj, proj, proj, proj, upad, bias)


def gla_bwd(proj, upad, bias, states, do, prev, *, reverse, name):
    S_ = proj.shape[0]
    T = _tile(S_, 512)
    nt, nc = S_ // T, T // GLA_CHUNK
    H, DK, DV, C = GLA_HEADS, GLA_DK, GLA_DV, GLA_CHUNK
    tix = (lambda i: i) if reverse else (lambda i: nt - 1 - i)
    qscale = DK ** -0.5
    with_prev = prev is not None
    NCP = max(8, nc)

    def body(*refs):
        (q_ref, k_ref, v_ref, lr_ref, up_ref, b_ref, st_ref, do_ref) = refs[:8]
        n_in = 12 if with_prev else 8
        pq_ref, pk_ref, pv_ref, plr_ref = refs[8:12] if with_prev else (None,) * 4
        dq_ref, dk_ref, dv_ref, dlr_ref, dup_ref, dbias_ref = refs[n_in:n_in + 6]
        dst_s, qd_s, ki_s, ke_s, dec_s, dqd_s, dki_s, dke_s, ddec_s = refs[n_in + 6:]
        first = pl.program_id(0) == 0

        @pl.when(first)
        def _():
            dst_s[...] = jnp.zeros_like(dst_s)
            dup_ref[...] = jnp.zeros_like(dup_ref)
            dbias_ref[...] = jnp.zeros_like(dbias_ref)

        row = lax.broadcasted_iota(jnp.int32, (T, DK), 0) % C
        edge_row = 0 if reverse else C - 1
        mask = _gla_mask(reverse)
        lr = lr_ref[...]
        dlr = plr_ref[...].astype(F32) if with_prev else jnp.zeros((T, LANES), F32)

        def decayed(h):
            cs = slice(h * DK, (h + 1) * DK)
            z, b, bl, blc = _gla_gates(lr, up_ref, b_ref, cs, row, reverse)
            q = q_ref[:, cs].astype(F32) * qscale
            k = k_ref[:, cs].astype(F32)
            eb, einv, eend = jnp.exp(b), jnp.exp(-b), jnp.exp(bl - b)
            return z, bl, blc, eb, einv, eend, q * eb, k * einv, k * eend

        for h in range(H):
            cs = slice(h * DK, (h + 1) * DK)
            _, bl, _, _, _, _, qd, ki, ke = decayed(h)
            qd_s[:, cs] = qd.astype(BF16)
            ki_s[:, cs] = ki.astype(BF16)
            ke_s[:, cs] = ke.astype(BF16)
            dec_s[:, cs] = jnp.exp(bl)

        def chunk(ci, carry):
            c = ci if reverse else (nc - 1 - ci)
            r0 = pl.multiple_of(c * C, C)
            s0 = pl.multiple_of(c * DV, DV)
            for h in range(H):
                cs = slice(h * DK, (h + 1) * DK)
                vs = slice(h * DV, (h + 1) * DV)
                qdc, kic, kec = qd_s[pl.ds(r0, C), cs], ki_s[pl.ds(r0, C), cs], ke_s[pl.ds(r0, C), cs]
                v = v_ref[pl.ds(r0, C), vs]
                doc = do_ref[pl.ds(r0, C), vs]
                st = st_ref[h, pl.ds(s0, DV), :]
                dst = dst_s[h]
                dstb = dst.astype(BF16)
                a = jnp.where(mask, _nt(qdc, kic), 0.0).astype(BF16)
                da = jnp.where(mask, _nt(doc, v), 0.0).astype(BF16)
                dvc = _tn(a, doc) + _nt(kec, dstb)
                if with_prev:
                    dvc = dvc + pv_ref[pl.ds(r0, C), vs].astype(F32)
                dv_ref[pl.ds(r0, C), vs] = dvc.astype(BF16)
                dqd_s[pl.ds(r0, C), cs] = _nn(da, kic) + _nn(doc, st)
                dki_s[pl.ds(r0, C), cs] = _tn(da, qdc)
                dke_s[pl.ds(r0, C), cs] = _nn(v, dstb)
                ddec_s[h, pl.ds(c, 1), :] = jnp.sum(dst * st.astype(F32), axis=0, keepdims=True)
                dst_s[h] = dec_s[pl.ds(r0, 1), cs] * dst + _tn(doc, qdc)
            return carry

        lax.fori_loop(0, nc, chunk, 0)

        for h in range(H):
            cs = slice(h * DK, (h + 1) * DK)
            z, bl, blc, eb, einv, eend, qd, ki, ke = decayed(h)
            dqd, dki, dke = dqd_s[:, cs], dki_s[:, cs], dke_s[:, cs]
            dq = dqd * eb * qscale
            dk = dki * einv + dke * eend
            if with_prev:
                dq = dq + pq_ref[:, cs].astype(F32)
                dk = dk + pk_ref[:, cs].astype(F32)
            dq_ref[:, cs] = dq.astype(BF16)
            dk_ref[:, cs] = dk.astype(BF16)
            wke = dke * ke
            db = dqd * qd - dki * ki - wke
            dbl = jnp.sum(wke.reshape(nc, C, DK), axis=1) + ddec_s[h, 0:nc, :] * jnp.exp(blc)
            dbl_b = jnp.broadcast_to(dbl.reshape(nc, 1, DK), (nc, C, DK)).reshape(T, DK)
            db = db + jnp.where(row == edge_row, dbl_b, 0.0)
            dg = _chunk_scan(db, row, not reverse)
            dz = dg * (1.0 / GLA_GATE_NORMALIZER) * (1.0 / (1.0 + jnp.exp(z)))
            dzb = dz.astype(BF16)
            dbias_ref[:, cs] += jnp.sum(dz, axis=0, keepdims=True)
            dup_ref[:, cs] += _tn(lr, dzb)
            dlr = dlr + _nt(dzb, up_ref[:, cs])
        dlr_ref[...] = dlr.astype(BF16)

    tile = lambda w, cb: pl.BlockSpec((T, w), lambda i: (tix(i), cb))
    in_specs = [tile(QK_W, C_GQ // QK_W), tile(QK_W, C_GK // QK_W), tile(V_W, C_GV // V_W), tile(LANES, C_LR // LANES),
                pl.BlockSpec((LANES, QK_W), lambda i: (0, 0)), pl.BlockSpec((1, QK_W), lambda i: (0, 0)),
                pl.BlockSpec((H, nc * DV, DK), lambda i: (0, tix(i), 0)), tile(V_W, 0)]
    args = [proj, proj, proj, proj, upad, bias, states, do]
    if with_prev:
        in_specs += [tile(QK_W, 0), tile(QK_W, 0), tile(V_W, 0), tile(LANES, 0)]
        args += list(prev)
    return pl.pallas_call(
        body, name=name, grid=(nt,), in_specs=in_specs,
        out_specs=[tile(QK_W, 0), tile(QK_W, 0), tile(V_W, 0), tile(LANES, 0),
                   pl.BlockSpec((LANES, QK_W), lambda i: (0, 0)), pl.BlockSpec((1, QK_W), lambda i: (0, 0))],
        out_shape=[SDS((S_, QK_W), BF16), SDS((S_, QK_W), BF16), SDS((S_, V_W), BF16), SDS((S_, LANES), BF16),
                   SDS((LANES, QK_W), F32), SDS((1, QK_W), F32)],
        scratch_shapes=[pltpu.VMEM((H, DV, DK), F32), pltpu.VMEM((T, QK_W), BF16), pltpu.VMEM((T, QK_W), BF16),
                        pltpu.VMEM((T, QK_W), BF16), pltpu.VMEM((T, QK_W), F32), pltpu.VMEM((T, QK_W), F32),
                        pltpu.VMEM((T, QK_W), F32), pltpu.VMEM((T, QK_W), F32), pltpu.VMEM((H, NCP, DK), F32)],
        compiler_params=_cp("arbitrary"))(*args)


def _head_norm_fwd(o, width):
    out = []
    for h in range(o.shape[1] // width):
        oh = o[:, h * width:(h + 1) * width]
        rs = lax.rsqrt(jnp.mean(oh * oh, axis=-1, keepdims=True) + EPS)
        out.append((oh * rs, rs))
    return out


def gla_post(o_f, o_b, proj, gain, *, name):
    S_ = proj.shape[0]
    T = _tile(S_, 512)

    def body(of_ref, ob_ref, r_ref, g_ref, y_ref):
        o = of_ref[...] + ob_ref[...]
        for h, (ohat, _) in enumerate(_head_norm_fwd(o, GLA_DV)):
            vs = slice(h * GLA_DV, (h + 1) * GLA_DV)
            r = r_ref[:, vs].astype(F32)
            y_ref[:, vs] = (ohat * g_ref[:, vs] * (r / (1.0 + jnp.exp(-r)))).astype(BF16)

    blk = pl.BlockSpec((T, V_W), lambda i: (i, 0))
    return pl.pallas_call(
        body, name=name, grid=(S_ // T,),
        in_specs=[blk, blk, pl.BlockSpec((T, V_W), lambda i: (i, C_GR // V_W)), pl.BlockSpec((1, V_W), lambda i: (0, 0))],
        out_specs=blk, out_shape=SDS((S_, V_W), BF16), compiler_params=_cp("parallel"))(o_f, o_b, proj, gain)


def gla_post_bwd(dmix, o_f, o_b, proj, gain, *, name):
    S_ = proj.shape[0]
    T = _tile(S_, 512)

    def body(dy_ref, of_ref, ob_ref, r_ref, g_ref, do_ref, dr_ref, gg_ref):
        @pl.when(pl.program_id(0) == 0)
        def _():
            gg_ref[...] = jnp.zeros_like(gg_ref)
        o = of_ref[...] + ob_ref[...]
        for h, (ohat, rs) in enumerate(_head_norm_fwd(o, GLA_DV)):
            vs = slice(h * GLA_DV, (h + 1) * GLA_DV)
            r = r_ref[:, vs].astype(F32)
            dy = dy_ref[:, vs].astype(F32)
            sg = 1.0 / (1.0 + jnp.exp(-r))
            gate = r * sg
            n = ohat * g_ref[:, vs]
            dr_ref[:, vs] = (dy * n * (sg * (1.0 + r * (1.0 - sg)))).astype(BF16)
            dn = dy * gate
            u = dn * g_ref[:, vs]
            do_ref[:, vs] = (rs * (u - ohat * jnp.mean(u * ohat, axis=-1, keepdims=True))).astype(BF16)
            gg_ref[:, vs] += jnp.sum(dn * ohat, axis=0, keepdims=True)

    blk = pl.BlockSpec((T, V_W), lambda i: (i, 0))
    vec = pl.BlockSpec((1, V_W), lambda i: (0, 0))
    return pl.pallas_call(
        body, name=name, grid=(S_ // T,),
        in_specs=[blk, blk, blk, pl.BlockSpec((T, V_W), lambda i: (i, C_GR // V_W)), vec],
        out_specs=[blk, blk, vec], out_shape=[SDS((S_, V_W), BF16), SDS((S_, V_W), BF16), SDS((1, V_W), F32)],
        compiler_params=_cp("arbitrary"))(dmix, o_f, o_b, proj, gain)


def _t5_bucket_np(rel):
    half = REL_BUCKETS // 2
    max_exact = half // 2
    ret = np.where(rel > 0, half, 0)
    n = np.abs(rel)
    nf = np.maximum(n, 1).astype(np.float32)
    large = max_exact + (np.log(nf / np.float32(max_exact)) / np.float32(math.log(REL_MAX_DISTANCE / max_exact))
                         * np.float32(half - max_exact)).astype(np.int32)
    large = np.minimum(large, half - 1)
    return ret + np.where(n < max_exact, n, large)


def _rel_onehot(dilation):
    rel = np.arange(-DIL_W, DIL_W + 1)
    return np.eye(REL_BUCKETS, dtype=np.float32)[_t5_bucket_np(rel * dilation)]


def _toeplitz(vec, tq, flip):
    H = vec.shape[0]
    wlen = tq + 2 * DIL_W
    lu = wlen + tq
    if flip:
        vec = vec[:, ::-1]
    u = jnp.full((H, lu), NEG_INF, F32)
    u = lax.dynamic_update_slice(u, vec, (0, 0))
    t = jnp.tile(u, (1, tq))[:, :tq * (lu - 1)].reshape(H, tq, lu - 1)
    return t[:, :, :wlen]


def _untoeplitz(ds, tq):
    H = ds.shape[0]
    wlen = tq + 2 * DIL_W
    lu = wlen + tq
    p = jnp.pad(ds, ((0, 0), (0, 0), (0, lu - 1 - wlen))).reshape(H, tq * (lu - 1))
    p = jnp.pad(p, ((0, 0), (0, tq))).reshape(H, tq, lu)
    return jnp.sum(p, axis=1)[:, :2 * DIL_W + 1]


DIL_TQ = 512
DIL_SUB = 128


def _dil_band(tq):
    sb = min(tq, DIL_SUB)
    return sb, sb + 2 * DIL_W


def _dil_window(p_ref, o_ref, n_ref, tq):
    return jnp.concatenate([p_ref[tq - DIL_W:, :], o_ref[...], n_ref[:DIL_W, :]], axis=0)


def _dil_specs(tq, nt, col0, ncols_per_r):
    cb = lambda h, r: r * ncols_per_r + col0 + h
    own = pl.BlockSpec((tq, LANES), lambda h, r, i: (i, cb(h, r)))
    prev = pl.BlockSpec((tq, LANES), lambda h, r, i: (jnp.maximum(i - 1, 0), cb(h, r)))
    nxt = pl.BlockSpec((tq, LANES), lambda h, r, i: (jnp.minimum(i + 1, nt - 1), cb(h, r)))
    return prev, own, nxt


def dil_fwd(qkv, bias, *, d, name):
    l = qkv.shape[0]
    tq = _tile(l, DIL_TQ)
    nt = l // tq
    H = DIL_HEADS
    scale = DIL_HEAD_DIM ** -0.5

    sb, bw = _dil_band(tq)

    def body(q_ref, kp, ko, kn, vp, vo, vn, b_ref, o_ref, l_ref):
        i = pl.program_id(2)
        kw = _dil_window(kp, ko, kn, tq)
        vw = _dil_window(vp, vo, vn, tq)
        col = lax.broadcasted_iota(jnp.int32, (sb, bw), 1)
        for j in range(tq // sb):
            rows, win = slice(j * sb, (j + 1) * sb), slice(j * sb, j * sb + bw)
            s = _nt(q_ref[rows, :], kw[win]) * scale + b_ref[0]
            kpos = i * tq + j * sb - DIL_W + col
            s = jnp.where((kpos >= 0) & (kpos < l), s, NEG_INF)
            m = jnp.max(s, axis=-1, keepdims=True)
            p = jnp.exp(s - m)
            den = jnp.sum(p, axis=-1, keepdims=True)
            o_ref[rows, :] = _nn(p.astype(BF16), vw[win]) / den
            l_ref[rows, :] = jnp.broadcast_to(m + jnp.log(den), (sb, LANES))

    _, q_spec, _ = _dil_specs(tq, nt, 0, 12)
    out_spec = pl.BlockSpec((tq, LANES), lambda h, r, i: (i, r * H + h))
    return pl.pallas_call(
        body, name=name, grid=(H, d, nt),
        in_specs=[q_spec, *_dil_specs(tq, nt, 4, 12), *_dil_specs(tq, nt, 8, 12),
                  pl.BlockSpec((1, sb, bw), lambda h, r, i: (h, 0, 0))],
        out_specs=[out_spec, out_spec], out_shape=[SDS((l, d * DIL_WIDTH), F32), SDS((l, d * DIL_WIDTH), F32)],
        compiler_params=_cp("parallel", "parallel", "arbitrary"))(qkv, qkv, qkv, qkv, qkv, qkv, qkv, bias)


def dil_bwd_q(qkv, bias, do, lse, corr, *, d, name):
    l = qkv.shape[0]
    tq = _tile(l, DIL_TQ)
    nt = l // tq
    H = DIL_HEADS
    scale = DIL_HEAD_DIM ** -0.5

    sb, bw = _dil_band(tq)

    def body(q_ref, kp, ko, kn, vp, vo, vn, b_ref, do_ref, l_ref, c_ref, dq_ref, ds_ref):
        r, i = pl.program_id(1), pl.program_id(2)
        kw = _dil_window(kp, ko, kn, tq)
        vw = _dil_window(vp, vo, vn, tq)
        col = lax.broadcasted_iota(jnp.int32, (sb, bw), 1)
        ds_sum = jnp.zeros((sb, bw), F32)
        for j in range(tq // sb):
            rows, win = slice(j * sb, (j + 1) * sb), slice(j * sb, j * sb + bw)
            s = _nt(q_ref[rows, :], kw[win]) * scale + b_ref[0]
            kpos = i * tq + j * sb - DIL_W + col
            p = jnp.where((kpos >= 0) & (kpos < l), jnp.exp(s - l_ref[rows, 0:1]), 0.0)
            ds = p * (_nt(do_ref[rows, :], vw[win]) - c_ref[rows, 0:1])
            dq_ref[rows, :] = _nn(ds.astype(BF16), kw[win]) * scale
            ds_sum = ds_sum + ds

        @pl.when((r == 0) & (i == 0))
        def _():
            ds_ref[0] = ds_sum

        @pl.when((r > 0) | (i > 0))
        def _():
            ds_ref[0] += ds_sum

    _, q_spec, _ = _dil_specs(tq, nt, 0, 12)
    t_spec = pl.BlockSpec((tq, LANES), lambda h, r, i: (i, r * H + h))
    b_spec = pl.BlockSpec((1, sb, bw), lambda h, r, i: (h, 0, 0))
    return pl.pallas_call(
        body, name=name, grid=(H, d, nt),
        in_specs=[q_spec, *_dil_specs(tq, nt, 4, 12), *_dil_specs(tq, nt, 8, 12), b_spec, t_spec, t_spec, t_spec],
        out_specs=[t_spec, b_spec], out_shape=[SDS((l, d * DIL_WIDTH), F32), SDS((H, sb, bw), F32)],
        compiler_params=_cp("parallel", "arbitrary", "arbitrary"))(qkv, qkv, qkv, qkv, qkv, qkv, qkv, bias, do, lse, corr)


def dil_bwd_kv(qkv, bias_t, do, lse, corr, *, d, name):
    l = qkv.shape[0]
    tq = _tile(l, DIL_TQ)
    nt = l // tq
    H = DIL_HEADS
    scale = DIL_HEAD_DIM ** -0.5

    sb, bw = _dil_band(tq)

    def body(k_ref, v_ref, qp, qo, qn, dp, do_, dn, lp, lo, ln, cp, co, cn, b_ref, dk_ref, dv_ref):
        i = pl.program_id(2)
        qw = _dil_window(qp, qo, qn, tq)
        dow = _dil_window(dp, do_, dn, tq)
        lrow = _dil_window(lp, lo, ln, tq).T[0:1, :]
        crow = _dil_window(cp, co, cn, tq).T[0:1, :]
        col = lax.broadcasted_iota(jnp.int32, (sb, bw), 1)
        for j in range(tq // sb):
            rows, win = slice(j * sb, (j + 1) * sb), slice(j * sb, j * sb + bw)
            st = _nt(k_ref[rows, :], qw[win]) * scale + b_ref[0]
            qpos = i * tq + j * sb - DIL_W + col
            pt = jnp.where((qpos >= 0) & (qpos < l), jnp.exp(st - lrow[:, win]), 0.0)
            dv_ref[rows, :] = _nn(pt.astype(BF16), dow[win])
            dst = pt * (_nt(v_ref[rows, :], dow[win]) - crow[:, win])
            dk_ref[rows, :] = _nn(dst.astype(BF16), qw[win]) * scale

    _, k_spec, _ = _dil_specs(tq, nt, 4, 12)
    _, v_spec, _ = _dil_specs(tq, nt, 8, 12)
    t_spec = pl.BlockSpec((tq, LANES), lambda h, r, i: (i, r * H + h))
    return pl.pallas_call(
        body, name=name, grid=(H, d, nt),
        in_specs=[k_spec, v_spec, *_dil_specs(tq, nt, 0, 12), *_dil_specs(tq, nt, 0, 4), *_dil_specs(tq, nt, 0, 4),
                  *_dil_specs(tq, nt, 0, 4), pl.BlockSpec((1, sb, bw), lambda h, r, i: (h, 0, 0))],
        out_specs=[t_spec, t_spec], out_shape=[SDS((l, d * DIL_WIDTH), F32), SDS((l, d * DIL_WIDTH), F32)],
        compiler_params=_cp("parallel", "parallel", "arbitrary"))(
            qkv, qkv, qkv, qkv, qkv, do, do, do, lse, lse, lse, corr, corr, corr, bias_t)


def _dil_weights(l1, l2, l3):
    m = jnp.maximum(jnp.maximum(l1, l2), l3)
    e = [jnp.exp(x - m) for x in (l1, l2, l3)]
    tot = e[0] + e[1] + e[2]
    return [x / tot for x in e]


def dil_combine(os_, ls_, gain, *, name):
    S_ = os_[0].shape[0]
    T = _tile(S_, 512)

    def body(o1, o2, o3, l1, l2, l3, g_ref, y_ref):
        w = _dil_weights(l1[...], l2[...], l3[...])
        o = w[0] * o1[...] + w[1] * o2[...] + w[2] * o3[...]
        for h, (ohat, _) in enumerate(_head_norm_fwd(o, DIL_HEAD_DIM)):
            cs = slice(h * DIL_HEAD_DIM, (h + 1) * DIL_HEAD_DIM)
            y_ref[:, cs] = (ohat * g_ref[:, cs]).astype(BF16)

    blk = pl.BlockSpec((T, DIL_WIDTH), lambda i: (i, 0))
    return pl.pallas_call(
        body, name=name, grid=(S_ // T,), in_specs=[blk] * 6 + [pl.BlockSpec((1, DIL_WIDTH), lambda i: (0, 0))],
        out_specs=blk, out_shape=SDS((S_, DIL_WIDTH), BF16), compiler_params=_cp("parallel"))(*os_, *ls_, gain)


def dil_combine_bwd(dmix, os_, ls_, gain, *, name):
    S_ = os_[0].shape[0]
    T = _tile(S_, 512)
    E = DIL_HEAD_DIM

    def body(dy_ref, o1, o2, o3, l1, l2, l3, g_ref, d1, d2, d3, c1, c2, c3, gg_ref):
        @pl.when(pl.program_id(0) == 0)
        def _():
            gg_ref[...] = jnp.zeros_like(gg_ref)
        w = _dil_weights(l1[...], l2[...], l3[...])
        o = w[0] * o1[...] + w[1] * o2[...] + w[2] * o3[...]
        for h, (ohat, rs) in enumerate(_head_norm_fwd(o, E)):
            cs = slice(h * E, (h + 1) * E)
            dy = dy_ref[:, cs].astype(F32)
            u = dy * g_ref[:, cs]
            do = rs * (u - ohat * jnp.mean(u * ohat, axis=-1, keepdims=True))
            gg_ref[:, cs] += jnp.sum(dy * ohat, axis=0, keepdims=True)
            tot = jnp.sum(do * o[:, cs], axis=-1, keepdims=True)
            for wr, d_ref, c_ref in zip(w, (d1, d2, d3), (c1, c2, c3)):
                d_ref[:, cs] = (wr[:, cs] * do).astype(BF16)
                c_ref[:, cs] = wr[:, cs] * tot

    blk = pl.BlockSpec((T, DIL_WIDTH), lambda i: (i, 0))
    vec = pl.BlockSpec((1, DIL_WIDTH), lambda i: (0, 0))
    return pl.pallas_call(
        body, name=name, grid=(S_ // T,),
        in_specs=[pl.BlockSpec((T, DIL_WIDTH), lambda i: (i, V_W // DIL_WIDTH))] + [blk] * 6 + [vec],
        out_specs=[blk] * 6 + [vec],
        out_shape=[SDS((S_, DIL_WIDTH), BF16)] * 3 + [SDS((S_, DIL_WIDTH), F32)] * 3 + [SDS((1, DIL_WIDTH), F32)],
        compiler_params=_cp("arbitrary"))(dmix, *os_, *ls_, gain)


def _mem_softmax(q, k, scale):
    s = _nt(q, k) * scale
    e = jnp.exp(s - jnp.max(s, axis=-1, keepdims=True))
    return e / jnp.sum(e, axis=-1, keepdims=True)


def mem_fwd(proj, kv, gain, *, name):
    S_ = proj.shape[0]
    M = kv.shape[0]
    T = _tile(S_, 1024)
    E, H = MEM_HEAD_DIM, MEM_HEADS
    scale = E ** -0.5

    def body(q_ref, k_ref, v_ref, g_ref, y_ref):
        p = _mem_softmax(q_ref[...], k_ref[...], scale)
        o = _nn(p.astype(BF16), v_ref[...])
        rs = lax.rsqrt(jnp.mean(o * o, axis=-1, keepdims=True) + EPS)
        y_ref[...] = (o * rs * g_ref[...]).astype(BF16)

    return pl.pallas_call(
        body, name=name, grid=(H, S_ // T),
        in_specs=[pl.BlockSpec((T, E), lambda h, i: (i, C_MQ // E + h)), pl.BlockSpec((M, E), lambda h, i: (0, h)),
                  pl.BlockSpec((M, E), lambda h, i: (0, H + h)), pl.BlockSpec((1, E), lambda h, i: (0, h))],
        out_specs=pl.BlockSpec((T, E), lambda h, i: (i, h)), out_shape=SDS((S_, MEM_WIDTH), BF16),
        compiler_params=_cp("parallel", "parallel"))(proj, kv, kv, gain)


def mem_bwd(dmix, proj, kv, gain, *, name):
    S_ = proj.shape[0]
    M = kv.shape[0]
    T = _tile(S_, 1024)
    E, H = MEM_HEAD_DIM, MEM_HEADS
    scale = E ** -0.5

    def body(dy_ref, q_ref, k_ref, v_ref, g_ref, dq_ref, dk_ref, dv_ref, gg_ref):
        @pl.when(pl.program_id(1) == 0)
        def _():
            dk_ref[...] = jnp.zeros_like(dk_ref)
            dv_ref[...] = jnp.zeros_like(dv_ref)
            gg_ref[...] = jnp.zeros_like(gg_ref)
        q, k, v = q_ref[...], k_ref[...], v_ref[...]
        p = _mem_softmax(q, k, scale)
        pb = p.astype(BF16)
        o = _nn(pb, v)
        rs = lax.rsqrt(jnp.mean(o * o, axis=-1, keepdims=True) + EPS)
        ohat = o * rs
        dy = dy_ref[...].astype(F32)
        u = dy * g_ref[...]
        do = (rs * (u - ohat * jnp.mean(u * ohat, axis=-1, keepdims=True))).astype(BF16)
        gg_ref[...] += jnp.sum(dy * ohat, axis=0, keepdims=True)
        dv_ref[...] += _tn(pb, do)
        dp = _nt(do, v)
        ds = (p * (dp - jnp.sum(p * dp, axis=-1, keepdims=True))).astype(BF16)
        dq_ref[...] = (_nn(ds, k) * scale).astype(BF16)
        dk_ref[...] += _tn(ds, q) * scale

    return pl.pallas_call(
        body, name=name, grid=(H, S_ // T),
        in_specs=[pl.BlockSpec((T, E), lambda h, i: (i, (V_W + DIL_WIDTH) // E + h)),
                  pl.BlockSpec((T, E), lambda h, i: (i, C_MQ // E + h)), pl.BlockSpec((M, E), lambda h, i: (0, h)),
                  pl.BlockSpec((M, E), lambda h, i: (0, H + h)), pl.BlockSpec((1, E), lambda h, i: (0, h))],
        out_specs=[pl.BlockSpec((T, E), lambda h, i: (i, h)), pl.BlockSpec((M, E), lambda h, i: (0, h)),
                   pl.BlockSpec((M, E), lambda h, i: (0, h)), pl.BlockSpec((1, E), lambda h, i: (0, h))],
        out_shape=[SDS((S_, MEM_WIDTH), BF16), SDS((M, MEM_WIDTH), F32), SDS((M, MEM_WIDTH), F32),
                   SDS((1, MEM_WIDTH), F32)],
        compiler_params=_cp("parallel", "arbitrary"))(dmix, proj, kv, kv, gain)


def _strided(a, d):
    return a if d == 1 else a.reshape(a.shape[0] // d, d * a.shape[1])


def _unstrided(a, d):
    return a if d == 1 else a.reshape(a.shape[0] * d, a.shape[1] // d)


def _dil_tables(rel_bias, seq):
    tabs = []
    for d in DIL_DILATIONS:
        sb, _ = _dil_band(min(seq // d, DIL_TQ))
        vec = jnp.sum(jnp.asarray(_rel_onehot(d))[:, :, None] * rel_bias[None], axis=1).T
        tabs.append((_toeplitz(vec, sb, False), _toeplitz(vec, sb, True)))
    return tabs


def _rel_bias_grad(ds_sums, seq):
    g = jnp.zeros((REL_BUCKETS, DIL_HEADS), F32)
    for d, ds in zip(DIL_DILATIONS, ds_sums):
        gvec = _untoeplitz(ds, _dil_band(min(seq // d, DIL_TQ))[0])
        g = g + jnp.sum(jnp.asarray(_rel_onehot(d))[:, :, None] * gvec.T[:, None, :], axis=0)
    return g


def _pad_gate_up(up, row0):
    return jnp.zeros((LANES, QK_W), BF16).at[row0:row0 + GLA_GATE_RANK].set(up.astype(BF16))


def _layer_fwd(x, mem, p, tabs, l, carry):
    tag = f"l{l}"
    proj, hn1 = norm_mm(x, p["norm_mix"], p["w_in"], tm=TM_FWD, tn=IN_TILE, name=f"in_proj_{tag}", comm=carry.get("in"))
    o_f, st_f = gla_fwd(proj, p["up_f"], p["bias_f"], reverse=False, name=f"gla_fwd_f_{tag}")
    o_b, st_b = gla_fwd(proj, p["up_b"], p["bias_b"], reverse=True, name=f"gla_fwd_b_{tag}")
    gla_out = gla_post(o_f, o_b, proj, p["gla_norm"], name=f"gla_post_{tag}")
    qkv = proj[:, C_DQ:C_MQ]
    os_, ls_ = [], []
    for d, (bias, _) in zip(DIL_DILATIONS, tabs):
        o, lse = dil_fwd(_strided(qkv, d), bias, d=d, name=f"dil_fwd_d{d}_{tag}")
        os_.append(_unstrided(o, d))
        ls_.append(_unstrided(lse, d))
    dil_out = dil_combine(os_, ls_, p["dil_norm"], name=f"dil_combine_{tag}")
    kv, hmem = norm_mm(mem, p["mem_norm"], p["w_mem_kv"], tm=256, tn=1024, name=f"mem_kv_{tag}")
    mem_out = mem_fwd(proj, kv, p["mem_out_norm"], name=f"mem_fwd_{tag}")
    mixed = jnp.concatenate([gla_out, dil_out, mem_out], axis=1)
    x1 = mm_res(mixed, p["w_out"], x, relu2=False, tm=TM_FWD, tn=TN, tk=TK, name=f"out_proj_{tag}",
                comm=carry.get("out"))
    a, hn2 = norm_mm(x1, p["norm_mlp"], p["w_up"], tm=TM_FWD, tn=TN, name=f"up_proj_{tag}", comm=carry.get("up"))
    x2 = mm_res(a, p["w_down"], x1, relu2=True, tm=TM_FWD, tn=TN, tk=TK, name=f"down_proj_{tag}",
                comm=carry.get("down"))
    saved = dict(x=x, proj=proj, hn1=hn1, o_f=o_f, o_b=o_b, st_f=st_f, st_b=st_b, qkv=qkv, os=os_, ls=ls_,
                 kv=kv, hmem=hmem, mixed=mixed, x1=x1, a=a, hn2=hn2)
    return x2, saved


class GradScatter:
    SLOTS = {"down_x": [(1, "w_in")], "down_w": [(1, "w_out"), (1, "w_mem_kv")], "up_x": [(0, "w_down")],
             "in_x": [(0, "w_up")]}

    def __init__(self, exchange):
        self.exchange = exchange
        self.ready = {}
        self.sent = []

    def offer(self, l, name, g):
        self.ready[(l, name)] = g

    def _take(self, keys, name=None):
        keys = [k for k in keys if k in self.ready]
        if not (keys and self.exchange):
            return None
        ex = ChipExchange([self.ready.pop(k) for k in keys], scatter=True)
        self.sent.append((keys, ex))
        return ex

    def carrier(self, l, slot):
        return self._take([(l + dl, n) for dl, n in self.SLOTS[slot]])

    def flush(self, name):
        ex = self._take(sorted(self.ready))
        if ex is not None:
            ex.run(name)

    def received(self):
        return {k: r for keys, ex in self.sent for k, r in zip(keys, ex.result)}


def _quarters(g):
    return g.reshape(N_CHIPS, g.shape[0] // N_CHIPS, g.shape[1])


def _layer_bwd(dx2, mem, p, tabs, s, l, sink):
    tag = f"l{l}"
    seq = dx2.shape[0]
    g = {}
    da = mm_nt(dx2, p["w_down"], s["a"], tm=TM_FWD, tn=TN, name=f"down_bwd_x_{tag}", comm=sink.carrier(l, "down_x"))
    g_down = mm_tn(s["a"], dx2, relu2=True, tm=TM_W, tn=TN_W, tk=TK_W, out="rows", name=f"down_bwd_w_{tag}",
                   comm=sink.carrier(l, "down_w"))
    sink.offer(l, "w_down", _quarters(g_down))
    sink.offer(l, "w_up", mm_tn(s["hn2"], da, relu2=False, tm=TM_W, tn=TN_W, tk=TK_W, out="cols", name=f"up_bwd_w_{tag}"))
    dx1, g["norm_mlp"] = mm_nt_normbwd(da, p["w_up"], s["x1"], p["norm_mlp"], dx2, tm=TM_NORM_BWD, tk=TK_NORM_BWD,
                                       name=f"up_bwd_x_{tag}", comm=sink.carrier(l, "up_x"))
    dmix = mm_nt(dx1, p["w_out"], None, tm=TM_FWD, tn=TN, name=f"out_bwd_x_{tag}")
    sink.offer(l, "w_out", _quarters(mm_tn(s["mixed"], dx1, relu2=False, tm=TM_W, tn=TN_W, tk=TK_W, out="rows",
                                           name=f"out_bwd_w_{tag}")))
    proj = s["proj"]
    do, dr, g["gla_norm"] = gla_post_bwd(dmix, s["o_f"], s["o_b"], proj, p["gla_norm"], name=f"gla_post_bwd_{tag}")
    r_f = gla_bwd(proj, p["up_f"], p["bias_f"], s["st_f"], do, None, reverse=False, name=f"gla_bwd_f_{tag}")
    r_b = gla_bwd(proj, p["up_b"], p["bias_b"], s["st_b"], do, r_f[:4], reverse=True, name=f"gla_bwd_b_{tag}")
    dgq, dgk, dgv, dlr = r_b[:4]
    g["up_f"], g["bias_f"] = r_f[4][0:GLA_GATE_RANK], r_f[5]
    g["up_b"], g["bias_b"] = r_b[4][GLA_GATE_RANK:2 * GLA_GATE_RANK], r_b[5]
    *branch, g["dil_norm"] = dil_combine_bwd(dmix, s["os"], s["ls"], p["dil_norm"], name=f"dil_combine_bwd_{tag}")
    dos, corrs = branch[:3], branch[3:]
    ddq = ddk = ddv = None
    ds_sums = []
    for r, (d, (bias, bias_t)) in enumerate(zip(DIL_DILATIONS, tabs)):
        qkv_d = _strided(s["qkv"], d)
        do_d, lse_d, corr_d = _strided(dos[r], d), _strided(s["ls"][r], d), _strided(corrs[r], d)
        dq_d, ds_sum = dil_bwd_q(qkv_d, bias, do_d, lse_d, corr_d, d=d, name=f"dil_bwd_q_d{d}_{tag}")
        dk_d, dv_d = dil_bwd_kv(qkv_d, bias_t, do_d, lse_d, corr_d, d=d, name=f"dil_bwd_kv_d{d}_{tag}")
        ds_sums.append(ds_sum)
        dq_d, dk_d, dv_d = _unstrided(dq_d, d), _unstrided(dk_d, d), _unstrided(dv_d, d)
        ddq, ddk, ddv = (dq_d, dk_d, dv_d) if ddq is None else (ddq + dq_d, ddk + dk_d, ddv + dv_d)
    g["rel_bias"] = _rel_bias_grad(ds_sums, seq)
    dmq, dmk, dmv, g["mem_out_norm"] = mem_bwd(dmix, proj, s["kv"], p["mem_out_norm"], name=f"mem_bwd_{tag}")
    dkv = jnp.concatenate([dmk, dmv], axis=1).astype(BF16)
    sink.offer(l, "w_mem_kv", _quarters(mm_tn(s["hmem"], dkv, relu2=False, tm=TM_W, tn=TN_W, tk=TK_W, out="rows",
                                              name=f"mem_kv_bwd_w_{tag}")))
    _, g["mem_norm"] = mm_nt_normbwd(dkv, p["w_mem_kv"], mem, p["mem_norm"], jnp.zeros_like(mem), tm=TM_NORM_BWD,
                                     tk=TK_NORM_BWD, name=f"mem_kv_bwd_x_{tag}")
    dproj = jnp.concatenate([dgq, dgk, dgv, dr, ddq.astype(BF16), ddk.astype(BF16), ddv.astype(BF16), dmq, dlr,
                             jnp.zeros((seq, IN_PAD - C_LR - LANES), BF16)], axis=1)
    g_in = mm_tn(s["hn1"], dproj, relu2=False, tm=TM_W, tn=IN_TILE, tk=TK_W, name=f"in_bwd_w_{tag}")
    sink.offer(l, "w_in", _shard_w_in_grad(g_in))
    dx0, g["norm_mix"] = mm_nt_normbwd(dproj, p["w_in"], s["x"], p["norm_mix"], dx1, tm=TM_NORM_BWD, tk=IN_TILE,
                                       name=f"in_bwd_x_{tag}", comm=sink.carrier(l, "in_x"))
    return dx0, g


def _shard_w_in_grad(g):
    g = jnp.concatenate([g[:, :LR_COL], g[:, C_LR:C_LR + 2 * GLA_GATE_RANK], g[:, LR_COL:C_LR]], axis=1).astype(BF16)
    w = IN_WIDTH // N_CHIPS
    return jnp.stack([g[:, j * w:(j + 1) * w] for j in range(N_CHIPS)], axis=0)


def local_step(x, mem, target, depth, layer_weights, norm_final, rel_bias, fwd_carry, sink):
    tabs = _dil_tables(rel_bias, x.shape[0])
    saved, layers = [], []
    for l in range(depth):
        layers.append(layer_weights(l))
        x, s = _layer_fwd(x, mem, layers[l], tabs, l, fwd_carry(l))
        saved.append(s)
    dx, g_final, loss = loss_head(x, norm_final, target, tm=TM_LOSS, name="loss_head")
    grads = [None] * depth
    for l in reversed(range(depth)):
        dx, grads[l] = _layer_bwd(dx, mem, layers[l], tabs, saved[l], l, sink)
    return loss, dx, grads, g_final


def sibling_exchange(bufs, *, name):
    n = len(bufs)

    def body(*refs):
        ins, outs = refs[:n], refs[n:2 * n]
        send, recv = refs[2 * n:]
        x, y, c = _mesh_pos()
        cps = [pltpu.make_async_remote_copy(src_ref=ins[a], dst_ref=outs[a], send_sem=send.at[a], recv_sem=recv.at[a],
                                            device_id=(x, y, 1 - c), device_id_type=MESH) for a in range(n)]
        for cp in cps:
            cp.start()
        for cp in cps:
            cp.wait()

    return pl.pallas_call(
        body, name=name, in_specs=[HBM_SPEC] * n, out_specs=[HBM_SPEC] * n,
        out_shape=[SDS(b.shape, b.dtype) for b in bufs],
        scratch_shapes=[pltpu.SemaphoreType.DMA((n,)), pltpu.SemaphoreType.DMA((n,))])(*bufs)


def all_reduce_small(v, *, name):
    R = v.shape[0]

    def body(v_ref, o_ref, slots, send, recv):
        x, y, c = _mesh_pos()
        me = 4 * x + 2 * y + c
        flip = lambda p, f: 1 - p if f else p
        peers = [(flip(x, k & 4), flip(y, k & 2), flip(c, k & 1)) for k in range(1, N_DEV)]
        slots[pl.ds(me, 1)] = v_ref[...][None]
        cps = [pltpu.make_async_remote_copy(src_ref=v_ref, dst_ref=slots.at[me], send_sem=send.at[k], recv_sem=recv.at[k],
                                            device_id=peer, device_id_type=MESH) for k, peer in enumerate(peers)]
        for cp in cps:
            cp.start()
        for k, (px, py, pc) in enumerate(peers):
            pltpu.make_async_remote_copy(src_ref=v_ref, dst_ref=slots.at[4 * px + 2 * py + pc], send_sem=send.at[k],
                                         recv_sem=recv.at[k], device_id=(px, py, pc), device_id_type=MESH).wait_recv()
        for cp in cps:
            cp.wait_send()
        acc = slots[0]
        for s in range(1, N_DEV):
            acc = acc + slots[s]
        o_ref[...] = acc

    vm = pl.BlockSpec(memory_space=pltpu.VMEM)
    return pl.pallas_call(
        body, name=name, in_specs=[vm], out_specs=vm, out_shape=SDS(v.shape, F32),
        scratch_shapes=[pltpu.VMEM((N_DEV, R, LANES), F32), pltpu.SemaphoreType.DMA((N_DEV - 1,)),
                        pltpu.SemaphoreType.DMA((N_DEV - 1,))])(v)


def sum_chips(recv, *, name):
    _, R, C = recv.shape
    tr = _tile(R, 512)

    def body(r_ref, o_ref):
        acc = r_ref[0].astype(F32)
        for j in range(1, N_CHIPS):
            acc = acc + r_ref[j].astype(F32)
        o_ref[...] = acc

    return pl.pallas_call(
        body, name=name, grid=(R // tr,), in_specs=[pl.BlockSpec((N_CHIPS, tr, C), lambda i: (0, i, 0))],
        out_specs=pl.BlockSpec((tr, C), lambda i: (i, 0)), out_shape=SDS((R, C), F32),
        compiler_params=_cp("parallel"))(recv)


def adamw(parts, w, m, v, *, name):
    R, C = w.shape
    n = len(parts)
    tr = _tile(R, 512)
    while 2 * (n + 7) * tr * C * 4 > ELEMWISE_BLOCK_BYTES and tr % 16 == 0:
        tr //= 2
    c1 = 1.0 - ADAM_B1 ** ADAM_STEP
    c2 = 1.0 - ADAM_B2 ** ADAM_STEP

    def body(*refs):
        w_ref, m_ref, v_ref = refs[n:n + 3]
        g_ref, d_ref, nm_ref, nv_ref = refs[n + 3:]
        g = refs[0][...]
        for r in refs[1:n]:
            g = g + r[...]
        nm = ADAM_B1 * m_ref[...] + (1.0 - ADAM_B1) * g
        nv = ADAM_B2 * v_ref[...] + (1.0 - ADAM_B2) * (g * g)
        g_ref[...] = g
        nm_ref[...] = nm
        nv_ref[...] = nv
        d_ref[...] = -ADAM_LR * ((nm / c1) / (jnp.sqrt(nv / c2) + ADAM_EPS) + ADAM_WD * w_ref[...])

    blk = pl.BlockSpec((tr, C), lambda i: (i, 0))
    return pl.pallas_call(
        body, name=name, grid=(R // tr,), in_specs=[blk] * (n + 3), out_specs=[blk] * 4,
        out_shape=[SDS((R, C), F32)] * 4, compiler_params=_cp("parallel"))(*parts, w, m, v)


PACK_ROWS = 8


def _pack(parts):
    rows = []
    for a in parts:
        flat = a.reshape(-1).astype(F32)
        n = -(-flat.shape[0] // (PACK_ROWS * LANES)) * PACK_ROWS * LANES
        rows.append(jnp.pad(flat, (0, n - flat.shape[0])).reshape(-1, LANES))
    return jnp.concatenate(rows, axis=0)


def _unpack(slab, shapes):
    out, r = [], 0
    for shp in shapes:
        n = int(np.prod(shp))
        nr = -(-n // (PACK_ROWS * LANES)) * PACK_ROWS
        out.append(slab[r:r + nr].reshape(-1)[:n].reshape(shp))
        r += nr
    return out


SMALL = ["norm_mix", "gla_gate_bias_fwd", "gla_gate_bias_bwd", "gla_norm", "rel_bias", "dil_norm", "mem_norm",
         "mem_out_norm", "norm_mlp", "norm_final"]
GATE_UPS = ["gla_gate_up_fwd", "gla_gate_up_bwd"]
BIG = ["w_in", "w_mem_kv", "w_out", "w_up", "w_down"]
WEIGHTS = ["norm_mix", "w_in", "gla_gate_up_fwd", "gla_gate_bias_fwd", "gla_gate_up_bwd", "gla_gate_bias_bwd", "gla_norm",
           "rel_bias", "dil_norm", "mem_norm", "w_mem_kv", "mem_out_norm", "w_out", "norm_mlp", "w_up", "w_down",
           "norm_final"]


FWD_CARRY = {"in": ["w_in"], "out": ["w_out", "w_mem_kv"], "up": ["w_up"], "down": ["w_down"]}


def _kernel_layout(name, g):
    if name == "w_up":
        return g
    if name == "w_in":
        full = jnp.concatenate([g[j] for j in range(N_CHIPS)], axis=1)
        return jnp.concatenate([full[:, :LR_COL], full[:, LR_COL + 2 * GLA_GATE_RANK:],
                                full[:, LR_COL:LR_COL + 2 * GLA_GATE_RANK],
                                jnp.zeros((full.shape[0], IN_PAD - IN_WIDTH), BF16)], axis=1)
    return g.reshape(-1, g.shape[-1])


def kernel(x, mem, norm_mix, w_in, gla_gate_up_fwd, gla_gate_bias_fwd, gla_gate_up_bwd, gla_gate_bias_bwd, gla_norm, rel_bias, dil_norm, mem_norm, w_mem_kv, mem_out_norm, w_out, norm_mlp, w_up, w_down, norm_final, loss_target, m_norm_mix, m_w_in, m_gla_gate_up_fwd, m_gla_gate_bias_fwd, m_gla_gate_up_bwd, m_gla_gate_bias_bwd, m_gla_norm, m_rel_bias, m_dil_norm, m_mem_norm, m_w_mem_kv, m_mem_out_norm, m_w_out, m_norm_mlp, m_w_up, m_w_down, m_norm_final, v_norm_mix, v_w_in, v_gla_gate_up_fwd, v_gla_gate_bias_fwd, v_gla_gate_up_bwd, v_gla_gate_bias_bwd, v_gla_norm, v_rel_bias, v_dil_norm, v_mem_norm, v_w_mem_kv, v_mem_out_norm, v_w_out, v_norm_mlp, v_w_up, v_w_down, v_norm_final):
    vals = dict(locals())
    W = {n: vals[n] for n in WEIGHTS}
    M1 = {n: vals["m_" + n] for n in WEIGHTS}
    V2 = {n: vals["v_" + n] for n in WEIGHTS}
    depth = w_in.shape[0]
    chip = 2 * lax.axis_index("x") + lax.axis_index("y")

    wb = {n: W[n].astype(BF16) for n in BIG + GATE_UPS}
    first = ChipExchange([wb[n] for n in BIG + GATE_UPS], scatter=False, layers=[0] * len(BIG) + [None] * len(GATE_UPS))
    first.run("gather_layer0")
    gathered = {(0, n): r for n, r in zip(BIG, first.result)}
    gate_up = dict(zip(GATE_UPS, first.result[len(BIG):]))
    carried = {}
    row = lambda a: a.reshape(1, -1)

    def fwd_carry(l):
        if l + 1 == depth:
            return {}
        carried[l] = {slot: ChipExchange([wb[n] for n in names], scatter=False, layers=[l + 1] * len(names))
                      for slot, names in FWD_CARRY.items()}
        return carried[l]

    def layer_weights(l):
        if l > 0:
            for slot, names in FWD_CARRY.items():
                gathered.update({(l, n): r for n, r in zip(names, carried[l - 1][slot].result)})
        p = {n: _kernel_layout(n, gathered[(l, n)]) for n in BIG}
        full_up = lambda n: jnp.concatenate([gate_up[n][j, l] for j in range(N_CHIPS)], axis=1)
        p["up_f"] = _pad_gate_up(full_up("gla_gate_up_fwd"), 0)
        p["up_b"] = _pad_gate_up(full_up("gla_gate_up_bwd"), GLA_GATE_RANK)
        p["bias_f"], p["bias_b"] = row(gla_gate_bias_fwd[l]), row(gla_gate_bias_bwd[l])
        for n in ("norm_mix", "gla_norm", "dil_norm", "mem_norm", "mem_out_norm", "norm_mlp"):
            p[n] = row(W[n][l])
        return p

    sink = GradScatter(exchange=True)
    loss, grad_x, grads, g_final = local_step(x[0], mem[0], loss_target[0], depth, layer_weights, row(norm_final),
                                              rel_bias, fwd_carry, sink)
    sink.flush("scatter_last_grads")
    recv = sink.received()

    flat2 = lambda a: a.reshape(-1, a.shape[-1])
    part = [jnp.concatenate([sum_chips(recv[(l, n)].reshape(N_CHIPS, -1, recv[(l, n)].shape[-1]),
                                       name=f"sum_chips_{n}_l{l}") for l in range(depth)], axis=0) for n in BIG]
    other = sibling_exchange(part, name="swap_partial_sums")
    out = {}
    for n, pa, ob in zip(BIG, part, other):
        res = adamw([pa, ob], flat2(W[n]), flat2(M1[n]), flat2(V2[n]), name=f"adamw_{n}")
        out[n] = [r.reshape(W[n].shape) for r in res]

    small_g = {}
    for n, key in (("norm_mix", "norm_mix"), ("gla_gate_bias_fwd", "bias_f"), ("gla_gate_bias_bwd", "bias_b"),
                   ("gla_norm", "gla_norm"), ("dil_norm", "dil_norm"), ("mem_norm", "mem_norm"),
                   ("mem_out_norm", "mem_out_norm"), ("norm_mlp", "norm_mlp"),
                   ("gla_gate_up_fwd", "up_f"), ("gla_gate_up_bwd", "up_b")):
        small_g[n] = jnp.stack([grads[l][key].reshape(W[n].shape[1:] if n not in GATE_UPS else (GLA_GATE_RANK, QK_W))
                                for l in range(depth)], axis=0)
    small_g["rel_bias"] = sum(grads[l]["rel_bias"] for l in range(depth))
    small_g["norm_final"] = g_final.reshape(-1)
    names = SMALL + GATE_UPS
    slab = all_reduce_small(_pack([small_g[n] for n in names] + [loss[:, 0]]), name="all_reduce_small")
    *summed, loss_sum = _unpack(slab, [small_g[n].shape for n in names] + [(1,)])
    summed = dict(zip(names, summed))
    for n in GATE_UPS:
        summed[n] = lax.dynamic_slice_in_dim(summed[n], chip * LANES, LANES, axis=2)
    names_s = SMALL + GATE_UPS
    shapes = [W[n].shape for n in names_s]
    res = adamw([_pack([summed[n] for n in names_s])], _pack([W[n] for n in names_s]), _pack([M1[n] for n in names_s]),
                _pack([V2[n] for n in names_s]), name="adamw_small")
    for n, *r in zip(names_s, *[_unpack(t, shapes) for t in res]):
        out[n] = r

    return (loss_sum.reshape(()), grad_x[None], *[out[n][0] for n in WEIGHTS], *[out[n][1] for n in WEIGHTS],
            *[out[n][2] for n in WEIGHTS], *[out[n][3] for n in WEIGHTS])
```

```python
import functools
import math

import numpy as np
import jax
import jax.numpy as jnp
from jax import lax
from jax.experimental import pallas as pl
from jax.experimental.pallas import tpu as pltpu

F32, BF16 = jnp.float32, jnp.bfloat16
SDS = jax.ShapeDtypeStruct

DEPTH = 4
GLA_HEADS, GLA_DK, GLA_DV = 4, 128, 256
GLA_GATE_RANK = 16
GLA_GATE_NORMALIZER = 16.0
GLA_CHUNK = 64
DIL_HEADS, DIL_HEAD_DIM = 4, 128
DIL_DILATIONS = (1, 4, 16)
DIL_W = 64
MEM_HEADS, MEM_HEAD_DIM = 4, 128
REL_BUCKETS, REL_MAX_DISTANCE = 32, 1024
EPS = 1e-6
NEG_INF = -1e30
QK_W, V_W, DIL_WIDTH, MEM_WIDTH = 512, 1024, 512, 512
IN_WIDTH = 5152
LR_COL = 3072
IN_PAD = 5376
C_GQ, C_GK, C_GV, C_GR, C_DQ, C_MQ, C_LR = 0, 512, 1024, 2048, 3072, 4608, 5120
IN_TILE = 1792

ADAM_LR, ADAM_B1, ADAM_B2, ADAM_EPS, ADAM_WD, ADAM_STEP = 0.001, 0.9, 0.999, 1e-08, 0.01, 10

V7X_VMEM_BYTES = 64 * 1024 * 1024
VMEM_LIMIT = V7X_VMEM_BYTES * 7 // 8
LANES = 128

TM_FWD = 512
TN = 2048
TK = 2048
TM_NORM_BWD = 512
TK_NORM_BWD = 2048
NORM_BWD_ROWS = 64
TM_W, TN_W, TK_W = 1024, 1024, 2048
TM_LOSS = 256
ELEMWISE_BLOCK_BYTES = 24 * 1024 * 1024

NT_DIMS = (((1,), (1,)), ((), ()))
TN_DIMS = (((0,), (0,)), ((), ()))


def _nt(a, b):
    return lax.dot_general(a, b, NT_DIMS, preferred_element_type=F32)


def _tn(a, b):
    return lax.dot_general(a, b, TN_DIMS, preferred_element_type=F32)


def _nn(a, b):
    return jnp.dot(a, b, preferred_element_type=F32)


def _cp(*sem):
    return pltpu.CompilerParams(dimension_semantics=sem, vmem_limit_bytes=VMEM_LIMIT)


def _tile(n, t):
    t = min(n, t)
    assert n % t == 0, (n, t)
    return t


MESH = pl.DeviceIdType.MESH
HBM_SPEC = pl.BlockSpec(memory_space=pltpu.HBM)
N_CHIPS = 4
N_DEV = 8


def _mesh_pos():
    return lax.axis_index("x"), lax.axis_index("y"), lax.axis_index("c")


class ChipExchange:
    def __init__(self, bufs, *, scatter, layers=None, rows=None):
        self.bufs = list(bufs)
        self.n = n = len(self.bufs)
        self.scatter = scatter
        self.layers = list(layers) if layers is not None else [None] * n
        self.rows = list(rows) if rows is not None else [None] * n
        dma = pltpu.SemaphoreType.DMA
        if scatter:
            shp = [b.shape if r is None else (b.shape[0], r[1]) + b.shape[2:] for b, r in zip(self.bufs, self.rows)]
            self.out_shape = [SDS(s, b.dtype) for s, b in zip(shp, self.bufs)]
            self.scratch = [dma((3 * n,)), dma((3 * n,)), dma((n,))]
        else:
            shp = [b.shape if l is None else b.shape[1:] for b, l in zip(self.bufs, self.layers)]
            assert all(s[0] % 2 == 0 for s in shp), shp
            self.out_shape = [SDS((N_CHIPS,) + s, b.dtype) for s, b in zip(shp, self.bufs)]
            self.scratch = [dma((3 * n,)), dma((3 * n,)), dma((n,)), dma((3 * n,)), dma((3 * n,))]
        self.result = None

    def _plan(self, ins, outs, sems, finishing):
        x, y, c = _mesh_pos()
        me = 2 * x + y
        chips = [(1 - x, y), (x, 1 - y), (1 - x, 1 - y)]
        remote = lambda src, dst, ss, rs, k, dev: pltpu.make_async_remote_copy(
            src_ref=src, dst_ref=dst, send_sem=ss.at[k], recv_sem=rs.at[k], device_id=dev, device_id_type=MESH)
        kept, sent, landing, passed, handed = [], [], [], [], []
        for a in range(self.n):
            src = ins[a] if self.layers[a] is None else ins[a].at[self.layers[a]]
            if self.scatter:
                r = self.rows[a]
                pick = (lambda j: src.at[j]) if r is None else (lambda j: src.at[j, pl.ds(r[0], r[1])])
                kept.append(pltpu.make_async_copy(pick(me), outs[a].at[me], sems[2].at[a]))
                for j, (px, py) in enumerate(chips):
                    k, peer = 3 * a + j, 2 * px + py
                    sent.append(remote(pick(peer), outs[a].at[me], sems[0], sems[1], k, (px, py, c)))
                    if finishing:
                        landing.append(remote(pick(me), outs[a].at[peer], sems[0], sems[1], k, (px, py, c)))
                continue
            half = self.out_shape[a].shape[1] // 2
            mine, other = pl.ds(c * half, half), pl.ds((1 - c) * half, half)
            kept.append(pltpu.make_async_copy(src, outs[a].at[me], sems[2].at[a]))
            for j, (px, py) in enumerate(chips):
                k, peer = 3 * a + j, 2 * px + py
                sent.append(remote(src.at[mine], outs[a].at[me, mine], sems[0], sems[1], k, (px, py, c)))
                if finishing:
                    landing.append(remote(src.at[mine], outs[a].at[peer, mine], sems[0], sems[1], k, (px, py, c)))
                    passed.append(remote(outs[a].at[peer, mine], outs[a].at[peer, mine], sems[3], sems[4], k, (x, y, 1 - c)))
                    handed.append(remote(outs[a].at[peer, other], outs[a].at[peer, other], sems[3], sems[4], k,
                                         (x, y, 1 - c)))
        return kept, sent, landing, passed, handed

    def start(self, ins, outs, sems):
        kept, sent, _, _, _ = self._plan(ins, outs, sems, False)
        for cp in kept + sent:
            cp.start()

    def finish(self, ins, outs, sems):
        kept, sent, landing, passed, handed = self._plan(ins, outs, sems, True)
        for k, cp in enumerate(landing):
            cp.wait_recv()
            if passed:
                passed[k].start()
        for cp in handed:
            cp.wait_recv()
        for cp in sent + passed:
            cp.wait_send()
        for cp in kept:
            cp.wait()

    def run(self, name):
        n = self.n

        def body(*refs):
            self.start(refs[:n], refs[n:2 * n], refs[2 * n:])
            self.finish(refs[:n], refs[n:2 * n], refs[2 * n:])

        self.result = pl.pallas_call(body, name=name, in_specs=[HBM_SPEC] * n, out_specs=[HBM_SPEC] * n,
                                     out_shape=self.out_shape, scratch_shapes=self.scratch)(*self.bufs)
        return self.result


def _call(body, *, name, grid, in_specs, out_specs, out_shape, sem, args, scratch_shapes=(), comm=None):
    if comm is None:
        return pl.pallas_call(body, name=name, grid=grid, in_specs=list(in_specs), out_specs=list(out_specs),
                              out_shape=list(out_shape), scratch_shapes=list(scratch_shapes),
                              compiler_params=_cp(*sem))(*args)
    n_in, n_out, n_scr, nc = len(in_specs), len(out_shape), len(scratch_shapes), comm.n

    def wrapped(*refs):
        ins, cin = refs[:n_in], refs[n_in:n_in + nc]
        outs, cout = refs[n_in + nc:n_in + nc + n_out], refs[n_in + nc + n_out:n_in + 2 * nc + n_out]
        scr, csem = refs[n_in + 2 * nc + n_out:n_in + 2 * nc + n_out + n_scr], refs[n_in + 2 * nc + n_out + n_scr:]
        ids = [pl.program_id(d) for d in range(len(grid))]
        first = functools.reduce(jnp.logical_and, [i == 0 for i in ids])
        last = functools.reduce(jnp.logical_and, [i == g - 1 for i, g in zip(ids, grid)])

        @pl.when(first)
        def _():
            comm.start(cin, cout, csem)
        body(*ins, *outs, *scr)

        @pl.when(last)
        def _():
            comm.finish(cin, cout, csem)

    res = pl.pallas_call(
        wrapped, name=name, grid=grid, in_specs=list(in_specs) + [HBM_SPEC] * nc,
        out_specs=list(out_specs) + [HBM_SPEC] * nc, out_shape=list(out_shape) + comm.out_shape,
        scratch_shapes=list(scratch_shapes) + comm.scratch,
        compiler_params=_cp(*("arbitrary",) * len(grid)))(*args, *comm.bufs)
    comm.result = res[n_out:]
    return res[:n_out]


def _w_spec(w, rows, tn, col_of):
    if w.ndim == 2:
        tn = _tile(w.shape[1], tn)
        return w.shape[1], tn, pl.BlockSpec((rows, tn), lambda *g: (0, col_of(*g)))
    tn = _tile(w.shape[2], tn)
    per = w.shape[2] // tn
    return N_CHIPS * w.shape[2], tn, pl.BlockSpec((None, rows, tn), lambda *g: (col_of(*g) // per, 0, col_of(*g) % per))


def norm_mm(x, gain, w, *, tm, tn, name, comm=None):
    M, D = x.shape
    tm = _tile(M, tm)
    N, tn, w_spec = _w_spec(w, D, tn, lambda i, j: j)

    def body(x_ref, g_ref, w_ref, o_ref, hn_ref):
        @pl.when(pl.program_id(1) == 0)
        def _():
            xf = x_ref[...]
            r = lax.rsqrt(jnp.mean(xf * xf, axis=-1, keepdims=True) + EPS)
            hn_ref[...] = (xf * r * g_ref[...]).astype(BF16)
        o_ref[...] = _nn(hn_ref[...], w_ref[...]).astype(BF16)

    return _call(
        body, name=name, grid=(M // tm, N // tn),
        in_specs=[pl.BlockSpec((tm, D), lambda i, j: (i, 0)), pl.BlockSpec((1, D), lambda i, j: (0, 0)), w_spec],
        out_specs=[pl.BlockSpec((tm, tn), lambda i, j: (i, j)), pl.BlockSpec((tm, D), lambda i, j: (i, 0))],
        out_shape=[SDS((M, N), BF16), SDS((M, D), BF16)], sem=("parallel", "arbitrary"), args=(x, gain, w), comm=comm)


def mm_res(a, w, res, *, relu2, tm, tn, tk, name, comm=None):
    M, K = a.shape
    N = w.shape[1]
    tm, tn, tk = _tile(M, tm), _tile(N, tn), _tile(K, tk)

    def body(a_ref, w_ref, r_ref, o_ref):
        @pl.when(pl.program_id(2) == 0)
        def _():
            o_ref[...] = r_ref[...]
        av = a_ref[...]
        if relu2:
            af = jnp.maximum(av.astype(F32), 0.0)
            av = (af * af).astype(BF16)
        o_ref[...] += _nn(av, w_ref[...])

    return _call(
        body, name=name, grid=(M // tm, N // tn, K // tk),
        in_specs=[pl.BlockSpec((tm, tk), lambda i, j, k: (i, k)), pl.BlockSpec((tk, tn), lambda i, j, k: (k, j)),
                  pl.BlockSpec((tm, tn), lambda i, j, k: (i, j))],
        out_specs=[pl.BlockSpec((tm, tn), lambda i, j, k: (i, j))], out_shape=[SDS((M, N), F32)],
        sem=("parallel", "parallel", "arbitrary"), args=(a, w, res), comm=comm)[0]


def mm_nt(dy, w, a, *, tm, tn, name, comm=None):
    M, K = dy.shape
    N = w.shape[0]
    tm, tn = _tile(M, tm), _tile(N, tn)
    with_a = a is not None

    def body(*refs):
        if with_a:
            dy_ref, w_ref, a_ref, o_ref, dyb = refs
        else:
            dy_ref, w_ref, o_ref, dyb = refs

        @pl.when(pl.program_id(1) == 0)
        def _():
            dyb[...] = dy_ref[...].astype(BF16)
        r = _nt(dyb[...], w_ref[...])
        if with_a:
            r = r * (2.0 * jnp.maximum(a_ref[...].astype(F32), 0.0))
        o_ref[...] = r.astype(BF16)

    in_specs = [pl.BlockSpec((tm, K), lambda i, j: (i, 0)), pl.BlockSpec((tn, K), lambda i, j: (j, 0))]
    args = [dy, w]
    if with_a:
        in_specs.append(pl.BlockSpec((tm, tn), lambda i, j: (i, j)))
        args.append(a)
    return _call(
        body, name=name, grid=(M // tm, N // tn), in_specs=in_specs,
        out_specs=[pl.BlockSpec((tm, tn), lambda i, j: (i, j))], out_shape=[SDS((M, N), BF16)],
        scratch_shapes=[pltpu.VMEM((tm, K), BF16)], sem=("parallel", "arbitrary"), args=args, comm=comm)[0]


def mm_nt_normbwd(dy, w, x, gain, dx_in, *, tm, tk, name, comm=None):
    M, K = dy.shape
    D = x.shape[1]
    tm = _tile(M, tm)
    _, tk, w_spec = _w_spec(w, D, tk, lambda i, k: k)
    nk = K // tk
    rc = _tile(tm, NORM_BWD_ROWS)

    def body(dy_ref, w_ref, x_ref, g_ref, dxin_ref, dx_ref, gg_ref, acc):
        i, k = pl.program_id(0), pl.program_id(1)

        @pl.when(k == 0)
        def _():
            acc[...] = jnp.zeros_like(acc)
        acc[...] += _nt(dy_ref[...], w_ref[...])

        @pl.when(k == nk - 1)
        def _():
            def rows(c, part):
                rs = pl.ds(pl.multiple_of(c * rc, rc), rc)
                dh, xf = acc[rs, :], x_ref[rs, :]
                r = lax.rsqrt(jnp.mean(xf * xf, axis=-1, keepdims=True) + EPS)
                xhat = xf * r
                u = dh * g_ref[...]
                dx_ref[rs, :] = dxin_ref[rs, :] + r * (u - xhat * jnp.mean(u * xhat, axis=-1, keepdims=True))
                return part + jnp.sum(dh * xhat, axis=0, keepdims=True)

            part = lax.fori_loop(0, tm // rc, rows, jnp.zeros((1, D), F32))

            @pl.when(i == 0)
            def _():
                gg_ref[...] = part

            @pl.when(i > 0)
            def _():
                gg_ref[...] += part

    return _call(
        body, name=name, grid=(M // tm, nk),
        in_specs=[pl.BlockSpec((tm, tk), lambda i, k: (i, k)), w_spec,
                  pl.BlockSpec((tm, D), lambda i, k: (i, 0)), pl.BlockSpec((1, D), lambda i, k: (0, 0)),
                  pl.BlockSpec((tm, D), lambda i, k: (i, 0))],
        out_specs=[pl.BlockSpec((tm, D), lambda i, k: (i, 0)), pl.BlockSpec((1, D), lambda i, k: (0, 0))],
        out_shape=[SDS((M, D), F32), SDS((1, D), F32)], scratch_shapes=[pltpu.VMEM((tm, D), F32)],
        sem=("arbitrary", "arbitrary"), args=(dy, w, x, gain, dx_in), comm=comm)


def mm_tn(a, dy, *, relu2, tm, tn, tk, name, out="f32", comm=None):
    S_, Ka = a.shape
    N = dy.shape[1]
    tm, tk = _tile(Ka, tm), _tile(S_, tk)
    tn = _tile(N // N_CHIPS if out == "cols" else N, tn)
    nk = S_ // tk

    def body(a_ref, dy_ref, o_ref, acc):
        k = pl.program_id(2)

        @pl.when(k == 0)
        def _():
            acc[...] = jnp.zeros_like(acc)
        av = a_ref[...]
        if relu2:
            af = jnp.maximum(av.astype(F32), 0.0)
            av = (af * af).astype(BF16)
        acc[...] += _tn(av, dy_ref[...].astype(BF16))

        @pl.when(k == nk - 1)
        def _():
            o_ref[...] = acc[...].astype(o_ref.dtype)

    if out == "cols":
        per = N // N_CHIPS // tn
        o_spec = pl.BlockSpec((None, tm, tn), lambda i, j, k: (j // per, i, j % per))
        o_shape = SDS((N_CHIPS, Ka, N // N_CHIPS), BF16)
    else:
        o_spec = pl.BlockSpec((tm, tn), lambda i, j, k: (i, j))
        o_shape = SDS((Ka, N), F32 if out == "f32" else BF16)
    return _call(
        body, name=name, grid=(Ka // tm, N // tn, nk),
        in_specs=[pl.BlockSpec((tk, tm), lambda i, j, k: (k, i)), pl.BlockSpec((tk, tn), lambda i, j, k: (k, j))],
        out_specs=[o_spec], out_shape=[o_shape], scratch_shapes=[pltpu.VMEM((tm, tn), F32)],
        sem=("parallel", "parallel", "arbitrary"), args=(a, dy), comm=comm)[0]


def loss_head(x, gain, target, *, tm, name):
    M, D = x.shape
    tm = _tile(M, tm)

    def body(x_ref, g_ref, t_ref, dx_ref, gg_ref, l_ref):
        i = pl.program_id(0)
        xf = x_ref[...]
        r = lax.rsqrt(jnp.mean(xf * xf, axis=-1, keepdims=True) + EPS)
        xhat = xf * r
        e = xhat * g_ref[...] - t_ref[...]
        lpart = 0.5 * jnp.sum(jnp.mean(e * e, axis=-1, keepdims=True), axis=0, keepdims=True)
        dy = e * (1.0 / D)
        u = dy * g_ref[...]
        dx_ref[...] = r * (u - xhat * jnp.mean(u * xhat, axis=-1, keepdims=True))
        gpart = jnp.sum(dy * xhat, axis=0, keepdims=True)
        lrow = jnp.broadcast_to(lpart, (1, LANES))

        @pl.when(i == 0)
        def _():
            gg_ref[...] = gpart
            l_ref[...] = lrow

        @pl.when(i > 0)
        def _():
            gg_ref[...] += gpart
            l_ref[...] += lrow

    return pl.pallas_call(
        body, name=name, grid=(M // tm,),
        in_specs=[pl.BlockSpec((tm, D), lambda i: (i, 0)), pl.BlockSpec((1, D), lambda i: (0, 0)),
                  pl.BlockSpec((tm, D), lambda i: (i, 0))],
        out_specs=[pl.BlockSpec((tm, D), lambda i: (i, 0)), pl.BlockSpec((1, D), lambda i: (0, 0)),
                   pl.BlockSpec((1, LANES), lambda i: (0, 0))],
        out_shape=[SDS((M, D), F32), SDS((1, D), F32), SDS((1, LANES), F32)],
        compiler_params=_cp("arbitrary"))(x, gain, target)


def _log_sigmoid(z):
    return jnp.minimum(z, 0.0) - jnp.log(1.0 + jnp.exp(-jnp.abs(z)))


def _chunk_scan(x, row, reverse):
    n = x.shape[0]
    s = 1
    while s < GLA_CHUNK:
        if reverse:
            x = x + jnp.where(row < GLA_CHUNK - s, pltpu.roll(x, n - s, 0), 0.0)
        else:
            x = x + jnp.where(row >= s, pltpu.roll(x, s, 0), 0.0)
        s *= 2
    return x


def _chunk_edge(b, reverse):
    t = b.shape[0]
    nc = t // GLA_CHUNK
    b3 = b.reshape(nc, GLA_CHUNK, b.shape[1])
    e = b3[:, 0:1, :] if reverse else b3[:, GLA_CHUNK - 1:GLA_CHUNK, :]
    return jnp.broadcast_to(e, b3.shape).reshape(b.shape), e.reshape(nc, b.shape[1])


def _gla_gates(lr, up_ref, bias_ref, cs, row, reverse):
    z = _nn(lr, up_ref[:, cs]) + bias_ref[:, cs]
    g = _log_sigmoid(z) * (1.0 / GLA_GATE_NORMALIZER)
    b = _chunk_scan(g, row, reverse)
    bl, blc = _chunk_edge(b, reverse)
    return z, b, bl, blc


def _gla_mask(reverse):
    r = lax.broadcasted_iota(jnp.int32, (GLA_CHUNK, GLA_CHUNK), 0)
    c = lax.broadcasted_iota(jnp.int32, (GLA_CHUNK, GLA_CHUNK), 1)
    return (r <= c) if reverse else (r >= c)


def gla_fwd(proj, upad, bias, *, reverse, name):
    S_ = proj.shape[0]
    T = _tile(S_, 512)
    nt, nc = S_ // T, T // GLA_CHUNK
    H, DK, DV, C = GLA_HEADS, GLA_DK, GLA_DV, GLA_CHUNK
    tix = (lambda i: nt - 1 - i) if reverse else (lambda i: i)
    qscale = DK ** -0.5

    def body(q_ref, k_ref, v_ref, lr_ref, up_ref, b_ref, o_ref, st_ref, st_s, qd_s, ki_s, ke_s, dec_s):
        @pl.when(pl.program_id(0) == 0)
        def _():
            st_s[...] = jnp.zeros_like(st_s)
        row = lax.broadcasted_iota(jnp.int32, (T, DK), 0) % C
        mask = _gla_mask(reverse)
        lr = lr_ref[...]
        for h in range(H):
            cs = slice(h * DK, (h + 1) * DK)
            _, b, bl, _ = _gla_gates(lr, up_ref, b_ref, cs, row, reverse)
            q = q_ref[:, cs].astype(F32) * qscale
            k = k_ref[:, cs].astype(F32)
            qd_s[:, cs] = (q * jnp.exp(b)).astype(BF16)
            ki_s[:, cs] = (k * jnp.exp(-b)).astype(BF16)
            ke_s[:, cs] = (k * jnp.exp(bl - b)).astype(BF16)
            dec_s[:, cs] = jnp.exp(bl)

        def chunk(ci, carry):
            c = (nc - 1 - ci) if reverse else ci
            r0 = pl.multiple_of(c * C, C)
            s0 = pl.multiple_of(c * DV, DV)
            for h in range(H):
                cs = slice(h * DK, (h + 1) * DK)
                vs = slice(h * DV, (h + 1) * DV)
                qd, ki, ke = qd_s[pl.ds(r0, C), cs], ki_s[pl.ds(r0, C), cs], ke_s[pl.ds(r0, C), cs]
                v = v_ref[pl.ds(r0, C), vs]
                st = st_s[h]
                stb = st.astype(BF16)
                st_ref[h, pl.ds(s0, DV), :] = stb
                a = jnp.where(mask, _nt(qd, ki), 0.0).astype(BF16)
                o_ref[pl.ds(r0, C), vs] = _nn(a, v) + _nt(qd, stb)
                st_s[h] = dec_s[pl.ds(r0, 1), cs] * st + _tn(v, ke)
            return carry

        lax.fori_loop(0, nc, chunk, 0)

    return pl.pallas_call(
        body, name=name, grid=(nt,),
        in_specs=[pl.BlockSpec((T, QK_W), lambda i: (tix(i), C_GQ // QK_W)),
                  pl.BlockSpec((T, QK_W), lambda i: (tix(i), C_GK // QK_W)),
                  pl.BlockSpec((T, V_W), lambda i: (tix(i), C_GV // V_W)),
                  pl.BlockSpec((T, LANES), lambda i: (tix(i), C_LR // LANES)),
                  pl.BlockSpec((LANES, QK_W), lambda i: (0, 0)), pl.BlockSpec((1, QK_W), lambda i: (0, 0))],
        out_specs=[pl.BlockSpec((T, V_W), lambda i: (tix(i), 0)),
                   pl.BlockSpec((H, nc * DV, DK), lambda i: (0, tix(i), 0))],
        out_shape=[SDS((S_, V_W), F32), SDS((H, S_ // C * DV, DK), BF16)],
        scratch_shapes=[pltpu.VMEM((H, DV, DK), F32), pltpu.VMEM((T, QK_W), BF16), pltpu.VMEM((T, QK_W), BF16),
                        pltpu.VMEM((T, QK_W), BF16), pltpu.VMEM((T, QK_W), F32)],
        compiler_params=_cp("arbitrary"))(proj, proj, proj, proj, upad, bias)


def gla_bwd(proj, upad, bias, states, do, prev, *, reverse, name):
    S_ = proj.shape[0]
    T = _tile(S_, 512)
    nt, nc = S_ // T, T // GLA_CHUNK
    H, DK, DV, C = GLA_HEADS, GLA_DK, GLA_DV, GLA_CHUNK
    tix = (lambda i: i) if reverse else (lambda i: nt - 1 - i)
    qscale = DK ** -0.5
    with_prev = prev is not None
    NCP = max(8, nc)

    def body(*refs):
        (q_ref, k_ref, v_ref, lr_ref, up_ref, b_ref, st_ref, do_ref) = refs[:8]
        n_in = 12 if with_prev else 8
        pq_ref, pk_ref, pv_ref, plr_ref = refs[8:12] if with_prev else (None,) * 4
        dq_ref, dk_ref, dv_ref, dlr_ref, dup_ref, dbias_ref = refs[n_in:n_in + 6]
        dst_s, qd_s, ki_s, ke_s, dec_s, dqd_s, dki_s, dke_s, ddec_s = refs[n_in + 6:]
        first = pl.program_id(0) == 0

        @pl.when(first)
        def _():
            dst_s[...] = jnp.zeros_like(dst_s)
            dup_ref[...] = jnp.zeros_like(dup_ref)
            dbias_ref[...] = jnp.zeros_like(dbias_ref)

        row = lax.broadcasted_iota(jnp.int32, (T, DK), 0) % C
        edge_row = 0 if reverse else C - 1
        mask = _gla_mask(reverse)
        lr = lr_ref[...]
        dlr = plr_ref[...].astype(F32) if with_prev else jnp.zeros((T, LANES), F32)

        def decayed(h):
            cs = slice(h * DK, (h + 1) * DK)
            z, b, bl, blc = _gla_gates(lr, up_ref, b_ref, cs, row, reverse)
            q = q_ref[:, cs].astype(F32) * qscale
            k = k_ref[:, cs].astype(F32)
            eb, einv, eend = jnp.exp(b), jnp.exp(-b), jnp.exp(bl - b)
            return z, bl, blc, eb, einv, eend, q * eb, k * einv, k * eend

        for h in range(H):
            cs = slice(h * DK, (h + 1) * DK)
            _, bl, _, _, _, _, qd, ki, ke = decayed(h)
            qd_s[:, cs] = qd.astype(BF16)
            ki_s[:, cs] = ki.astype(BF16)
            ke_s[:, cs] = ke.astype(BF16)
            dec_s[:, cs] = jnp.exp(bl)

        def chunk(ci, carry):
            c = ci if reverse else (nc - 1 - ci)
            r0 = pl.multiple_of(c * C, C)
            s0 = pl.multiple_of(c * DV, DV)
            for h in range(H):
                cs = slice(h * DK, (h + 1) * DK)
                vs = slice(h * DV, (h + 1) * DV)
                qdc, kic, kec = qd_s[pl.ds(r0, C), cs], ki_s[pl.ds(r0, C), cs], ke_s[pl.ds(r0, C), cs]
                v = v_ref[pl.ds(r0, C), vs]
                doc = do_ref[pl.ds(r0, C), vs]
                st = st_ref[h, pl.ds(s0, DV), :]
                dst = dst_s[h]
                dstb = dst.astype(BF16)
                a = jnp.where(mask, _nt(qdc, kic), 0.0).astype(BF16)
                da = jnp.where(mask, _nt(doc, v), 0.0).astype(BF16)
                dvc = _tn(a, doc) + _nt(kec, dstb)
                if with_prev:
                    dvc = dvc + pv_ref[pl.ds(r0, C), vs].astype(F32)
                dv_ref[pl.ds(r0, C), vs] = dvc.astype(BF16)
                dqd_s[pl.ds(r0, C), cs] = _nn(da, kic) + _nn(doc, st)
                dki_s[pl.ds(r0, C), cs] = _tn(da, qdc)
                dke_s[pl.ds(r0, C), cs] = _nn(v, dstb)
                ddec_s[h, pl.ds(c, 1), :] = jnp.sum(dst * st.astype(F32), axis=0, keepdims=True)
                dst_s[h] = dec_s[pl.ds(r0, 1), cs] * dst + _tn(doc, qdc)
            return carry

        lax.fori_loop(0, nc, chunk, 0)

        for h in range(H):
            cs = slice(h * DK, (h + 1) * DK)
            z, bl, blc, eb, einv, eend, qd, ki, ke = decayed(h)
            dqd, dki, dke = dqd_s[:, cs], dki_s[:, cs], dke_s[:, cs]
            dq = dqd * eb * qscale
            dk = dki * einv + dke * eend
            if with_prev:
                dq = dq + pq_ref[:, cs].astype(F32)
                dk = dk + pk_ref[:, cs].astype(F32)
            dq_ref[:, cs] = dq.astype(BF16)
            dk_ref[:, cs] = dk.astype(BF16)
            wke = dke * ke
            db = dqd * qd - dki * ki - wke
            dbl = jnp.sum(wke.reshape(nc, C, DK), axis=1) + ddec_s[h, 0:nc, :] * jnp.exp(blc)
            dbl_b = jnp.broadcast_to(dbl.reshape(nc, 1, DK), (nc, C, DK)).reshape(T, DK)
            db = db + jnp.where(row == edge_row, dbl_b, 0.0)
            dg = _chunk_scan(db, row, not reverse)
            dz = dg * (1.0 / GLA_GATE_NORMALIZER) * (1.0 / (1.0 + jnp.exp(z)))
            dzb = dz.astype(BF16)
            dbias_ref[:, cs] += jnp.sum(dz, axis=0, keepdims=True)
            dup_ref[:, cs] += _tn(lr, dzb)
            dlr = dlr + _nt(dzb, up_ref[:, cs])
        dlr_ref[...] = dlr.astype(BF16)

    tile = lambda w, cb: pl.BlockSpec((T, w), lambda i: (tix(i), cb))
    in_specs = [tile(QK_W, C_GQ // QK_W), tile(QK_W, C_GK // QK_W), tile(V_W, C_GV // V_W), tile(LANES, C_LR // LANES),
                pl.BlockSpec((LANES, QK_W), lambda i: (0, 0)), pl.BlockSpec((1, QK_W), lambda i: (0, 0)),
                pl.BlockSpec((H, nc * DV, DK), lambda i: (0, tix(i), 0)), tile(V_W, 0)]
    args = [proj, proj, proj, proj, upad, bias, states, do]
    if with_prev:
        in_specs += [tile(QK_W, 0), tile(QK_W, 0), tile(V_W, 0), tile(LANES, 0)]
        args += list(prev)
    return pl.pallas_call(
        body, name=name, grid=(nt,), in_specs=in_specs,
        out_specs=[tile(QK_W, 0), tile(QK_W, 0), tile(V_W, 0), tile(LANES, 0),
                   pl.BlockSpec((LANES, QK_W), lambda i: (0, 0)), pl.BlockSpec((1, QK_W), lambda i: (0, 0))],
        out_shape=[SDS((S_, QK_W), BF16), SDS((S_, QK_W), BF16), SDS((S_, V_W), BF16), SDS((S_, LANES), BF16),
                   SDS((LANES, QK_W), F32), SDS((1, QK_W), F32)],
        scratch_shapes=[pltpu.VMEM((H, DV, DK), F32), pltpu.VMEM((T, QK_W), BF16), pltpu.VMEM((T, QK_W), BF16),
                        pltpu.VMEM((T, QK_W), BF16), pltpu.VMEM((T, QK_W), F32), pltpu.VMEM((T, QK_W), F32),
                        pltpu.VMEM((T, QK_W), F32), pltpu.VMEM((T, QK_W), F32), pltpu.VMEM((H, NCP, DK), F32)],
        compiler_params=_cp("arbitrary"))(*args)


def _head_norm_fwd(o, width):
    out = []
    for h in range(o.shape[1] // width):
        oh = o[:, h * width:(h + 1) * width]
        rs = lax.rsqrt(jnp.mean(oh * oh, axis=-1, keepdims=True) + EPS)
        out.append((oh * rs, rs))
    return out


def gla_post(o_f, o_b, proj, gain, *, name):
    S_ = proj.shape[0]
    T = _tile(S_, 512)

    def body(of_ref, ob_ref, r_ref, g_ref, y_ref):
        o = of_ref[...] + ob_ref[...]
        for h, (ohat, _) in enumerate(_head_norm_fwd(o, GLA_DV)):
            vs = slice(h * GLA_DV, (h + 1) * GLA_DV)
            r = r_ref[:, vs].astype(F32)
            y_ref[:, vs] = (ohat * g_ref[:, vs] * (r / (1.0 + jnp.exp(-r)))).astype(BF16)

    blk = pl.BlockSpec((T, V_W), lambda i: (i, 0))
    return pl.pallas_call(
        body, name=name, grid=(S_ // T,),
        in_specs=[blk, blk, pl.BlockSpec((T, V_W), lambda i: (i, C_GR // V_W)), pl.BlockSpec((1, V_W), lambda i: (0, 0))],
        out_specs=blk, out_shape=SDS((S_, V_W), BF16), compiler_params=_cp("parallel"))(o_f, o_b, proj, gain)


def gla_post_bwd(dmix, o_f, o_b, proj, gain, *, name):
    S_ = proj.shape[0]
    T = _tile(S_, 512)

    def body(dy_ref, of_ref, ob_ref, r_ref, g_ref, do_ref, dr_ref, gg_ref):
        @pl.when(pl.program_id(0) == 0)
        def _():
            gg_ref[...] = jnp.zeros_like(gg_ref)
        o = of_ref[...] + ob_ref[...]
        for h, (ohat, rs) in enumerate(_head_norm_fwd(o, GLA_DV)):
            vs = slice(h * GLA_DV, (h + 1) * GLA_DV)
            r = r_ref[:, vs].astype(F32)
            dy = dy_ref[:, vs].astype(F32)
            sg = 1.0 / (1.0 + jnp.exp(-r))
            gate = r * sg
            n = ohat * g_ref[:, vs]
            dr_ref[:, vs] = (dy * n * (sg * (1.0 + r * (1.0 - sg)))).astype(BF16)
            dn = dy * gate
            u = dn * g_ref[:, vs]
            do_ref[:, vs] = (rs * (u - ohat * jnp.mean(u * ohat, axis=-1, keepdims=True))).astype(BF16)
            gg_ref[:, vs] += jnp.sum(dn * ohat, axis=0, keepdims=True)

    blk = pl.BlockSpec((T, V_W), lambda i: (i, 0))
    vec = pl.BlockSpec((1, V_W), lambda i: (0, 0))
    return pl.pallas_call(
        body, name=name, grid=(S_ // T,),
        in_specs=[blk, blk, blk, pl.BlockSpec((T, V_W), lambda i: (i, C_GR // V_W)), vec],
        out_specs=[blk, blk, vec], out_shape=[SDS((S_, V_W), BF16), SDS((S_, V_W), BF16), SDS((1, V_W), F32)],
        compiler_params=_cp("arbitrary"))(dmix, o_f, o_b, proj, gain)


def _t5_bucket_np(rel):
    half = REL_BUCKETS // 2
    max_exact = half // 2
    ret = np.where(rel > 0, half, 0)
    n = np.abs(rel)
    nf = np.maximum(n, 1).astype(np.float32)
    large = max_exact + (np.log(nf / np.float32(max_exact)) / np.float32(math.log(REL_MAX_DISTANCE / max_exact))
                         * np.float32(half - max_exact)).astype(np.int32)
    large = np.minimum(large, half - 1)
    return ret + np.where(n < max_exact, n, large)


def _rel_onehot(dilation):
    rel = np.arange(-DIL_W, DIL_W + 1)
    return np.eye(REL_BUCKETS, dtype=np.float32)[_t5_bucket_np(rel * dilation)]


def _toeplitz(vec, tq, flip):
    H = vec.shape[0]
    wlen = tq + 2 * DIL_W
    lu = wlen + tq
    if flip:
        vec = vec[:, ::-1]
    u = jnp.full((H, lu), NEG_INF, F32)
    u = lax.dynamic_update_slice(u, vec, (0, 0))
    t = jnp.tile(u, (1, tq))[:, :tq * (lu - 1)].reshape(H, tq, lu - 1)
    return t[:, :, :wlen]


def _untoeplitz(ds, tq):
    H = ds.shape[0]
    wlen = tq + 2 * DIL_W
    lu = wlen + tq
    p = jnp.pad(ds, ((0, 0), (0, 0), (0, lu - 1 - wlen))).reshape(H, tq * (lu - 1))
    p = jnp.pad(p, ((0, 0), (0, tq))).reshape(H, tq, lu)
    return jnp.sum(p, axis=1)[:, :2 * DIL_W + 1]


DIL_TQ = 1024
DIL_SUB = 128


def _dil_band(tq):
    sb = min(tq, DIL_SUB)
    return sb, sb + 2 * DIL_W


def _dil_window(p_ref, o_ref, n_ref, tq):
    return jnp.concatenate([p_ref[tq - DIL_W:, :], o_ref[...], n_ref[:DIL_W, :]], axis=0)


def _dil_specs(tq, nt, col0, ncols_per_r):
    cb = lambda h, r: r * ncols_per_r + col0 + h
    own = pl.BlockSpec((tq, LANES), lambda h, r, i: (i, cb(h, r)))
    prev = pl.BlockSpec((tq, LANES), lambda h, r, i: (jnp.maximum(i - 1, 0), cb(h, r)))
    nxt = pl.BlockSpec((tq, LANES), lambda h, r, i: (jnp.minimum(i + 1, nt - 1), cb(h, r)))
    return prev, own, nxt


def dil_fwd(qkv, bias, *, d, name):
    l = qkv.shape[0]
    tq = _tile(l, DIL_TQ)
    nt = l // tq
    H = DIL_HEADS
    scale = DIL_HEAD_DIM ** -0.5

    sb, bw = _dil_band(tq)

    def body(q_ref, kp, ko, kn, vp, vo, vn, b_ref, o_ref, l_ref):
        i = pl.program_id(2)
        kw = _dil_window(kp, ko, kn, tq)
        vw = _dil_window(vp, vo, vn, tq)
        col = lax.broadcasted_iota(jnp.int32, (sb, bw), 1)
        for j in range(tq // sb):
            rows, win = slice(j * sb, (j + 1) * sb), slice(j * sb, j * sb + bw)
            s = _nt(q_ref[rows, :], kw[win]) * scale + b_ref[0]
            kpos = i * tq + j * sb - DIL_W + col
            s = jnp.where((kpos >= 0) & (kpos < l), s, NEG_INF)
            m = jnp.max(s, axis=-1, keepdims=True)
            p = jnp.exp(s - m)
            den = jnp.sum(p, axis=-1, keepdims=True)
            o_ref[rows, :] = (_nn(p.astype(BF16), vw[win]) / den).astype(BF16)
            l_ref[rows, :] = jnp.broadcast_to(m + jnp.log(den), (sb, LANES))

    _, q_spec, _ = _dil_specs(tq, nt, 0, 12)
    out_spec = pl.BlockSpec((tq, LANES), lambda h, r, i: (i, r * H + h))
    return pl.pallas_call(
        body, name=name, grid=(H, d, nt),
        in_specs=[q_spec, *_dil_specs(tq, nt, 4, 12), *_dil_specs(tq, nt, 8, 12),
                  pl.BlockSpec((1, sb, bw), lambda h, r, i: (h, 0, 0))],
        out_specs=[out_spec, out_spec], out_shape=[SDS((l, d * DIL_WIDTH), BF16), SDS((l, d * DIL_WIDTH), F32)],
        compiler_params=_cp("parallel", "parallel", "arbitrary"))(qkv, qkv, qkv, qkv, qkv, qkv, qkv, bias)


def dil_bwd_q(qkv, bias, do, lse, corr, *, d, name):
    l = qkv.shape[0]
    tq = _tile(l, DIL_TQ)
    nt = l // tq
    H = DIL_HEADS
    scale = DIL_HEAD_DIM ** -0.5

    sb, bw = _dil_band(tq)

    def body(q_ref, kp, ko, kn, vp, vo, vn, b_ref, do_ref, l_ref, c_ref, dq_ref, ds_ref):
        r, i = pl.program_id(1), pl.program_id(2)
        kw = _dil_window(kp, ko, kn, tq)
        vw = _dil_window(vp, vo, vn, tq)
        col = lax.broadcasted_iota(jnp.int32, (sb, bw), 1)
        ds_sum = jnp.zeros((sb, bw), F32)
        for j in range(tq // sb):
            rows, win = slice(j * sb, (j + 1) * sb), slice(j * sb, j * sb + bw)
            s = _nt(q_ref[rows, :], kw[win]) * scale + b_ref[0]
            kpos = i * tq + j * sb - DIL_W + col
            p = jnp.where((kpos >= 0) & (kpos < l), jnp.exp(s - l_ref[rows, 0:1]), 0.0)
            ds = p * (_nt(do_ref[rows, :], vw[win]) - c_ref[rows, 0:1])
            dq_ref[rows, :] = (_nn(ds.astype(BF16), kw[win]) * scale).astype(BF16)
            ds_sum = ds_sum + ds

        @pl.when((r == 0) & (i == 0))
        def _():
            ds_ref[0] = ds_sum

        @pl.when((r > 0) | (i > 0))
        def _():
            ds_ref[0] += ds_sum

    _, q_spec, _ = _dil_specs(tq, nt, 0, 12)
    t_spec = pl.BlockSpec((tq, LANES), lambda h, r, i: (i, r * H + h))
    b_spec = pl.BlockSpec((1, sb, bw), lambda h, r, i: (h, 0, 0))
    return pl.pallas_call(
        body, name=name, grid=(H, d, nt),
        in_specs=[q_spec, *_dil_specs(tq, nt, 4, 12), *_dil_specs(tq, nt, 8, 12), b_spec, t_spec, t_spec, t_spec],
        out_specs=[t_spec, b_spec], out_shape=[SDS((l, d * DIL_WIDTH), BF16), SDS((H, sb, bw), F32)],
        compiler_params=_cp("parallel", "arbitrary", "arbitrary"))(qkv, qkv, qkv, qkv, qkv, qkv, qkv, bias, do, lse, corr)


def dil_bwd_kv(qkv, bias_t, do, lse, corr, *, d, name):
    l = qkv.shape[0]
    tq = _tile(l, DIL_TQ)
    nt = l // tq
    H = DIL_HEADS
    scale = DIL_HEAD_DIM ** -0.5

    sb, bw = _dil_band(tq)

    def body(k_ref, v_ref, qp, qo, qn, dp, do_, dn, lp, lo, ln, cp, co, cn, b_ref, dk_ref, dv_ref):
        i = pl.program_id(2)
        qw = _dil_window(qp, qo, qn, tq)
        dow = _dil_window(dp, do_, dn, tq)
        lrow = _dil_window(lp, lo, ln, tq).T[0:1, :]
        crow = _dil_window(cp, co, cn, tq).T[0:1, :]
        col = lax.broadcasted_iota(jnp.int32, (sb, bw), 1)
        for j in range(tq // sb):
            rows, win = slice(j * sb, (j + 1) * sb), slice(j * sb, j * sb + bw)
            st = _nt(k_ref[rows, :], qw[win]) * scale + b_ref[0]
            qpos = i * tq + j * sb - DIL_W + col
            pt = jnp.where((qpos >= 0) & (qpos < l), jnp.exp(st - lrow[:, win]), 0.0)
            dv_ref[rows, :] = _nn(pt.astype(BF16), dow[win]).astype(BF16)
            dst = pt * (_nt(v_ref[rows, :], dow[win]) - crow[:, win])
            dk_ref[rows, :] = (_nn(dst.astype(BF16), qw[win]) * scale).astype(BF16)

    _, k_spec, _ = _dil_specs(tq, nt, 4, 12)
    _, v_spec, _ = _dil_specs(tq, nt, 8, 12)
    t_spec = pl.BlockSpec((tq, LANES), lambda h, r, i: (i, r * H + h))
    return pl.pallas_call(
        body, name=name, grid=(H, d, nt),
        in_specs=[k_spec, v_spec, *_dil_specs(tq, nt, 0, 12), *_dil_specs(tq, nt, 0, 4), *_dil_specs(tq, nt, 0, 4),
                  *_dil_specs(tq, nt, 0, 4), pl.BlockSpec((1, sb, bw), lambda h, r, i: (h, 0, 0))],
        out_specs=[t_spec, t_spec], out_shape=[SDS((l, d * DIL_WIDTH), BF16), SDS((l, d * DIL_WIDTH), BF16)],
        compiler_params=_cp("parallel", "parallel", "arbitrary"))(
            qkv, qkv, qkv, qkv, qkv, do, do, do, lse, lse, lse, corr, corr, corr, bias_t)


def _dil_weights(l1, l2, l3):
    m = jnp.maximum(jnp.maximum(l1, l2), l3)
    e = [jnp.exp(x - m) for x in (l1, l2, l3)]
    tot = e[0] + e[1] + e[2]
    return [x / tot for x in e]


def dil_combine(os_, ls_, gain, *, name):
    S_ = os_[0].shape[0]
    T = _tile(S_, 512)

    def body(o1, o2, o3, l1, l2, l3, g_ref, y_ref):
        w = _dil_weights(l1[...], l2[...], l3[...])
        o = w[0] * o1[...] + w[1] * o2[...] + w[2] * o3[...]
        for h, (ohat, _) in enumerate(_head_norm_fwd(o, DIL_HEAD_DIM)):
            cs = slice(h * DIL_HEAD_DIM, (h + 1) * DIL_HEAD_DIM)
            y_ref[:, cs] = (ohat * g_ref[:, cs]).astype(BF16)

    blk = pl.BlockSpec((T, DIL_WIDTH), lambda i: (i, 0))
    return pl.pallas_call(
        body, name=name, grid=(S_ // T,), in_specs=[blk] * 6 + [pl.BlockSpec((1, DIL_WIDTH), lambda i: (0, 0))],
        out_specs=blk, out_shape=SDS((S_, DIL_WIDTH), BF16), compiler_params=_cp("parallel"))(*os_, *ls_, gain)


def dil_combine_bwd(dmix, os_, ls_, gain, *, name):
    S_ = os_[0].shape[0]
    T = _tile(S_, 512)
    E = DIL_HEAD_DIM

    def body(dy_ref, o1, o2, o3, l1, l2, l3, g_ref, d1, d2, d3, c1, c2, c3, gg_ref):
        @pl.when(pl.program_id(0) == 0)
        def _():
            gg_ref[...] = jnp.zeros_like(gg_ref)
        w = _dil_weights(l1[...], l2[...], l3[...])
        o = w[0] * o1[...] + w[1] * o2[...] + w[2] * o3[...]
        for h, (ohat, rs) in enumerate(_head_norm_fwd(o, E)):
            cs = slice(h * E, (h + 1) * E)
            dy = dy_ref[:, cs].astype(F32)
            u = dy * g_ref[:, cs]
            do = rs * (u - ohat * jnp.mean(u * ohat, axis=-1, keepdims=True))
            gg_ref[:, cs] += jnp.sum(dy * ohat, axis=0, keepdims=True)
            tot = jnp.sum(do * o[:, cs], axis=-1, keepdims=True)
            for wr, d_ref, c_ref in zip(w, (d1, d2, d3), (c1, c2, c3)):
                d_ref[:, cs] = (wr[:, cs] * do).astype(BF16)
                c_ref[:, cs] = wr[:, cs] * tot

    blk = pl.BlockSpec((T, DIL_WIDTH), lambda i: (i, 0))
    vec = pl.BlockSpec((1, DIL_WIDTH), lambda i: (0, 0))
    return pl.pallas_call(
        body, name=name, grid=(S_ // T,),
        in_specs=[pl.BlockSpec((T, DIL_WIDTH), lambda i: (i, V_W // DIL_WIDTH))] + [blk] * 6 + [vec],
        out_specs=[blk] * 6 + [vec],
        out_shape=[SDS((S_, DIL_WIDTH), BF16)] * 3 + [SDS((S_, DIL_WIDTH), F32)] * 3 + [SDS((1, DIL_WIDTH), F32)],
        compiler_params=_cp("arbitrary"))(dmix, *os_, *ls_, gain)


def _mem_softmax(q, k, scale):
    s = _nt(q, k) * scale
    e = jnp.exp(s - jnp.max(s, axis=-1, keepdims=True))
    return e / jnp.sum(e, axis=-1, keepdims=True)


def mem_fwd(proj, kv, gain, *, name):
    S_ = proj.shape[0]
    M = kv.shape[0]
    T = _tile(S_, 1024)
    E, H = MEM_HEAD_DIM, MEM_HEADS
    scale = E ** -0.5

    def body(q_ref, k_ref, v_ref, g_ref, y_ref):
        p = _mem_softmax(q_ref[...], k_ref[...], scale)
        o = _nn(p.astype(BF16), v_ref[...])
        rs = lax.rsqrt(jnp.mean(o * o, axis=-1, keepdims=True) + EPS)
        y_ref[...] = (o * rs * g_ref[...]).astype(BF16)

    return pl.pallas_call(
        body, name=name, grid=(H, S_ // T),
        in_specs=[pl.BlockSpec((T, E), lambda h, i: (i, C_MQ // E + h)), pl.BlockSpec((M, E), lambda h, i: (0, h)),
                  pl.BlockSpec((M, E), lambda h, i: (0, H + h)), pl.BlockSpec((1, E), lambda h, i: (0, h))],
        out_specs=pl.BlockSpec((T, E), lambda h, i: (i, h)), out_shape=SDS((S_, MEM_WIDTH), BF16),
        compiler_params=_cp("parallel", "parallel"))(proj, kv, kv, gain)


def mem_bwd(dmix, proj, kv, gain, *, name):
    S_ = proj.shape[0]
    M = kv.shape[0]
    T = _tile(S_, 1024)
    E, H = MEM_HEAD_DIM, MEM_HEADS
    scale = E ** -0.5

    def body(dy_ref, q_ref, k_ref, v_ref, g_ref, dq_ref, dk_ref, dv_ref, gg_ref):
        @pl.when(pl.program_id(1) == 0)
        def _():
            dk_ref[...] = jnp.zeros_like(dk_ref)
            dv_ref[...] = jnp.zeros_like(dv_ref)
            gg_ref[...] = jnp.zeros_like(gg_ref)
        q, k, v = q_ref[...], k_ref[...], v_ref[...]
        p = _mem_softmax(q, k, scale)
        pb = p.astype(BF16)
        o = _nn(pb, v)
        rs = lax.rsqrt(jnp.mean(o * o, axis=-1, keepdims=True) + EPS)
        ohat = o * rs
        dy = dy_ref[...].astype(F32)
        u = dy * g_ref[...]
        do = (rs * (u - ohat * jnp.mean(u * ohat, axis=-1, keepdims=True))).astype(BF16)
        gg_ref[...] += jnp.sum(dy * ohat, axis=0, keepdims=True)
        dv_ref[...] += _tn(pb, do)
        dp = _nt(do, v)
        ds = (p * (dp - jnp.sum(p * dp, axis=-1, keepdims=True))).astype(BF16)
        dq_ref[...] = (_nn(ds, k) * scale).astype(BF16)
        dk_ref[...] += _tn(ds, q) * scale

    return pl.pallas_call(
        body, name=name, grid=(H, S_ // T),
        in_specs=[pl.BlockSpec((T, E), lambda h, i: (i, (V_W + DIL_WIDTH) // E + h)),
                  pl.BlockSpec((T, E), lambda h, i: (i, C_MQ // E + h)), pl.BlockSpec((M, E), lambda h, i: (0, h)),
                  pl.BlockSpec((M, E), lambda h, i: (0, H + h)), pl.BlockSpec((1, E), lambda h, i: (0, h))],
        out_specs=[pl.BlockSpec((T, E), lambda h, i: (i, h)), pl.BlockSpec((M, E), lambda h, i: (0, h)),
                   pl.BlockSpec((M, E), lambda h, i: (0, h)), pl.BlockSpec((1, E), lambda h, i: (0, h))],
        out_shape=[SDS((S_, MEM_WIDTH), BF16), SDS((M, MEM_WIDTH), F32), SDS((M, MEM_WIDTH), F32),
                   SDS((1, MEM_WIDTH), F32)],
        compiler_params=_cp("parallel", "arbitrary"))(dmix, proj, kv, kv, gain)


def _strided(a, d):
    return a if d == 1 else a.reshape(a.shape[0] // d, d * a.shape[1])


def _unstrided(a, d):
    return a if d == 1 else a.reshape(a.shape[0] * d, a.shape[1] // d)


def _dil_tables(rel_bias, seq):
    tabs = []
    for d in DIL_DILATIONS:
        sb, _ = _dil_band(min(seq // d, DIL_TQ))
        vec = jnp.sum(jnp.asarray(_rel_onehot(d))[:, :, None] * rel_bias[None], axis=1).T
        tabs.append((_toeplitz(vec, sb, False), _toeplitz(vec, sb, True)))
    return tabs


def _rel_bias_grad(ds_sums, seq):
    g = jnp.zeros((REL_BUCKETS, DIL_HEADS), F32)
    for d, ds in zip(DIL_DILATIONS, ds_sums):
        gvec = _untoeplitz(ds, _dil_band(min(seq // d, DIL_TQ))[0])
        g = g + jnp.sum(jnp.asarray(_rel_onehot(d))[:, :, None] * gvec.T[:, None, :], axis=0)
    return g


def _pad_gate_up(up, row0):
    return jnp.zeros((LANES, QK_W), BF16).at[row0:row0 + GLA_GATE_RANK].set(up.astype(BF16))


def _layer_fwd(x, mem, p, tabs, l, carry):
    tag = f"l{l}"
    proj, hn1 = norm_mm(x, p["norm_mix"], p["w_in"], tm=TM_FWD, tn=IN_TILE, name=f"in_proj_{tag}", comm=carry.get("in"))
    o_f, st_f = gla_fwd(proj, p["up_f"], p["bias_f"], reverse=False, name=f"gla_fwd_f_{tag}")
    o_b, st_b = gla_fwd(proj, p["up_b"], p["bias_b"], reverse=True, name=f"gla_fwd_b_{tag}")
    gla_out = gla_post(o_f, o_b, proj, p["gla_norm"], name=f"gla_post_{tag}")
    qkv = proj[:, C_DQ:C_MQ]
    os_, ls_ = [], []
    for d, (bias, _) in zip(DIL_DILATIONS, tabs):
        o, lse = dil_fwd(_strided(qkv, d), bias, d=d, name=f"dil_fwd_d{d}_{tag}")
        os_.append(_unstrided(o, d))
        ls_.append(_unstrided(lse, d))
    dil_out = dil_combine(os_, ls_, p["dil_norm"], name=f"dil_combine_{tag}")
    kv, hmem = norm_mm(mem, p["mem_norm"], p["w_mem_kv"], tm=256, tn=1024, name=f"mem_kv_{tag}")
    mem_out = mem_fwd(proj, kv, p["mem_out_norm"], name=f"mem_fwd_{tag}")
    mixed = jnp.concatenate([gla_out, dil_out, mem_out], axis=1)
    x1 = mm_res(mixed, p["w_out"], x, relu2=False, tm=TM_FWD, tn=TN, tk=TK, name=f"out_proj_{tag}",
                comm=carry.get("out"))
    a, hn2 = norm_mm(x1, p["norm_mlp"], p["w_up"], tm=TM_FWD, tn=TN, name=f"up_proj_{tag}", comm=carry.get("up"))
    x2 = mm_res(a, p["w_down"], x1, relu2=True, tm=TM_FWD, tn=TN, tk=TK, name=f"down_proj_{tag}",
                comm=carry.get("down"))
    saved = dict(x=x, proj=proj, hn1=hn1, o_f=o_f, o_b=o_b, st_f=st_f, st_b=st_b, qkv=qkv, os=os_, ls=ls_,
                 kv=kv, hmem=hmem, mixed=mixed, x1=x1, a=a, hn2=hn2)
    return x2, saved


class GradScatter:
    SLOTS = {"down_x": [(1, "w_in", 0, 1)], "down_w": [(1, "w_out", 0, 1), (1, "w_mem_kv", 0, 1)],
             "up_w": [(0, "w_down", 0, 2)], "up_x": [(0, "w_down", 1, 2)],
             "in_w": [(0, "w_up", 0, 2)], "in_x": [(0, "w_up", 1, 2)]}
    LAYER0 = {"in_w": [(0, "w_out", 0, 1), (0, "w_mem_kv", 0, 1)], "in_x": [(0, "w_in", 0, 1)]}

    def __init__(self, exchange):
        self.exchange = exchange
        self.ready = {}
        self.taken = set()
        self.sent = []

    def offer(self, l, name, g):
        self.ready[(l, name)] = g

    def _take(self, keys):
        keys = [k for k in keys if k[:2] in self.ready and k not in self.taken]
        if not (keys and self.exchange):
            return None
        bufs = [self.ready[k[:2]] for k in keys]
        rows = [(k[2] * (b.shape[1] // k[3]), b.shape[1] // k[3]) for k, b in zip(keys, bufs)]
        ex = ChipExchange(bufs, scatter=True, rows=rows)
        self.taken.update(keys)
        self.sent.append((keys, ex))
        return ex

    def carrier(self, l, slot):
        plan = self.SLOTS[slot] + (self.LAYER0.get(slot, []) if l == 0 else [])
        return self._take([(l + dl, n, i, m) for dl, n, i, m in plan])

    def flush(self, name):
        pieces = {k[:2]: max([t[3] for t in self.taken if t[:2] == k[:2]] + [1]) for k in self.ready}
        ex = self._take(sorted((l, n, i, m) for (l, n), m in pieces.items() for i in range(m)))
        if ex is not None:
            ex.run(name)

    def received(self):
        got = {}
        for keys, ex in self.sent:
            for k, r in zip(keys, ex.result):
                got.setdefault(k[:2], {})[k[2]] = r
        return {k: [v[i] for i in sorted(v)] for k, v in got.items()}


def _quarters(g):
    return g.reshape(N_CHIPS, g.shape[0] // N_CHIPS, g.shape[1])


def _layer_bwd(dx2, mem, p, tabs, s, l, sink):
    tag = f"l{l}"
    seq = dx2.shape[0]
    g = {}
    da = mm_nt(dx2, p["w_down"], s["a"], tm=TM_FWD, tn=TN, name=f"down_bwd_x_{tag}", comm=sink.carrier(l, "down_x"))
    g_down = mm_tn(s["a"], dx2, relu2=True, tm=TM_W, tn=TN_W, tk=TK_W, out="rows", name=f"down_bwd_w_{tag}",
                   comm=sink.carrier(l, "down_w"))
    sink.offer(l, "w_down", _quarters(g_down))
    g_up = mm_tn(s["hn2"], da, relu2=False, tm=TM_W, tn=TN_W, tk=TK_W, out="cols", name=f"up_bwd_w_{tag}",
                 comm=sink.carrier(l, "up_w"))
    sink.offer(l, "w_up", g_up)
    dx1, g["norm_mlp"] = mm_nt_normbwd(da, p["w_up"], s["x1"], p["norm_mlp"], dx2, tm=TM_NORM_BWD, tk=TK_NORM_BWD,
                                       name=f"up_bwd_x_{tag}", comm=sink.carrier(l, "up_x"))
    dmix = mm_nt(dx1, p["w_out"], None, tm=TM_FWD, tn=TN, name=f"out_bwd_x_{tag}")
    sink.offer(l, "w_out", _quarters(mm_tn(s["mixed"], dx1, relu2=False, tm=TM_W, tn=TN_W, tk=TK_W, out="rows",
                                           name=f"out_bwd_w_{tag}")))
    proj = s["proj"]
    do, dr, g["gla_norm"] = gla_post_bwd(dmix, s["o_f"], s["o_b"], proj, p["gla_norm"], name=f"gla_post_bwd_{tag}")
    r_f = gla_bwd(proj, p["up_f"], p["bias_f"], s["st_f"], do, None, reverse=False, name=f"gla_bwd_f_{tag}")
    r_b = gla_bwd(proj, p["up_b"], p["bias_b"], s["st_b"], do, r_f[:4], reverse=True, name=f"gla_bwd_b_{tag}")
    dgq, dgk, dgv, dlr = r_b[:4]
    g["up_f"], g["bias_f"] = r_f[4][0:GLA_GATE_RANK], r_f[5]
    g["up_b"], g["bias_b"] = r_b[4][GLA_GATE_RANK:2 * GLA_GATE_RANK], r_b[5]
    *branch, g["dil_norm"] = dil_combine_bwd(dmix, s["os"], s["ls"], p["dil_norm"], name=f"dil_combine_bwd_{tag}")
    dos, corrs = branch[:3], branch[3:]
    ddq = ddk = ddv = None
    ds_sums = []
    for r, (d, (bias, bias_t)) in enumerate(zip(DIL_DILATIONS, tabs)):
        qkv_d = _strided(s["qkv"], d)
        do_d, lse_d, corr_d = _strided(dos[r], d), _strided(s["ls"][r], d), _strided(corrs[r], d)
        dq_d, ds_sum = dil_bwd_q(qkv_d, bias, do_d, lse_d, corr_d, d=d, name=f"dil_bwd_q_d{d}_{tag}")
        dk_d, dv_d = dil_bwd_kv(qkv_d, bias_t, do_d, lse_d, corr_d, d=d, name=f"dil_bwd_kv_d{d}_{tag}")
        ds_sums.append(ds_sum)
        dq_d, dk_d, dv_d = [_unstrided(t, d).astype(F32) for t in (dq_d, dk_d, dv_d)]
        ddq, ddk, ddv = (dq_d, dk_d, dv_d) if ddq is None else (ddq + dq_d, ddk + dk_d, ddv + dv_d)
    g["rel_bias"] = _rel_bias_grad(ds_sums, seq)
    dmq, dmk, dmv, g["mem_out_norm"] = mem_bwd(dmix, proj, s["kv"], p["mem_out_norm"], name=f"mem_bwd_{tag}")
    dkv = jnp.concatenate([dmk, dmv], axis=1).astype(BF16)
    sink.offer(l, "w_mem_kv", _quarters(mm_tn(s["hmem"], dkv, relu2=False, tm=TM_W, tn=TN_W, tk=TK_W, out="rows",
                                              name=f"mem_kv_bwd_w_{tag}")))
    _, g["mem_norm"] = mm_nt_normbwd(dkv, p["w_mem_kv"], mem, p["mem_norm"], jnp.zeros_like(mem), tm=TM_NORM_BWD,
                                     tk=TK_NORM_BWD, name=f"mem_kv_bwd_x_{tag}")
    dproj = jnp.concatenate([dgq, dgk, dgv, dr, ddq.astype(BF16), ddk.astype(BF16), ddv.astype(BF16), dmq, dlr,
                             jnp.zeros((seq, IN_PAD - C_LR - LANES), BF16)], axis=1)
    g_in = mm_tn(s["hn1"], dproj, relu2=False, tm=TM_W, tn=IN_TILE, tk=TK_W, name=f"in_bwd_w_{tag}",
                 comm=sink.carrier(l, "in_w"))
    sink.offer(l, "w_in", _shard_w_in_grad(g_in))
    dx0, g["norm_mix"] = mm_nt_normbwd(dproj, p["w_in"], s["x"], p["norm_mix"], dx1, tm=TM_NORM_BWD, tk=IN_TILE,
                                       name=f"in_bwd_x_{tag}", comm=sink.carrier(l, "in_x"))
    return dx0, g


def _shard_w_in_grad(g):
    g = jnp.concatenate([g[:, :LR_COL], g[:, C_LR:C_LR + 2 * GLA_GATE_RANK], g[:, LR_COL:C_LR]], axis=1).astype(BF16)
    w = IN_WIDTH // N_CHIPS
    return jnp.stack([g[:, j * w:(j + 1) * w] for j in range(N_CHIPS)], axis=0)


def local_step(x, mem, target, depth, layer_weights, norm_final, rel_bias, fwd_carry, sink):
    tabs = _dil_tables(rel_bias, x.shape[0])
    saved, layers = [], []
    for l in range(depth):
        layers.append(layer_weights(l))
        x, s = _layer_fwd(x, mem, layers[l], tabs, l, fwd_carry(l))
        saved.append(s)
    dx, g_final, loss = loss_head(x, norm_final, target, tm=TM_LOSS, name="loss_head")
    grads = [None] * depth
    for l in reversed(range(depth)):
        dx, grads[l] = _layer_bwd(dx, mem, layers[l], tabs, saved[l], l, sink)
    return loss, dx, grads, g_final


def sibling_exchange(bufs, *, name):
    n = len(bufs)

    def body(*refs):
        ins, outs = refs[:n], refs[n:2 * n]
        send, recv = refs[2 * n:]
        x, y, c = _mesh_pos()
        cps = [pltpu.make_async_remote_copy(src_ref=ins[a], dst_ref=outs[a], send_sem=send.at[a], recv_sem=recv.at[a],
                                            device_id=(x, y, 1 - c), device_id_type=MESH) for a in range(n)]
        for cp in cps:
            cp.start()
        for cp in cps:
            cp.wait()

    return pl.pallas_call(
        body, name=name, in_specs=[HBM_SPEC] * n, out_specs=[HBM_SPEC] * n,
        out_shape=[SDS(b.shape, b.dtype) for b in bufs],
        scratch_shapes=[pltpu.SemaphoreType.DMA((n,)), pltpu.SemaphoreType.DMA((n,))])(*bufs)


def all_reduce_small(v, *, name):
    R = v.shape[0]

    def body(v_ref, o_ref, slots, send, recv):
        x, y, c = _mesh_pos()
        me = 4 * x + 2 * y + c
        flip = lambda p, f: 1 - p if f else p
        peers = [(flip(x, k & 4), flip(y, k & 2), flip(c, k & 1)) for k in range(1, N_DEV)]
        slots[pl.ds(me, 1)] = v_ref[...][None]
        cps = [pltpu.make_async_remote_copy(src_ref=v_ref, dst_ref=slots.at[me], send_sem=send.at[k], recv_sem=recv.at[k],
                                            device_id=peer, device_id_type=MESH) for k, peer in enumerate(peers)]
        for cp in cps:
            cp.start()
        for k, (px, py, pc) in enumerate(peers):
            pltpu.make_async_remote_copy(src_ref=v_ref, dst_ref=slots.at[4 * px + 2 * py + pc], send_sem=send.at[k],
                                         recv_sem=recv.at[k], device_id=(px, py, pc), device_id_type=MESH).wait_recv()
        for cp in cps:
            cp.wait_send()
        acc = slots[0]
        for s in range(1, N_DEV):
            acc = acc + slots[s]
        o_ref[...] = acc

    vm = pl.BlockSpec(memory_space=pltpu.VMEM)
    return pl.pallas_call(
        body, name=name, in_specs=[vm], out_specs=vm, out_shape=SDS(v.shape, F32),
        scratch_shapes=[pltpu.VMEM((N_DEV, R, LANES), F32), pltpu.SemaphoreType.DMA((N_DEV - 1,)),
                        pltpu.SemaphoreType.DMA((N_DEV - 1,))])(v)


def sum_chips(recv, *, name):
    _, R, C = recv.shape
    tr = _tile(R, 512)

    def body(r_ref, o_ref):
        acc = r_ref[0].astype(F32)
        for j in range(1, N_CHIPS):
            acc = acc + r_ref[j].astype(F32)
        o_ref[...] = acc

    return pl.pallas_call(
        body, name=name, grid=(R // tr,), in_specs=[pl.BlockSpec((N_CHIPS, tr, C), lambda i: (0, i, 0))],
        out_specs=pl.BlockSpec((tr, C), lambda i: (i, 0)), out_shape=SDS((R, C), F32),
        compiler_params=_cp("parallel"))(recv)


def adamw(parts, w, m, v, *, name):
    R, C = w.shape
    n = len(parts)
    tr = _tile(R, 512)
    while 2 * (n + 7) * tr * C * 4 > ELEMWISE_BLOCK_BYTES and tr % 16 == 0:
        tr //= 2
    c1 = 1.0 - ADAM_B1 ** ADAM_STEP
    c2 = 1.0 - ADAM_B2 ** ADAM_STEP

    def body(*refs):
        w_ref, m_ref, v_ref = refs[n:n + 3]
        g_ref, d_ref, nm_ref, nv_ref = refs[n + 3:]
        g = refs[0][...]
        for r in refs[1:n]:
            g = g + r[...]
        nm = ADAM_B1 * m_ref[...] + (1.0 - ADAM_B1) * g
        nv = ADAM_B2 * v_ref[...] + (1.0 - ADAM_B2) * (g * g)
        g_ref[...] = g
        nm_ref[...] = nm
        nv_ref[...] = nv
        d_ref[...] = -ADAM_LR * ((nm / c1) / (jnp.sqrt(nv / c2) + ADAM_EPS) + ADAM_WD * w_ref[...])

    blk = pl.BlockSpec((tr, C), lambda i: (i, 0))
    return pl.pallas_call(
        body, name=name, grid=(R // tr,), in_specs=[blk] * (n + 3), out_specs=[blk] * 4,
        out_shape=[SDS((R, C), F32)] * 4, compiler_params=_cp("parallel"))(*parts, w, m, v)


PACK_ROWS = 8


def _pack(parts):
    rows = []
    for a in parts:
        flat = a.reshape(-1).astype(F32)
        n = -(-flat.shape[0] // (PACK_ROWS * LANES)) * PACK_ROWS * LANES
        rows.append(jnp.pad(flat, (0, n - flat.shape[0])).reshape(-1, LANES))
    return jnp.concatenate(rows, axis=0)


def _unpack(slab, shapes):
    out, r = [], 0
    for shp in shapes:
        n = int(np.prod(shp))
        nr = -(-n // (PACK_ROWS * LANES)) * PACK_ROWS
        out.append(slab[r:r + nr].reshape(-1)[:n].reshape(shp))
        r += nr
    return out


SMALL = ["norm_mix", "gla_gate_bias_fwd", "gla_gate_bias_bwd", "gla_norm", "rel_bias", "dil_norm", "mem_norm",
         "mem_out_norm", "norm_mlp", "norm_final"]
GATE_UPS = ["gla_gate_up_fwd", "gla_gate_up_bwd"]
BIG = ["w_in", "w_mem_kv", "w_out", "w_up", "w_down"]
WEIGHTS = ["norm_mix", "w_in", "gla_gate_up_fwd", "gla_gate_bias_fwd", "gla_gate_up_bwd", "gla_gate_bias_bwd", "gla_norm",
           "rel_bias", "dil_norm", "mem_norm", "w_mem_kv", "mem_out_norm", "w_out", "norm_mlp", "w_up", "w_down",
           "norm_final"]


FWD_CARRY = {"in": ["w_in"], "out": ["w_out", "w_mem_kv"], "up": ["w_up"], "down": ["w_down"]}


def _kernel_layout(name, g):
    if name == "w_up":
        return g
    if name == "w_in":
        full = jnp.concatenate([g[j] for j in range(N_CHIPS)], axis=1)
        return jnp.concatenate([full[:, :LR_COL], full[:, LR_COL + 2 * GLA_GATE_RANK:],
                                full[:, LR_COL:LR_COL + 2 * GLA_GATE_RANK],
                                jnp.zeros((full.shape[0], IN_PAD - IN_WIDTH), BF16)], axis=1)
    return g.reshape(-1, g.shape[-1])


def kernel(x, mem, norm_mix, w_in, gla_gate_up_fwd, gla_gate_bias_fwd, gla_gate_up_bwd, gla_gate_bias_bwd, gla_norm, rel_bias, dil_norm, mem_norm, w_mem_kv, mem_out_norm, w_out, norm_mlp, w_up, w_down, norm_final, loss_target, m_norm_mix, m_w_in, m_gla_gate_up_fwd, m_gla_gate_bias_fwd, m_gla_gate_up_bwd, m_gla_gate_bias_bwd, m_gla_norm, m_rel_bias, m_dil_norm, m_mem_norm, m_w_mem_kv, m_mem_out_norm, m_w_out, m_norm_mlp, m_w_up, m_w_down, m_norm_final, v_norm_mix, v_w_in, v_gla_gate_up_fwd, v_gla_gate_bias_fwd, v_gla_gate_up_bwd, v_gla_gate_bias_bwd, v_gla_norm, v_rel_bias, v_dil_norm, v_mem_norm, v_w_mem_kv, v_mem_out_norm, v_w_out, v_norm_mlp, v_w_up, v_w_down, v_norm_final):
    vals = dict(locals())
    W = {n: vals[n] for n in WEIGHTS}
    M1 = {n: vals["m_" + n] for n in WEIGHTS}
    V2 = {n: vals["v_" + n] for n in WEIGHTS}
    depth = w_in.shape[0]
    chip = 2 * lax.axis_index("x") + lax.axis_index("y")

    wb = {n: W[n].astype(BF16) for n in BIG + GATE_UPS}
    first = ChipExchange([wb[n] for n in BIG + GATE_UPS], scatter=False, layers=[0] * len(BIG) + [None] * len(GATE_UPS))
    first.run("gather_layer0")
    gathered = {(0, n): r for n, r in zip(BIG, first.result)}
    gate_up = dict(zip(GATE_UPS, first.result[len(BIG):]))
    carried = {}
    row = lambda a: a.reshape(1, -1)

    def fwd_carry(l):
        if l + 1 == depth:
            return {}
        carried[l] = {slot: ChipExchange([wb[n] for n in names], scatter=False, layers=[l + 1] * len(names))
                      for slot, names in FWD_CARRY.items()}
        return carried[l]

    def layer_weights(l):
        if l > 0:
            for slot, names in FWD_CARRY.items():
                gathered.update({(l, n): r for n, r in zip(names, carried[l - 1][slot].result)})
        p = {n: _kernel_layout(n, gathered[(l, n)]) for n in BIG}
        full_up = lambda n: jnp.concatenate([gate_up[n][j, l] for j in range(N_CHIPS)], axis=1)
        p["up_f"] = _pad_gate_up(full_up("gla_gate_up_fwd"), 0)
        p["up_b"] = _pad_gate_up(full_up("gla_gate_up_bwd"), GLA_GATE_RANK)
        p["bias_f"], p["bias_b"] = row(gla_gate_bias_fwd[l]), row(gla_gate_bias_bwd[l])
        for n in ("norm_mix", "gla_norm", "dil_norm", "mem_norm", "mem_out_norm", "norm_mlp"):
            p[n] = row(W[n][l])
        return p

    sink = GradScatter(exchange=True)
    loss, grad_x, grads, g_final = local_step(x[0], mem[0], loss_target[0], depth, layer_weights, row(norm_final),
                                              rel_bias, fwd_carry, sink)
    sink.flush("scatter_last_grads")
    recv = sink.received()

    flat2 = lambda a: a.reshape(-1, a.shape[-1])
    part = [jnp.concatenate([sum_chips(r, name=f"sum_chips_{n}_l{l}_p{i}") for l in range(depth)
                             for i, r in enumerate(recv[(l, n)])], axis=0) for n in BIG]
    other = sibling_exchange(part, name="swap_partial_sums")
    out = {}
    for n, pa, ob in zip(BIG, part, other):
        res = adamw([pa, ob], flat2(W[n]), flat2(M1[n]), flat2(V2[n]), name=f"adamw_{n}")
        out[n] = [r.reshape(W[n].shape) for r in res]

    small_g = {}
    for n, key in (("norm_mix", "norm_mix"), ("gla_gate_bias_fwd", "bias_f"), ("gla_gate_bias_bwd", "bias_b"),
                   ("gla_norm", "gla_norm"), ("dil_norm", "dil_norm"), ("mem_norm", "mem_norm"),
                   ("mem_out_norm", "mem_out_norm"), ("norm_mlp", "norm_mlp"),
                   ("gla_gate_up_fwd", "up_f"), ("gla_gate_up_bwd", "up_b")):
        small_g[n] = jnp.stack([grads[l][key].reshape(W[n].shape[1:] if n not in GATE_UPS else (GLA_GATE_RANK, QK_W))
                                for l in range(depth)], axis=0)
    small_g["rel_bias"] = sum(grads[l]["rel_bias"] for l in range(depth))
    small_g["norm_final"] = g_final.reshape(-1)
    names = SMALL + GATE_UPS
    slab = all_reduce_small(_pack([small_g[n] for n in names] + [loss[:, 0]]), name="all_reduce_small")
    *summed, loss_sum = _unpack(slab, [small_g[n].shape for n in names] + [(1,)])
    summed = dict(zip(names, summed))
    for n in GATE_UPS:
        summed[n] = lax.dynamic_slice_in_dim(summed[n], chip * LANES, LANES, axis=2)
    names_s = SMALL + GATE_UPS
    shapes = [W[n].shape for n in names_s]
    res = adamw([_pack([summed[n] for n in names_s])], _pack([W[n] for n in names_s]), _pack([M1[n] for n in names_s]),
                _pack([V2[n] for n in names_s]), name="adamw_small")
    for n, *r in zip(names_s, *[_unpack(t, shapes) for t in res]):
        out[n] = r

    return (loss_sum.reshape(()), grad_x[None], *[out[n][0] for n in WEIGHTS], *[out[n][1] for n in WEIGHTS],
            *[out[n][2] for n in WEIGHTS], *[out[n][3] for n in WEIGHTS])
```

```python
import functools
import math

import numpy as np
import jax
import jax.numpy as jnp
from jax import lax
from jax.experimental import pallas as pl
from jax.experimental.pallas import tpu as pltpu

F32, BF16 = jnp.float32, jnp.bfloat16
SDS = jax.ShapeDtypeStruct

DEPTH = 4
GLA_HEADS, GLA_DK, GLA_DV = 4, 128, 256
GLA_GATE_RANK = 16
GLA_GATE_NORMALIZER = 16.0
GLA_CHUNK = 64
DIL_HEADS, DIL_HEAD_DIM = 4, 128
DIL_DILATIONS = (1, 4, 16)
DIL_W = 64
MEM_HEADS, MEM_HEAD_DIM = 4, 128
REL_BUCKETS, REL_MAX_DISTANCE = 32, 1024
EPS = 1e-6
NEG_INF = -1e30
QK_W, V_W, DIL_WIDTH, MEM_WIDTH = 512, 1024, 512, 512
IN_WIDTH = 5152
LR_COL = 3072
IN_PAD = 5376
C_GQ, C_GK, C_GV, C_GR, C_DQ, C_MQ, C_LR = 0, 512, 1024, 2048, 3072, 4608, 5120
IN_TILE = 1792

ADAM_LR, ADAM_B1, ADAM_B2, ADAM_EPS, ADAM_WD, ADAM_STEP = 0.001, 0.9, 0.999, 1e-08, 0.01, 10

V7X_VMEM_BYTES = 64 * 1024 * 1024
VMEM_LIMIT = V7X_VMEM_BYTES * 7 // 8
LANES = 128

TM_FWD = 512
TM_NT = 1024
TN = 2048
TK = 2048
TM_NORM_BWD = 512
TK_NORM_BWD = 2048
NORM_BWD_ROWS = 64
TM_W, TN_W, TK_W = 1024, 1024, 2048
TM_LOSS = 256
ELEMWISE_BLOCK_BYTES = 24 * 1024 * 1024

NT_DIMS = (((1,), (1,)), ((), ()))
TN_DIMS = (((0,), (0,)), ((), ()))


def _nt(a, b):
    return lax.dot_general(a, b, NT_DIMS, preferred_element_type=F32)


def _tn(a, b):
    return lax.dot_general(a, b, TN_DIMS, preferred_element_type=F32)


def _nn(a, b):
    return jnp.dot(a, b, preferred_element_type=F32)


def _cp(*sem):
    return pltpu.CompilerParams(dimension_semantics=sem, vmem_limit_bytes=VMEM_LIMIT)


def _tile(n, t):
    t = min(n, t)
    assert n % t == 0, (n, t)
    return t


MESH = pl.DeviceIdType.MESH
HBM_SPEC = pl.BlockSpec(memory_space=pltpu.HBM)
N_CHIPS = 4
N_DEV = 8


def _mesh_pos():
    return lax.axis_index("x"), lax.axis_index("y"), lax.axis_index("c")


class ChipExchange:
    def __init__(self, bufs, *, scatter, layers=None, rows=None):
        self.bufs = list(bufs)
        self.n = n = len(self.bufs)
        self.scatter = scatter
        self.layers = list(layers) if layers is not None else [None] * n
        self.rows = list(rows) if rows is not None else [None] * n
        dma = pltpu.SemaphoreType.DMA
        if scatter:
            shp = [b.shape if r is None else (b.shape[0], r[1]) + b.shape[2:] for b, r in zip(self.bufs, self.rows)]
            self.out_shape = [SDS(s, b.dtype) for s, b in zip(shp, self.bufs)]
            self.scratch = [dma((3 * n,)), dma((3 * n,)), dma((n,))]
        else:
            shp = [b.shape if l is None else b.shape[1:] for b, l in zip(self.bufs, self.layers)]
            assert all(s[0] % 2 == 0 for s in shp), shp
            self.out_shape = [SDS((N_CHIPS,) + s, b.dtype) for s, b in zip(shp, self.bufs)]
            self.scratch = [dma((3 * n,)), dma((3 * n,)), dma((n,)), dma((3 * n,)), dma((3 * n,))]
        self.result = None

    def _plan(self, ins, outs, sems, finishing):
        x, y, c = _mesh_pos()
        me = 2 * x + y
        chips = [(1 - x, y), (x, 1 - y), (1 - x, 1 - y)]
        remote = lambda src, dst, ss, rs, k, dev: pltpu.make_async_remote_copy(
            src_ref=src, dst_ref=dst, send_sem=ss.at[k], recv_sem=rs.at[k], device_id=dev, device_id_type=MESH)
        kept, sent, landing, passed, handed = [], [], [], [], []
        for a in range(self.n):
            src = ins[a] if self.layers[a] is None else ins[a].at[self.layers[a]]
            if self.scatter:
                r = self.rows[a]
                pick = (lambda j: src.at[j]) if r is None else (lambda j: src.at[j, pl.ds(r[0], r[1])])
                kept.append(pltpu.make_async_copy(pick(me), outs[a].at[me], sems[2].at[a]))
                for j, (px, py) in enumerate(chips):
                    k, peer = 3 * a + j, 2 * px + py
                    sent.append(remote(pick(peer), outs[a].at[me], sems[0], sems[1], k, (px, py, c)))
                    if finishing:
                        landing.append(remote(pick(me), outs[a].at[peer], sems[0], sems[1], k, (px, py, c)))
                continue
            half = self.out_shape[a].shape[1] // 2
            mine, other = pl.ds(c * half, half), pl.ds((1 - c) * half, half)
            kept.append(pltpu.make_async_copy(src, outs[a].at[me], sems[2].at[a]))
            for j, (px, py) in enumerate(chips):
                k, peer = 3 * a + j, 2 * px + py
                sent.append(remote(src.at[mine], outs[a].at[me, mine], sems[0], sems[1], k, (px, py, c)))
                if finishing:
                    landing.append(remote(src.at[mine], outs[a].at[peer, mine], sems[0], sems[1], k, (px, py, c)))
                    passed.append(remote(outs[a].at[peer, mine], outs[a].at[peer, mine], sems[3], sems[4], k, (x, y, 1 - c)))
                    handed.append(remote(outs[a].at[peer, other], outs[a].at[peer, other], sems[3], sems[4], k,
                                         (x, y, 1 - c)))
        return kept, sent, landing, passed, handed

    def start(self, ins, outs, sems):
        kept, sent, _, _, _ = self._plan(ins, outs, sems, False)
        for cp in kept + sent:
            cp.start()

    def finish(self, ins, outs, sems):
        kept, sent, landing, passed, handed = self._plan(ins, outs, sems, True)
        for k, cp in enumerate(landing):
            cp.wait_recv()
            if passed:
                passed[k].start()
        for cp in handed:
            cp.wait_recv()
        for cp in sent + passed:
            cp.wait_send()
        for cp in kept:
            cp.wait()

    def run(self, name):
        n = self.n

        def body(*refs):
            self.start(refs[:n], refs[n:2 * n], refs[2 * n:])
            self.finish(refs[:n], refs[n:2 * n], refs[2 * n:])

        self.result = pl.pallas_call(body, name=name, in_specs=[HBM_SPEC] * n, out_specs=[HBM_SPEC] * n,
                                     out_shape=self.out_shape, scratch_shapes=self.scratch)(*self.bufs)
        return self.result


def _call(body, *, name, grid, in_specs, out_specs, out_shape, sem, args, scratch_shapes=(), comm=None):
    if comm is None:
        return pl.pallas_call(body, name=name, grid=grid, in_specs=list(in_specs), out_specs=list(out_specs),
                              out_shape=list(out_shape), scratch_shapes=list(scratch_shapes),
                              compiler_params=_cp(*sem))(*args)
    n_in, n_out, n_scr, nc = len(in_specs), len(out_shape), len(scratch_shapes), comm.n

    def wrapped(*refs):
        ins, cin = refs[:n_in], refs[n_in:n_in + nc]
        outs, cout = refs[n_in + nc:n_in + nc + n_out], refs[n_in + nc + n_out:n_in + 2 * nc + n_out]
        scr, csem = refs[n_in + 2 * nc + n_out:n_in + 2 * nc + n_out + n_scr], refs[n_in + 2 * nc + n_out + n_scr:]
        ids = [pl.program_id(d) for d in range(len(grid))]
        first = functools.reduce(jnp.logical_and, [i == 0 for i in ids])
        last = functools.reduce(jnp.logical_and, [i == g - 1 for i, g in zip(ids, grid)])

        @pl.when(first)
        def _():
            comm.start(cin, cout, csem)
        body(*ins, *outs, *scr)

        @pl.when(last)
        def _():
            comm.finish(cin, cout, csem)

    res = pl.pallas_call(
        wrapped, name=name, grid=grid, in_specs=list(in_specs) + [HBM_SPEC] * nc,
        out_specs=list(out_specs) + [HBM_SPEC] * nc, out_shape=list(out_shape) + comm.out_shape,
        scratch_shapes=list(scratch_shapes) + comm.scratch,
        compiler_params=_cp(*("arbitrary",) * len(grid)))(*args, *comm.bufs)
    comm.result = res[n_out:]
    return res[:n_out]


def _w_spec(w, rows, tn, col_of):
    if w.ndim == 2:
        tn = _tile(w.shape[1], tn)
        return w.shape[1], tn, pl.BlockSpec((rows, tn), lambda *g: (0, col_of(*g)))
    tn = _tile(w.shape[2], tn)
    per = w.shape[2] // tn
    return N_CHIPS * w.shape[2], tn, pl.BlockSpec((None, rows, tn), lambda *g: (col_of(*g) // per, 0, col_of(*g) % per))


def norm_mm(x, gain, w, *, tm, tn, name, comm=None):
    M, D = x.shape
    tm = _tile(M, tm)
    N, tn, w_spec = _w_spec(w, D, tn, lambda i, j: j)

    def body(x_ref, g_ref, w_ref, o_ref, hn_ref):
        @pl.when(pl.program_id(1) == 0)
        def _():
            xf = x_ref[...]
            r = lax.rsqrt(jnp.mean(xf * xf, axis=-1, keepdims=True) + EPS)
            hn_ref[...] = (xf * r * g_ref[...]).astype(BF16)
        o_ref[...] = _nn(hn_ref[...], w_ref[...]).astype(BF16)

    return _call(
        body, name=name, grid=(M // tm, N // tn),
        in_specs=[pl.BlockSpec((tm, D), lambda i, j: (i, 0)), pl.BlockSpec((1, D), lambda i, j: (0, 0)), w_spec],
        out_specs=[pl.BlockSpec((tm, tn), lambda i, j: (i, j)), pl.BlockSpec((tm, D), lambda i, j: (i, 0))],
        out_shape=[SDS((M, N), BF16), SDS((M, D), BF16)], sem=("parallel", "arbitrary"), args=(x, gain, w), comm=comm)


def mm_res(a, w, res, *, relu2, tm, tn, tk, name, comm=None):
    M, K = a.shape
    N = w.shape[1]
    tm, tn, tk = _tile(M, tm), _tile(N, tn), _tile(K, tk)

    def body(a_ref, w_ref, r_ref, o_ref):
        @pl.when(pl.program_id(2) == 0)
        def _():
            o_ref[...] = r_ref[...]
        av = a_ref[...]
        if relu2:
            af = jnp.maximum(av.astype(F32), 0.0)
            av = (af * af).astype(BF16)
        o_ref[...] += _nn(av, w_ref[...])

    return _call(
        body, name=name, grid=(M // tm, N // tn, K // tk),
        in_specs=[pl.BlockSpec((tm, tk), lambda i, j, k: (i, k)), pl.BlockSpec((tk, tn), lambda i, j, k: (k, j)),
                  pl.BlockSpec((tm, tn), lambda i, j, k: (i, j))],
        out_specs=[pl.BlockSpec((tm, tn), lambda i, j, k: (i, j))], out_shape=[SDS((M, N), F32)],
        sem=("parallel", "parallel", "arbitrary"), args=(a, w, res), comm=comm)[0]


def mm_nt(dy, w, a, *, tm, tn, name, comm=None):
    M, K = dy.shape
    N = w.shape[0]
    tm, tn = _tile(M, tm), _tile(N, tn)
    with_a = a is not None

    def body(*refs):
        if with_a:
            dy_ref, w_ref, a_ref, o_ref = refs
        else:
            dy_ref, w_ref, o_ref = refs
        r = _nt(dy_ref[...], w_ref[...])
        if with_a:
            r = r * (2.0 * jnp.maximum(a_ref[...].astype(F32), 0.0))
        o_ref[...] = r.astype(BF16)

    in_specs = [pl.BlockSpec((tm, K), lambda i, j: (i, 0)), pl.BlockSpec((tn, K), lambda i, j: (j, 0))]
    args = [dy, w]
    if with_a:
        in_specs.append(pl.BlockSpec((tm, tn), lambda i, j: (i, j)))
        args.append(a)
    return _call(
        body, name=name, grid=(M // tm, N // tn), in_specs=in_specs,
        out_specs=[pl.BlockSpec((tm, tn), lambda i, j: (i, j))], out_shape=[SDS((M, N), BF16)],
        sem=("parallel", "arbitrary"), args=args, comm=comm)[0]


def mm_nt_normbwd(dy, w, x, gain, dx_in, *, tm, tk, name, comm=None):
    M, K = dy.shape
    D = x.shape[1]
    tm = _tile(M, tm)
    _, tk, w_spec = _w_spec(w, D, tk, lambda i, k: k)
    nk = K // tk
    rc = _tile(tm, NORM_BWD_ROWS)

    def body(dy_ref, w_ref, x_ref, g_ref, dxin_ref, dx_ref, dxb_ref, gg_ref, acc):
        i, k = pl.program_id(0), pl.program_id(1)

        @pl.when(k == 0)
        def _():
            acc[...] = jnp.zeros_like(acc)
        acc[...] += _nt(dy_ref[...], w_ref[...])

        @pl.when(k == nk - 1)
        def _():
            def rows(c, part):
                rs = pl.ds(pl.multiple_of(c * rc, rc), rc)
                dh, xf = acc[rs, :], x_ref[rs, :]
                r = lax.rsqrt(jnp.mean(xf * xf, axis=-1, keepdims=True) + EPS)
                xhat = xf * r
                u = dh * g_ref[...]
                dx = dxin_ref[rs, :] + r * (u - xhat * jnp.mean(u * xhat, axis=-1, keepdims=True))
                dx_ref[rs, :] = dx
                dxb_ref[rs, :] = dx.astype(BF16)
                return part + jnp.sum(dh * xhat, axis=0, keepdims=True)

            part = lax.fori_loop(0, tm // rc, rows, jnp.zeros((1, D), F32))

            @pl.when(i == 0)
            def _():
                gg_ref[...] = part

            @pl.when(i > 0)
            def _():
                gg_ref[...] += part

    return _call(
        body, name=name, grid=(M // tm, nk),
        in_specs=[pl.BlockSpec((tm, tk), lambda i, k: (i, k)), w_spec,
                  pl.BlockSpec((tm, D), lambda i, k: (i, 0)), pl.BlockSpec((1, D), lambda i, k: (0, 0)),
                  pl.BlockSpec((tm, D), lambda i, k: (i, 0))],
        out_specs=[pl.BlockSpec((tm, D), lambda i, k: (i, 0)), pl.BlockSpec((tm, D), lambda i, k: (i, 0)),
                   pl.BlockSpec((1, D), lambda i, k: (0, 0))],
        out_shape=[SDS((M, D), F32), SDS((M, D), BF16), SDS((1, D), F32)], scratch_shapes=[pltpu.VMEM((tm, D), F32)],
        sem=("arbitrary", "arbitrary"), args=(dy, w, x, gain, dx_in), comm=comm)


def mm_tn(a, dy, *, relu2, tm, tn, tk, name, out="f32", comm=None):
    S_, Ka = a.shape
    N = dy.shape[1]
    tm, tk = _tile(Ka, tm), _tile(S_, tk)
    tn = _tile(N // N_CHIPS if out == "cols" else N, tn)
    nk = S_ // tk

    def body(a_ref, dy_ref, o_ref, acc):
        k = pl.program_id(2)

        @pl.when(k == 0)
        def _():
            acc[...] = jnp.zeros_like(acc)
        av = a_ref[...]
        if relu2:
            af = jnp.maximum(av.astype(F32), 0.0)
            av = (af * af).astype(BF16)
        acc[...] += _tn(av, dy_ref[...].astype(BF16))

        @pl.when(k == nk - 1)
        def _():
            o_ref[...] = acc[...].astype(o_ref.dtype)

    if out == "cols":
        per = N // N_CHIPS // tn
        o_spec = pl.BlockSpec((None, tm, tn), lambda i, j, k: (j // per, i, j % per))
        o_shape = SDS((N_CHIPS, Ka, N // N_CHIPS), BF16)
    else:
        o_spec = pl.BlockSpec((tm, tn), lambda i, j, k: (i, j))
        o_shape = SDS((Ka, N), F32 if out == "f32" else BF16)
    return _call(
        body, name=name, grid=(Ka // tm, N // tn, nk),
        in_specs=[pl.BlockSpec((tk, tm), lambda i, j, k: (k, i)), pl.BlockSpec((tk, tn), lambda i, j, k: (k, j))],
        out_specs=[o_spec], out_shape=[o_shape], scratch_shapes=[pltpu.VMEM((tm, tn), F32)],
        sem=("parallel", "parallel", "arbitrary"), args=(a, dy), comm=comm)[0]


def loss_head(x, gain, target, *, tm, name):
    M, D = x.shape
    tm = _tile(M, tm)

    def body(x_ref, g_ref, t_ref, dx_ref, dxb_ref, gg_ref, l_ref):
        i = pl.program_id(0)
        xf = x_ref[...]
        r = lax.rsqrt(jnp.mean(xf * xf, axis=-1, keepdims=True) + EPS)
        xhat = xf * r
        e = xhat * g_ref[...] - t_ref[...]
        lpart = 0.5 * jnp.sum(jnp.mean(e * e, axis=-1, keepdims=True), axis=0, keepdims=True)
        dy = e * (1.0 / D)
        u = dy * g_ref[...]
        dx = r * (u - xhat * jnp.mean(u * xhat, axis=-1, keepdims=True))
        dx_ref[...] = dx
        dxb_ref[...] = dx.astype(BF16)
        gpart = jnp.sum(dy * xhat, axis=0, keepdims=True)
        lrow = jnp.broadcast_to(lpart, (1, LANES))

        @pl.when(i == 0)
        def _():
            gg_ref[...] = gpart
            l_ref[...] = lrow

        @pl.when(i > 0)
        def _():
            gg_ref[...] += gpart
            l_ref[...] += lrow

    return pl.pallas_call(
        body, name=name, grid=(M // tm,),
        in_specs=[pl.BlockSpec((tm, D), lambda i: (i, 0)), pl.BlockSpec((1, D), lambda i: (0, 0)),
                  pl.BlockSpec((tm, D), lambda i: (i, 0))],
        out_specs=[pl.BlockSpec((tm, D), lambda i: (i, 0)), pl.BlockSpec((tm, D), lambda i: (i, 0)),
                   pl.BlockSpec((1, D), lambda i: (0, 0)), pl.BlockSpec((1, LANES), lambda i: (0, 0))],
        out_shape=[SDS((M, D), F32), SDS((M, D), BF16), SDS((1, D), F32), SDS((1, LANES), F32)],
        compiler_params=_cp("arbitrary"))(x, gain, target)


def _log_sigmoid(z):
    return jnp.minimum(z, 0.0) - jnp.log(1.0 + jnp.exp(-jnp.abs(z)))


def _chunk_scan(x, row, reverse):
    n = x.shape[0]
    s = 1
    while s < GLA_CHUNK:
        if reverse:
            x = x + jnp.where(row < GLA_CHUNK - s, pltpu.roll(x, n - s, 0), 0.0)
        else:
            x = x + jnp.where(row >= s, pltpu.roll(x, s, 0), 0.0)
        s *= 2
    return x


def _chunk_edge(b, reverse):
    t = b.shape[0]
    nc = t // GLA_CHUNK
    b3 = b.reshape(nc, GLA_CHUNK, b.shape[1])
    e = b3[:, 0:1, :] if reverse else b3[:, GLA_CHUNK - 1:GLA_CHUNK, :]
    return jnp.broadcast_to(e, b3.shape).reshape(b.shape), e.reshape(nc, b.shape[1])


def _gla_gates(lr, up_ref, bias_ref, cs, row, reverse):
    z = _nn(lr, up_ref[:, cs]) + bias_ref[:, cs]
    g = _log_sigmoid(z) * (1.0 / GLA_GATE_NORMALIZER)
    b = _chunk_scan(g, row, reverse)
    bl, blc = _chunk_edge(b, reverse)
    return z, b, bl, blc


def _gla_mask(reverse):
    r = lax.broadcasted_iota(jnp.int32, (GLA_CHUNK, GLA_CHUNK), 0)
    c = lax.broadcasted_iota(jnp.int32, (GLA_CHUNK, GLA_CHUNK), 1)
    return (r <= c) if reverse else (r >= c)


def gla_fwd(proj, upad, bias, *, reverse, name, comm=None):
    S_ = proj.shape[0]
    T = _tile(S_, 512)
    nt, nc = S_ // T, T // GLA_CHUNK
    H, DK, DV, C = GLA_HEADS, GLA_DK, GLA_DV, GLA_CHUNK
    tix = (lambda i: nt - 1 - i) if reverse else (lambda i: i)
    qscale = DK ** -0.5

    def body(q_ref, k_ref, v_ref, lr_ref, up_ref, b_ref, o_ref, st_ref, st_s, qd_s, ki_s, ke_s, dec_s):
        @pl.when(pl.program_id(0) == 0)
        def _():
            st_s[...] = jnp.zeros_like(st_s)
        row = lax.broadcasted_iota(jnp.int32, (T, DK), 0) % C
        mask = _gla_mask(reverse)
        lr = lr_ref[...]
        for h in range(H):
            cs = slice(h * DK, (h + 1) * DK)
            _, b, bl, _ = _gla_gates(lr, up_ref, b_ref, cs, row, reverse)
            q = q_ref[:, cs].astype(F32) * qscale
            k = k_ref[:, cs].astype(F32)
            qd_s[:, cs] = (q * jnp.exp(b)).astype(BF16)
            ki_s[:, cs] = (k * jnp.exp(-b)).astype(BF16)
            ke_s[:, cs] = (k * jnp.exp(bl - b)).astype(BF16)
            dec_s[:, cs] = jnp.exp(bl)

        def chunk(ci, carry):
            c = (nc - 1 - ci) if reverse else ci
            r0 = pl.multiple_of(c * C, C)
            s0 = pl.multiple_of(c * DV, DV)
            for h in range(H):
                cs = slice(h * DK, (h + 1) * DK)
                vs = slice(h * DV, (h + 1) * DV)
                qd, ki, ke = qd_s[pl.ds(r0, C), cs], ki_s[pl.ds(r0, C), cs], ke_s[pl.ds(r0, C), cs]
                v = v_ref[pl.ds(r0, C), vs]
                st = st_s[h]
                stb = st.astype(BF16)
                st_ref[h, pl.ds(s0, DV), :] = stb
                a = jnp.where(mask, _nt(qd, ki), 0.0).astype(BF16)
                o_ref[pl.ds(r0, C), vs] = _nn(a, v) + _nt(qd, stb)
                st_s[h] = dec_s[pl.ds(r0, 1), cs] * st + _tn(v, ke)
            return carry

        lax.fori_loop(0, nc, chunk, 0)

    return _call(
        body, name=name, grid=(nt,),
        in_specs=[pl.BlockSpec((T, QK_W), lambda i: (tix(i), C_GQ // QK_W)),
                  pl.BlockSpec((T, QK_W), lambda i: (tix(i), C_GK // QK_W)),
                  pl.BlockSpec((T, V_W), lambda i: (tix(i), C_GV // V_W)),
                  pl.BlockSpec((T, LANES), lambda i: (tix(i), C_LR // LANES)),
                  pl.BlockSpec((LANES, QK_W), lambda i: (0, 0)), pl.BlockSpec((1, QK_W), lambda i: (0, 0))],
        out_specs=[pl.BlockSpec((T, V_W), lambda i: (tix(i), 0)),
                   pl.BlockSpec((H, nc * DV, DK), lambda i: (0, tix(i), 0))],
        out_shape=[SDS((S_, V_W), F32), SDS((H, S_ // C * DV, DK), BF16)],
        scratch_shapes=[pltpu.VMEM((H, DV, DK), F32), pltpu.VMEM((T, QK_W), BF16), pltpu.VMEM((T, QK_W), BF16),
                        pltpu.VMEM((T, QK_W), BF16), pltpu.VMEM((T, QK_W), F32)],
        sem=("arbitrary",), args=(proj, proj, proj, proj, upad, bias), comm=comm)


def gla_bwd(proj, upad, bias, states, do, prev, *, reverse, name):
    S_ = proj.shape[0]
    T = _tile(S_, 512)
    nt, nc = S_ // T, T // GLA_CHUNK
    H, DK, DV, C = GLA_HEADS, GLA_DK, GLA_DV, GLA_CHUNK
    tix = (lambda i: i) if reverse else (lambda i: nt - 1 - i)
    qscale = DK ** -0.5
    with_prev = prev is not None
    NCP = max(8, nc)

    def body(*refs):
        (q_ref, k_ref, v_ref, lr_ref, up_ref, b_ref, st_ref, do_ref) = refs[:8]
        n_in = 12 if with_prev else 8
        pq_ref, pk_ref, pv_ref, plr_ref = refs[8:12] if with_prev else (None,) * 4
        dq_ref, dk_ref, dv_ref, dlr_ref, dup_ref, dbias_ref = refs[n_in:n_in + 6]
        dst_s, qd_s, ki_s, ke_s, dec_s, dqd_s, dki_s, dke_s, ddec_s, z_s, eb_s, einv_s, eend_s = refs[n_in + 6:]
        first = pl.program_id(0) == 0

        @pl.when(first)
        def _():
            dst_s[...] = jnp.zeros_like(dst_s)
            dup_ref[...] = jnp.zeros_like(dup_ref)
            dbias_ref[...] = jnp.zeros_like(dbias_ref)

        row = lax.broadcasted_iota(jnp.int32, (T, DK), 0) % C
        edge_row = 0 if reverse else C - 1
        mask = _gla_mask(reverse)
        lr = lr_ref[...]
        dlr = plr_ref[...].astype(F32) if with_prev else jnp.zeros((T, LANES), F32)

        for h in range(H):
            cs = slice(h * DK, (h + 1) * DK)
            z, b, bl, _ = _gla_gates(lr, up_ref, b_ref, cs, row, reverse)
            q = q_ref[:, cs].astype(F32) * qscale
            k = k_ref[:, cs].astype(F32)
            eb, einv, eend = jnp.exp(b), jnp.exp(-b), jnp.exp(bl - b)
            z_s[:, cs], eb_s[:, cs], einv_s[:, cs], eend_s[:, cs] = z, eb, einv, eend
            qd_s[:, cs] = (q * eb).astype(BF16)
            ki_s[:, cs] = (k * einv).astype(BF16)
            ke_s[:, cs] = (k * eend).astype(BF16)
            dec_s[:, cs] = jnp.exp(bl)

        def chunk(ci, carry):
            c = ci if reverse else (nc - 1 - ci)
            r0 = pl.multiple_of(c * C, C)
            s0 = pl.multiple_of(c * DV, DV)
            for h in range(H):
                cs = slice(h * DK, (h + 1) * DK)
                vs = slice(h * DV, (h + 1) * DV)
                qdc, kic, kec = qd_s[pl.ds(r0, C), cs], ki_s[pl.ds(r0, C), cs], ke_s[pl.ds(r0, C), cs]
                v = v_ref[pl.ds(r0, C), vs]
                doc = do_ref[pl.ds(r0, C), vs]
                st = st_ref[h, pl.ds(s0, DV), :]
                dst = dst_s[h]
                dstb = dst.astype(BF16)
                a = jnp.where(mask, _nt(qdc, kic), 0.0).astype(BF16)
                da = jnp.where(mask, _nt(doc, v), 0.0).astype(BF16)
                dvc = _tn(a, doc) + _nt(kec, dstb)
                if with_prev:
                    dvc = dvc + pv_ref[pl.ds(r0, C), vs].astype(F32)
                dv_ref[pl.ds(r0, C), vs] = dvc.astype(BF16)
                dqd_s[pl.ds(r0, C), cs] = _nn(da, kic) + _nn(doc, st)
                dki_s[pl.ds(r0, C), cs] = _tn(da, qdc)
                dke_s[pl.ds(r0, C), cs] = _nn(v, dstb)
                ddec_s[h, pl.ds(c, 1), :] = jnp.sum(dst * st.astype(F32), axis=0, keepdims=True)
                dst_s[h] = dec_s[pl.ds(r0, 1), cs] * dst + _tn(doc, qdc)
            return carry

        lax.fori_loop(0, nc, chunk, 0)

        for h in range(H):
            cs = slice(h * DK, (h + 1) * DK)
            z, eb, einv, eend = z_s[:, cs], eb_s[:, cs], einv_s[:, cs], eend_s[:, cs]
            q = q_ref[:, cs].astype(F32) * qscale
            k = k_ref[:, cs].astype(F32)
            qd, ki, ke = q * eb, k * einv, k * eend
            dec = dec_s[:, cs].reshape(nc, C, DK)[:, 0, :]
            dqd, dki, dke = dqd_s[:, cs], dki_s[:, cs], dke_s[:, cs]
            dq = dqd * eb * qscale
            dk = dki * einv + dke * eend
            if with_prev:
                dq = dq + pq_ref[:, cs].astype(F32)
                dk = dk + pk_ref[:, cs].astype(F32)
            dq_ref[:, cs] = dq.astype(BF16)
            dk_ref[:, cs] = dk.astype(BF16)
            wke = dke * ke
            db = dqd * qd - dki * ki - wke
            dbl = jnp.sum(wke.reshape(nc, C, DK), axis=1) + ddec_s[h, 0:nc, :] * dec
            dbl_b = jnp.broadcast_to(dbl.reshape(nc, 1, DK), (nc, C, DK)).reshape(T, DK)
            db = db + jnp.where(row == edge_row, dbl_b, 0.0)
            dg = _chunk_scan(db, row, not reverse)
            dz = dg * (1.0 / GLA_GATE_NORMALIZER) * (1.0 / (1.0 + jnp.exp(z)))
            dzb = dz.astype(BF16)
            dbias_ref[:, cs] += jnp.sum(dz, axis=0, keepdims=True)
            dup_ref[:, cs] += _tn(lr, dzb)
            dlr = dlr + _nt(dzb, up_ref[:, cs])
        dlr_ref[...] = dlr.astype(BF16)

    tile = lambda w, cb: pl.BlockSpec((T, w), lambda i: (tix(i), cb))
    in_specs = [tile(QK_W, C_GQ // QK_W), tile(QK_W, C_GK // QK_W), tile(V_W, C_GV // V_W), tile(LANES, C_LR // LANES),
                pl.BlockSpec((LANES, QK_W), lambda i: (0, 0)), pl.BlockSpec((1, QK_W), lambda i: (0, 0)),
                pl.BlockSpec((H, nc * DV, DK), lambda i: (0, tix(i), 0)), tile(V_W, 0)]
    args = [proj, proj, proj, proj, upad, bias, states, do]
    if with_prev:
        in_specs += [tile(QK_W, 0), tile(QK_W, 0), tile(V_W, 0), tile(LANES, 0)]
        args += list(prev)
    return pl.pallas_call(
        body, name=name, grid=(nt,), in_specs=in_specs,
        out_specs=[tile(QK_W, 0), tile(QK_W, 0), tile(V_W, 0), tile(LANES, 0),
                   pl.BlockSpec((LANES, QK_W), lambda i: (0, 0)), pl.BlockSpec((1, QK_W), lambda i: (0, 0))],
        out_shape=[SDS((S_, QK_W), BF16), SDS((S_, QK_W), BF16), SDS((S_, V_W), BF16), SDS((S_, LANES), BF16),
                   SDS((LANES, QK_W), F32), SDS((1, QK_W), F32)],
        scratch_shapes=[pltpu.VMEM((H, DV, DK), F32), pltpu.VMEM((T, QK_W), BF16), pltpu.VMEM((T, QK_W), BF16),
                        pltpu.VMEM((T, QK_W), BF16), pltpu.VMEM((T, QK_W), F32), pltpu.VMEM((T, QK_W), F32),
                        pltpu.VMEM((T, QK_W), F32), pltpu.VMEM((T, QK_W), F32), pltpu.VMEM((H, NCP, DK), F32)]
        + [pltpu.VMEM((T, QK_W), F32)] * 4,
        compiler_params=_cp("arbitrary"))(*args)


def _head_norm_fwd(o, width):
    out = []
    for h in range(o.shape[1] // width):
        oh = o[:, h * width:(h + 1) * width]
        rs = lax.rsqrt(jnp.mean(oh * oh, axis=-1, keepdims=True) + EPS)
        out.append((oh * rs, rs))
    return out


def gla_post(o_f, o_b, proj, gain, *, name):
    S_ = proj.shape[0]
    T = _tile(S_, 512)

    def body(of_ref, ob_ref, r_ref, g_ref, y_ref):
        o = of_ref[...] + ob_ref[...]
        for h, (ohat, _) in enumerate(_head_norm_fwd(o, GLA_DV)):
            vs = slice(h * GLA_DV, (h + 1) * GLA_DV)
            r = r_ref[:, vs].astype(F32)
            y_ref[:, vs] = (ohat * g_ref[:, vs] * (r / (1.0 + jnp.exp(-r)))).astype(BF16)

    blk = pl.BlockSpec((T, V_W), lambda i: (i, 0))
    return pl.pallas_call(
        body, name=name, grid=(S_ // T,),
        in_specs=[blk, blk, pl.BlockSpec((T, V_W), lambda i: (i, C_GR // V_W)), pl.BlockSpec((1, V_W), lambda i: (0, 0))],
        out_specs=blk, out_shape=SDS((S_, V_W), BF16), compiler_params=_cp("parallel"))(o_f, o_b, proj, gain)


def gla_post_bwd(dmix, o_f, o_b, proj, gain, *, name):
    S_ = proj.shape[0]
    T = _tile(S_, 512)

    def body(dy_ref, of_ref, ob_ref, r_ref, g_ref, do_ref, dr_ref, gg_ref):
        @pl.when(pl.program_id(0) == 0)
        def _():
            gg_ref[...] = jnp.zeros_like(gg_ref)
        o = of_ref[...] + ob_ref[...]
        for h, (ohat, rs) in enumerate(_head_norm_fwd(o, GLA_DV)):
            vs = slice(h * GLA_DV, (h + 1) * GLA_DV)
            r = r_ref[:, vs].astype(F32)
            dy = dy_ref[:, vs].astype(F32)
            sg = 1.0 / (1.0 + jnp.exp(-r))
            gate = r * sg
            n = ohat * g_ref[:, vs]
            dr_ref[:, vs] = (dy * n * (sg * (1.0 + r * (1.0 - sg)))).astype(BF16)
            dn = dy * gate
            u = dn * g_ref[:, vs]
            do_ref[:, vs] = (rs * (u - ohat * jnp.mean(u * ohat, axis=-1, keepdims=True))).astype(BF16)
            gg_ref[:, vs] += jnp.sum(dn * ohat, axis=0, keepdims=True)

    blk = pl.BlockSpec((T, V_W), lambda i: (i, 0))
    vec = pl.BlockSpec((1, V_W), lambda i: (0, 0))
    return pl.pallas_call(
        body, name=name, grid=(S_ // T,),
        in_specs=[blk, blk, blk, pl.BlockSpec((T, V_W), lambda i: (i, C_GR // V_W)), vec],
        out_specs=[blk, blk, vec], out_shape=[SDS((S_, V_W), BF16), SDS((S_, V_W), BF16), SDS((1, V_W), F32)],
        compiler_params=_cp("arbitrary"))(dmix, o_f, o_b, proj, gain)


def _t5_bucket_np(rel):
    half = REL_BUCKETS // 2
    max_exact = half // 2
    ret = np.where(rel > 0, half, 0)
    n = np.abs(rel)
    nf = np.maximum(n, 1).astype(np.float32)
    large = max_exact + (np.log(nf / np.float32(max_exact)) / np.float32(math.log(REL_MAX_DISTANCE / max_exact))
                         * np.float32(half - max_exact)).astype(np.int32)
    large = np.minimum(large, half - 1)
    return ret + np.where(n < max_exact, n, large)


def _rel_onehot(dilation):
    rel = np.arange(-DIL_W, DIL_W + 1)
    return np.eye(REL_BUCKETS, dtype=np.float32)[_t5_bucket_np(rel * dilation)]


def _toeplitz(vec, tq, flip):
    H = vec.shape[0]
    wlen = tq + 2 * DIL_W
    lu = wlen + tq
    if flip:
        vec = vec[:, ::-1]
    u = jnp.full((H, lu), NEG_INF, F32)
    u = lax.dynamic_update_slice(u, vec, (0, 0))
    t = jnp.tile(u, (1, tq))[:, :tq * (lu - 1)].reshape(H, tq, lu - 1)
    return t[:, :, :wlen]


def _untoeplitz(ds, tq):
    H = ds.shape[0]
    wlen = tq + 2 * DIL_W
    lu = wlen + tq
    p = jnp.pad(ds, ((0, 0), (0, 0), (0, lu - 1 - wlen))).reshape(H, tq * (lu - 1))
    p = jnp.pad(p, ((0, 0), (0, tq))).reshape(H, tq, lu)
    return jnp.sum(p, axis=1)[:, :2 * DIL_W + 1]


DIL_TQ = 1024
DIL_SUB = 128


def _dil_band(tq):
    sb = min(tq, DIL_SUB)
    return sb, sb + 2 * DIL_W


def _dil_window(p_ref, o_ref, n_ref, tq):
    return jnp.concatenate([p_ref[tq - DIL_W:, :], o_ref[...], n_ref[:DIL_W, :]], axis=0)


def _dil_specs(tq, nt, col0, ncols_per_r):
    cb = lambda h, r: r * ncols_per_r + col0 + h
    own = pl.BlockSpec((tq, LANES), lambda h, r, i: (i, cb(h, r)))
    prev = pl.BlockSpec((tq, LANES), lambda h, r, i: (jnp.maximum(i - 1, 0), cb(h, r)))
    nxt = pl.BlockSpec((tq, LANES), lambda h, r, i: (jnp.minimum(i + 1, nt - 1), cb(h, r)))
    return prev, own, nxt


def dil_fwd(qkv, bias, *, d, name):
    l = qkv.shape[0]
    tq = _tile(l, DIL_TQ)
    nt = l // tq
    H = DIL_HEADS
    scale = DIL_HEAD_DIM ** -0.5

    sb, bw = _dil_band(tq)

    def body(q_ref, kp, ko, kn, vp, vo, vn, b_ref, o_ref, l_ref):
        i = pl.program_id(2)
        kw = _dil_window(kp, ko, kn, tq)
        vw = _dil_window(vp, vo, vn, tq)
        col = lax.broadcasted_iota(jnp.int32, (sb, bw), 1)
        for j in range(tq // sb):
            rows, win = slice(j * sb, (j + 1) * sb), slice(j * sb, j * sb + bw)
            s = _nt(q_ref[rows, :], kw[win]) * scale + b_ref[0]
            kpos = i * tq + j * sb - DIL_W + col
            s = jnp.where((kpos >= 0) & (kpos < l), s, NEG_INF)
            m = jnp.max(s, axis=-1, keepdims=True)
            p = jnp.exp(s - m)
            den = jnp.sum(p, axis=-1, keepdims=True)
            o_ref[rows, :] = (_nn(p.astype(BF16), vw[win]) / den).astype(BF16)
            l_ref[rows, :] = jnp.broadcast_to(m + jnp.log(den), (sb, LANES))

    _, q_spec, _ = _dil_specs(tq, nt, 0, 12)
    out_spec = pl.BlockSpec((tq, LANES), lambda h, r, i: (i, r * H + h))
    return pl.pallas_call(
        body, name=name, grid=(H, d, nt),
        in_specs=[q_spec, *_dil_specs(tq, nt, 4, 12), *_dil_specs(tq, nt, 8, 12),
                  pl.BlockSpec((1, sb, bw), lambda h, r, i: (h, 0, 0))],
        out_specs=[out_spec, out_spec], out_shape=[SDS((l, d * DIL_WIDTH), BF16), SDS((l, d * DIL_WIDTH), F32)],
        compiler_params=_cp("parallel", "parallel", "arbitrary"))(qkv, qkv, qkv, qkv, qkv, qkv, qkv, bias)


def dil_bwd_q(qkv, bias, do, lse, corr, *, d, name):
    l = qkv.shape[0]
    tq = _tile(l, DIL_TQ)
    nt = l // tq
    H = DIL_HEADS
    scale = DIL_HEAD_DIM ** -0.5

    sb, bw = _dil_band(tq)

    def body(q_ref, kp, ko, kn, vp, vo, vn, b_ref, do_ref, l_ref, c_ref, dq_ref, ds_ref):
        r, i = pl.program_id(1), pl.program_id(2)
        kw = _dil_window(kp, ko, kn, tq)
        vw = _dil_window(vp, vo, vn, tq)
        col = lax.broadcasted_iota(jnp.int32, (sb, bw), 1)
        ds_sum = jnp.zeros((sb, bw), F32)
        for j in range(tq // sb):
            rows, win = slice(j * sb, (j + 1) * sb), slice(j * sb, j * sb + bw)
            s = _nt(q_ref[rows, :], kw[win]) * scale + b_ref[0]
            kpos = i * tq + j * sb - DIL_W + col
            p = jnp.where((kpos >= 0) & (kpos < l), jnp.exp(s - l_ref[rows, 0:1]), 0.0)
            ds = p * (_nt(do_ref[rows, :], vw[win]) - c_ref[rows, 0:1])
            dq_ref[rows, :] = (_nn(ds.astype(BF16), kw[win]) * scale).astype(BF16)
            ds_sum = ds_sum + ds

        @pl.when((r == 0) & (i == 0))
        def _():
            ds_ref[0] = ds_sum

        @pl.when((r > 0) | (i > 0))
        def _():
            ds_ref[0] += ds_sum

    _, q_spec, _ = _dil_specs(tq, nt, 0, 12)
    t_spec = pl.BlockSpec((tq, LANES), lambda h, r, i: (i, r * H + h))
    b_spec = pl.BlockSpec((1, sb, bw), lambda h, r, i: (h, 0, 0))
    return pl.pallas_call(
        body, name=name, grid=(H, d, nt),
        in_specs=[q_spec, *_dil_specs(tq, nt, 4, 12), *_dil_specs(tq, nt, 8, 12), b_spec, t_spec, t_spec, t_spec],
        out_specs=[t_spec, b_spec], out_shape=[SDS((l, d * DIL_WIDTH), BF16), SDS((H, sb, bw), F32)],
        compiler_params=_cp("parallel", "arbitrary", "arbitrary"))(qkv, qkv, qkv, qkv, qkv, qkv, qkv, bias, do, lse, corr)


def dil_bwd_kv(qkv, bias_t, do, lse, corr, *, d, name):
    l = qkv.shape[0]
    tq = _tile(l, DIL_TQ)
    nt = l // tq
    H = DIL_HEADS
    scale = DIL_HEAD_DIM ** -0.5

    sb, bw = _dil_band(tq)

    def body(k_ref, v_ref, qp, qo, qn, dp, do_, dn, lp, lo, ln, cp, co, cn, b_ref, dk_ref, dv_ref):
        i = pl.program_id(2)
        qw = _dil_window(qp, qo, qn, tq)
        dow = _dil_window(dp, do_, dn, tq)
        lrow = _dil_window(lp, lo, ln, tq).T[0:1, :]
        crow = _dil_window(cp, co, cn, tq).T[0:1, :]
        col = lax.broadcasted_iota(jnp.int32, (sb, bw), 1)
        for j in range(tq // sb):
            rows, win = slice(j * sb, (j + 1) * sb), slice(j * sb, j * sb + bw)
            st = _nt(k_ref[rows, :], qw[win]) * scale + b_ref[0]
            qpos = i * tq + j * sb - DIL_W + col
            pt = jnp.where((qpos >= 0) & (qpos < l), jnp.exp(st - lrow[:, win]), 0.0)
            dv_ref[rows, :] = _nn(pt.astype(BF16), dow[win]).astype(BF16)
            dst = pt * (_nt(v_ref[rows, :], dow[win]) - crow[:, win])
            dk_ref[rows, :] = (_nn(dst.astype(BF16), qw[win]) * scale).astype(BF16)

    _, k_spec, _ = _dil_specs(tq, nt, 4, 12)
    _, v_spec, _ = _dil_specs(tq, nt, 8, 12)
    t_spec = pl.BlockSpec((tq, LANES), lambda h, r, i: (i, r * H + h))
    return pl.pallas_call(
        body, name=name, grid=(H, d, nt),
        in_specs=[k_spec, v_spec, *_dil_specs(tq, nt, 0, 12), *_dil_specs(tq, nt, 0, 4), *_dil_specs(tq, nt, 0, 4),
                  *_dil_specs(tq, nt, 0, 4), pl.BlockSpec((1, sb, bw), lambda h, r, i: (h, 0, 0))],
        out_specs=[t_spec, t_spec], out_shape=[SDS((l, d * DIL_WIDTH), BF16), SDS((l, d * DIL_WIDTH), BF16)],
        compiler_params=_cp("parallel", "parallel", "arbitrary"))(
            qkv, qkv, qkv, qkv, qkv, do, do, do, lse, lse, lse, corr, corr, corr, bias_t)


def _dil_weights(l1, l2, l3):
    m = jnp.maximum(jnp.maximum(l1, l2), l3)
    e = [jnp.exp(x - m) for x in (l1, l2, l3)]
    tot = e[0] + e[1] + e[2]
    return [x / tot for x in e]


def dil_combine(os_, ls_, gain, *, name):
    S_ = os_[0].shape[0]
    T = _tile(S_, 512)

    def body(o1, o2, o3, l1, l2, l3, g_ref, y_ref):
        w = _dil_weights(l1[...], l2[...], l3[...])
        o = w[0] * o1[...] + w[1] * o2[...] + w[2] * o3[...]
        for h, (ohat, _) in enumerate(_head_norm_fwd(o, DIL_HEAD_DIM)):
            cs = slice(h * DIL_HEAD_DIM, (h + 1) * DIL_HEAD_DIM)
            y_ref[:, cs] = (ohat * g_ref[:, cs]).astype(BF16)

    blk = pl.BlockSpec((T, DIL_WIDTH), lambda i: (i, 0))
    return pl.pallas_call(
        body, name=name, grid=(S_ // T,), in_specs=[blk] * 6 + [pl.BlockSpec((1, DIL_WIDTH), lambda i: (0, 0))],
        out_specs=blk, out_shape=SDS((S_, DIL_WIDTH), BF16), compiler_params=_cp("parallel"))(*os_, *ls_, gain)


def dil_combine_bwd(dmix, os_, ls_, gain, *, name):
    S_ = os_[0].shape[0]
    T = _tile(S_, 512)
    E = DIL_HEAD_DIM

    def body(dy_ref, o1, o2, o3, l1, l2, l3, g_ref, d1, d2, d3, c1, c2, c3, gg_ref):
        @pl.when(pl.program_id(0) == 0)
        def _():
            gg_ref[...] = jnp.zeros_like(gg_ref)
        w = _dil_weights(l1[...], l2[...], l3[...])
        o = w[0] * o1[...] + w[1] * o2[...] + w[2] * o3[...]
        for h, (ohat, rs) in enumerate(_head_norm_fwd(o, E)):
            cs = slice(h * E, (h + 1) * E)
            dy = dy_ref[:, cs].astype(F32)
            u = dy * g_ref[:, cs]
            do = rs * (u - ohat * jnp.mean(u * ohat, axis=-1, keepdims=True))
            gg_ref[:, cs] += jnp.sum(dy * ohat, axis=0, keepdims=True)
            tot = jnp.sum(do * o[:, cs], axis=-1, keepdims=True)
            for wr, d_ref, c_ref in zip(w, (d1, d2, d3), (c1, c2, c3)):
                d_ref[:, cs] = (wr[:, cs] * do).astype(BF16)
                c_ref[:, cs] = wr[:, cs] * tot

    blk = pl.BlockSpec((T, DIL_WIDTH), lambda i: (i, 0))
    vec = pl.BlockSpec((1, DIL_WIDTH), lambda i: (0, 0))
    return pl.pallas_call(
        body, name=name, grid=(S_ // T,),
        in_specs=[pl.BlockSpec((T, DIL_WIDTH), lambda i: (i, V_W // DIL_WIDTH))] + [blk] * 6 + [vec],
        out_specs=[blk] * 6 + [vec],
        out_shape=[SDS((S_, DIL_WIDTH), BF16)] * 3 + [SDS((S_, DIL_WIDTH), F32)] * 3 + [SDS((1, DIL_WIDTH), F32)],
        compiler_params=_cp("arbitrary"))(dmix, *os_, *ls_, gain)


def _mem_softmax(q, k, scale):
    s = _nt(q, k) * scale
    e = jnp.exp(s - jnp.max(s, axis=-1, keepdims=True))
    return e / jnp.sum(e, axis=-1, keepdims=True)


def mem_fwd(proj, kv, gain, *, name):
    S_ = proj.shape[0]
    M = kv.shape[0]
    T = _tile(S_, 1024)
    E, H = MEM_HEAD_DIM, MEM_HEADS
    scale = E ** -0.5

    def body(q_ref, k_ref, v_ref, g_ref, y_ref):
        p = _mem_softmax(q_ref[...], k_ref[...], scale)
        o = _nn(p.astype(BF16), v_ref[...])
        rs = lax.rsqrt(jnp.mean(o * o, axis=-1, keepdims=True) + EPS)
        y_ref[...] = (o * rs * g_ref[...]).astype(BF16)

    return pl.pallas_call(
        body, name=name, grid=(H, S_ // T),
        in_specs=[pl.BlockSpec((T, E), lambda h, i: (i, C_MQ // E + h)), pl.BlockSpec((M, E), lambda h, i: (0, h)),
                  pl.BlockSpec((M, E), lambda h, i: (0, H + h)), pl.BlockSpec((1, E), lambda h, i: (0, h))],
        out_specs=pl.BlockSpec((T, E), lambda h, i: (i, h)), out_shape=SDS((S_, MEM_WIDTH), BF16),
        compiler_params=_cp("parallel", "parallel"))(proj, kv, kv, gain)


def mem_bwd(dmix, proj, kv, gain, *, name):
    S_ = proj.shape[0]
    M = kv.shape[0]
    T = _tile(S_, 1024)
    E, H = MEM_HEAD_DIM, MEM_HEADS
    scale = E ** -0.5

    def body(dy_ref, q_ref, k_ref, v_ref, g_ref, dq_ref, dk_ref, dv_ref, gg_ref):
        @pl.when(pl.program_id(1) == 0)
        def _():
            dk_ref[...] = jnp.zeros_like(dk_ref)
            dv_ref[...] = jnp.zeros_like(dv_ref)
            gg_ref[...] = jnp.zeros_like(gg_ref)
        q, k, v = q_ref[...], k_ref[...], v_ref[...]
        p = _mem_softmax(q, k, scale)
        pb = p.astype(BF16)
        o = _nn(pb, v)
        rs = lax.rsqrt(jnp.mean(o * o, axis=-1, keepdims=True) + EPS)
        ohat = o * rs
        dy = dy_ref[...].astype(F32)
        u = dy * g_ref[...]
        do = (rs * (u - ohat * jnp.mean(u * ohat, axis=-1, keepdims=True))).astype(BF16)
        gg_ref[...] += jnp.sum(dy * ohat, axis=0, keepdims=True)
        dv_ref[...] += _tn(pb, do)
        dp = _nt(do, v)
        ds = (p * (dp - jnp.sum(p * dp, axis=-1, keepdims=True))).astype(BF16)
        dq_ref[...] = (_nn(ds, k) * scale).astype(BF16)
        dk_ref[...] += _tn(ds, q) * scale

    return pl.pallas_call(
        body, name=name, grid=(H, S_ // T),
        in_specs=[pl.BlockSpec((T, E), lambda h, i: (i, (V_W + DIL_WIDTH) // E + h)),
                  pl.BlockSpec((T, E), lambda h, i: (i, C_MQ // E + h)), pl.BlockSpec((M, E), lambda h, i: (0, h)),
                  pl.BlockSpec((M, E), lambda h, i: (0, H + h)), pl.BlockSpec((1, E), lambda h, i: (0, h))],
        out_specs=[pl.BlockSpec((T, E), lambda h, i: (i, h)), pl.BlockSpec((M, E), lambda h, i: (0, h)),
                   pl.BlockSpec((M, E), lambda h, i: (0, h)), pl.BlockSpec((1, E), lambda h, i: (0, h))],
        out_shape=[SDS((S_, MEM_WIDTH), BF16), SDS((M, MEM_WIDTH), F32), SDS((M, MEM_WIDTH), F32),
                   SDS((1, MEM_WIDTH), F32)],
        compiler_params=_cp("parallel", "arbitrary"))(dmix, proj, kv, kv, gain)


def _strided(a, d):
    return a if d == 1 else a.reshape(a.shape[0] // d, d * a.shape[1])


def _unstrided(a, d):
    return a if d == 1 else a.reshape(a.shape[0] * d, a.shape[1] // d)


def _dil_tables(rel_bias, seq):
    tabs = []
    for d in DIL_DILATIONS:
        sb, _ = _dil_band(min(seq // d, DIL_TQ))
        vec = jnp.sum(jnp.asarray(_rel_onehot(d))[:, :, None] * rel_bias[None], axis=1).T
        tabs.append((_toeplitz(vec, sb, False), _toeplitz(vec, sb, True)))
    return tabs


def _rel_bias_grad(ds_sums, seq):
    g = jnp.zeros((REL_BUCKETS, DIL_HEADS), F32)
    for d, ds in zip(DIL_DILATIONS, ds_sums):
        gvec = _untoeplitz(ds, _dil_band(min(seq // d, DIL_TQ))[0])
        g = g + jnp.sum(jnp.asarray(_rel_onehot(d))[:, :, None] * gvec.T[:, None, :], axis=0)
    return g


def _pad_gate_up(up, row0):
    return jnp.zeros((LANES, QK_W), BF16).at[row0:row0 + GLA_GATE_RANK].set(up.astype(BF16))


def _layer_fwd(x, mem, p, tabs, l, carry):
    tag = f"l{l}"
    proj, hn1 = norm_mm(x, p["norm_mix"], p["w_in"], tm=TM_FWD, tn=IN_TILE, name=f"in_proj_{tag}", comm=carry.get("in"))
    o_f, st_f = gla_fwd(proj, p["up_f"], p["bias_f"], reverse=False, name=f"gla_fwd_f_{tag}", comm=carry.get("gla_f"))
    o_b, st_b = gla_fwd(proj, p["up_b"], p["bias_b"], reverse=True, name=f"gla_fwd_b_{tag}", comm=carry.get("gla_b"))
    gla_out = gla_post(o_f, o_b, proj, p["gla_norm"], name=f"gla_post_{tag}")
    qkv = proj[:, C_DQ:C_MQ]
    os_, ls_, ls_strided = [], [], []
    for d, (bias, _) in zip(DIL_DILATIONS, tabs):
        o, lse = dil_fwd(_strided(qkv, d), bias, d=d, name=f"dil_fwd_d{d}_{tag}")
        os_.append(_unstrided(o, d))
        ls_.append(_unstrided(lse, d))
        ls_strided.append(lse)
    dil_out = dil_combine(os_, ls_, p["dil_norm"], name=f"dil_combine_{tag}")
    kv, hmem = norm_mm(mem, p["mem_norm"], p["w_mem_kv"], tm=256, tn=1024, name=f"mem_kv_{tag}")
    mem_out = mem_fwd(proj, kv, p["mem_out_norm"], name=f"mem_fwd_{tag}")
    mixed = jnp.concatenate([gla_out, dil_out, mem_out], axis=1)
    x1 = mm_res(mixed, p["w_out"], x, relu2=False, tm=TM_FWD, tn=TN, tk=TK, name=f"out_proj_{tag}",
                comm=carry.get("out"))
    a, hn2 = norm_mm(x1, p["norm_mlp"], p["w_up"], tm=TM_FWD, tn=TN, name=f"up_proj_{tag}", comm=carry.get("up"))
    x2 = mm_res(a, p["w_down"], x1, relu2=True, tm=TM_FWD, tn=TN, tk=TK, name=f"down_proj_{tag}",
                comm=carry.get("down"))
    saved = dict(x=x, proj=proj, hn1=hn1, o_f=o_f, o_b=o_b, st_f=st_f, st_b=st_b, qkv=qkv, os=os_, ls=ls_,
                 ls_strided=ls_strided, kv=kv, hmem=hmem, mixed=mixed, x1=x1, a=a, hn2=hn2)
    return x2, saved


class GradScatter:
    SLOTS = {"down_x": [(1, "w_in", 0, 1)], "down_w": [(1, "w_out", 0, 1), (1, "w_mem_kv", 0, 1)],
             "up_w": [(0, "w_down", 0, 2)], "up_x": [(0, "w_down", 1, 2)],
             "in_w": [(0, "w_up", 0, 2)], "in_x": [(0, "w_up", 1, 2)]}
    LAYER0 = {"in_w": [(0, "w_out", 0, 1), (0, "w_mem_kv", 0, 1)], "in_x": [(0, "w_in", 0, 1)]}

    def __init__(self, exchange):
        self.exchange = exchange
        self.ready = {}
        self.taken = set()
        self.sent = []

    def offer(self, l, name, g):
        self.ready[(l, name)] = g

    def _take(self, keys):
        keys = [k for k in keys if k[:2] in self.ready and k not in self.taken]
        if not (keys and self.exchange):
            return None
        bufs = [self.ready[k[:2]] for k in keys]
        rows = [(k[2] * (b.shape[1] // k[3]), b.shape[1] // k[3]) for k, b in zip(keys, bufs)]
        ex = ChipExchange(bufs, scatter=True, rows=rows)
        self.taken.update(keys)
        self.sent.append((keys, ex))
        return ex

    def carrier(self, l, slot):
        plan = self.SLOTS[slot] + (self.LAYER0.get(slot, []) if l == 0 else [])
        return self._take([(l + dl, n, i, m) for dl, n, i, m in plan])

    def flush(self, name):
        pieces = {k[:2]: max([t[3] for t in self.taken if t[:2] == k[:2]] + [1]) for k in self.ready}
        ex = self._take(sorted((l, n, i, m) for (l, n), m in pieces.items() for i in range(m)))
        if ex is not None:
            ex.run(name)

    def received(self):
        got = {}
        for keys, ex in self.sent:
            for k, r in zip(keys, ex.result):
                got.setdefault(k[:2], {})[k[2]] = r
        return {k: [v[i] for i in sorted(v)] for k, v in got.items()}


def _quarters(g):
    return g.reshape(N_CHIPS, g.shape[0] // N_CHIPS, g.shape[1])


def _layer_bwd(dx2, dx2b, mem, p, tabs, s, l, sink):
    tag = f"l{l}"
    seq = dx2.shape[0]
    g = {}
    da = mm_nt(dx2b, p["w_down"], s["a"], tm=TM_NT, tn=TN, name=f"down_bwd_x_{tag}", comm=sink.carrier(l, "down_x"))
    g_down = mm_tn(s["a"], dx2b, relu2=True, tm=TM_W, tn=TN_W, tk=TK_W, out="rows", name=f"down_bwd_w_{tag}",
                   comm=sink.carrier(l, "down_w"))
    sink.offer(l, "w_down", _quarters(g_down))
    g_up = mm_tn(s["hn2"], da, relu2=False, tm=TM_W, tn=TN_W, tk=TK_W, out="cols", name=f"up_bwd_w_{tag}",
                 comm=sink.carrier(l, "up_w"))
    sink.offer(l, "w_up", g_up)
    dx1, dx1b, g["norm_mlp"] = mm_nt_normbwd(da, p["w_up"], s["x1"], p["norm_mlp"], dx2, tm=TM_NORM_BWD, tk=TK_NORM_BWD,
                                       name=f"up_bwd_x_{tag}", comm=sink.carrier(l, "up_x"))
    dmix = mm_nt(dx1b, p["w_out"], None, tm=TM_NT, tn=TN, name=f"out_bwd_x_{tag}")
    sink.offer(l, "w_out", _quarters(mm_tn(s["mixed"], dx1b, relu2=False, tm=TM_W, tn=TN_W, tk=TK_W, out="rows",
                                           name=f"out_bwd_w_{tag}")))
    proj = s["proj"]
    do, dr, g["gla_norm"] = gla_post_bwd(dmix, s["o_f"], s["o_b"], proj, p["gla_norm"], name=f"gla_post_bwd_{tag}")
    r_f = gla_bwd(proj, p["up_f"], p["bias_f"], s["st_f"], do, None, reverse=False, name=f"gla_bwd_f_{tag}")
    r_b = gla_bwd(proj, p["up_b"], p["bias_b"], s["st_b"], do, r_f[:4], reverse=True, name=f"gla_bwd_b_{tag}")
    dgq, dgk, dgv, dlr = r_b[:4]
    g["up_f"], g["bias_f"] = r_f[4][0:GLA_GATE_RANK], r_f[5]
    g["up_b"], g["bias_b"] = r_b[4][GLA_GATE_RANK:2 * GLA_GATE_RANK], r_b[5]
    *branch, g["dil_norm"] = dil_combine_bwd(dmix, s["os"], s["ls"], p["dil_norm"], name=f"dil_combine_bwd_{tag}")
    dos, corrs = branch[:3], branch[3:]
    ddq = ddk = ddv = None
    ds_sums = []
    for r, (d, (bias, bias_t)) in enumerate(zip(DIL_DILATIONS, tabs)):
        qkv_d = _strided(s["qkv"], d)
        do_d, lse_d, corr_d = _strided(dos[r], d), s["ls_strided"][r], _strided(corrs[r], d)
        dq_d, ds_sum = dil_bwd_q(qkv_d, bias, do_d, lse_d, corr_d, d=d, name=f"dil_bwd_q_d{d}_{tag}")
        dk_d, dv_d = dil_bwd_kv(qkv_d, bias_t, do_d, lse_d, corr_d, d=d, name=f"dil_bwd_kv_d{d}_{tag}")
        ds_sums.append(ds_sum)
        dq_d, dk_d, dv_d = [_unstrided(t, d).astype(F32) for t in (dq_d, dk_d, dv_d)]
        ddq, ddk, ddv = (dq_d, dk_d, dv_d) if ddq is None else (ddq + dq_d, ddk + dk_d, ddv + dv_d)
    g["rel_bias"] = _rel_bias_grad(ds_sums, seq)
    dmq, dmk, dmv, g["mem_out_norm"] = mem_bwd(dmix, proj, s["kv"], p["mem_out_norm"], name=f"mem_bwd_{tag}")
    dkv = jnp.concatenate([dmk, dmv], axis=1).astype(BF16)
    sink.offer(l, "w_mem_kv", _quarters(mm_tn(s["hmem"], dkv, relu2=False, tm=TM_W, tn=TN_W, tk=TK_W, out="rows",
                                              name=f"mem_kv_bwd_w_{tag}")))
    _, _, g["mem_norm"] = mm_nt_normbwd(dkv, p["w_mem_kv"], mem, p["mem_norm"], jnp.zeros_like(mem), tm=TM_NORM_BWD,
                                     tk=TK_NORM_BWD, name=f"mem_kv_bwd_x_{tag}")
    dproj = jnp.concatenate([dgq, dgk, dgv, dr, ddq.astype(BF16), ddk.astype(BF16), ddv.astype(BF16), dmq, dlr,
                             jnp.zeros((seq, IN_PAD - C_LR - LANES), BF16)], axis=1)
    g_in = mm_tn(s["hn1"], dproj, relu2=False, tm=TM_W, tn=IN_TILE, tk=TK_W, name=f"in_bwd_w_{tag}",
                 comm=sink.carrier(l, "in_w"))
    sink.offer(l, "w_in", _shard_w_in_grad(g_in))
    dx0, dx0b, g["norm_mix"] = mm_nt_normbwd(dproj, p["w_in"], s["x"], p["norm_mix"], dx1, tm=TM_NORM_BWD, tk=IN_TILE,
                                             name=f"in_bwd_x_{tag}", comm=sink.carrier(l, "in_x"))
    return dx0, dx0b, g


def _shard_w_in_grad(g):
    g = jnp.concatenate([g[:, :LR_COL], g[:, C_LR:C_LR + 2 * GLA_GATE_RANK], g[:, LR_COL:C_LR]], axis=1).astype(BF16)
    w = IN_WIDTH // N_CHIPS
    return jnp.stack([g[:, j * w:(j + 1) * w] for j in range(N_CHIPS)], axis=0)


def local_step(x, mem, target, depth, layer_weights, norm_final, rel_bias, fwd_carry, sink):
    tabs = _dil_tables(rel_bias, x.shape[0])
    saved, layers = [], []
    for l in range(depth):
        layers.append(layer_weights(l))
        x, s = _layer_fwd(x, mem, layers[l], tabs, l, fwd_carry(l))
        saved.append(s)
    dx, dxb, g_final, loss = loss_head(x, norm_final, target, tm=TM_LOSS, name="loss_head")
    grads = [None] * depth
    for l in reversed(range(depth)):
        dx, dxb, grads[l] = _layer_bwd(dx, dxb, mem, layers[l], tabs, saved[l], l, sink)
    return loss, dx, grads, g_final


def sibling_exchange(bufs, *, name):
    n = len(bufs)

    def body(*refs):
        ins, outs = refs[:n], refs[n:2 * n]
        send, recv = refs[2 * n:]
        x, y, c = _mesh_pos()
        cps = [pltpu.make_async_remote_copy(src_ref=ins[a], dst_ref=outs[a], send_sem=send.at[a], recv_sem=recv.at[a],
                                            device_id=(x, y, 1 - c), device_id_type=MESH) for a in range(n)]
        for cp in cps:
            cp.start()
        for cp in cps:
            cp.wait()

    return pl.pallas_call(
        body, name=name, in_specs=[HBM_SPEC] * n, out_specs=[HBM_SPEC] * n,
        out_shape=[SDS(b.shape, b.dtype) for b in bufs],
        scratch_shapes=[pltpu.SemaphoreType.DMA((n,)), pltpu.SemaphoreType.DMA((n,))])(*bufs)


def all_reduce_small(v, *, name):
    R = v.shape[0]

    def body(v_ref, o_ref, slots, send, recv):
        x, y, c = _mesh_pos()
        me = 4 * x + 2 * y + c
        flip = lambda p, f: 1 - p if f else p
        peers = [(flip(x, k & 4), flip(y, k & 2), flip(c, k & 1)) for k in range(1, N_DEV)]
        slots[pl.ds(me, 1)] = v_ref[...][None]
        cps = [pltpu.make_async_remote_copy(src_ref=v_ref, dst_ref=slots.at[me], send_sem=send.at[k], recv_sem=recv.at[k],
                                            device_id=peer, device_id_type=MESH) for k, peer in enumerate(peers)]
        for cp in cps:
            cp.start()
        for k, (px, py, pc) in enumerate(peers):
            pltpu.make_async_remote_copy(src_ref=v_ref, dst_ref=slots.at[4 * px + 2 * py + pc], send_sem=send.at[k],
                                         recv_sem=recv.at[k], device_id=(px, py, pc), device_id_type=MESH).wait_recv()
        for cp in cps:
            cp.wait_send()
        acc = slots[0]
        for s in range(1, N_DEV):
            acc = acc + slots[s]
        o_ref[...] = acc

    vm = pl.BlockSpec(memory_space=pltpu.VMEM)
    return pl.pallas_call(
        body, name=name, in_specs=[vm], out_specs=vm, out_shape=SDS(v.shape, F32),
        scratch_shapes=[pltpu.VMEM((N_DEV, R, LANES), F32), pltpu.SemaphoreType.DMA((N_DEV - 1,)),
                        pltpu.SemaphoreType.DMA((N_DEV - 1,))])(v)


def sum_chips(recv, *, name):
    _, R, C = recv.shape
    tr = _tile(R, 512)

    def body(r_ref, o_ref):
        acc = r_ref[0].astype(F32)
        for j in range(1, N_CHIPS):
            acc = acc + r_ref[j].astype(F32)
        o_ref[...] = acc

    return pl.pallas_call(
        body, name=name, grid=(R // tr,), in_specs=[pl.BlockSpec((N_CHIPS, tr, C), lambda i: (0, i, 0))],
        out_specs=pl.BlockSpec((tr, C), lambda i: (i, 0)), out_shape=SDS((R, C), F32),
        compiler_params=_cp("parallel"))(recv)


def adamw(parts, w, m, v, *, name):
    R, C = w.shape
    n = len(parts)
    tr = _tile(R, 512)
    while 2 * (n + 7) * tr * C * 4 > ELEMWISE_BLOCK_BYTES and tr % 16 == 0:
        tr //= 2
    c1 = 1.0 - ADAM_B1 ** ADAM_STEP
    c2 = 1.0 - ADAM_B2 ** ADAM_STEP

    def body(*refs):
        w_ref, m_ref, v_ref = refs[n:n + 3]
        g_ref, d_ref, nm_ref, nv_ref = refs[n + 3:]
        g = refs[0][...]
        for r in refs[1:n]:
            g = g + r[...]
        nm = ADAM_B1 * m_ref[...] + (1.0 - ADAM_B1) * g
        nv = ADAM_B2 * v_ref[...] + (1.0 - ADAM_B2) * (g * g)
        g_ref[...] = g
        nm_ref[...] = nm
        nv_ref[...] = nv
        d_ref[...] = -ADAM_LR * ((nm / c1) / (jnp.sqrt(nv / c2) + ADAM_EPS) + ADAM_WD * w_ref[...])

    blk = pl.BlockSpec((tr, C), lambda i: (i, 0))
    return pl.pallas_call(
        body, name=name, grid=(R // tr,), in_specs=[blk] * (n + 3), out_specs=[blk] * 4,
        out_shape=[SDS((R, C), F32)] * 4, compiler_params=_cp("parallel"))(*parts, w, m, v)


PACK_ROWS = 8


def _pack(parts):
    rows = []
    for a in parts:
        flat = a.reshape(-1).astype(F32)
        n = -(-flat.shape[0] // (PACK_ROWS * LANES)) * PACK_ROWS * LANES
        rows.append(jnp.pad(flat, (0, n - flat.shape[0])).reshape(-1, LANES))
    return jnp.concatenate(rows, axis=0)


def _unpack(slab, shapes):
    out, r = [], 0
    for shp in shapes:
        n = int(np.prod(shp))
        nr = -(-n // (PACK_ROWS * LANES)) * PACK_ROWS
        out.append(slab[r:r + nr].reshape(-1)[:n].reshape(shp))
        r += nr
    return out


SMALL = ["norm_mix", "gla_gate_bias_fwd", "gla_gate_bias_bwd", "gla_norm", "rel_bias", "dil_norm", "mem_norm",
         "mem_out_norm", "norm_mlp", "norm_final"]
GATE_UPS = ["gla_gate_up_fwd", "gla_gate_up_bwd"]
BIG = ["w_in", "w_mem_kv", "w_out", "w_up", "w_down"]
WEIGHTS = ["norm_mix", "w_in", "gla_gate_up_fwd", "gla_gate_bias_fwd", "gla_gate_up_bwd", "gla_gate_bias_bwd", "gla_norm",
           "rel_bias", "dil_norm", "mem_norm", "w_mem_kv", "mem_out_norm", "w_out", "norm_mlp", "w_up", "w_down",
           "norm_final"]


FWD_CARRY = {"in": [(1, "w_in")], "out": [(1, "w_out"), (1, "w_mem_kv")], "up": [(1, "w_up")], "down": [(1, "w_down")]}
FWD_CARRY_LAYER0 = {"in": ["w_out", "w_mem_kv"], "gla_f": ["w_up"], "gla_b": ["w_down"]}


def _kernel_layout(name, g):
    if name == "w_up":
        return g
    if name == "w_in":
        full = jnp.concatenate([g[j] for j in range(N_CHIPS)], axis=1)
        return jnp.concatenate([full[:, :LR_COL], full[:, LR_COL + 2 * GLA_GATE_RANK:],
                                full[:, LR_COL:LR_COL + 2 * GLA_GATE_RANK],
                                jnp.zeros((full.shape[0], IN_PAD - IN_WIDTH), BF16)], axis=1)
    return g.reshape(-1, g.shape[-1])


def kernel(x, mem, norm_mix, w_in, gla_gate_up_fwd, gla_gate_bias_fwd, gla_gate_up_bwd, gla_gate_bias_bwd, gla_norm, rel_bias, dil_norm, mem_norm, w_mem_kv, mem_out_norm, w_out, norm_mlp, w_up, w_down, norm_final, loss_target, m_norm_mix, m_w_in, m_gla_gate_up_fwd, m_gla_gate_bias_fwd, m_gla_gate_up_bwd, m_gla_gate_bias_bwd, m_gla_norm, m_rel_bias, m_dil_norm, m_mem_norm, m_w_mem_kv, m_mem_out_norm, m_w_out, m_norm_mlp, m_w_up, m_w_down, m_norm_final, v_norm_mix, v_w_in, v_gla_gate_up_fwd, v_gla_gate_bias_fwd, v_gla_gate_up_bwd, v_gla_gate_bias_bwd, v_gla_norm, v_rel_bias, v_dil_norm, v_mem_norm, v_w_mem_kv, v_mem_out_norm, v_w_out, v_norm_mlp, v_w_up, v_w_down, v_norm_final):
    vals = dict(locals())
    W = {n: vals[n] for n in WEIGHTS}
    M1 = {n: vals["m_" + n] for n in WEIGHTS}
    V2 = {n: vals["v_" + n] for n in WEIGHTS}
    depth = w_in.shape[0]
    chip = 2 * lax.axis_index("x") + lax.axis_index("y")

    wb = {n: W[n].astype(BF16) for n in BIG + GATE_UPS}
    first = ChipExchange([wb[n] for n in ["w_in"] + GATE_UPS], scatter=False, layers=[0] + [None] * len(GATE_UPS))
    first.run("gather_first")
    gate_up = dict(zip(GATE_UPS, first.result[1:]))
    pending = {(0, "w_in"): (first, 0)}
    row = lambda a: a.reshape(1, -1)

    def fwd_carry(l):
        plan = {slot: [(l + dl, n) for dl, n in what if l + dl < depth] for slot, what in FWD_CARRY.items()}
        if l == 0:
            for slot, what in FWD_CARRY_LAYER0.items():
                plan[slot] = plan.get(slot, []) + [(0, n) for n in what]
        carry = {}
        for slot, keys in plan.items():
            if keys:
                carry[slot] = ChipExchange([wb[n] for _, n in keys], scatter=False, layers=[k for k, _ in keys])
                pending.update({key: (carry[slot], i) for i, key in enumerate(keys)})
        return carry

    class LayerWeights(dict):
        def __init__(self, l):
            super().__init__()
            self.l = l

        def __missing__(self, n):
            ex, i = pending[(self.l, n)]
            self[n] = _kernel_layout(n, ex.result[i])
            return self[n]

    def layer_weights(l):
        p = LayerWeights(l)
        full_up = lambda n: jnp.concatenate([gate_up[n][j, l] for j in range(N_CHIPS)], axis=1)
        p["up_f"] = _pad_gate_up(full_up("gla_gate_up_fwd"), 0)
        p["up_b"] = _pad_gate_up(full_up("gla_gate_up_bwd"), GLA_GATE_RANK)
        p["bias_f"], p["bias_b"] = row(gla_gate_bias_fwd[l]), row(gla_gate_bias_bwd[l])
        for n in ("norm_mix", "gla_norm", "dil_norm", "mem_norm", "mem_out_norm", "norm_mlp"):
            p[n] = row(W[n][l])
        return p

    sink = GradScatter(exchange=True)
    loss, grad_x, grads, g_final = local_step(x[0], mem[0], loss_target[0], depth, layer_weights, row(norm_final),
                                              rel_bias, fwd_carry, sink)
    sink.flush("scatter_last_grads")
    recv = sink.received()

    flat2 = lambda a: a.reshape(-1, a.shape[-1])
    part = [jnp.concatenate([sum_chips(r, name=f"sum_chips_{n}_l{l}_p{i}") for l in range(depth)
                             for i, r in enumerate(recv[(l, n)])], axis=0) for n in BIG]
    other = sibling_exchange(part, name="swap_partial_sums")
    out = {}
    for n, pa, ob in zip(BIG, part, other):
        res = adamw([pa, ob], flat2(W[n]), flat2(M1[n]), flat2(V2[n]), name=f"adamw_{n}")
        out[n] = [r.reshape(W[n].shape) for r in res]

    small_g = {}
    for n, key in (("norm_mix", "norm_mix"), ("gla_gate_bias_fwd", "bias_f"), ("gla_gate_bias_bwd", "bias_b"),
                   ("gla_norm", "gla_norm"), ("dil_norm", "dil_norm"), ("mem_norm", "mem_norm"),
                   ("mem_out_norm", "mem_out_norm"), ("norm_mlp", "norm_mlp"),
                   ("gla_gate_up_fwd", "up_f"), ("gla_gate_up_bwd", "up_b")):
        small_g[n] = jnp.stack([grads[l][key].reshape(W[n].shape[1:] if n not in GATE_UPS else (GLA_GATE_RANK, QK_W))
                                for l in range(depth)], axis=0)
    small_g["rel_bias"] = sum(grads[l]["rel_bias"] for l in range(depth))
    small_g["norm_final"] = g_final.reshape(-1)
    names = SMALL + GATE_UPS
    slab = all_reduce_small(_pack([small_g[n] for n in names] + [loss[:, 0]]), name="all_reduce_small")
    *summed, loss_sum = _unpack(slab, [small_g[n].shape for n in names] + [(1,)])
    summed = dict(zip(names, summed))
    for n in GATE_UPS:
        summed[n] = lax.dynamic_slice_in_dim(summed[n], chip * LANES, LANES, axis=2)
    names_s = SMALL + GATE_UPS
    shapes = [W[n].shape for n in names_s]
    res = adamw([_pack([summed[n] for n in names_s])], _pack([W[n] for n in names_s]), _pack([M1[n] for n in names_s]),
                _pack([V2[n] for n in names_s]), name="adamw_small")
    for n, *r in zip(names_s, *[_unpack(t, shapes) for t in res]):
        out[n] = r

    return (loss_sum.reshape(()), grad_x[None], *[out[n][0] for n in WEIGHTS], *[out[n][1] for n in WEIGHTS],
            *[out[n][2] for n in WEIGHTS], *[out[n][3] for n in WEIGHTS])
```

```python
import functools
import math

import numpy as np
import jax
import jax.numpy as jnp
from jax import lax
from jax.experimental import pallas as pl
from jax.experimental.pallas import tpu as pltpu

F32, BF16 = jnp.float32, jnp.bfloat16
SDS = jax.ShapeDtypeStruct

DEPTH = 4
GLA_HEADS, GLA_DK, GLA_DV = 4, 128, 256
GLA_GATE_RANK = 16
GLA_GATE_NORMALIZER = 16.0
GLA_CHUNK = 64
DIL_HEADS, DIL_HEAD_DIM = 4, 128
DIL_DILATIONS = (1, 4, 16)
DIL_W = 64
MEM_HEADS, MEM_HEAD_DIM = 4, 128
REL_BUCKETS, REL_MAX_DISTANCE = 32, 1024
EPS = 1e-6
NEG_INF = -1e30
QK_W, V_W, DIL_WIDTH, MEM_WIDTH = 512, 1024, 512, 512
IN_WIDTH = 5152
LR_COL = 3072
IN_PAD = 5376
C_GQ, C_GK, C_GV, C_GR, C_DQ, C_MQ, C_LR = 0, 512, 1024, 2048, 3072, 4608, 5120
IN_TILE = 1792

ADAM_LR, ADAM_B1, ADAM_B2, ADAM_EPS, ADAM_WD, ADAM_STEP = 0.001, 0.9, 0.999, 1e-08, 0.01, 10

V7X_VMEM_BYTES = 64 * 1024 * 1024
VMEM_LIMIT = V7X_VMEM_BYTES * 7 // 8
LANES = 128

TM_FWD = 1024
TM_NT = 1024
TN = 2048
TK = 2048
TM_NORM_BWD = 1024
TK_NORM_BWD = 1024
IN_TILE_K = 768
ONCE = pl.Buffered(1)
NORM_BWD_ROWS = 64
TM_W, TN_W, TK_W = 1024, 1024, 2048
TM_LOSS = 256
ELEMWISE_BLOCK_BYTES = 24 * 1024 * 1024

NT_DIMS = (((1,), (1,)), ((), ()))
TN_DIMS = (((0,), (0,)), ((), ()))


def _nt(a, b):
    return lax.dot_general(a, b, NT_DIMS, preferred_element_type=F32)


def _tn(a, b):
    return lax.dot_general(a, b, TN_DIMS, preferred_element_type=F32)


def _nn(a, b):
    return jnp.dot(a, b, preferred_element_type=F32)


def _cp(*sem):
    return pltpu.CompilerParams(dimension_semantics=sem, vmem_limit_bytes=VMEM_LIMIT)


def _tile(n, t):
    t = min(n, t)
    assert n % t == 0, (n, t)
    return t


MESH = pl.DeviceIdType.MESH
HBM_SPEC = pl.BlockSpec(memory_space=pltpu.HBM)
N_CHIPS = 4
N_DEV = 8


def _mesh_pos():
    return lax.axis_index("x"), lax.axis_index("y"), lax.axis_index("c")


class ChipExchange:
    def __init__(self, bufs, *, scatter, layers=None, rows=None):
        self.bufs = list(bufs)
        self.n = n = len(self.bufs)
        self.scatter = scatter
        self.layers = list(layers) if layers is not None else [None] * n
        self.rows = list(rows) if rows is not None else [None] * n
        dma = pltpu.SemaphoreType.DMA
        if scatter:
            shp = [b.shape if r is None else (b.shape[0], r[1]) + b.shape[2:] for b, r in zip(self.bufs, self.rows)]
            self.out_shape = [SDS(s, b.dtype) for s, b in zip(shp, self.bufs)]
            self.scratch = [dma((3 * n,)), dma((3 * n,)), dma((n,))]
        else:
            shp = [b.shape if l is None else b.shape[1:] for b, l in zip(self.bufs, self.layers)]
            assert all(s[0] % 2 == 0 for s in shp), shp
            self.out_shape = [SDS((N_CHIPS,) + s, b.dtype) for s, b in zip(shp, self.bufs)]
            self.scratch = [dma((3 * n,)), dma((3 * n,)), dma((n,)), dma((3 * n,)), dma((3 * n,))]
        self.result = None

    def _plan(self, ins, outs, sems, finishing):
        x, y, c = _mesh_pos()
        me = 2 * x + y
        chips = [(1 - x, y), (x, 1 - y), (1 - x, 1 - y)]
        remote = lambda src, dst, ss, rs, k, dev: pltpu.make_async_remote_copy(
            src_ref=src, dst_ref=dst, send_sem=ss.at[k], recv_sem=rs.at[k], device_id=dev, device_id_type=MESH)
        kept, sent, landing, passed, handed = [], [], [], [], []
        for a in range(self.n):
            src = ins[a] if self.layers[a] is None else ins[a].at[self.layers[a]]
            if self.scatter:
                r = self.rows[a]
                pick = (lambda j: src.at[j]) if r is None else (lambda j: src.at[j, pl.ds(r[0], r[1])])
                kept.append(pltpu.make_async_copy(pick(me), outs[a].at[me], sems[2].at[a]))
                for j, (px, py) in enumerate(chips):
                    k, peer = 3 * a + j, 2 * px + py
                    sent.append(remote(pick(peer), outs[a].at[me], sems[0], sems[1], k, (px, py, c)))
                    if finishing:
                        landing.append(remote(pick(me), outs[a].at[peer], sems[0], sems[1], k, (px, py, c)))
                continue
            half = self.out_shape[a].shape[1] // 2
            mine, other = pl.ds(c * half, half), pl.ds((1 - c) * half, half)
            kept.append(pltpu.make_async_copy(src, outs[a].at[me], sems[2].at[a]))
            for j, (px, py) in enumerate(chips):
                k, peer = 3 * a + j, 2 * px + py
                sent.append(remote(src.at[mine], outs[a].at[me, mine], sems[0], sems[1], k, (px, py, c)))
                if finishing:
                    landing.append(remote(src.at[mine], outs[a].at[peer, mine], sems[0], sems[1], k, (px, py, c)))
                    passed.append(remote(outs[a].at[peer, mine], outs[a].at[peer, mine], sems[3], sems[4], k, (x, y, 1 - c)))
                    handed.append(remote(outs[a].at[peer, other], outs[a].at[peer, other], sems[3], sems[4], k,
                                         (x, y, 1 - c)))
        return kept, sent, landing, passed, handed

    def start(self, ins, outs, sems):
        kept, sent, _, _, _ = self._plan(ins, outs, sems, False)
        for cp in kept + sent:
            cp.start()

    def finish(self, ins, outs, sems):
        kept, sent, landing, passed, handed = self._plan(ins, outs, sems, True)
        for k, cp in enumerate(landing):
            cp.wait_recv()
            if passed:
                passed[k].start()
        for cp in handed:
            cp.wait_recv()
        for cp in sent + passed:
            cp.wait_send()
        for cp in kept:
            cp.wait()

    def run(self, name):
        n = self.n

        def body(*refs):
            self.start(refs[:n], refs[n:2 * n], refs[2 * n:])
            self.finish(refs[:n], refs[n:2 * n], refs[2 * n:])

        self.result = pl.pallas_call(body, name=name, in_specs=[HBM_SPEC] * n, out_specs=[HBM_SPEC] * n,
                                     out_shape=self.out_shape, scratch_shapes=self.scratch)(*self.bufs)
        return self.result


def _call(body, *, name, grid, in_specs, out_specs, out_shape, sem, args, scratch_shapes=(), comm=None):
    if comm is None:
        return pl.pallas_call(body, name=name, grid=grid, in_specs=list(in_specs), out_specs=list(out_specs),
                              out_shape=list(out_shape), scratch_shapes=list(scratch_shapes),
                              compiler_params=_cp(*sem))(*args)
    n_in, n_out, n_scr, nc = len(in_specs), len(out_shape), len(scratch_shapes), comm.n

    def wrapped(*refs):
        ins, cin = refs[:n_in], refs[n_in:n_in + nc]
        outs, cout = refs[n_in + nc:n_in + nc + n_out], refs[n_in + nc + n_out:n_in + 2 * nc + n_out]
        scr, csem = refs[n_in + 2 * nc + n_out:n_in + 2 * nc + n_out + n_scr], refs[n_in + 2 * nc + n_out + n_scr:]
        ids = [pl.program_id(d) for d in range(len(grid))]
        first = functools.reduce(jnp.logical_and, [i == 0 for i in ids])
        last = functools.reduce(jnp.logical_and, [i == g - 1 for i, g in zip(ids, grid)])

        @pl.when(first)
        def _():
            comm.start(cin, cout, csem)
        body(*ins, *outs, *scr)

        @pl.when(last)
        def _():
            comm.finish(cin, cout, csem)

    res = pl.pallas_call(
        wrapped, name=name, grid=grid, in_specs=list(in_specs) + [HBM_SPEC] * nc,
        out_specs=list(out_specs) + [HBM_SPEC] * nc, out_shape=list(out_shape) + comm.out_shape,
        scratch_shapes=list(scratch_shapes) + comm.scratch,
        compiler_params=_cp(*("arbitrary",) * len(grid)))(*args, *comm.bufs)
    comm.result = res[n_out:]
    return res[:n_out]


def _w_spec(w, rows, tn, col_of):
    if w.ndim == 2:
        tn = _tile(w.shape[1], tn)
        return w.shape[1], tn, pl.BlockSpec((rows, tn), lambda *g: (0, col_of(*g)))
    tn = _tile(w.shape[2], tn)
    per = w.shape[2] // tn
    return N_CHIPS * w.shape[2], tn, pl.BlockSpec((None, rows, tn), lambda *g: (col_of(*g) // per, 0, col_of(*g) % per))


def norm_mm(x, gain, w, *, tm, tn, name, comm=None):
    M, D = x.shape
    tm = _tile(M, tm)
    N, tn, w_spec = _w_spec(w, D, tn, lambda i, j: j)

    rc = _tile(tm, NORM_BWD_ROWS)

    def body(x_ref, g_ref, w_ref, o_ref, hn_ref):
        @pl.when(pl.program_id(1) == 0)
        def _():
            def rows(c, carry):
                rs = pl.ds(pl.multiple_of(c * rc, rc), rc)
                xf = x_ref[rs, :]
                r = lax.rsqrt(jnp.mean(xf * xf, axis=-1, keepdims=True) + EPS)
                hn_ref[rs, :] = (xf * r * g_ref[...]).astype(BF16)
                return carry

            lax.fori_loop(0, tm // rc, rows, 0)
        o_ref[...] = _nn(hn_ref[...], w_ref[...]).astype(BF16)

    return _call(
        body, name=name, grid=(M // tm, N // tn),
        in_specs=[pl.BlockSpec((tm, D), lambda i, j: (i, 0), pipeline_mode=ONCE), pl.BlockSpec((1, D), lambda i, j: (0, 0)),
                  w_spec],
        out_specs=[pl.BlockSpec((tm, tn), lambda i, j: (i, j)), pl.BlockSpec((tm, D), lambda i, j: (i, 0))],
        out_shape=[SDS((M, N), BF16), SDS((M, D), BF16)], sem=("parallel", "arbitrary"), args=(x, gain, w), comm=comm)


def mm_res(a, w, res, *, relu2, tm, tn, tk, name, comm=None):
    M, K = a.shape
    N = w.shape[1]
    tm, tn, tk = _tile(M, tm), _tile(N, tn), _tile(K, tk)

    def body(a_ref, w_ref, r_ref, o_ref):
        @pl.when(pl.program_id(2) == 0)
        def _():
            o_ref[...] = r_ref[...]
        av = a_ref[...]
        if relu2:
            af = jnp.maximum(av.astype(F32), 0.0)
            av = (af * af).astype(BF16)
        o_ref[...] += _nn(av, w_ref[...])

    return _call(
        body, name=name, grid=(M // tm, N // tn, K // tk),
        in_specs=[pl.BlockSpec((tm, tk), lambda i, j, k: (i, k)), pl.BlockSpec((tk, tn), lambda i, j, k: (k, j)),
                  pl.BlockSpec((tm, tn), lambda i, j, k: (i, j), pipeline_mode=ONCE)],
        out_specs=[pl.BlockSpec((tm, tn), lambda i, j, k: (i, j))], out_shape=[SDS((M, N), F32)],
        sem=("parallel", "parallel", "arbitrary"), args=(a, w, res), comm=comm)[0]


def mm_nt(dy, w, a, *, tm, tn, name, comm=None):
    M, K = dy.shape
    N = w.shape[0]
    tm, tn = _tile(M, tm), _tile(N, tn)
    with_a = a is not None

    def body(*refs):
        if with_a:
            dy_ref, w_ref, a_ref, o_ref = refs
        else:
            dy_ref, w_ref, o_ref = refs
        r = _nt(dy_ref[...], w_ref[...])
        if with_a:
            r = r * (2.0 * jnp.maximum(a_ref[...].astype(F32), 0.0))
        o_ref[...] = r.astype(BF16)

    in_specs = [pl.BlockSpec((tm, K), lambda i, j: (i, 0)), pl.BlockSpec((tn, K), lambda i, j: (j, 0))]
    args = [dy, w]
    if with_a:
        in_specs.append(pl.BlockSpec((tm, tn), lambda i, j: (i, j)))
        args.append(a)
    return _call(
        body, name=name, grid=(M // tm, N // tn), in_specs=in_specs,
        out_specs=[pl.BlockSpec((tm, tn), lambda i, j: (i, j))], out_shape=[SDS((M, N), BF16)],
        sem=("parallel", "arbitrary"), args=args, comm=comm)[0]


def mm_nt_normbwd(dy, w, x, gain, dx_in, *, tm, tk, name, comm=None):
    M, K = dy.shape
    D = x.shape[1]
    tm = _tile(M, tm)
    _, tk, w_spec = _w_spec(w, D, tk, lambda i, k: k)
    nk = K // tk
    rc = _tile(tm, NORM_BWD_ROWS)

    def body(dy_ref, w_ref, x_ref, g_ref, dxin_ref, dx_ref, dxb_ref, gg_ref, acc):
        i, k = pl.program_id(0), pl.program_id(1)

        @pl.when(k == 0)
        def _():
            acc[...] = jnp.zeros_like(acc)
        acc[...] += _nt(dy_ref[...], w_ref[...])

        @pl.when(k == nk - 1)
        def _():
            def rows(c, part):
                rs = pl.ds(pl.multiple_of(c * rc, rc), rc)
                dh, xf = acc[rs, :], x_ref[rs, :]
                r = lax.rsqrt(jnp.mean(xf * xf, axis=-1, keepdims=True) + EPS)
                xhat = xf * r
                u = dh * g_ref[...]
                dx = dxin_ref[rs, :] + r * (u - xhat * jnp.mean(u * xhat, axis=-1, keepdims=True))
                dx_ref[rs, :] = dx
                dxb_ref[rs, :] = dx.astype(BF16)
                return part + jnp.sum(dh * xhat, axis=0, keepdims=True)

            part = lax.fori_loop(0, tm // rc, rows, jnp.zeros((1, D), F32))

            @pl.when(i == 0)
            def _():
                gg_ref[...] = part

            @pl.when(i > 0)
            def _():
                gg_ref[...] += part

    return _call(
        body, name=name, grid=(M // tm, nk),
        in_specs=[pl.BlockSpec((tm, tk), lambda i, k: (i, k)), w_spec,
                  pl.BlockSpec((tm, D), lambda i, k: (i, 0), pipeline_mode=ONCE), pl.BlockSpec((1, D), lambda i, k: (0, 0)),
                  pl.BlockSpec((tm, D), lambda i, k: (i, 0), pipeline_mode=ONCE)],
        out_specs=[pl.BlockSpec((tm, D), lambda i, k: (i, 0), pipeline_mode=ONCE),
                   pl.BlockSpec((tm, D), lambda i, k: (i, 0), pipeline_mode=ONCE),
                   pl.BlockSpec((1, D), lambda i, k: (0, 0))],
        out_shape=[SDS((M, D), F32), SDS((M, D), BF16), SDS((1, D), F32)], scratch_shapes=[pltpu.VMEM((tm, D), F32)],
        sem=("arbitrary", "arbitrary"), args=(dy, w, x, gain, dx_in), comm=comm)


def mm_tn(a, dy, *, relu2, tm, tn, tk, name, out="f32", comm=None):
    S_, Ka = a.shape
    N = dy.shape[1]
    tm, tk = _tile(Ka, tm), _tile(S_, tk)
    tn = _tile(N // N_CHIPS if out == "cols" else N, tn)
    nk = S_ // tk

    def body(a_ref, dy_ref, o_ref, acc):
        k = pl.program_id(2)

        @pl.when(k == 0)
        def _():
            acc[...] = jnp.zeros_like(acc)
        av = a_ref[...]
        if relu2:
            af = jnp.maximum(av.astype(F32), 0.0)
            av = (af * af).astype(BF16)
        acc[...] += _tn(av, dy_ref[...].astype(BF16))

        @pl.when(k == nk - 1)
        def _():
            o_ref[...] = acc[...].astype(o_ref.dtype)

    if out == "cols":
        per = N // N_CHIPS // tn
        o_spec = pl.BlockSpec((None, tm, tn), lambda i, j, k: (j // per, i, j % per))
        o_shape = SDS((N_CHIPS, Ka, N // N_CHIPS), BF16)
    else:
        o_spec = pl.BlockSpec((tm, tn), lambda i, j, k: (i, j))
        o_shape = SDS((Ka, N), F32 if out == "f32" else BF16)
    return _call(
        body, name=name, grid=(Ka // tm, N // tn, nk),
        in_specs=[pl.BlockSpec((tk, tm), lambda i, j, k: (k, i)), pl.BlockSpec((tk, tn), lambda i, j, k: (k, j))],
        out_specs=[o_spec], out_shape=[o_shape], scratch_shapes=[pltpu.VMEM((tm, tn), F32)],
        sem=("parallel", "parallel", "arbitrary"), args=(a, dy), comm=comm)[0]


def loss_head(x, gain, target, *, tm, name):
    M, D = x.shape
    tm = _tile(M, tm)

    def body(x_ref, g_ref, t_ref, dx_ref, dxb_ref, gg_ref, l_ref):
        i = pl.program_id(0)
        xf = x_ref[...]
        r = lax.rsqrt(jnp.mean(xf * xf, axis=-1, keepdims=True) + EPS)
        xhat = xf * r
        e = xhat * g_ref[...] - t_ref[...]
        lpart = 0.5 * jnp.sum(jnp.mean(e * e, axis=-1, keepdims=True), axis=0, keepdims=True)
        dy = e * (1.0 / D)
        u = dy * g_ref[...]
        dx = r * (u - xhat * jnp.mean(u * xhat, axis=-1, keepdims=True))
        dx_ref[...] = dx
        dxb_ref[...] = dx.astype(BF16)
        gpart = jnp.sum(dy * xhat, axis=0, keepdims=True)
        lrow = jnp.broadcast_to(lpart, (1, LANES))

        @pl.when(i == 0)
        def _():
            gg_ref[...] = gpart
            l_ref[...] = lrow

        @pl.when(i > 0)
        def _():
            gg_ref[...] += gpart
            l_ref[...] += lrow

    return pl.pallas_call(
        body, name=name, grid=(M // tm,),
        in_specs=[pl.BlockSpec((tm, D), lambda i: (i, 0)), pl.BlockSpec((1, D), lambda i: (0, 0)),
                  pl.BlockSpec((tm, D), lambda i: (i, 0))],
        out_specs=[pl.BlockSpec((tm, D), lambda i: (i, 0)), pl.BlockSpec((tm, D), lambda i: (i, 0)),
                   pl.BlockSpec((1, D), lambda i: (0, 0)), pl.BlockSpec((1, LANES), lambda i: (0, 0))],
        out_shape=[SDS((M, D), F32), SDS((M, D), BF16), SDS((1, D), F32), SDS((1, LANES), F32)],
        compiler_params=_cp("arbitrary"))(x, gain, target)


def _log_sigmoid(z):
    return jnp.minimum(z, 0.0) - jnp.log(1.0 + jnp.exp(-jnp.abs(z)))


def _chunk_scan(x, row, reverse):
    n = x.shape[0]
    s = 1
    while s < GLA_CHUNK:
        if reverse:
            x = x + jnp.where(row < GLA_CHUNK - s, pltpu.roll(x, n - s, 0), 0.0)
        else:
            x = x + jnp.where(row >= s, pltpu.roll(x, s, 0), 0.0)
        s *= 2
    return x


def _chunk_edge(b, reverse):
    t = b.shape[0]
    nc = t // GLA_CHUNK
    b3 = b.reshape(nc, GLA_CHUNK, b.shape[1])
    e = b3[:, 0:1, :] if reverse else b3[:, GLA_CHUNK - 1:GLA_CHUNK, :]
    return jnp.broadcast_to(e, b3.shape).reshape(b.shape), e.reshape(nc, b.shape[1])


def _gla_gates(lr, up_ref, bias_ref, cs, row, reverse):
    z = _nn(lr, up_ref[:, cs]) + bias_ref[:, cs]
    g = _log_sigmoid(z) * (1.0 / GLA_GATE_NORMALIZER)
    b = _chunk_scan(g, row, reverse)
    bl, blc = _chunk_edge(b, reverse)
    return z, b, bl, blc


def _gla_mask(reverse):
    r = lax.broadcasted_iota(jnp.int32, (GLA_CHUNK, GLA_CHUNK), 0)
    c = lax.broadcasted_iota(jnp.int32, (GLA_CHUNK, GLA_CHUNK), 1)
    return (r <= c) if reverse else (r >= c)


def gla_fwd(proj, upad, bias, *, reverse, name, comm=None):
    S_ = proj.shape[0]
    T = _tile(S_, 512)
    nt, nc = S_ // T, T // GLA_CHUNK
    H, DK, DV, C = GLA_HEADS, GLA_DK, GLA_DV, GLA_CHUNK
    tix = (lambda i: nt - 1 - i) if reverse else (lambda i: i)
    qscale = DK ** -0.5

    def body(q_ref, k_ref, v_ref, lr_ref, up_ref, b_ref, o_ref, st_ref, st_s, qd_s, ki_s, ke_s, dec_s):
        @pl.when(pl.program_id(0) == 0)
        def _():
            st_s[...] = jnp.zeros_like(st_s)
        row = lax.broadcasted_iota(jnp.int32, (T, DK), 0) % C
        mask = _gla_mask(reverse)
        lr = lr_ref[...]
        for h in range(H):
            cs = slice(h * DK, (h + 1) * DK)
            _, b, bl, _ = _gla_gates(lr, up_ref, b_ref, cs, row, reverse)
            q = q_ref[:, cs].astype(F32) * qscale
            k = k_ref[:, cs].astype(F32)
            qd_s[:, cs] = (q * jnp.exp(b)).astype(BF16)
            ki_s[:, cs] = (k * jnp.exp(-b)).astype(BF16)
            ke_s[:, cs] = (k * jnp.exp(bl - b)).astype(BF16)
            dec_s[:, cs] = jnp.exp(bl)

        def chunk(ci, carry):
            c = (nc - 1 - ci) if reverse else ci
            r0 = pl.multiple_of(c * C, C)
            s0 = pl.multiple_of(c * DV, DV)
            for h in range(H):
                cs = slice(h * DK, (h + 1) * DK)
                vs = slice(h * DV, (h + 1) * DV)
                qd, ki, ke = qd_s[pl.ds(r0, C), cs], ki_s[pl.ds(r0, C), cs], ke_s[pl.ds(r0, C), cs]
                v = v_ref[pl.ds(r0, C), vs]
                st = st_s[h]
                stb = st.astype(BF16)
                st_ref[h, pl.ds(s0, DV), :] = stb
                a = jnp.where(mask, _nt(qd, ki), 0.0).astype(BF16)
                o_ref[pl.ds(r0, C), vs] = _nn(a, v) + _nt(qd, stb)
                st_s[h] = dec_s[pl.ds(r0, 1), cs] * st + _tn(v, ke)
            return carry

        lax.fori_loop(0, nc, chunk, 0)

    return _call(
        body, name=name, grid=(nt,),
        in_specs=[pl.BlockSpec((T, QK_W), lambda i: (tix(i), C_GQ // QK_W)),
                  pl.BlockSpec((T, QK_W), lambda i: (tix(i), C_GK // QK_W)),
                  pl.BlockSpec((T, V_W), lambda i: (tix(i), C_GV // V_W)),
                  pl.BlockSpec((T, LANES), lambda i: (tix(i), C_LR // LANES)),
                  pl.BlockSpec((LANES, QK_W), lambda i: (0, 0)), pl.BlockSpec((1, QK_W), lambda i: (0, 0))],
        out_specs=[pl.BlockSpec((T, V_W), lambda i: (tix(i), 0)),
                   pl.BlockSpec((H, nc * DV, DK), lambda i: (0, tix(i), 0))],
        out_shape=[SDS((S_, V_W), F32), SDS((H, S_ // C * DV, DK), BF16)],
        scratch_shapes=[pltpu.VMEM((H, DV, DK), F32), pltpu.VMEM((T, QK_W), BF16), pltpu.VMEM((T, QK_W), BF16),
                        pltpu.VMEM((T, QK_W), BF16), pltpu.VMEM((T, QK_W), F32)],
        sem=("arbitrary",), args=(proj, proj, proj, proj, upad, bias), comm=comm)


def gla_bwd(proj, upad, bias, states, do, prev, *, reverse, name):
    S_ = proj.shape[0]
    T = _tile(S_, 512)
    nt, nc = S_ // T, T // GLA_CHUNK
    H, DK, DV, C = GLA_HEADS, GLA_DK, GLA_DV, GLA_CHUNK
    tix = (lambda i: i) if reverse else (lambda i: nt - 1 - i)
    qscale = DK ** -0.5
    with_prev = prev is not None
    NCP = max(8, nc)

    def body(*refs):
        (q_ref, k_ref, v_ref, lr_ref, up_ref, b_ref, st_ref, do_ref) = refs[:8]
        n_in = 12 if with_prev else 8
        pq_ref, pk_ref, pv_ref, plr_ref = refs[8:12] if with_prev else (None,) * 4
        dq_ref, dk_ref, dv_ref, dlr_ref, dup_ref, dbias_ref = refs[n_in:n_in + 6]
        dst_s, qd_s, ki_s, ke_s, dec_s, dqd_s, dki_s, dke_s, ddec_s, z_s, eb_s, einv_s, eend_s = refs[n_in + 6:]
        first = pl.program_id(0) == 0

        @pl.when(first)
        def _():
            dst_s[...] = jnp.zeros_like(dst_s)
            dup_ref[...] = jnp.zeros_like(dup_ref)
            dbias_ref[...] = jnp.zeros_like(dbias_ref)

        row = lax.broadcasted_iota(jnp.int32, (T, DK), 0) % C
        edge_row = 0 if reverse else C - 1
        mask = _gla_mask(reverse)
        lr = lr_ref[...]
        dlr = plr_ref[...].astype(F32) if with_prev else jnp.zeros((T, LANES), F32)

        for h in range(H):
            cs = slice(h * DK, (h + 1) * DK)
            z, b, bl, _ = _gla_gates(lr, up_ref, b_ref, cs, row, reverse)
            q = q_ref[:, cs].astype(F32) * qscale
            k = k_ref[:, cs].astype(F32)
            eb, einv, eend = jnp.exp(b), jnp.exp(-b), jnp.exp(bl - b)
            z_s[:, cs], eb_s[:, cs], einv_s[:, cs], eend_s[:, cs] = z, eb, einv, eend
            qd_s[:, cs] = (q * eb).astype(BF16)
            ki_s[:, cs] = (k * einv).astype(BF16)
            ke_s[:, cs] = (k * eend).astype(BF16)
            dec_s[:, cs] = jnp.exp(bl)

        def chunk(ci, carry):
            c = ci if reverse else (nc - 1 - ci)
            r0 = pl.multiple_of(c * C, C)
            s0 = pl.multiple_of(c * DV, DV)
            for h in range(H):
                cs = slice(h * DK, (h + 1) * DK)
                vs = slice(h * DV, (h + 1) * DV)
                qdc, kic, kec = qd_s[pl.ds(r0, C), cs], ki_s[pl.ds(r0, C), cs], ke_s[pl.ds(r0, C), cs]
                v = v_ref[pl.ds(r0, C), vs]
                doc = do_ref[pl.ds(r0, C), vs]
                st = st_ref[h, pl.ds(s0, DV), :]
                dst = dst_s[h]
                dstb = dst.astype(BF16)
                a = jnp.where(mask, _nt(qdc, kic), 0.0).astype(BF16)
                da = jnp.where(mask, _nt(doc, v), 0.0).astype(BF16)
                dvc = _tn(a, doc) + _nt(kec, dstb)
                if with_prev:
                    dvc = dvc + pv_ref[pl.ds(r0, C), vs].astype(F32)
                dv_ref[pl.ds(r0, C), vs] = dvc.astype(BF16)
                dqd_s[pl.ds(r0, C), cs] = _nn(da, kic) + _nn(doc, st)
                dki_s[pl.ds(r0, C), cs] = _tn(da, qdc)
                dke_s[pl.ds(r0, C), cs] = _nn(v, dstb)
                ddec_s[h, pl.ds(c, 1), :] = jnp.sum(dst * st.astype(F32), axis=0, keepdims=True)
                dst_s[h] = dec_s[pl.ds(r0, 1), cs] * dst + _tn(doc, qdc)
            return carry

        lax.fori_loop(0, nc, chunk, 0)

        for h in range(H):
            cs = slice(h * DK, (h + 1) * DK)
            z, eb, einv, eend = z_s[:, cs], eb_s[:, cs], einv_s[:, cs], eend_s[:, cs]
            q = q_ref[:, cs].astype(F32) * qscale
            k = k_ref[:, cs].astype(F32)
            qd, ki, ke = q * eb, k * einv, k * eend
            dec = dec_s[:, cs].reshape(nc, C, DK)[:, 0, :]
            dqd, dki, dke = dqd_s[:, cs], dki_s[:, cs], dke_s[:, cs]
            dq = dqd * eb * qscale
            dk = dki * einv + dke * eend
            if with_prev:
                dq = dq + pq_ref[:, cs].astype(F32)
                dk = dk + pk_ref[:, cs].astype(F32)
            dq_ref[:, cs] = dq.astype(BF16)
            dk_ref[:, cs] = dk.astype(BF16)
            wke = dke * ke
            db = dqd * qd - dki * ki - wke
            dbl = jnp.sum(wke.reshape(nc, C, DK), axis=1) + ddec_s[h, 0:nc, :] * dec
            dbl_b = jnp.broadcast_to(dbl.reshape(nc, 1, DK), (nc, C, DK)).reshape(T, DK)
            db = db + jnp.where(row == edge_row, dbl_b, 0.0)
            dg = _chunk_scan(db, row, not reverse)
            dz = dg * (1.0 / GLA_GATE_NORMALIZER) * (1.0 / (1.0 + jnp.exp(z)))
            dzb = dz.astype(BF16)
            dbias_ref[:, cs] += jnp.sum(dz, axis=0, keepdims=True)
            dup_ref[:, cs] += _tn(lr, dzb)
            dlr = dlr + _nt(dzb, up_ref[:, cs])
        dlr_ref[...] = dlr.astype(BF16)

    tile = lambda w, cb: pl.BlockSpec((T, w), lambda i: (tix(i), cb))
    in_specs = [tile(QK_W, C_GQ // QK_W), tile(QK_W, C_GK // QK_W), tile(V_W, C_GV // V_W), tile(LANES, C_LR // LANES),
                pl.BlockSpec((LANES, QK_W), lambda i: (0, 0)), pl.BlockSpec((1, QK_W), lambda i: (0, 0)),
                pl.BlockSpec((H, nc * DV, DK), lambda i: (0, tix(i), 0)), tile(V_W, 0)]
    args = [proj, proj, proj, proj, upad, bias, states, do]
    if with_prev:
        in_specs += [tile(QK_W, 0), tile(QK_W, 0), tile(V_W, 0), tile(LANES, 0)]
        args += list(prev)
    return pl.pallas_call(
        body, name=name, grid=(nt,), in_specs=in_specs,
        out_specs=[tile(QK_W, 0), tile(QK_W, 0), tile(V_W, 0), tile(LANES, 0),
                   pl.BlockSpec((LANES, QK_W), lambda i: (0, 0)), pl.BlockSpec((1, QK_W), lambda i: (0, 0))],
        out_shape=[SDS((S_, QK_W), BF16), SDS((S_, QK_W), BF16), SDS((S_, V_W), BF16), SDS((S_, LANES), BF16),
                   SDS((LANES, QK_W), F32), SDS((1, QK_W), F32)],
        scratch_shapes=[pltpu.VMEM((H, DV, DK), F32), pltpu.VMEM((T, QK_W), BF16), pltpu.VMEM((T, QK_W), BF16),
                        pltpu.VMEM((T, QK_W), BF16), pltpu.VMEM((T, QK_W), F32), pltpu.VMEM((T, QK_W), F32),
                        pltpu.VMEM((T, QK_W), F32), pltpu.VMEM((T, QK_W), F32), pltpu.VMEM((H, NCP, DK), F32)]
        + [pltpu.VMEM((T, QK_W), F32)] * 4,
        compiler_params=_cp("arbitrary"))(*args)


def _head_norm_fwd(o, width):
    out = []
    for h in range(o.shape[1] // width):
        oh = o[:, h * width:(h + 1) * width]
        rs = lax.rsqrt(jnp.mean(oh * oh, axis=-1, keepdims=True) + EPS)
        out.append((oh * rs, rs))
    return out


def gla_post(o_f, o_b, proj, gain, *, name):
    S_ = proj.shape[0]
    T = _tile(S_, 512)

    def body(of_ref, ob_ref, r_ref, g_ref, y_ref):
        o = of_ref[...] + ob_ref[...]
        for h, (ohat, _) in enumerate(_head_norm_fwd(o, GLA_DV)):
            vs = slice(h * GLA_DV, (h + 1) * GLA_DV)
            r = r_ref[:, vs].astype(F32)
            y_ref[:, vs] = (ohat * g_ref[:, vs] * (r / (1.0 + jnp.exp(-r)))).astype(BF16)

    blk = pl.BlockSpec((T, V_W), lambda i: (i, 0))
    return pl.pallas_call(
        body, name=name, grid=(S_ // T,),
        in_specs=[blk, blk, pl.BlockSpec((T, V_W), lambda i: (i, C_GR // V_W)), pl.BlockSpec((1, V_W), lambda i: (0, 0))],
        out_specs=blk, out_shape=SDS((S_, V_W), BF16), compiler_params=_cp("parallel"))(o_f, o_b, proj, gain)


def gla_post_bwd(dmix, o_f, o_b, proj, gain, *, name):
    S_ = proj.shape[0]
    T = _tile(S_, 512)

    def body(dy_ref, of_ref, ob_ref, r_ref, g_ref, do_ref, dr_ref, gg_ref):
        @pl.when(pl.program_id(0) == 0)
        def _():
            gg_ref[...] = jnp.zeros_like(gg_ref)
        o = of_ref[...] + ob_ref[...]
        for h, (ohat, rs) in enumerate(_head_norm_fwd(o, GLA_DV)):
            vs = slice(h * GLA_DV, (h + 1) * GLA_DV)
            r = r_ref[:, vs].astype(F32)
            dy = dy_ref[:, vs].astype(F32)
            sg = 1.0 / (1.0 + jnp.exp(-r))
            gate = r * sg
            n = ohat * g_ref[:, vs]
            dr_ref[:, vs] = (dy * n * (sg * (1.0 + r * (1.0 - sg)))).astype(BF16)
            dn = dy * gate
            u = dn * g_ref[:, vs]
            do_ref[:, vs] = (rs * (u - ohat * jnp.mean(u * ohat, axis=-1, keepdims=True))).astype(BF16)
            gg_ref[:, vs] += jnp.sum(dn * ohat, axis=0, keepdims=True)

    blk = pl.BlockSpec((T, V_W), lambda i: (i, 0))
    vec = pl.BlockSpec((1, V_W), lambda i: (0, 0))
    return pl.pallas_call(
        body, name=name, grid=(S_ // T,),
        in_specs=[blk, blk, blk, pl.BlockSpec((T, V_W), lambda i: (i, C_GR // V_W)), vec],
        out_specs=[blk, blk, vec], out_shape=[SDS((S_, V_W), BF16), SDS((S_, V_W), BF16), SDS((1, V_W), F32)],
        compiler_params=_cp("arbitrary"))(dmix, o_f, o_b, proj, gain)


def _t5_bucket_np(rel):
    half = REL_BUCKETS // 2
    max_exact = half // 2
    ret = np.where(rel > 0, half, 0)
    n = np.abs(rel)
    nf = np.maximum(n, 1).astype(np.float32)
    large = max_exact + (np.log(nf / np.float32(max_exact)) / np.float32(math.log(REL_MAX_DISTANCE / max_exact))
                         * np.float32(half - max_exact)).astype(np.int32)
    large = np.minimum(large, half - 1)
    return ret + np.where(n < max_exact, n, large)


def _rel_onehot(dilation):
    rel = np.arange(-DIL_W, DIL_W + 1)
    return np.eye(REL_BUCKETS, dtype=np.float32)[_t5_bucket_np(rel * dilation)]


def _toeplitz(vec, tq, flip):
    H = vec.shape[0]
    wlen = tq + 2 * DIL_W
    lu = wlen + tq
    if flip:
        vec = vec[:, ::-1]
    u = jnp.full((H, lu), NEG_INF, F32)
    u = lax.dynamic_update_slice(u, vec, (0, 0))
    t = jnp.tile(u, (1, tq))[:, :tq * (lu - 1)].reshape(H, tq, lu - 1)
    return t[:, :, :wlen]


def _untoeplitz(ds, tq):
    H = ds.shape[0]
    wlen = tq + 2 * DIL_W
    lu = wlen + tq
    p = jnp.pad(ds, ((0, 0), (0, 0), (0, lu - 1 - wlen))).reshape(H, tq * (lu - 1))
    p = jnp.pad(p, ((0, 0), (0, tq))).reshape(H, tq, lu)
    return jnp.sum(p, axis=1)[:, :2 * DIL_W + 1]


DIL_TQ = 1024
DIL_SUB = 128


def _dil_band(tq):
    sb = min(tq, DIL_SUB)
    return sb, sb + 2 * DIL_W


def _dil_window(p_ref, o_ref, n_ref, tq):
    return jnp.concatenate([p_ref[tq - DIL_W:, :], o_ref[...], n_ref[:DIL_W, :]], axis=0)


def _dil_specs(tq, nt, col0, ncols_per_r):
    cb = lambda h, r: r * ncols_per_r + col0 + h
    own = pl.BlockSpec((tq, LANES), lambda h, r, i: (i, cb(h, r)))
    prev = pl.BlockSpec((tq, LANES), lambda h, r, i: (jnp.maximum(i - 1, 0), cb(h, r)))
    nxt = pl.BlockSpec((tq, LANES), lambda h, r, i: (jnp.minimum(i + 1, nt - 1), cb(h, r)))
    return prev, own, nxt


def dil_fwd(qkv, bias, *, d, name):
    l = qkv.shape[0]
    tq = _tile(l, DIL_TQ)
    nt = l // tq
    H = DIL_HEADS
    scale = DIL_HEAD_DIM ** -0.5

    sb, bw = _dil_band(tq)

    def body(q_ref, kp, ko, kn, vp, vo, vn, b_ref, o_ref, l_ref):
        i = pl.program_id(2)
        kw = _dil_window(kp, ko, kn, tq)
        vw = _dil_window(vp, vo, vn, tq)
        col = lax.broadcasted_iota(jnp.int32, (sb, bw), 1)
        for j in range(tq // sb):
            rows, win = slice(j * sb, (j + 1) * sb), slice(j * sb, j * sb + bw)
            s = _nt(q_ref[rows, :], kw[win]) * scale + b_ref[0]
            kpos = i * tq + j * sb - DIL_W + col
            s = jnp.where((kpos >= 0) & (kpos < l), s, NEG_INF)
            m = jnp.max(s, axis=-1, keepdims=True)
            p = jnp.exp(s - m)
            den = jnp.sum(p, axis=-1, keepdims=True)
            o_ref[rows, :] = (_nn(p.astype(BF16), vw[win]) / den).astype(BF16)
            l_ref[rows, :] = jnp.broadcast_to(m + jnp.log(den), (sb, LANES))

    _, q_spec, _ = _dil_specs(tq, nt, 0, 12)
    out_spec = pl.BlockSpec((tq, LANES), lambda h, r, i: (i, r * H + h))
    return pl.pallas_call(
        body, name=name, grid=(H, d, nt),
        in_specs=[q_spec, *_dil_specs(tq, nt, 4, 12), *_dil_specs(tq, nt, 8, 12),
                  pl.BlockSpec((1, sb, bw), lambda h, r, i: (h, 0, 0))],
        out_specs=[out_spec, out_spec], out_shape=[SDS((l, d * DIL_WIDTH), BF16), SDS((l, d * DIL_WIDTH), F32)],
        compiler_params=_cp("parallel", "parallel", "arbitrary"))(qkv, qkv, qkv, qkv, qkv, qkv, qkv, bias)


def dil_bwd_q(qkv, bias, do, lse, corr, *, d, name):
    l = qkv.shape[0]
    tq = _tile(l, DIL_TQ)
    nt = l // tq
    H = DIL_HEADS
    scale = DIL_HEAD_DIM ** -0.5

    sb, bw = _dil_band(tq)

    def body(q_ref, kp, ko, kn, vp, vo, vn, b_ref, do_ref, l_ref, c_ref, dq_ref, ds_ref):
        r, i = pl.program_id(1), pl.program_id(2)
        kw = _dil_window(kp, ko, kn, tq)
        vw = _dil_window(vp, vo, vn, tq)
        col = lax.broadcasted_iota(jnp.int32, (sb, bw), 1)
        ds_sum = jnp.zeros((sb, bw), F32)
        for j in range(tq // sb):
            rows, win = slice(j * sb, (j + 1) * sb), slice(j * sb, j * sb + bw)
            s = _nt(q_ref[rows, :], kw[win]) * scale + b_ref[0]
            kpos = i * tq + j * sb - DIL_W + col
            p = jnp.where((kpos >= 0) & (kpos < l), jnp.exp(s - l_ref[rows, 0:1]), 0.0)
            ds = p * (_nt(do_ref[rows, :], vw[win]) - c_ref[rows, 0:1])
            dq_ref[rows, :] = (_nn(ds.astype(BF16), kw[win]) * scale).astype(BF16)
            ds_sum = ds_sum + ds

        @pl.when((r == 0) & (i == 0))
        def _():
            ds_ref[0] = ds_sum

        @pl.when((r > 0) | (i > 0))
        def _():
            ds_ref[0] += ds_sum

    _, q_spec, _ = _dil_specs(tq, nt, 0, 12)
    t_spec = pl.BlockSpec((tq, LANES), lambda h, r, i: (i, r * H + h))
    b_spec = pl.BlockSpec((1, sb, bw), lambda h, r, i: (h, 0, 0))
    return pl.pallas_call(
        body, name=name, grid=(H, d, nt),
        in_specs=[q_spec, *_dil_specs(tq, nt, 4, 12), *_dil_specs(tq, nt, 8, 12), b_spec, t_spec, t_spec, t_spec],
        out_specs=[t_spec, b_spec], out_shape=[SDS((l, d * DIL_WIDTH), BF16), SDS((H, sb, bw), F32)],
        compiler_params=_cp("parallel", "arbitrary", "arbitrary"))(qkv, qkv, qkv, qkv, qkv, qkv, qkv, bias, do, lse, corr)


def dil_bwd_kv(qkv, bias_t, do, lse, corr, *, d, name):
    l = qkv.shape[0]
    tq = _tile(l, DIL_TQ)
    nt = l // tq
    H = DIL_HEADS
    scale = DIL_HEAD_DIM ** -0.5

    sb, bw = _dil_band(tq)

    def body(k_ref, v_ref, qp, qo, qn, dp, do_, dn, lp, lo, ln, cp, co, cn, b_ref, dk_ref, dv_ref):
        i = pl.program_id(2)
        qw = _dil_window(qp, qo, qn, tq)
        dow = _dil_window(dp, do_, dn, tq)
        lrow = _dil_window(lp, lo, ln, tq).T[0:1, :]
        crow = _dil_window(cp, co, cn, tq).T[0:1, :]
        col = lax.broadcasted_iota(jnp.int32, (sb, bw), 1)
        for j in range(tq // sb):
            rows, win = slice(j * sb, (j + 1) * sb), slice(j * sb, j * sb + bw)
            st = _nt(k_ref[rows, :], qw[win]) * scale + b_ref[0]
            qpos = i * tq + j * sb - DIL_W + col
            pt = jnp.where((qpos >= 0) & (qpos < l), jnp.exp(st - lrow[:, win]), 0.0)
            dv_ref[rows, :] = _nn(pt.astype(BF16), dow[win]).astype(BF16)
            dst = pt * (_nt(v_ref[rows, :], dow[win]) - crow[:, win])
            dk_ref[rows, :] = (_nn(dst.astype(BF16), qw[win]) * scale).astype(BF16)

    _, k_spec, _ = _dil_specs(tq, nt, 4, 12)
    _, v_spec, _ = _dil_specs(tq, nt, 8, 12)
    t_spec = pl.BlockSpec((tq, LANES), lambda h, r, i: (i, r * H + h))
    return pl.pallas_call(
        body, name=name, grid=(H, d, nt),
        in_specs=[k_spec, v_spec, *_dil_specs(tq, nt, 0, 12), *_dil_specs(tq, nt, 0, 4), *_dil_specs(tq, nt, 0, 4),
                  *_dil_specs(tq, nt, 0, 4), pl.BlockSpec((1, sb, bw), lambda h, r, i: (h, 0, 0))],
        out_specs=[t_spec, t_spec], out_shape=[SDS((l, d * DIL_WIDTH), BF16), SDS((l, d * DIL_WIDTH), BF16)],
        compiler_params=_cp("parallel", "parallel", "arbitrary"))(
            qkv, qkv, qkv, qkv, qkv, do, do, do, lse, lse, lse, corr, corr, corr, bias_t)


def _dil_weights(l1, l2, l3):
    m = jnp.maximum(jnp.maximum(l1, l2), l3)
    e = [jnp.exp(x - m) for x in (l1, l2, l3)]
    tot = e[0] + e[1] + e[2]
    return [x / tot for x in e]


def dil_combine(os_, ls_, gain, *, name):
    S_ = os_[0].shape[0]
    T = _tile(S_, 512)

    def body(o1, o2, o3, l1, l2, l3, g_ref, y_ref):
        w = _dil_weights(l1[...], l2[...], l3[...])
        o = w[0] * o1[...] + w[1] * o2[...] + w[2] * o3[...]
        for h, (ohat, _) in enumerate(_head_norm_fwd(o, DIL_HEAD_DIM)):
            cs = slice(h * DIL_HEAD_DIM, (h + 1) * DIL_HEAD_DIM)
            y_ref[:, cs] = (ohat * g_ref[:, cs]).astype(BF16)

    blk = pl.BlockSpec((T, DIL_WIDTH), lambda i: (i, 0))
    return pl.pallas_call(
        body, name=name, grid=(S_ // T,), in_specs=[blk] * 6 + [pl.BlockSpec((1, DIL_WIDTH), lambda i: (0, 0))],
        out_specs=blk, out_shape=SDS((S_, DIL_WIDTH), BF16), compiler_params=_cp("parallel"))(*os_, *ls_, gain)


def dil_combine_bwd(dmix, os_, ls_, gain, *, name):
    S_ = os_[0].shape[0]
    T = _tile(S_, 512)
    E = DIL_HEAD_DIM

    def body(dy_ref, o1, o2, o3, l1, l2, l3, g_ref, d1, d2, d3, c1, c2, c3, gg_ref):
        @pl.when(pl.program_id(0) == 0)
        def _():
            gg_ref[...] = jnp.zeros_like(gg_ref)
        w = _dil_weights(l1[...], l2[...], l3[...])
        o = w[0] * o1[...] + w[1] * o2[...] + w[2] * o3[...]
        for h, (ohat, rs) in enumerate(_head_norm_fwd(o, E)):
            cs = slice(h * E, (h + 1) * E)
            dy = dy_ref[:, cs].astype(F32)
            u = dy * g_ref[:, cs]
            do = rs * (u - ohat * jnp.mean(u * ohat, axis=-1, keepdims=True))
            gg_ref[:, cs] += jnp.sum(dy * ohat, axis=0, keepdims=True)
            tot = jnp.sum(do * o[:, cs], axis=-1, keepdims=True)
            for wr, d_ref, c_ref in zip(w, (d1, d2, d3), (c1, c2, c3)):
                d_ref[:, cs] = (wr[:, cs] * do).astype(BF16)
                c_ref[:, cs] = wr[:, cs] * tot

    blk = pl.BlockSpec((T, DIL_WIDTH), lambda i: (i, 0))
    vec = pl.BlockSpec((1, DIL_WIDTH), lambda i: (0, 0))
    return pl.pallas_call(
        body, name=name, grid=(S_ // T,),
        in_specs=[pl.BlockSpec((T, DIL_WIDTH), lambda i: (i, V_W // DIL_WIDTH))] + [blk] * 6 + [vec],
        out_specs=[blk] * 6 + [vec],
        out_shape=[SDS((S_, DIL_WIDTH), BF16)] * 3 + [SDS((S_, DIL_WIDTH), F32)] * 3 + [SDS((1, DIL_WIDTH), F32)],
        compiler_params=_cp("arbitrary"))(dmix, *os_, *ls_, gain)


def _mem_softmax(q, k, scale):
    s = _nt(q, k) * scale
    e = jnp.exp(s - jnp.max(s, axis=-1, keepdims=True))
    return e / jnp.sum(e, axis=-1, keepdims=True)


def mem_fwd(proj, kv, gain, *, name):
    S_ = proj.shape[0]
    M = kv.shape[0]
    T = _tile(S_, 1024)
    E, H = MEM_HEAD_DIM, MEM_HEADS
    scale = E ** -0.5

    def body(q_ref, k_ref, v_ref, g_ref, y_ref):
        p = _mem_softmax(q_ref[...], k_ref[...], scale)
        o = _nn(p.astype(BF16), v_ref[...])
        rs = lax.rsqrt(jnp.mean(o * o, axis=-1, keepdims=True) + EPS)
        y_ref[...] = (o * rs * g_ref[...]).astype(BF16)

    return pl.pallas_call(
        body, name=name, grid=(H, S_ // T),
        in_specs=[pl.BlockSpec((T, E), lambda h, i: (i, C_MQ // E + h)), pl.BlockSpec((M, E), lambda h, i: (0, h)),
                  pl.BlockSpec((M, E), lambda h, i: (0, H + h)), pl.BlockSpec((1, E), lambda h, i: (0, h))],
        out_specs=pl.BlockSpec((T, E), lambda h, i: (i, h)), out_shape=SDS((S_, MEM_WIDTH), BF16),
        compiler_params=_cp("parallel", "parallel"))(proj, kv, kv, gain)


def mem_bwd(dmix, proj, kv, gain, *, name):
    S_ = proj.shape[0]
    M = kv.shape[0]
    T = _tile(S_, 1024)
    E, H = MEM_HEAD_DIM, MEM_HEADS
    scale = E ** -0.5

    def body(dy_ref, q_ref, k_ref, v_ref, g_ref, dq_ref, dk_ref, dv_ref, gg_ref):
        @pl.when(pl.program_id(1) == 0)
        def _():
            dk_ref[...] = jnp.zeros_like(dk_ref)
            dv_ref[...] = jnp.zeros_like(dv_ref)
            gg_ref[...] = jnp.zeros_like(gg_ref)
        q, k, v = q_ref[...], k_ref[...], v_ref[...]
        p = _mem_softmax(q, k, scale)
        pb = p.astype(BF16)
        o = _nn(pb, v)
        rs = lax.rsqrt(jnp.mean(o * o, axis=-1, keepdims=True) + EPS)
        ohat = o * rs
        dy = dy_ref[...].astype(F32)
        u = dy * g_ref[...]
        do = (rs * (u - ohat * jnp.mean(u * ohat, axis=-1, keepdims=True))).astype(BF16)
        gg_ref[...] += jnp.sum(dy * ohat, axis=0, keepdims=True)
        dv_ref[...] += _tn(pb, do)
        dp = _nt(do, v)
        ds = (p * (dp - jnp.sum(p * dp, axis=-1, keepdims=True))).astype(BF16)
        dq_ref[...] = (_nn(ds, k) * scale).astype(BF16)
        dk_ref[...] += _tn(ds, q) * scale

    return pl.pallas_call(
        body, name=name, grid=(H, S_ // T),
        in_specs=[pl.BlockSpec((T, E), lambda h, i: (i, (V_W + DIL_WIDTH) // E + h)),
                  pl.BlockSpec((T, E), lambda h, i: (i, C_MQ // E + h)), pl.BlockSpec((M, E), lambda h, i: (0, h)),
                  pl.BlockSpec((M, E), lambda h, i: (0, H + h)), pl.BlockSpec((1, E), lambda h, i: (0, h))],
        out_specs=[pl.BlockSpec((T, E), lambda h, i: (i, h)), pl.BlockSpec((M, E), lambda h, i: (0, h)),
                   pl.BlockSpec((M, E), lambda h, i: (0, h)), pl.BlockSpec((1, E), lambda h, i: (0, h))],
        out_shape=[SDS((S_, MEM_WIDTH), BF16), SDS((M, MEM_WIDTH), F32), SDS((M, MEM_WIDTH), F32),
                   SDS((1, MEM_WIDTH), F32)],
        compiler_params=_cp("parallel", "arbitrary"))(dmix, proj, kv, kv, gain)


def _strided(a, d):
    return a if d == 1 else a.reshape(a.shape[0] // d, d * a.shape[1])


def _unstrided(a, d):
    return a if d == 1 else a.reshape(a.shape[0] * d, a.shape[1] // d)


def _dil_tables(rel_bias, seq):
    tabs = []
    for d in DIL_DILATIONS:
        sb, _ = _dil_band(min(seq // d, DIL_TQ))
        vec = jnp.sum(jnp.asarray(_rel_onehot(d))[:, :, None] * rel_bias[None], axis=1).T
        tabs.append((_toeplitz(vec, sb, False), _toeplitz(vec, sb, True)))
    return tabs


def _rel_bias_grad(ds_sums, seq):
    g = jnp.zeros((REL_BUCKETS, DIL_HEADS), F32)
    for d, ds in zip(DIL_DILATIONS, ds_sums):
        gvec = _untoeplitz(ds, _dil_band(min(seq // d, DIL_TQ))[0])
        g = g + jnp.sum(jnp.asarray(_rel_onehot(d))[:, :, None] * gvec.T[:, None, :], axis=0)
    return g


def _pad_gate_up(up, row0):
    return jnp.zeros((LANES, QK_W), BF16).at[row0:row0 + GLA_GATE_RANK].set(up.astype(BF16))


def _layer_fwd(x, mem, p, tabs, l, carry):
    tag = f"l{l}"
    proj, hn1 = norm_mm(x, p["norm_mix"], p["w_in"], tm=TM_FWD, tn=IN_TILE, name=f"in_proj_{tag}", comm=carry.get("in"))
    o_f, st_f = gla_fwd(proj, p["up_f"], p["bias_f"], reverse=False, name=f"gla_fwd_f_{tag}", comm=carry.get("gla_f"))
    o_b, st_b = gla_fwd(proj, p["up_b"], p["bias_b"], reverse=True, name=f"gla_fwd_b_{tag}", comm=carry.get("gla_b"))
    gla_out = gla_post(o_f, o_b, proj, p["gla_norm"], name=f"gla_post_{tag}")
    qkv = proj[:, C_DQ:C_MQ]
    os_, ls_, ls_strided = [], [], []
    for d, (bias, _) in zip(DIL_DILATIONS, tabs):
        o, lse = dil_fwd(_strided(qkv, d), bias, d=d, name=f"dil_fwd_d{d}_{tag}")
        os_.append(_unstrided(o, d))
        ls_.append(_unstrided(lse, d))
        ls_strided.append(lse)
    dil_out = dil_combine(os_, ls_, p["dil_norm"], name=f"dil_combine_{tag}")
    kv, hmem = norm_mm(mem, p["mem_norm"], p["w_mem_kv"], tm=256, tn=1024, name=f"mem_kv_{tag}")
    mem_out = mem_fwd(proj, kv, p["mem_out_norm"], name=f"mem_fwd_{tag}")
    mixed = jnp.concatenate([gla_out, dil_out, mem_out], axis=1)
    x1 = mm_res(mixed, p["w_out"], x, relu2=False, tm=TM_FWD, tn=TN, tk=TK, name=f"out_proj_{tag}",
                comm=carry.get("out"))
    a, hn2 = norm_mm(x1, p["norm_mlp"], p["w_up"], tm=TM_FWD, tn=TN, name=f"up_proj_{tag}", comm=carry.get("up"))
    x2 = mm_res(a, p["w_down"], x1, relu2=True, tm=TM_FWD, tn=TN, tk=TK, name=f"down_proj_{tag}",
                comm=carry.get("down"))
    saved = dict(x=x, proj=proj, hn1=hn1, o_f=o_f, o_b=o_b, st_f=st_f, st_b=st_b, qkv=qkv, os=os_, ls=ls_,
                 ls_strided=ls_strided, kv=kv, hmem=hmem, mixed=mixed, x1=x1, a=a, hn2=hn2)
    return x2, saved


class GradScatter:
    SLOTS = {"down_x": [(1, "w_in", 0, 1)], "down_w": [(1, "w_out", 0, 1), (1, "w_mem_kv", 0, 1)],
             "up_w": [(0, "w_down", 0, 2)], "up_x": [(0, "w_down", 1, 2)],
             "in_w": [(0, "w_up", 0, 2)], "in_x": [(0, "w_up", 1, 2)]}
    LAYER0 = {"in_w": [(0, "w_out", 0, 1), (0, "w_mem_kv", 0, 1)], "in_x": [(0, "w_in", 0, 1)]}

    def __init__(self, exchange):
        self.exchange = exchange
        self.ready = {}
        self.taken = set()
        self.sent = []

    def offer(self, l, name, g):
        self.ready[(l, name)] = g

    def _take(self, keys):
        keys = [k for k in keys if k[:2] in self.ready and k not in self.taken]
        if not (keys and self.exchange):
            return None
        bufs = [self.ready[k[:2]] for k in keys]
        rows = [(k[2] * (b.shape[1] // k[3]), b.shape[1] // k[3]) for k, b in zip(keys, bufs)]
        ex = ChipExchange(bufs, scatter=True, rows=rows)
        self.taken.update(keys)
        self.sent.append((keys, ex))
        return ex

    def carrier(self, l, slot):
        plan = self.SLOTS[slot] + (self.LAYER0.get(slot, []) if l == 0 else [])
        return self._take([(l + dl, n, i, m) for dl, n, i, m in plan])

    def flush(self, name):
        pieces = {k[:2]: max([t[3] for t in self.taken if t[:2] == k[:2]] + [1]) for k in self.ready}
        ex = self._take(sorted((l, n, i, m) for (l, n), m in pieces.items() for i in range(m)))
        if ex is not None:
            ex.run(name)

    def received(self):
        got = {}
        for keys, ex in self.sent:
            for k, r in zip(keys, ex.result):
                got.setdefault(k[:2], {})[k[2]] = r
        return {k: [v[i] for i in sorted(v)] for k, v in got.items()}


def _quarters(g):
    return g.reshape(N_CHIPS, g.shape[0] // N_CHIPS, g.shape[1])


def _layer_bwd(dx2, dx2b, mem, p, tabs, s, l, sink):
    tag = f"l{l}"
    seq = dx2.shape[0]
    g = {}
    da = mm_nt(dx2b, p["w_down"], s["a"], tm=TM_NT, tn=TN, name=f"down_bwd_x_{tag}", comm=sink.carrier(l, "down_x"))
    g_down = mm_tn(s["a"], dx2b, relu2=True, tm=TM_W, tn=TN_W, tk=TK_W, out="rows", name=f"down_bwd_w_{tag}",
                   comm=sink.carrier(l, "down_w"))
    sink.offer(l, "w_down", _quarters(g_down))
    g_up = mm_tn(s["hn2"], da, relu2=False, tm=TM_W, tn=TN_W, tk=TK_W, out="cols", name=f"up_bwd_w_{tag}",
                 comm=sink.carrier(l, "up_w"))
    sink.offer(l, "w_up", g_up)
    dx1, dx1b, g["norm_mlp"] = mm_nt_normbwd(da, p["w_up"], s["x1"], p["norm_mlp"], dx2, tm=TM_NORM_BWD, tk=TK_NORM_BWD,
                                       name=f"up_bwd_x_{tag}", comm=sink.carrier(l, "up_x"))
    dmix = mm_nt(dx1b, p["w_out"], None, tm=TM_NT, tn=TN, name=f"out_bwd_x_{tag}")
    sink.offer(l, "w_out", _quarters(mm_tn(s["mixed"], dx1b, relu2=False, tm=TM_W, tn=TN_W, tk=TK_W, out="rows",
                                           name=f"out_bwd_w_{tag}")))
    proj = s["proj"]
    do, dr, g["gla_norm"] = gla_post_bwd(dmix, s["o_f"], s["o_b"], proj, p["gla_norm"], name=f"gla_post_bwd_{tag}")
    r_f = gla_bwd(proj, p["up_f"], p["bias_f"], s["st_f"], do, None, reverse=False, name=f"gla_bwd_f_{tag}")
    r_b = gla_bwd(proj, p["up_b"], p["bias_b"], s["st_b"], do, r_f[:4], reverse=True, name=f"gla_bwd_b_{tag}")
    dgq, dgk, dgv, dlr = r_b[:4]
    g["up_f"], g["bias_f"] = r_f[4][0:GLA_GATE_RANK], r_f[5]
    g["up_b"], g["bias_b"] = r_b[4][GLA_GATE_RANK:2 * GLA_GATE_RANK], r_b[5]
    *branch, g["dil_norm"] = dil_combine_bwd(dmix, s["os"], s["ls"], p["dil_norm"], name=f"dil_combine_bwd_{tag}")
    dos, corrs = branch[:3], branch[3:]
    ddq = ddk = ddv = None
    ds_sums = []
    for r, (d, (bias, bias_t)) in enumerate(zip(DIL_DILATIONS, tabs)):
        qkv_d = _strided(s["qkv"], d)
        do_d, lse_d, corr_d = _strided(dos[r], d), s["ls_strided"][r], _strided(corrs[r], d)
        dq_d, ds_sum = dil_bwd_q(qkv_d, bias, do_d, lse_d, corr_d, d=d, name=f"dil_bwd_q_d{d}_{tag}")
        dk_d, dv_d = dil_bwd_kv(qkv_d, bias_t, do_d, lse_d, corr_d, d=d, name=f"dil_bwd_kv_d{d}_{tag}")
        ds_sums.append(ds_sum)
        dq_d, dk_d, dv_d = [_unstrided(t, d).astype(F32) for t in (dq_d, dk_d, dv_d)]
        ddq, ddk, ddv = (dq_d, dk_d, dv_d) if ddq is None else (ddq + dq_d, ddk + dk_d, ddv + dv_d)
    g["rel_bias"] = _rel_bias_grad(ds_sums, seq)
    dmq, dmk, dmv, g["mem_out_norm"] = mem_bwd(dmix, proj, s["kv"], p["mem_out_norm"], name=f"mem_bwd_{tag}")
    dkv = jnp.concatenate([dmk, dmv], axis=1).astype(BF16)
    sink.offer(l, "w_mem_kv", _quarters(mm_tn(s["hmem"], dkv, relu2=False, tm=TM_W, tn=TN_W, tk=TK_W, out="rows",
                                              name=f"mem_kv_bwd_w_{tag}")))
    _, _, g["mem_norm"] = mm_nt_normbwd(dkv, p["w_mem_kv"], mem, p["mem_norm"], jnp.zeros_like(mem), tm=TM_NORM_BWD,
                                     tk=TK_NORM_BWD, name=f"mem_kv_bwd_x_{tag}")
    dproj = jnp.concatenate([dgq, dgk, dgv, dr, ddq.astype(BF16), ddk.astype(BF16), ddv.astype(BF16), dmq, dlr,
                             jnp.zeros((seq, IN_PAD - C_LR - LANES), BF16)], axis=1)
    g_in = mm_tn(s["hn1"], dproj, relu2=False, tm=TM_W, tn=IN_TILE, tk=TK_W, name=f"in_bwd_w_{tag}",
                 comm=sink.carrier(l, "in_w"))
    sink.offer(l, "w_in", _shard_w_in_grad(g_in))
    dx0, dx0b, g["norm_mix"] = mm_nt_normbwd(dproj, p["w_in"], s["x"], p["norm_mix"], dx1, tm=TM_NORM_BWD, tk=IN_TILE_K,
                                             name=f"in_bwd_x_{tag}", comm=sink.carrier(l, "in_x"))
    return dx0, dx0b, g


def _shard_w_in_grad(g):
    g = jnp.concatenate([g[:, :LR_COL], g[:, C_LR:C_LR + 2 * GLA_GATE_RANK], g[:, LR_COL:C_LR]], axis=1).astype(BF16)
    w = IN_WIDTH // N_CHIPS
    return jnp.stack([g[:, j * w:(j + 1) * w] for j in range(N_CHIPS)], axis=0)


def local_step(x, mem, target, depth, layer_weights, norm_final, rel_bias, fwd_carry, sink):
    tabs = _dil_tables(rel_bias, x.shape[0])
    saved, layers = [], []
    for l in range(depth):
        layers.append(layer_weights(l))
        x, s = _layer_fwd(x, mem, layers[l], tabs, l, fwd_carry(l))
        saved.append(s)
    dx, dxb, g_final, loss = loss_head(x, norm_final, target, tm=TM_LOSS, name="loss_head")
    grads = [None] * depth
    for l in reversed(range(depth)):
        dx, dxb, grads[l] = _layer_bwd(dx, dxb, mem, layers[l], tabs, saved[l], l, sink)
    return loss, dx, grads, g_final


def sibling_exchange(bufs, *, name):
    n = len(bufs)

    def body(*refs):
        ins, outs = refs[:n], refs[n:2 * n]
        send, recv = refs[2 * n:]
        x, y, c = _mesh_pos()
        cps = [pltpu.make_async_remote_copy(src_ref=ins[a], dst_ref=outs[a], send_sem=send.at[a], recv_sem=recv.at[a],
                                            device_id=(x, y, 1 - c), device_id_type=MESH) for a in range(n)]
        for cp in cps:
            cp.start()
        for cp in cps:
            cp.wait()

    return pl.pallas_call(
        body, name=name, in_specs=[HBM_SPEC] * n, out_specs=[HBM_SPEC] * n,
        out_shape=[SDS(b.shape, b.dtype) for b in bufs],
        scratch_shapes=[pltpu.SemaphoreType.DMA((n,)), pltpu.SemaphoreType.DMA((n,))])(*bufs)


def all_reduce_small(v, *, name):
    R = v.shape[0]

    def body(v_ref, o_ref, slots, send, recv):
        x, y, c = _mesh_pos()
        me = 4 * x + 2 * y + c
        flip = lambda p, f: 1 - p if f else p
        peers = [(flip(x, k & 4), flip(y, k & 2), flip(c, k & 1)) for k in range(1, N_DEV)]
        slots[pl.ds(me, 1)] = v_ref[...][None]
        cps = [pltpu.make_async_remote_copy(src_ref=v_ref, dst_ref=slots.at[me], send_sem=send.at[k], recv_sem=recv.at[k],
                                            device_id=peer, device_id_type=MESH) for k, peer in enumerate(peers)]
        for cp in cps:
            cp.start()
        for k, (px, py, pc) in enumerate(peers):
            pltpu.make_async_remote_copy(src_ref=v_ref, dst_ref=slots.at[4 * px + 2 * py + pc], send_sem=send.at[k],
                                         recv_sem=recv.at[k], device_id=(px, py, pc), device_id_type=MESH).wait_recv()
        for cp in cps:
            cp.wait_send()
        acc = slots[0]
        for s in range(1, N_DEV):
            acc = acc + slots[s]
        o_ref[...] = acc

    vm = pl.BlockSpec(memory_space=pltpu.VMEM)
    return pl.pallas_call(
        body, name=name, in_specs=[vm], out_specs=vm, out_shape=SDS(v.shape, F32),
        scratch_shapes=[pltpu.VMEM((N_DEV, R, LANES), F32), pltpu.SemaphoreType.DMA((N_DEV - 1,)),
                        pltpu.SemaphoreType.DMA((N_DEV - 1,))])(v)


def sum_chips(recv, *, name):
    _, R, C = recv.shape
    tr = _tile(R, 512)

    def body(r_ref, o_ref):
        acc = r_ref[0].astype(F32)
        for j in range(1, N_CHIPS):
            acc = acc + r_ref[j].astype(F32)
        o_ref[...] = acc

    return pl.pallas_call(
        body, name=name, grid=(R // tr,), in_specs=[pl.BlockSpec((N_CHIPS, tr, C), lambda i: (0, i, 0))],
        out_specs=pl.BlockSpec((tr, C), lambda i: (i, 0)), out_shape=SDS((R, C), F32),
        compiler_params=_cp("parallel"))(recv)


def adamw(parts, w, m, v, *, name):
    R, C = w.shape
    n = len(parts)
    tr = _tile(R, 512)
    while 2 * (n + 7) * tr * C * 4 > ELEMWISE_BLOCK_BYTES and tr % 16 == 0:
        tr //= 2
    c1 = 1.0 - ADAM_B1 ** ADAM_STEP
    c2 = 1.0 - ADAM_B2 ** ADAM_STEP

    def body(*refs):
        w_ref, m_ref, v_ref = refs[n:n + 3]
        g_ref, d_ref, nm_ref, nv_ref = refs[n + 3:]
        g = refs[0][...]
        for r in refs[1:n]:
            g = g + r[...]
        nm = ADAM_B1 * m_ref[...] + (1.0 - ADAM_B1) * g
        nv = ADAM_B2 * v_ref[...] + (1.0 - ADAM_B2) * (g * g)
        g_ref[...] = g
        nm_ref[...] = nm
        nv_ref[...] = nv
        d_ref[...] = -ADAM_LR * ((nm / c1) / (jnp.sqrt(nv / c2) + ADAM_EPS) + ADAM_WD * w_ref[...])

    blk = pl.BlockSpec((tr, C), lambda i: (i, 0))
    return pl.pallas_call(
        body, name=name, grid=(R // tr,), in_specs=[blk] * (n + 3), out_specs=[blk] * 4,
        out_shape=[SDS((R, C), F32)] * 4, compiler_params=_cp("parallel"))(*parts, w, m, v)


PACK_ROWS = 8


def _pack(parts):
    rows = []
    for a in parts:
        flat = a.reshape(-1).astype(F32)
        n = -(-flat.shape[0] // (PACK_ROWS * LANES)) * PACK_ROWS * LANES
        rows.append(jnp.pad(flat, (0, n - flat.shape[0])).reshape(-1, LANES))
    return jnp.concatenate(rows, axis=0)


def _unpack(slab, shapes):
    out, r = [], 0
    for shp in shapes:
        n = int(np.prod(shp))
        nr = -(-n // (PACK_ROWS * LANES)) * PACK_ROWS
        out.append(slab[r:r + nr].reshape(-1)[:n].reshape(shp))
        r += nr
    return out


SMALL = ["norm_mix", "gla_gate_bias_fwd", "gla_gate_bias_bwd", "gla_norm", "rel_bias", "dil_norm", "mem_norm",
         "mem_out_norm", "norm_mlp", "norm_final"]
GATE_UPS = ["gla_gate_up_fwd", "gla_gate_up_bwd"]
BIG = ["w_in", "w_mem_kv", "w_out", "w_up", "w_down"]
WEIGHTS = ["norm_mix", "w_in", "gla_gate_up_fwd", "gla_gate_bias_fwd", "gla_gate_up_bwd", "gla_gate_bias_bwd", "gla_norm",
           "rel_bias", "dil_norm", "mem_norm", "w_mem_kv", "mem_out_norm", "w_out", "norm_mlp", "w_up", "w_down",
           "norm_final"]


FWD_CARRY = {"in": [(1, "w_in")], "out": [(1, "w_out"), (1, "w_mem_kv")], "up": [(1, "w_up")], "down": [(1, "w_down")]}
FWD_CARRY_LAYER0 = {"in": ["w_out", "w_mem_kv"], "gla_f": ["w_up"], "gla_b": ["w_down"]}


def _kernel_layout(name, g):
    if name == "w_up":
        return g
    if name == "w_in":
        full = jnp.concatenate([g[j] for j in range(N_CHIPS)], axis=1)
        return jnp.concatenate([full[:, :LR_COL], full[:, LR_COL + 2 * GLA_GATE_RANK:],
                                full[:, LR_COL:LR_COL + 2 * GLA_GATE_RANK],
                                jnp.zeros((full.shape[0], IN_PAD - IN_WIDTH), BF16)], axis=1)
    return g.reshape(-1, g.shape[-1])


def kernel(x, mem, norm_mix, w_in, gla_gate_up_fwd, gla_gate_bias_fwd, gla_gate_up_bwd, gla_gate_bias_bwd, gla_norm, rel_bias, dil_norm, mem_norm, w_mem_kv, mem_out_norm, w_out, norm_mlp, w_up, w_down, norm_final, loss_target, m_norm_mix, m_w_in, m_gla_gate_up_fwd, m_gla_gate_bias_fwd, m_gla_gate_up_bwd, m_gla_gate_bias_bwd, m_gla_norm, m_rel_bias, m_dil_norm, m_mem_norm, m_w_mem_kv, m_mem_out_norm, m_w_out, m_norm_mlp, m_w_up, m_w_down, m_norm_final, v_norm_mix, v_w_in, v_gla_gate_up_fwd, v_gla_gate_bias_fwd, v_gla_gate_up_bwd, v_gla_gate_bias_bwd, v_gla_norm, v_rel_bias, v_dil_norm, v_mem_norm, v_w_mem_kv, v_mem_out_norm, v_w_out, v_norm_mlp, v_w_up, v_w_down, v_norm_final):
    vals = dict(locals())
    W = {n: vals[n] for n in WEIGHTS}
    M1 = {n: vals["m_" + n] for n in WEIGHTS}
    V2 = {n: vals["v_" + n] for n in WEIGHTS}
    depth = w_in.shape[0]
    chip = 2 * lax.axis_index("x") + lax.axis_index("y")

    wb = {n: W[n].astype(BF16) for n in BIG + GATE_UPS}
    first = ChipExchange([wb[n] for n in ["w_in"] + GATE_UPS], scatter=False, layers=[0] + [None] * len(GATE_UPS))
    first.run("gather_first")
    gate_up = dict(zip(GATE_UPS, first.result[1:]))
    pending = {(0, "w_in"): (first, 0)}
    row = lambda a: a.reshape(1, -1)

    def fwd_carry(l):
        plan = {slot: [(l + dl, n) for dl, n in what if l + dl < depth] for slot, what in FWD_CARRY.items()}
        if l == 0:
            for slot, what in FWD_CARRY_LAYER0.items():
                plan[slot] = plan.get(slot, []) + [(0, n) for n in what]
        carry = {}
        for slot, keys in plan.items():
            if keys:
                carry[slot] = ChipExchange([wb[n] for _, n in keys], scatter=False, layers=[k for k, _ in keys])
                pending.update({key: (carry[slot], i) for i, key in enumerate(keys)})
        return carry

    class LayerWeights(dict):
        def __init__(self, l):
            super().__init__()
            self.l = l

        def __missing__(self, n):
            ex, i = pending[(self.l, n)]
            self[n] = _kernel_layout(n, ex.result[i])
            return self[n]

    def layer_weights(l):
        p = LayerWeights(l)
        full_up = lambda n: jnp.concatenate([gate_up[n][j, l] for j in range(N_CHIPS)], axis=1)
        p["up_f"] = _pad_gate_up(full_up("gla_gate_up_fwd"), 0)
        p["up_b"] = _pad_gate_up(full_up("gla_gate_up_bwd"), GLA_GATE_RANK)
        p["bias_f"], p["bias_b"] = row(gla_gate_bias_fwd[l]), row(gla_gate_bias_bwd[l])
        for n in ("norm_mix", "gla_norm", "dil_norm", "mem_norm", "mem_out_norm", "norm_mlp"):
            p[n] = row(W[n][l])
        return p

    sink = GradScatter(exchange=True)
    loss, grad_x, grads, g_final = local_step(x[0], mem[0], loss_target[0], depth, layer_weights, row(norm_final),
                                              rel_bias, fwd_carry, sink)
    sink.flush("scatter_last_grads")
    recv = sink.received()

    flat2 = lambda a: a.reshape(-1, a.shape[-1])
    part = [jnp.concatenate([sum_chips(r, name=f"sum_chips_{n}_l{l}_p{i}") for l in range(depth)
                             for i, r in enumerate(recv[(l, n)])], axis=0) for n in BIG]
    other = sibling_exchange(part, name="swap_partial_sums")
    out = {}
    for n, pa, ob in zip(BIG, part, other):
        res = adamw([pa, ob], flat2(W[n]), flat2(M1[n]), flat2(V2[n]), name=f"adamw_{n}")
        out[n] = [r.reshape(W[n].shape) for r in res]

    small_g = {}
    for n, key in (("norm_mix", "norm_mix"), ("gla_gate_bias_fwd", "bias_f"), ("gla_gate_bias_bwd", "bias_b"),
                   ("gla_norm", "gla_norm"), ("dil_norm", "dil_norm"), ("mem_norm", "mem_norm"),
                   ("mem_out_norm", "mem_out_norm"), ("norm_mlp", "norm_mlp"),
                   ("gla_gate_up_fwd", "up_f"), ("gla_gate_up_bwd", "up_b")):
        small_g[n] = jnp.stack([grads[l][key].reshape(W[n].shape[1:] if n not in GATE_UPS else (GLA_GATE_RANK, QK_W))
                                for l in range(depth)], axis=0)
    small_g["rel_bias"] = sum(grads[l]["rel_bias"] for l in range(depth))
    small_g["norm_final"] = g_final.reshape(-1)
    names = SMALL + GATE_UPS
    slab = all_reduce_small(_pack([small_g[n] for n in names] + [loss[:, 0]]), name="all_reduce_small")
    *summed, loss_sum = _unpack(slab, [small_g[n].shape for n in names] + [(1,)])
    summed = dict(zip(names, summed))
    for n in GATE_UPS:
        summed[n] = lax.dynamic_slice_in_dim(summed[n], chip * LANES, LANES, axis=2)
    names_s = SMALL + GATE_UPS
    shapes = [W[n].shape for n in names_s]
    res = adamw([_pack([summed[n] for n in names_s])], _pack([W[n] for n in names_s]), _pack([M1[n] for n in names_s]),
                _pack([V2[n] for n in names_s]), name="adamw_small")
    for n, *r in zip(names_s, *[_unpack(t, shapes) for t in res]):
        out[n] = r

    return (loss_sum.reshape(()), grad_x[None], *[out[n][0] for n in WEIGHTS], *[out[n][1] for n in WEIGHTS],
            *[out[n][2] for n in WEIGHTS], *[out[n][3] for n in WEIGHTS])
```

```python
import functools
import math

import numpy as np
import jax
import jax.numpy as jnp
from jax import lax
from jax.experimental import pallas as pl
from jax.experimental.pallas import tpu as pltpu

F32, BF16 = jnp.float32, jnp.bfloat16
SDS = jax.ShapeDtypeStruct

DEPTH = 4
GLA_HEADS, GLA_DK, GLA_DV = 4, 128, 256
GLA_GATE_RANK = 16
GLA_GATE_NORMALIZER = 16.0
GLA_CHUNK = 64
DIL_HEADS, DIL_HEAD_DIM = 4, 128
DIL_DILATIONS = (1, 4, 16)
DIL_W = 64
MEM_HEADS, MEM_HEAD_DIM = 4, 128
REL_BUCKETS, REL_MAX_DISTANCE = 32, 1024
EPS = 1e-6
NEG_INF = -1e30
QK_W, V_W, DIL_WIDTH, MEM_WIDTH = 512, 1024, 512, 512
IN_WIDTH = 5152
LR_COL = 3072
IN_PAD = 5376
C_GQ, C_GK, C_GV, C_GR, C_DQ, C_MQ, C_LR = 0, 512, 1024, 2048, 3072, 4608, 5120
IN_TILE = 1792

ADAM_LR, ADAM_B1, ADAM_B2, ADAM_EPS, ADAM_WD, ADAM_STEP = 0.001, 0.9, 0.999, 1e-08, 0.01, 10

V7X_VMEM_BYTES = 64 * 1024 * 1024
VMEM_LIMIT = V7X_VMEM_BYTES * 7 // 8
LANES = 128

TM_FWD = 512
TM_NT = 1024
TN = 2048
TK = 2048
TM_NORM_BWD = 512
TK_NORM_BWD = 2048
NORM_BWD_ROWS = 64
TM_W, TN_W, TK_W = 1024, 1024, 2048
TM_LOSS = 256
ELEMWISE_BLOCK_BYTES = 24 * 1024 * 1024

NT_DIMS = (((1,), (1,)), ((), ()))
TN_DIMS = (((0,), (0,)), ((), ()))


def _nt(a, b):
    return lax.dot_general(a, b, NT_DIMS, preferred_element_type=F32)


def _tn(a, b):
    return lax.dot_general(a, b, TN_DIMS, preferred_element_type=F32)


def _nn(a, b):
    return jnp.dot(a, b, preferred_element_type=F32)


def _cp(*sem):
    return pltpu.CompilerParams(dimension_semantics=sem, vmem_limit_bytes=VMEM_LIMIT)


def _tile(n, t):
    t = min(n, t)
    assert n % t == 0, (n, t)
    return t


MESH = pl.DeviceIdType.MESH
HBM_SPEC = pl.BlockSpec(memory_space=pltpu.HBM)
N_CHIPS = 4
N_DEV = 8


def _mesh_pos():
    return lax.axis_index("x"), lax.axis_index("y"), lax.axis_index("c")


class ChipExchange:
    def __init__(self, bufs, *, scatter, layers=None, rows=None):
        self.bufs = list(bufs)
        self.n = n = len(self.bufs)
        self.scatter = scatter
        self.layers = list(layers) if layers is not None else [None] * n
        self.rows = list(rows) if rows is not None else [None] * n
        dma = pltpu.SemaphoreType.DMA
        if scatter:
            shp = [b.shape if r is None else (b.shape[0], r[1]) + b.shape[2:] for b, r in zip(self.bufs, self.rows)]
            self.out_shape = [SDS(s, b.dtype) for s, b in zip(shp, self.bufs)]
            self.scratch = [dma((3 * n,)), dma((3 * n,)), dma((n,))]
        else:
            shp = [b.shape if l is None else b.shape[1:] for b, l in zip(self.bufs, self.layers)]
            assert all(s[0] % 2 == 0 for s in shp), shp
            self.out_shape = [SDS((N_CHIPS,) + s, b.dtype) for s, b in zip(shp, self.bufs)]
            self.scratch = [dma((3 * n,)), dma((3 * n,)), dma((n,)), dma((3 * n,)), dma((3 * n,))]
        self.result = None

    def _plan(self, ins, outs, sems, finishing):
        x, y, c = _mesh_pos()
        me = 2 * x + y
        chips = [(1 - x, y), (x, 1 - y), (1 - x, 1 - y)]
        remote = lambda src, dst, ss, rs, k, dev: pltpu.make_async_remote_copy(
            src_ref=src, dst_ref=dst, send_sem=ss.at[k], recv_sem=rs.at[k], device_id=dev, device_id_type=MESH)
        kept, sent, landing, passed, handed = [], [], [], [], []
        for a in range(self.n):
            src = ins[a] if self.layers[a] is None else ins[a].at[self.layers[a]]
            if self.scatter:
                r = self.rows[a]
                pick = (lambda j: src.at[j]) if r is None else (lambda j: src.at[j, pl.ds(r[0], r[1])])
                kept.append(pltpu.make_async_copy(pick(me), outs[a].at[me], sems[2].at[a]))
                for j, (px, py) in enumerate(chips):
                    k, peer = 3 * a + j, 2 * px + py
                    sent.append(remote(pick(peer), outs[a].at[me], sems[0], sems[1], k, (px, py, c)))
                    if finishing:
                        landing.append(remote(pick(me), outs[a].at[peer], sems[0], sems[1], k, (px, py, c)))
                continue
            half = self.out_shape[a].shape[1] // 2
            mine, other = pl.ds(c * half, half), pl.ds((1 - c) * half, half)
            kept.append(pltpu.make_async_copy(src, outs[a].at[me], sems[2].at[a]))
            for j, (px, py) in enumerate(chips):
                k, peer = 3 * a + j, 2 * px + py
                sent.append(remote(src.at[mine], outs[a].at[me, mine], sems[0], sems[1], k, (px, py, c)))
                if finishing:
                    landing.append(remote(src.at[mine], outs[a].at[peer, mine], sems[0], sems[1], k, (px, py, c)))
                    passed.append(remote(outs[a].at[peer, mine], outs[a].at[peer, mine], sems[3], sems[4], k, (x, y, 1 - c)))
                    handed.append(remote(outs[a].at[peer, other], outs[a].at[peer, other], sems[3], sems[4], k,
                                         (x, y, 1 - c)))
        return kept, sent, landing, passed, handed

    def start(self, ins, outs, sems):
        kept, sent, _, _, _ = self._plan(ins, outs, sems, False)
        for cp in kept + sent:
            cp.start()

    def finish(self, ins, outs, sems):
        kept, sent, landing, passed, handed = self._plan(ins, outs, sems, True)
        for k, cp in enumerate(landing):
            cp.wait_recv()
            if passed:
                passed[k].start()
        for cp in handed:
            cp.wait_recv()
        for cp in sent + passed:
            cp.wait_send()
        for cp in kept:
            cp.wait()

    def run(self, name):
        n = self.n

        def body(*refs):
            self.start(refs[:n], refs[n:2 * n], refs[2 * n:])
            self.finish(refs[:n], refs[n:2 * n], refs[2 * n:])

        self.result = pl.pallas_call(body, name=name, in_specs=[HBM_SPEC] * n, out_specs=[HBM_SPEC] * n,
                                     out_shape=self.out_shape, scratch_shapes=self.scratch)(*self.bufs)
        return self.result


def _call(body, *, name, grid, in_specs, out_specs, out_shape, sem, args, scratch_shapes=(), comm=None):
    if comm is None:
        return pl.pallas_call(body, name=name, grid=grid, in_specs=list(in_specs), out_specs=list(out_specs),
                              out_shape=list(out_shape), scratch_shapes=list(scratch_shapes),
                              compiler_params=_cp(*sem))(*args)
    n_in, n_out, n_scr, nc = len(in_specs), len(out_shape), len(scratch_shapes), comm.n

    def wrapped(*refs):
        ins, cin = refs[:n_in], refs[n_in:n_in + nc]
        outs, cout = refs[n_in + nc:n_in + nc + n_out], refs[n_in + nc + n_out:n_in + 2 * nc + n_out]
        scr, csem = refs[n_in + 2 * nc + n_out:n_in + 2 * nc + n_out + n_scr], refs[n_in + 2 * nc + n_out + n_scr:]
        ids = [pl.program_id(d) for d in range(len(grid))]
        first = functools.reduce(jnp.logical_and, [i == 0 for i in ids])
        last = functools.reduce(jnp.logical_and, [i == g - 1 for i, g in zip(ids, grid)])

        @pl.when(first)
        def _():
            comm.start(cin, cout, csem)
        body(*ins, *outs, *scr)

        @pl.when(last)
        def _():
            comm.finish(cin, cout, csem)

    res = pl.pallas_call(
        wrapped, name=name, grid=grid, in_specs=list(in_specs) + [HBM_SPEC] * nc,
        out_specs=list(out_specs) + [HBM_SPEC] * nc, out_shape=list(out_shape) + comm.out_shape,
        scratch_shapes=list(scratch_shapes) + comm.scratch,
        compiler_params=_cp(*("arbitrary",) * len(grid)))(*args, *comm.bufs)
    comm.result = res[n_out:]
    return res[:n_out]


def _w_spec(w, rows, tn, col_of):
    if w.ndim == 2:
        tn = _tile(w.shape[1], tn)
        return w.shape[1], tn, pl.BlockSpec((rows, tn), lambda *g: (0, col_of(*g)))
    tn = _tile(w.shape[2], tn)
    per = w.shape[2] // tn
    return N_CHIPS * w.shape[2], tn, pl.BlockSpec((None, rows, tn), lambda *g: (col_of(*g) // per, 0, col_of(*g) % per))


def norm_mm(x, gain, w, *, tm, tn, name, comm=None):
    M, D = x.shape
    tm = _tile(M, tm)
    N, tn, w_spec = _w_spec(w, D, tn, lambda i, j: j)

    def body(x_ref, g_ref, w_ref, o_ref, hn_ref):
        @pl.when(pl.program_id(1) == 0)
        def _():
            xf = x_ref[...]
            r = lax.rsqrt(jnp.mean(xf * xf, axis=-1, keepdims=True) + EPS)
            hn_ref[...] = (xf * r * g_ref[...]).astype(BF16)
        o_ref[...] = _nn(hn_ref[...], w_ref[...]).astype(BF16)

    return _call(
        body, name=name, grid=(M // tm, N // tn),
        in_specs=[pl.BlockSpec((tm, D), lambda i, j: (i, 0)), pl.BlockSpec((1, D), lambda i, j: (0, 0)),
                  w_spec],
        out_specs=[pl.BlockSpec((tm, tn), lambda i, j: (i, j)), pl.BlockSpec((tm, D), lambda i, j: (i, 0))],
        out_shape=[SDS((M, N), BF16), SDS((M, D), BF16)], sem=("parallel", "arbitrary"), args=(x, gain, w), comm=comm)


def mm_res(a, w, res, *, relu2, tm, tn, tk, name, comm=None):
    M, K = a.shape
    N = w.shape[1]
    tm, tn, tk = _tile(M, tm), _tile(N, tn), _tile(K, tk)

    def body(a_ref, w_ref, r_ref, o_ref):
        @pl.when(pl.program_id(2) == 0)
        def _():
            o_ref[...] = r_ref[...]
        av = a_ref[...]
        if relu2:
            af = jnp.maximum(av.astype(F32), 0.0)
            av = (af * af).astype(BF16)
        o_ref[...] += _nn(av, w_ref[...])

    return _call(
        body, name=name, grid=(M // tm, N // tn, K // tk),
        in_specs=[pl.BlockSpec((tm, tk), lambda i, j, k: (i, k)), pl.BlockSpec((tk, tn), lambda i, j, k: (k, j)),
                  pl.BlockSpec((tm, tn), lambda i, j, k: (i, j))],
        out_specs=[pl.BlockSpec((tm, tn), lambda i, j, k: (i, j))], out_shape=[SDS((M, N), F32)],
        sem=("parallel", "parallel", "arbitrary"), args=(a, w, res), comm=comm)[0]


def mm_nt(dy, w, a, *, tm, tn, name, comm=None):
    M, K = dy.shape
    N = w.shape[0]
    tm, tn = _tile(M, tm), _tile(N, tn)
    with_a = a is not None

    def body(*refs):
        if with_a:
            dy_ref, w_ref, a_ref, o_ref = refs
        else:
            dy_ref, w_ref, o_ref = refs
        r = _nt(dy_ref[...], w_ref[...])
        if with_a:
            r = r * (2.0 * jnp.maximum(a_ref[...].astype(F32), 0.0))
        o_ref[...] = r.astype(BF16)

    in_specs = [pl.BlockSpec((tm, K), lambda i, j: (i, 0)), pl.BlockSpec((tn, K), lambda i, j: (j, 0))]
    args = [dy, w]
    if with_a:
        in_specs.append(pl.BlockSpec((tm, tn), lambda i, j: (i, j)))
        args.append(a)
    return _call(
        body, name=name, grid=(M // tm, N // tn), in_specs=in_specs,
        out_specs=[pl.BlockSpec((tm, tn), lambda i, j: (i, j))], out_shape=[SDS((M, N), BF16)],
        sem=("parallel", "arbitrary"), args=args, comm=comm)[0]


def mm_nt_normbwd(dy, w, x, gain, dx_in, *, tm, tk, name, comm=None):
    M, K = dy.shape
    D = x.shape[1]
    tm = _tile(M, tm)
    _, tk, w_spec = _w_spec(w, D, tk, lambda i, k: k)
    nk = K // tk
    rc = _tile(tm, NORM_BWD_ROWS)

    def body(dy_ref, w_ref, x_ref, g_ref, dxin_ref, dx_ref, dxb_ref, gg_ref, acc):
        i, k = pl.program_id(0), pl.program_id(1)

        @pl.when(k == 0)
        def _():
            acc[...] = jnp.zeros_like(acc)
        acc[...] += _nt(dy_ref[...], w_ref[...])

        @pl.when(k == nk - 1)
        def _():
            def rows(c, part):
                rs = pl.ds(pl.multiple_of(c * rc, rc), rc)
                dh, xf = acc[rs, :], x_ref[rs, :]
                r = lax.rsqrt(jnp.mean(xf * xf, axis=-1, keepdims=True) + EPS)
                xhat = xf * r
                u = dh * g_ref[...]
                dx = dxin_ref[rs, :] + r * (u - xhat * jnp.mean(u * xhat, axis=-1, keepdims=True))
                dx_ref[rs, :] = dx
                dxb_ref[rs, :] = dx.astype(BF16)
                return part + jnp.sum(dh * xhat, axis=0, keepdims=True)

            part = lax.fori_loop(0, tm // rc, rows, jnp.zeros((1, D), F32))

            @pl.when(i == 0)
            def _():
                gg_ref[...] = part

            @pl.when(i > 0)
            def _():
                gg_ref[...] += part

    return _call(
        body, name=name, grid=(M // tm, nk),
        in_specs=[pl.BlockSpec((tm, tk), lambda i, k: (i, k)), w_spec,
                  pl.BlockSpec((tm, D), lambda i, k: (i, 0)), pl.BlockSpec((1, D), lambda i, k: (0, 0)),
                  pl.BlockSpec((tm, D), lambda i, k: (i, 0))],
        out_specs=[pl.BlockSpec((tm, D), lambda i, k: (i, 0)),
                   pl.BlockSpec((tm, D), lambda i, k: (i, 0)),
                   pl.BlockSpec((1, D), lambda i, k: (0, 0))],
        out_shape=[SDS((M, D), F32), SDS((M, D), BF16), SDS((1, D), F32)], scratch_shapes=[pltpu.VMEM((tm, D), F32)],
        sem=("arbitrary", "arbitrary"), args=(dy, w, x, gain, dx_in), comm=comm)


def mm_tn(a, dy, *, relu2, tm, tn, tk, name, out="f32", comm=None):
    S_, Ka = a.shape
    N = dy.shape[1]
    tm, tk = _tile(Ka, tm), _tile(S_, tk)
    tn = _tile(N // N_CHIPS if out == "cols" else N, tn)
    nk = S_ // tk

    def body(a_ref, dy_ref, o_ref, acc):
        k = pl.program_id(2)

        @pl.when(k == 0)
        def _():
            acc[...] = jnp.zeros_like(acc)
        av = a_ref[...]
        if relu2:
            af = jnp.maximum(av.astype(F32), 0.0)
            av = (af * af).astype(BF16)
        acc[...] += _tn(av, dy_ref[...].astype(BF16))

        @pl.when(k == nk - 1)
        def _():
            o_ref[...] = acc[...].astype(o_ref.dtype)

    if out == "cols":
        per = N // N_CHIPS // tn
        o_spec = pl.BlockSpec((None, tm, tn), lambda i, j, k: (j // per, i, j % per))
        o_shape = SDS((N_CHIPS, Ka, N // N_CHIPS), BF16)
    else:
        o_spec = pl.BlockSpec((tm, tn), lambda i, j, k: (i, j))
        o_shape = SDS((Ka, N), F32 if out == "f32" else BF16)
    return _call(
        body, name=name, grid=(Ka // tm, N // tn, nk),
        in_specs=[pl.BlockSpec((tk, tm), lambda i, j, k: (k, i)), pl.BlockSpec((tk, tn), lambda i, j, k: (k, j))],
        out_specs=[o_spec], out_shape=[o_shape], scratch_shapes=[pltpu.VMEM((tm, tn), F32)],
        sem=("parallel", "parallel", "arbitrary"), args=(a, dy), comm=comm)[0]


def loss_head(x, gain, target, *, tm, name):
    M, D = x.shape
    tm = _tile(M, tm)

    def body(x_ref, g_ref, t_ref, dx_ref, dxb_ref, gg_ref, l_ref):
        i = pl.program_id(0)
        xf = x_ref[...]
        r = lax.rsqrt(jnp.mean(xf * xf, axis=-1, keepdims=True) + EPS)
        xhat = xf * r
        e = xhat * g_ref[...] - t_ref[...]
        lpart = 0.5 * jnp.sum(jnp.mean(e * e, axis=-1, keepdims=True), axis=0, keepdims=True)
        dy = e * (1.0 / D)
        u = dy * g_ref[...]
        dx = r * (u - xhat * jnp.mean(u * xhat, axis=-1, keepdims=True))
        dx_ref[...] = dx
        dxb_ref[...] = dx.astype(BF16)
        gpart = jnp.sum(dy * xhat, axis=0, keepdims=True)
        lrow = jnp.broadcast_to(lpart, (1, LANES))

        @pl.when(i == 0)
        def _():
            gg_ref[...] = gpart
            l_ref[...] = lrow

        @pl.when(i > 0)
        def _():
            gg_ref[...] += gpart
            l_ref[...] += lrow

    return pl.pallas_call(
        body, name=name, grid=(M // tm,),
        in_specs=[pl.BlockSpec((tm, D), lambda i: (i, 0)), pl.BlockSpec((1, D), lambda i: (0, 0)),
                  pl.BlockSpec((tm, D), lambda i: (i, 0))],
        out_specs=[pl.BlockSpec((tm, D), lambda i: (i, 0)), pl.BlockSpec((tm, D), lambda i: (i, 0)),
                   pl.BlockSpec((1, D), lambda i: (0, 0)), pl.BlockSpec((1, LANES), lambda i: (0, 0))],
        out_shape=[SDS((M, D), F32), SDS((M, D), BF16), SDS((1, D), F32), SDS((1, LANES), F32)],
        compiler_params=_cp("arbitrary"))(x, gain, target)


def _log_sigmoid(z):
    return jnp.minimum(z, 0.0) - jnp.log(1.0 + jnp.exp(-jnp.abs(z)))


def _chunk_scan(x, row, reverse):
    n = x.shape[0]
    s = 1
    while s < GLA_CHUNK:
        if reverse:
            x = x + jnp.where(row < GLA_CHUNK - s, pltpu.roll(x, n - s, 0), 0.0)
        else:
            x = x + jnp.where(row >= s, pltpu.roll(x, s, 0), 0.0)
        s *= 2
    return x


def _chunk_edge(b, reverse):
    t = b.shape[0]
    nc = t // GLA_CHUNK
    b3 = b.reshape(nc, GLA_CHUNK, b.shape[1])
    e = b3[:, 0:1, :] if reverse else b3[:, GLA_CHUNK - 1:GLA_CHUNK, :]
    return jnp.broadcast_to(e, b3.shape).reshape(b.shape), e.reshape(nc, b.shape[1])


def _gla_gates(lr, up_ref, bias_ref, cs, row, reverse):
    z = _nn(lr, up_ref[:, cs]) + bias_ref[:, cs]
    g = _log_sigmoid(z) * (1.0 / GLA_GATE_NORMALIZER)
    b = _chunk_scan(g, row, reverse)
    bl, blc = _chunk_edge(b, reverse)
    return z, b, bl, blc


def _gla_mask(reverse):
    r = lax.broadcasted_iota(jnp.int32, (GLA_CHUNK, GLA_CHUNK), 0)
    c = lax.broadcasted_iota(jnp.int32, (GLA_CHUNK, GLA_CHUNK), 1)
    return (r <= c) if reverse else (r >= c)


def gla_fwd(proj, upad, bias, *, reverse, name, comm=None):
    S_ = proj.shape[0]
    T = _tile(S_, 512)
    nt, nc = S_ // T, T // GLA_CHUNK
    H, DK, DV, C = GLA_HEADS, GLA_DK, GLA_DV, GLA_CHUNK
    tix = (lambda i: nt - 1 - i) if reverse else (lambda i: i)
    qscale = DK ** -0.5

    def body(q_ref, k_ref, v_ref, lr_ref, up_ref, b_ref, o_ref, st_ref, st_s, qd_s, ki_s, ke_s, dec_s):
        @pl.when(pl.program_id(0) == 0)
        def _():
            st_s[...] = jnp.zeros_like(st_s)
        row = lax.broadcasted_iota(jnp.int32, (T, DK), 0) % C
        mask = _gla_mask(reverse)
        lr = lr_ref[...]
        for h in range(H):
            cs = slice(h * DK, (h + 1) * DK)
            _, b, bl, _ = _gla_gates(lr, up_ref, b_ref, cs, row, reverse)
            q = q_ref[:, cs].astype(F32) * qscale
            k = k_ref[:, cs].astype(F32)
            qd_s[:, cs] = (q * jnp.exp(b)).astype(BF16)
            ki_s[:, cs] = (k * jnp.exp(-b)).astype(BF16)
            ke_s[:, cs] = (k * jnp.exp(bl - b)).astype(BF16)
            dec_s[:, cs] = jnp.exp(bl)

        def chunk(ci, carry):
            c = (nc - 1 - ci) if reverse else ci
            r0 = pl.multiple_of(c * C, C)
            s0 = pl.multiple_of(c * DV, DV)
            for h in range(H):
                cs = slice(h * DK, (h + 1) * DK)
                vs = slice(h * DV, (h + 1) * DV)
                qd, ki, ke = qd_s[pl.ds(r0, C), cs], ki_s[pl.ds(r0, C), cs], ke_s[pl.ds(r0, C), cs]
                v = v_ref[pl.ds(r0, C), vs]
                st = st_s[h]
                stb = st.astype(BF16)
                st_ref[h, pl.ds(s0, DV), :] = stb
                a = jnp.where(mask, _nt(qd, ki), 0.0).astype(BF16)
                o_ref[pl.ds(r0, C), vs] = _nn(a, v) + _nt(qd, stb)
                st_s[h] = dec_s[pl.ds(r0, 1), cs] * st + _tn(v, ke)
            return carry

        lax.fori_loop(0, nc, chunk, 0)

    return _call(
        body, name=name, grid=(nt,),
        in_specs=[pl.BlockSpec((T, QK_W), lambda i: (tix(i), C_GQ // QK_W)),
                  pl.BlockSpec((T, QK_W), lambda i: (tix(i), C_GK // QK_W)),
                  pl.BlockSpec((T, V_W), lambda i: (tix(i), C_GV // V_W)),
                  pl.BlockSpec((T, LANES), lambda i: (tix(i), C_LR // LANES)),
                  pl.BlockSpec((LANES, QK_W), lambda i: (0, 0)), pl.BlockSpec((1, QK_W), lambda i: (0, 0))],
        out_specs=[pl.BlockSpec((T, V_W), lambda i: (tix(i), 0)),
                   pl.BlockSpec((H, nc * DV, DK), lambda i: (0, tix(i), 0))],
        out_shape=[SDS((S_, V_W), F32), SDS((H, S_ // C * DV, DK), BF16)],
        scratch_shapes=[pltpu.VMEM((H, DV, DK), F32), pltpu.VMEM((T, QK_W), BF16), pltpu.VMEM((T, QK_W), BF16),
                        pltpu.VMEM((T, QK_W), BF16), pltpu.VMEM((T, QK_W), F32)],
        sem=("arbitrary",), args=(proj, proj, proj, proj, upad, bias), comm=comm)


def gla_bwd(proj, upad, bias, states, do, prev, *, reverse, name):
    S_ = proj.shape[0]
    T = _tile(S_, 512)
    nt, nc = S_ // T, T // GLA_CHUNK
    H, DK, DV, C = GLA_HEADS, GLA_DK, GLA_DV, GLA_CHUNK
    tix = (lambda i: i) if reverse else (lambda i: nt - 1 - i)
    qscale = DK ** -0.5
    with_prev = prev is not None
    NCP = max(8, nc)

    def body(*refs):
        (q_ref, k_ref, v_ref, lr_ref, up_ref, b_ref, st_ref, do_ref) = refs[:8]
        n_in = 12 if with_prev else 8
        pq_ref, pk_ref, pv_ref, plr_ref = refs[8:12] if with_prev else (None,) * 4
        dq_ref, dk_ref, dv_ref, dlr_ref, dup_ref, dbias_ref = refs[n_in:n_in + 6]
        dst_s, qd_s, ki_s, ke_s, dec_s, dqd_s, dki_s, dke_s, ddec_s, z_s, eb_s, einv_s, eend_s = refs[n_in + 6:]
        first = pl.program_id(0) == 0

        @pl.when(first)
        def _():
            dst_s[...] = jnp.zeros_like(dst_s)
            dup_ref[...] = jnp.zeros_like(dup_ref)
            dbias_ref[...] = jnp.zeros_like(dbias_ref)

        row = lax.broadcasted_iota(jnp.int32, (T, DK), 0) % C
        edge_row = 0 if reverse else C - 1
        mask = _gla_mask(reverse)
        lr = lr_ref[...]
        dlr = plr_ref[...].astype(F32) if with_prev else jnp.zeros((T, LANES), F32)

        for h in range(H):
            cs = slice(h * DK, (h + 1) * DK)
            z, b, bl, _ = _gla_gates(lr, up_ref, b_ref, cs, row, reverse)
            q = q_ref[:, cs].astype(F32) * qscale
            k = k_ref[:, cs].astype(F32)
            eb, einv, eend = jnp.exp(b), jnp.exp(-b), jnp.exp(bl - b)
            z_s[:, cs], eb_s[:, cs], einv_s[:, cs], eend_s[:, cs] = z, eb, einv, eend
            qd_s[:, cs] = (q * eb).astype(BF16)
            ki_s[:, cs] = (k * einv).astype(BF16)
            ke_s[:, cs] = (k * eend).astype(BF16)
            dec_s[:, cs] = jnp.exp(bl)

        def chunk(ci, carry):
            c = ci if reverse else (nc - 1 - ci)
            r0 = pl.multiple_of(c * C, C)
            s0 = pl.multiple_of(c * DV, DV)
            for h in range(H):
                cs = slice(h * DK, (h + 1) * DK)
                vs = slice(h * DV, (h + 1) * DV)
                qdc, kic, kec = qd_s[pl.ds(r0, C), cs], ki_s[pl.ds(r0, C), cs], ke_s[pl.ds(r0, C), cs]
                v = v_ref[pl.ds(r0, C), vs]
                doc = do_ref[pl.ds(r0, C), vs]
                st = st_ref[h, pl.ds(s0, DV), :]
                dst = dst_s[h]
                dstb = dst.astype(BF16)
                a = jnp.where(mask, _nt(qdc, kic), 0.0).astype(BF16)
                da = jnp.where(mask, _nt(doc, v), 0.0).astype(BF16)
                dvc = _tn(a, doc) + _nt(kec, dstb)
                if with_prev:
                    dvc = dvc + pv_ref[pl.ds(r0, C), vs].astype(F32)
                dv_ref[pl.ds(r0, C), vs] = dvc.astype(BF16)
                dqd_s[pl.ds(r0, C), cs] = _nn(da, kic) + _nn(doc, st)
                dki_s[pl.ds(r0, C), cs] = _tn(da, qdc)
                dke_s[pl.ds(r0, C), cs] = _nn(v, dstb)
                ddec_s[h, pl.ds(c, 1), :] = jnp.sum(dst * st.astype(F32), axis=0, keepdims=True)
                dst_s[h] = dec_s[pl.ds(r0, 1), cs] * dst + _tn(doc, qdc)
            return carry

        lax.fori_loop(0, nc, chunk, 0)

        for h in range(H):
            cs = slice(h * DK, (h + 1) * DK)
            z, eb, einv, eend = z_s[:, cs], eb_s[:, cs], einv_s[:, cs], eend_s[:, cs]
            q = q_ref[:, cs].astype(F32) * qscale
            k = k_ref[:, cs].astype(F32)
            qd, ki, ke = q * eb, k * einv, k * eend
            dec = dec_s[:, cs].reshape(nc, C, DK)[:, 0, :]
            dqd, dki, dke = dqd_s[:, cs], dki_s[:, cs], dke_s[:, cs]
            dq = dqd * eb * qscale
            dk = dki * einv + dke * eend
            if with_prev:
                dq = dq + pq_ref[:, cs].astype(F32)
                dk = dk + pk_ref[:, cs].astype(F32)
            dq_ref[:, cs] = dq.astype(BF16)
            dk_ref[:, cs] = dk.astype(BF16)
            wke = dke * ke
            db = dqd * qd - dki * ki - wke
            dbl = jnp.sum(wke.reshape(nc, C, DK), axis=1) + ddec_s[h, 0:nc, :] * dec
            dbl_b = jnp.broadcast_to(dbl.reshape(nc, 1, DK), (nc, C, DK)).reshape(T, DK)
            db = db + jnp.where(row == edge_row, dbl_b, 0.0)
            dg = _chunk_scan(db, row, not reverse)
            dz = dg * (1.0 / GLA_GATE_NORMALIZER) * (1.0 / (1.0 + jnp.exp(z)))
            dzb = dz.astype(BF16)
            dbias_ref[:, cs] += jnp.sum(dz, axis=0, keepdims=True)
            dup_ref[:, cs] += _tn(lr, dzb)
            dlr = dlr + _nt(dzb, up_ref[:, cs])
        dlr_ref[...] = dlr.astype(BF16)

    tile = lambda w, cb: pl.BlockSpec((T, w), lambda i: (tix(i), cb))
    in_specs = [tile(QK_W, C_GQ // QK_W), tile(QK_W, C_GK // QK_W), tile(V_W, C_GV // V_W), tile(LANES, C_LR // LANES),
                pl.BlockSpec((LANES, QK_W), lambda i: (0, 0)), pl.BlockSpec((1, QK_W), lambda i: (0, 0)),
                pl.BlockSpec((H, nc * DV, DK), lambda i: (0, tix(i), 0)), tile(V_W, 0)]
    args = [proj, proj, proj, proj, upad, bias, states, do]
    if with_prev:
        in_specs += [tile(QK_W, 0), tile(QK_W, 0), tile(V_W, 0), tile(LANES, 0)]
        args += list(prev)
    return pl.pallas_call(
        body, name=name, grid=(nt,), in_specs=in_specs,
        out_specs=[tile(QK_W, 0), tile(QK_W, 0), tile(V_W, 0), tile(LANES, 0),
                   pl.BlockSpec((LANES, QK_W), lambda i: (0, 0)), pl.BlockSpec((1, QK_W), lambda i: (0, 0))],
        out_shape=[SDS((S_, QK_W), BF16), SDS((S_, QK_W), BF16), SDS((S_, V_W), BF16), SDS((S_, LANES), BF16),
                   SDS((LANES, QK_W), F32), SDS((1, QK_W), F32)],
        scratch_shapes=[pltpu.VMEM((H, DV, DK), F32), pltpu.VMEM((T, QK_W), BF16), pltpu.VMEM((T, QK_W), BF16),
                        pltpu.VMEM((T, QK_W), BF16), pltpu.VMEM((T, QK_W), F32), pltpu.VMEM((T, QK_W), F32),
                        pltpu.VMEM((T, QK_W), F32), pltpu.VMEM((T, QK_W), F32), pltpu.VMEM((H, NCP, DK), F32)]
        + [pltpu.VMEM((T, QK_W), F32)] * 4,
        compiler_params=_cp("arbitrary"))(*args)


def _head_norm_fwd(o, width):
    out = []
    for h in range(o.shape[1] // width):
        oh = o[:, h * width:(h + 1) * width]
        rs = lax.rsqrt(jnp.mean(oh * oh, axis=-1, keepdims=True) + EPS)
        out.append((oh * rs, rs))
    return out


def gla_post(o_f, o_b, proj, gain, *, name):
    S_ = proj.shape[0]
    T = _tile(S_, 512)

    def body(of_ref, ob_ref, r_ref, g_ref, y_ref):
        o = of_ref[...] + ob_ref[...]
        for h, (ohat, _) in enumerate(_head_norm_fwd(o, GLA_DV)):
            vs = slice(h * GLA_DV, (h + 1) * GLA_DV)
            r = r_ref[:, vs].astype(F32)
            y_ref[:, vs] = (ohat * g_ref[:, vs] * (r / (1.0 + jnp.exp(-r)))).astype(BF16)

    blk = pl.BlockSpec((T, V_W), lambda i: (i, 0))
    return pl.pallas_call(
        body, name=name, grid=(S_ // T,),
        in_specs=[blk, blk, pl.BlockSpec((T, V_W), lambda i: (i, C_GR // V_W)), pl.BlockSpec((1, V_W), lambda i: (0, 0))],
        out_specs=blk, out_shape=SDS((S_, V_W), BF16), compiler_params=_cp("parallel"))(o_f, o_b, proj, gain)


def gla_post_bwd(dmix, o_f, o_b, proj, gain, *, name):
    S_ = proj.shape[0]
    T = _tile(S_, 512)

    def body(dy_ref, of_ref, ob_ref, r_ref, g_ref, do_ref, dr_ref, gg_ref):
        @pl.when(pl.program_id(0) == 0)
        def _():
            gg_ref[...] = jnp.zeros_like(gg_ref)
        o = of_ref[...] + ob_ref[...]
        for h, (ohat, rs) in enumerate(_head_norm_fwd(o, GLA_DV)):
            vs = slice(h * GLA_DV, (h + 1) * GLA_DV)
            r = r_ref[:, vs].astype(F32)
            dy = dy_ref[:, vs].astype(F32)
            sg = 1.0 / (1.0 + jnp.exp(-r))
            gate = r * sg
            n = ohat * g_ref[:, vs]
            dr_ref[:, vs] = (dy * n * (sg * (1.0 + r * (1.0 - sg)))).astype(BF16)
            dn = dy * gate
            u = dn * g_ref[:, vs]
            do_ref[:, vs] = (rs * (u - ohat * jnp.mean(u * ohat, axis=-1, keepdims=True))).astype(BF16)
            gg_ref[:, vs] += jnp.sum(dn * ohat, axis=0, keepdims=True)

    blk = pl.BlockSpec((T, V_W), lambda i: (i, 0))
    vec = pl.BlockSpec((1, V_W), lambda i: (0, 0))
    return pl.pallas_call(
        body, name=name, grid=(S_ // T,),
        in_specs=[blk, blk, blk, pl.BlockSpec((T, V_W), lambda i: (i, C_GR // V_W)), vec],
        out_specs=[blk, blk, vec], out_shape=[SDS((S_, V_W), BF16), SDS((S_, V_W), BF16), SDS((1, V_W), F32)],
        compiler_params=_cp("arbitrary"))(dmix, o_f, o_b, proj, gain)


def _t5_bucket_np(rel):
    half = REL_BUCKETS // 2
    max_exact = half // 2
    ret = np.where(rel > 0, half, 0)
    n = np.abs(rel)
    nf = np.maximum(n, 1).astype(np.float32)
    large = max_exact + (np.log(nf / np.float32(max_exact)) / np.float32(math.log(REL_MAX_DISTANCE / max_exact))
                         * np.float32(half - max_exact)).astype(np.int32)
    large = np.minimum(large, half - 1)
    return ret + np.where(n < max_exact, n, large)


def _rel_onehot(dilation):
    rel = np.arange(-DIL_W, DIL_W + 1)
    return np.eye(REL_BUCKETS, dtype=np.float32)[_t5_bucket_np(rel * dilation)]


def _toeplitz(vec, tq, flip):
    H = vec.shape[0]
    wlen = tq + 2 * DIL_W
    lu = wlen + tq
    if flip:
        vec = vec[:, ::-1]
    u = jnp.full((H, lu), NEG_INF, F32)
    u = lax.dynamic_update_slice(u, vec, (0, 0))
    t = jnp.tile(u, (1, tq))[:, :tq * (lu - 1)].reshape(H, tq, lu - 1)
    return t[:, :, :wlen]


def _untoeplitz(ds, tq):
    H = ds.shape[0]
    wlen = tq + 2 * DIL_W
    lu = wlen + tq
    p = jnp.pad(ds, ((0, 0), (0, 0), (0, lu - 1 - wlen))).reshape(H, tq * (lu - 1))
    p = jnp.pad(p, ((0, 0), (0, tq))).reshape(H, tq, lu)
    return jnp.sum(p, axis=1)[:, :2 * DIL_W + 1]


DIL_TQ = 1024
DIL_SUB = 128


def _dil_band(tq):
    sb = min(tq, DIL_SUB)
    return sb, sb + 2 * DIL_W


def _dil_window(p_ref, o_ref, n_ref, tq):
    return jnp.concatenate([p_ref[tq - DIL_W:, :], o_ref[...], n_ref[:DIL_W, :]], axis=0)


def _dil_specs(tq, nt, col0, ncols_per_r):
    cb = lambda h, r: r * ncols_per_r + col0 + h
    own = pl.BlockSpec((tq, LANES), lambda h, r, i: (i, cb(h, r)))
    prev = pl.BlockSpec((tq, LANES), lambda h, r, i: (jnp.maximum(i - 1, 0), cb(h, r)))
    nxt = pl.BlockSpec((tq, LANES), lambda h, r, i: (jnp.minimum(i + 1, nt - 1), cb(h, r)))
    return prev, own, nxt


def dil_fwd(qkv, bias, *, d, name):
    l = qkv.shape[0]
    tq = _tile(l, DIL_TQ)
    nt = l // tq
    H = DIL_HEADS
    scale = DIL_HEAD_DIM ** -0.5

    sb, bw = _dil_band(tq)

    def body(q_ref, kp, ko, kn, vp, vo, vn, b_ref, o_ref, l_ref):
        i = pl.program_id(2)
        kw = _dil_window(kp, ko, kn, tq)
        vw = _dil_window(vp, vo, vn, tq)
        col = lax.broadcasted_iota(jnp.int32, (sb, bw), 1)
        for j in range(tq // sb):
            rows, win = slice(j * sb, (j + 1) * sb), slice(j * sb, j * sb + bw)
            s = _nt(q_ref[rows, :], kw[win]) * scale + b_ref[0]
            kpos = i * tq + j * sb - DIL_W + col
            s = jnp.where((kpos >= 0) & (kpos < l), s, NEG_INF)
            m = jnp.max(s, axis=-1, keepdims=True)
            p = jnp.exp(s - m)
            den = jnp.sum(p, axis=-1, keepdims=True)
            o_ref[rows, :] = (_nn(p.astype(BF16), vw[win]) / den).astype(BF16)
            l_ref[rows, :] = jnp.broadcast_to(m + jnp.log(den), (sb, LANES))

    _, q_spec, _ = _dil_specs(tq, nt, 0, 12)
    out_spec = pl.BlockSpec((tq, LANES), lambda h, r, i: (i, r * H + h))
    return pl.pallas_call(
        body, name=name, grid=(H, d, nt),
        in_specs=[q_spec, *_dil_specs(tq, nt, 4, 12), *_dil_specs(tq, nt, 8, 12),
                  pl.BlockSpec((1, sb, bw), lambda h, r, i: (h, 0, 0))],
        out_specs=[out_spec, out_spec], out_shape=[SDS((l, d * DIL_WIDTH), BF16), SDS((l, d * DIL_WIDTH), F32)],
        compiler_params=_cp("parallel", "parallel", "arbitrary"))(qkv, qkv, qkv, qkv, qkv, qkv, qkv, bias)


def dil_bwd_q(qkv, bias, do, lse, corr, *, d, name):
    l = qkv.shape[0]
    tq = _tile(l, DIL_TQ)
    nt = l // tq
    H = DIL_HEADS
    scale = DIL_HEAD_DIM ** -0.5

    sb, bw = _dil_band(tq)

    def body(q_ref, kp, ko, kn, vp, vo, vn, b_ref, do_ref, l_ref, c_ref, dq_ref, ds_ref):
        r, i = pl.program_id(1), pl.program_id(2)
        kw = _dil_window(kp, ko, kn, tq)
        vw = _dil_window(vp, vo, vn, tq)
        col = lax.broadcasted_iota(jnp.int32, (sb, bw), 1)
        ds_sum = jnp.zeros((sb, bw), F32)
        for j in range(tq // sb):
            rows, win = slice(j * sb, (j + 1) * sb), slice(j * sb, j * sb + bw)
            s = _nt(q_ref[rows, :], kw[win]) * scale + b_ref[0]
            kpos = i * tq + j * sb - DIL_W + col
            p = jnp.where((kpos >= 0) & (kpos < l), jnp.exp(s - l_ref[rows, 0:1]), 0.0)
            ds = p * (_nt(do_ref[rows, :], vw[win]) - c_ref[rows, 0:1])
            dq_ref[rows, :] = (_nn(ds.astype(BF16), kw[win]) * scale).astype(BF16)
            ds_sum = ds_sum + ds

        @pl.when((r == 0) & (i == 0))
        def _():
            ds_ref[0] = ds_sum

        @pl.when((r > 0) | (i > 0))
        def _():
            ds_ref[0] += ds_sum

    _, q_spec, _ = _dil_specs(tq, nt, 0, 12)
    t_spec = pl.BlockSpec((tq, LANES), lambda h, r, i: (i, r * H + h))
    b_spec = pl.BlockSpec((1, sb, bw), lambda h, r, i: (h, 0, 0))
    return pl.pallas_call(
        body, name=name, grid=(H, d, nt),
        in_specs=[q_spec, *_dil_specs(tq, nt, 4, 12), *_dil_specs(tq, nt, 8, 12), b_spec, t_spec, t_spec, t_spec],
        out_specs=[t_spec, b_spec], out_shape=[SDS((l, d * DIL_WIDTH), BF16), SDS((H, sb, bw), F32)],
        compiler_params=_cp("parallel", "arbitrary", "arbitrary"))(qkv, qkv, qkv, qkv, qkv, qkv, qkv, bias, do, lse, corr)


def dil_bwd_kv(qkv, bias_t, do, lse, corr, *, d, name):
    l = qkv.shape[0]
    tq = _tile(l, DIL_TQ)
    nt = l // tq
    H = DIL_HEADS
    scale = DIL_HEAD_DIM ** -0.5

    sb, bw = _dil_band(tq)

    def body(k_ref, v_ref, qp, qo, qn, dp, do_, dn, lp, lo, ln, cp, co, cn, b_ref, dk_ref, dv_ref):
        i = pl.program_id(2)
        qw = _dil_window(qp, qo, qn, tq)
        dow = _dil_window(dp, do_, dn, tq)
        lrow = _dil_window(lp, lo, ln, tq).T[0:1, :]
        crow = _dil_window(cp, co, cn, tq).T[0:1, :]
        col = lax.broadcasted_iota(jnp.int32, (sb, bw), 1)
        for j in range(tq // sb):
            rows, win = slice(j * sb, (j + 1) * sb), slice(j * sb, j * sb + bw)
            st = _nt(k_ref[rows, :], qw[win]) * scale + b_ref[0]
            qpos = i * tq + j * sb - DIL_W + col
            pt = jnp.where((qpos >= 0) & (qpos < l), jnp.exp(st - lrow[:, win]), 0.0)
            dv_ref[rows, :] = _nn(pt.astype(BF16), dow[win]).astype(BF16)
            dst = pt * (_nt(v_ref[rows, :], dow[win]) - crow[:, win])
            dk_ref[rows, :] = (_nn(dst.astype(BF16), qw[win]) * scale).astype(BF16)

    _, k_spec, _ = _dil_specs(tq, nt, 4, 12)
    _, v_spec, _ = _dil_specs(tq, nt, 8, 12)
    t_spec = pl.BlockSpec((tq, LANES), lambda h, r, i: (i, r * H + h))
    return pl.pallas_call(
        body, name=name, grid=(H, d, nt),
        in_specs=[k_spec, v_spec, *_dil_specs(tq, nt, 0, 12), *_dil_specs(tq, nt, 0, 4), *_dil_specs(tq, nt, 0, 4),
                  *_dil_specs(tq, nt, 0, 4), pl.BlockSpec((1, sb, bw), lambda h, r, i: (h, 0, 0))],
        out_specs=[t_spec, t_spec], out_shape=[SDS((l, d * DIL_WIDTH), BF16), SDS((l, d * DIL_WIDTH), BF16)],
        compiler_params=_cp("parallel", "parallel", "arbitrary"))(
            qkv, qkv, qkv, qkv, qkv, do, do, do, lse, lse, lse, corr, corr, corr, bias_t)


def _dil_weights(l1, l2, l3):
    m = jnp.maximum(jnp.maximum(l1, l2), l3)
    e = [jnp.exp(x - m) for x in (l1, l2, l3)]
    tot = e[0] + e[1] + e[2]
    return [x / tot for x in e]


def dil_combine(os_, ls_, gain, *, name):
    S_ = os_[0].shape[0]
    T = _tile(S_, 512)

    def body(o1, o2, o3, l1, l2, l3, g_ref, y_ref):
        w = _dil_weights(l1[...], l2[...], l3[...])
        o = w[0] * o1[...] + w[1] * o2[...] + w[2] * o3[...]
        for h, (ohat, _) in enumerate(_head_norm_fwd(o, DIL_HEAD_DIM)):
            cs = slice(h * DIL_HEAD_DIM, (h + 1) * DIL_HEAD_DIM)
            y_ref[:, cs] = (ohat * g_ref[:, cs]).astype(BF16)

    blk = pl.BlockSpec((T, DIL_WIDTH), lambda i: (i, 0))
    return pl.pallas_call(
        body, name=name, grid=(S_ // T,), in_specs=[blk] * 6 + [pl.BlockSpec((1, DIL_WIDTH), lambda i: (0, 0))],
        out_specs=blk, out_shape=SDS((S_, DIL_WIDTH), BF16), compiler_params=_cp("parallel"))(*os_, *ls_, gain)


def dil_combine_bwd(dmix, os_, ls_, gain, *, name):
    S_ = os_[0].shape[0]
    T = _tile(S_, 512)
    E = DIL_HEAD_DIM

    def body(dy_ref, o1, o2, o3, l1, l2, l3, g_ref, d1, d2, d3, c1, c2, c3, gg_ref):
        @pl.when(pl.program_id(0) == 0)
        def _():
            gg_ref[...] = jnp.zeros_like(gg_ref)
        w = _dil_weights(l1[...], l2[...], l3[...])
        o = w[0] * o1[...] + w[1] * o2[...] + w[2] * o3[...]
        for h, (ohat, rs) in enumerate(_head_norm_fwd(o, E)):
            cs = slice(h * E, (h + 1) * E)
            dy = dy_ref[:, cs].astype(F32)
            u = dy * g_ref[:, cs]
            do = rs * (u - ohat * jnp.mean(u * ohat, axis=-1, keepdims=True))
            gg_ref[:, cs] += jnp.sum(dy * ohat, axis=0, keepdims=True)
            tot = jnp.sum(do * o[:, cs], axis=-1, keepdims=True)
            for wr, d_ref, c_ref in zip(w, (d1, d2, d3), (c1, c2, c3)):
                d_ref[:, cs] = (wr[:, cs] * do).astype(BF16)
                c_ref[:, cs] = wr[:, cs] * tot

    blk = pl.BlockSpec((T, DIL_WIDTH), lambda i: (i, 0))
    vec = pl.BlockSpec((1, DIL_WIDTH), lambda i: (0, 0))
    return pl.pallas_call(
        body, name=name, grid=(S_ // T,),
        in_specs=[pl.BlockSpec((T, DIL_WIDTH), lambda i: (i, V_W // DIL_WIDTH))] + [blk] * 6 + [vec],
        out_specs=[blk] * 6 + [vec],
        out_shape=[SDS((S_, DIL_WIDTH), BF16)] * 3 + [SDS((S_, DIL_WIDTH), F32)] * 3 + [SDS((1, DIL_WIDTH), F32)],
        compiler_params=_cp("arbitrary"))(dmix, *os_, *ls_, gain)


def _mem_softmax(q, k, scale):
    s = _nt(q, k) * scale
    e = jnp.exp(s - jnp.max(s, axis=-1, keepdims=True))
    return e / jnp.sum(e, axis=-1, keepdims=True)


def mem_fwd(proj, kv, gain, *, name):
    S_ = proj.shape[0]
    M = kv.shape[0]
    T = _tile(S_, 1024)
    E, H = MEM_HEAD_DIM, MEM_HEADS
    scale = E ** -0.5

    def body(q_ref, k_ref, v_ref, g_ref, y_ref):
        p = _mem_softmax(q_ref[...], k_ref[...], scale)
        o = _nn(p.astype(BF16), v_ref[...])
        rs = lax.rsqrt(jnp.mean(o * o, axis=-1, keepdims=True) + EPS)
        y_ref[...] = (o * rs * g_ref[...]).astype(BF16)

    return pl.pallas_call(
        body, name=name, grid=(H, S_ // T),
        in_specs=[pl.BlockSpec((T, E), lambda h, i: (i, C_MQ // E + h)), pl.BlockSpec((M, E), lambda h, i: (0, h)),
                  pl.BlockSpec((M, E), lambda h, i: (0, H + h)), pl.BlockSpec((1, E), lambda h, i: (0, h))],
        out_specs=pl.BlockSpec((T, E), lambda h, i: (i, h)), out_shape=SDS((S_, MEM_WIDTH), BF16),
        compiler_params=_cp("parallel", "parallel"))(proj, kv, kv, gain)


def mem_bwd(dmix, proj, kv, gain, *, name):
    S_ = proj.shape[0]
    M = kv.shape[0]
    T = _tile(S_, 1024)
    E, H = MEM_HEAD_DIM, MEM_HEADS
    scale = E ** -0.5

    def body(dy_ref, q_ref, k_ref, v_ref, g_ref, dq_ref, dk_ref, dv_ref, gg_ref):
        @pl.when(pl.program_id(1) == 0)
        def _():
            dk_ref[...] = jnp.zeros_like(dk_ref)
            dv_ref[...] = jnp.zeros_like(dv_ref)
            gg_ref[...] = jnp.zeros_like(gg_ref)
        q, k, v = q_ref[...], k_ref[...], v_ref[...]
        p = _mem_softmax(q, k, scale)
        pb = p.astype(BF16)
        o = _nn(pb, v)
        rs = lax.rsqrt(jnp.mean(o * o, axis=-1, keepdims=True) + EPS)
        ohat = o * rs
        dy = dy_ref[...].astype(F32)
        u = dy * g_ref[...]
        do = (rs * (u - ohat * jnp.mean(u * ohat, axis=-1, keepdims=True))).astype(BF16)
        gg_ref[...] += jnp.sum(dy * ohat, axis=0, keepdims=True)
        dv_ref[...] += _tn(pb, do)
        dp = _nt(do, v)
        ds = (p * (dp - jnp.sum(p * dp, axis=-1, keepdims=True))).astype(BF16)
        dq_ref[...] = (_nn(ds, k) * scale).astype(BF16)
        dk_ref[...] += _tn(ds, q) * scale

    return pl.pallas_call(
        body, name=name, grid=(H, S_ // T),
        in_specs=[pl.BlockSpec((T, E), lambda h, i: (i, (V_W + DIL_WIDTH) // E + h)),
                  pl.BlockSpec((T, E), lambda h, i: (i, C_MQ // E + h)), pl.BlockSpec((M, E), lambda h, i: (0, h)),
                  pl.BlockSpec((M, E), lambda h, i: (0, H + h)), pl.BlockSpec((1, E), lambda h, i: (0, h))],
        out_specs=[pl.BlockSpec((T, E), lambda h, i: (i, h)), pl.BlockSpec((M, E), lambda h, i: (0, h)),
                   pl.BlockSpec((M, E), lambda h, i: (0, h)), pl.BlockSpec((1, E), lambda h, i: (0, h))],
        out_shape=[SDS((S_, MEM_WIDTH), BF16), SDS((M, MEM_WIDTH), F32), SDS((M, MEM_WIDTH), F32),
                   SDS((1, MEM_WIDTH), F32)],
        compiler_params=_cp("parallel", "arbitrary"))(dmix, proj, kv, kv, gain)


def _strided(a, d):
    return a if d == 1 else a.reshape(a.shape[0] // d, d * a.shape[1])


def _unstrided(a, d):
    return a if d == 1 else a.reshape(a.shape[0] * d, a.shape[1] // d)


def _dil_tables(rel_bias, seq):
    tabs = []
    for d in DIL_DILATIONS:
        sb, _ = _dil_band(min(seq // d, DIL_TQ))
        vec = jnp.sum(jnp.asarray(_rel_onehot(d))[:, :, None] * rel_bias[None], axis=1).T
        tabs.append((_toeplitz(vec, sb, False), _toeplitz(vec, sb, True)))
    return tabs


def _rel_bias_grad(ds_sums, seq):
    g = jnp.zeros((REL_BUCKETS, DIL_HEADS), F32)
    for d, ds in zip(DIL_DILATIONS, ds_sums):
        gvec = _untoeplitz(ds, _dil_band(min(seq // d, DIL_TQ))[0])
        g = g + jnp.sum(jnp.asarray(_rel_onehot(d))[:, :, None] * gvec.T[:, None, :], axis=0)
    return g


def _pad_gate_up(up, row0):
    return jnp.zeros((LANES, QK_W), BF16).at[row0:row0 + GLA_GATE_RANK].set(up.astype(BF16))


def _layer_fwd(x, mem, p, tabs, l, carry):
    tag = f"l{l}"
    proj, hn1 = norm_mm(x, p["norm_mix"], p["w_in"], tm=TM_FWD, tn=IN_TILE, name=f"in_proj_{tag}", comm=carry.get("in"))
    o_f, st_f = gla_fwd(proj, p["up_f"], p["bias_f"], reverse=False, name=f"gla_fwd_f_{tag}", comm=carry.get("gla_f"))
    o_b, st_b = gla_fwd(proj, p["up_b"], p["bias_b"], reverse=True, name=f"gla_fwd_b_{tag}", comm=carry.get("gla_b"))
    gla_out = gla_post(o_f, o_b, proj, p["gla_norm"], name=f"gla_post_{tag}")
    qkv = proj[:, C_DQ:C_MQ]
    os_, ls_, ls_strided = [], [], []
    for d, (bias, _) in zip(DIL_DILATIONS, tabs):
        o, lse = dil_fwd(_strided(qkv, d), bias, d=d, name=f"dil_fwd_d{d}_{tag}")
        os_.append(_unstrided(o, d))
        ls_.append(_unstrided(lse, d))
        ls_strided.append(lse)
    dil_out = dil_combine(os_, ls_, p["dil_norm"], name=f"dil_combine_{tag}")
    kv, hmem = norm_mm(mem, p["mem_norm"], p["w_mem_kv"], tm=256, tn=1024, name=f"mem_kv_{tag}")
    mem_out = mem_fwd(proj, kv, p["mem_out_norm"], name=f"mem_fwd_{tag}")
    mixed = jnp.concatenate([gla_out, dil_out, mem_out], axis=1)
    x1 = mm_res(mixed, p["w_out"], x, relu2=False, tm=TM_FWD, tn=TN, tk=TK, name=f"out_proj_{tag}",
                comm=carry.get("out"))
    a, hn2 = norm_mm(x1, p["norm_mlp"], p["w_up"], tm=TM_FWD, tn=TN, name=f"up_proj_{tag}", comm=carry.get("up"))
    x2 = mm_res(a, p["w_down"], x1, relu2=True, tm=TM_FWD, tn=TN, tk=TK, name=f"down_proj_{tag}",
                comm=carry.get("down"))
    saved = dict(x=x, proj=proj, hn1=hn1, o_f=o_f, o_b=o_b, st_f=st_f, st_b=st_b, qkv=qkv, os=os_, ls=ls_,
                 ls_strided=ls_strided, kv=kv, hmem=hmem, mixed=mixed, x1=x1, a=a, hn2=hn2)
    return x2, saved


class GradScatter:
    SLOTS = {"down_x": [(1, "w_in", 0, 1)], "down_w": [(1, "w_out", 0, 1), (1, "w_mem_kv", 0, 1)],
             "up_w": [(0, "w_down", 0, 2)], "up_x": [(0, "w_down", 1, 2)],
             "in_w": [(0, "w_up", 0, 2)], "in_x": [(0, "w_up", 1, 2)]}
    LAYER0 = {"in_w": [(0, "w_out", 0, 1), (0, "w_mem_kv", 0, 1)], "in_x": [(0, "w_in", 0, 1)]}

    def __init__(self, exchange):
        self.exchange = exchange
        self.ready = {}
        self.taken = set()
        self.sent = []

    def offer(self, l, name, g):
        self.ready[(l, name)] = g

    def _take(self, keys):
        keys = [k for k in keys if k[:2] in self.ready and k not in self.taken]
        if not (keys and self.exchange):
            return None
        bufs = [self.ready[k[:2]] for k in keys]
        rows = [(k[2] * (b.shape[1] // k[3]), b.shape[1] // k[3]) for k, b in zip(keys, bufs)]
        ex = ChipExchange(bufs, scatter=True, rows=rows)
        self.taken.update(keys)
        self.sent.append((keys, ex))
        return ex

    def carrier(self, l, slot):
        plan = self.SLOTS[slot] + (self.LAYER0.get(slot, []) if l == 0 else [])
        return self._take([(l + dl, n, i, m) for dl, n, i, m in plan])

    def flush(self, name):
        pieces = {k[:2]: max([t[3] for t in self.taken if t[:2] == k[:2]] + [1]) for k in self.ready}
        ex = self._take(sorted((l, n, i, m) for (l, n), m in pieces.items() for i in range(m)))
        if ex is not None:
            ex.run(name)

    def received(self):
        got = {}
        for keys, ex in self.sent:
            for k, r in zip(keys, ex.result):
                got.setdefault(k[:2], {})[k[2]] = r
        return {k: [v[i] for i in sorted(v)] for k, v in got.items()}


def _quarters(g):
    return g.reshape(N_CHIPS, g.shape[0] // N_CHIPS, g.shape[1])


def _layer_bwd(dx2, dx2b, mem, p, tabs, s, l, sink):
    tag = f"l{l}"
    seq = dx2.shape[0]
    g = {}
    da = mm_nt(dx2b, p["w_down"], s["a"], tm=TM_NT, tn=TN, name=f"down_bwd_x_{tag}", comm=sink.carrier(l, "down_x"))
    g_down = mm_tn(s["a"], dx2b, relu2=True, tm=TM_W, tn=TN_W, tk=TK_W, out="rows", name=f"down_bwd_w_{tag}",
                   comm=sink.carrier(l, "down_w"))
    sink.offer(l, "w_down", _quarters(g_down))
    g_up = mm_tn(s["hn2"], da, relu2=False, tm=TM_W, tn=TN_W, tk=TK_W, out="cols", name=f"up_bwd_w_{tag}",
                 comm=sink.carrier(l, "up_w"))
    sink.offer(l, "w_up", g_up)
    dx1, dx1b, g["norm_mlp"] = mm_nt_normbwd(da, p["w_up"], s["x1"], p["norm_mlp"], dx2, tm=TM_NORM_BWD, tk=TK_NORM_BWD,
                                       name=f"up_bwd_x_{tag}", comm=sink.carrier(l, "up_x"))
    dmix = mm_nt(dx1b, p["w_out"], None, tm=TM_NT, tn=TN, name=f"out_bwd_x_{tag}")
    sink.offer(l, "w_out", _quarters(mm_tn(s["mixed"], dx1b, relu2=False, tm=TM_W, tn=TN_W, tk=TK_W, out="rows",
                                           name=f"out_bwd_w_{tag}")))
    proj = s["proj"]
    do, dr, g["gla_norm"] = gla_post_bwd(dmix, s["o_f"], s["o_b"], proj, p["gla_norm"], name=f"gla_post_bwd_{tag}")
    r_f = gla_bwd(proj, p["up_f"], p["bias_f"], s["st_f"], do, None, reverse=False, name=f"gla_bwd_f_{tag}")
    r_b = gla_bwd(proj, p["up_b"], p["bias_b"], s["st_b"], do, r_f[:4], reverse=True, name=f"gla_bwd_b_{tag}")
    dgq, dgk, dgv, dlr = r_b[:4]
    g["up_f"], g["bias_f"] = r_f[4][0:GLA_GATE_RANK], r_f[5]
    g["up_b"], g["bias_b"] = r_b[4][GLA_GATE_RANK:2 * GLA_GATE_RANK], r_b[5]
    *branch, g["dil_norm"] = dil_combine_bwd(dmix, s["os"], s["ls"], p["dil_norm"], name=f"dil_combine_bwd_{tag}")
    dos, corrs = branch[:3], branch[3:]
    ddq = ddk = ddv = None
    ds_sums = []
    for r, (d, (bias, bias_t)) in enumerate(zip(DIL_DILATIONS, tabs)):
        qkv_d = _strided(s["qkv"], d)
        do_d, lse_d, corr_d = _strided(dos[r], d), s["ls_strided"][r], _strided(corrs[r], d)
        dq_d, ds_sum = dil_bwd_q(qkv_d, bias, do_d, lse_d, corr_d, d=d, name=f"dil_bwd_q_d{d}_{tag}")
        dk_d, dv_d = dil_bwd_kv(qkv_d, bias_t, do_d, lse_d, corr_d, d=d, name=f"dil_bwd_kv_d{d}_{tag}")
        ds_sums.append(ds_sum)
        dq_d, dk_d, dv_d = [_unstrided(t, d).astype(F32) for t in (dq_d, dk_d, dv_d)]
        ddq, ddk, ddv = (dq_d, dk_d, dv_d) if ddq is None else (ddq + dq_d, ddk + dk_d, ddv + dv_d)
    g["rel_bias"] = _rel_bias_grad(ds_sums, seq)
    dmq, dmk, dmv, g["mem_out_norm"] = mem_bwd(dmix, proj, s["kv"], p["mem_out_norm"], name=f"mem_bwd_{tag}")
    dkv = jnp.concatenate([dmk, dmv], axis=1).astype(BF16)
    sink.offer(l, "w_mem_kv", _quarters(mm_tn(s["hmem"], dkv, relu2=False, tm=TM_W, tn=TN_W, tk=TK_W, out="rows",
                                              name=f"mem_kv_bwd_w_{tag}")))
    _, _, g["mem_norm"] = mm_nt_normbwd(dkv, p["w_mem_kv"], mem, p["mem_norm"], jnp.zeros_like(mem), tm=TM_NORM_BWD,
                                     tk=TK_NORM_BWD, name=f"mem_kv_bwd_x_{tag}")
    dproj = jnp.concatenate([dgq, dgk, dgv, dr, ddq.astype(BF16), ddk.astype(BF16), ddv.astype(BF16), dmq, dlr,
                             jnp.zeros((seq, IN_PAD - C_LR - LANES), BF16)], axis=1)
    g_in = mm_tn(s["hn1"], dproj, relu2=False, tm=TM_W, tn=IN_TILE, tk=TK_W, name=f"in_bwd_w_{tag}",
                 comm=sink.carrier(l, "in_w"))
    sink.offer(l, "w_in", _shard_w_in_grad(g_in))
    dx0, dx0b, g["norm_mix"] = mm_nt_normbwd(dproj, p["w_in"], s["x"], p["norm_mix"], dx1, tm=TM_NORM_BWD, tk=IN_TILE,
                                             name=f"in_bwd_x_{tag}", comm=sink.carrier(l, "in_x"))
    return dx0, dx0b, g


def _shard_w_in_grad(g):
    g = jnp.concatenate([g[:, :LR_COL], g[:, C_LR:C_LR + 2 * GLA_GATE_RANK], g[:, LR_COL:C_LR]], axis=1).astype(BF16)
    w = IN_WIDTH // N_CHIPS
    return jnp.stack([g[:, j * w:(j + 1) * w] for j in range(N_CHIPS)], axis=0)


def local_step(x, mem, target, depth, layer_weights, norm_final, rel_bias, fwd_carry, sink):
    tabs = _dil_tables(rel_bias, x.shape[0])
    saved, layers = [], []
    for l in range(depth):
        layers.append(layer_weights(l))
        x, s = _layer_fwd(x, mem, layers[l], tabs, l, fwd_carry(l))
        saved.append(s)
    dx, dxb, g_final, loss = loss_head(x, norm_final, target, tm=TM_LOSS, name="loss_head")
    grads = [None] * depth
    for l in reversed(range(depth)):
        dx, dxb, grads[l] = _layer_bwd(dx, dxb, mem, layers[l], tabs, saved[l], l, sink)
    return loss, dx, grads, g_final


def sibling_exchange(bufs, *, name):
    n = len(bufs)

    def body(*refs):
        ins, outs = refs[:n], refs[n:2 * n]
        send, recv = refs[2 * n:]
        x, y, c = _mesh_pos()
        cps = [pltpu.make_async_remote_copy(src_ref=ins[a], dst_ref=outs[a], send_sem=send.at[a], recv_sem=recv.at[a],
                                            device_id=(x, y, 1 - c), device_id_type=MESH) for a in range(n)]
        for cp in cps:
            cp.start()
        for cp in cps:
            cp.wait()

    return pl.pallas_call(
        body, name=name, in_specs=[HBM_SPEC] * n, out_specs=[HBM_SPEC] * n,
        out_shape=[SDS(b.shape, b.dtype) for b in bufs],
        scratch_shapes=[pltpu.SemaphoreType.DMA((n,)), pltpu.SemaphoreType.DMA((n,))])(*bufs)


def all_reduce_small(v, *, name):
    R = v.shape[0]

    def body(v_ref, o_ref, slots, send, recv):
        x, y, c = _mesh_pos()
        me = 4 * x + 2 * y + c
        flip = lambda p, f: 1 - p if f else p
        peers = [(flip(x, k & 4), flip(y, k & 2), flip(c, k & 1)) for k in range(1, N_DEV)]
        slots[pl.ds(me, 1)] = v_ref[...][None]
        cps = [pltpu.make_async_remote_copy(src_ref=v_ref, dst_ref=slots.at[me], send_sem=send.at[k], recv_sem=recv.at[k],
                                            device_id=peer, device_id_type=MESH) for k, peer in enumerate(peers)]
        for cp in cps:
            cp.start()
        for k, (px, py, pc) in enumerate(peers):
            pltpu.make_async_remote_copy(src_ref=v_ref, dst_ref=slots.at[4 * px + 2 * py + pc], send_sem=send.at[k],
                                         recv_sem=recv.at[k], device_id=(px, py, pc), device_id_type=MESH).wait_recv()
        for cp in cps:
            cp.wait_send()
        acc = slots[0]
        for s in range(1, N_DEV):
            acc = acc + slots[s]
        o_ref[...] = acc

    vm = pl.BlockSpec(memory_space=pltpu.VMEM)
    return pl.pallas_call(
        body, name=name, in_specs=[vm], out_specs=vm, out_shape=SDS(v.shape, F32),
        scratch_shapes=[pltpu.VMEM((N_DEV, R, LANES), F32), pltpu.SemaphoreType.DMA((N_DEV - 1,)),
                        pltpu.SemaphoreType.DMA((N_DEV - 1,))])(v)


def sum_chips(recv, *, name):
    _, R, C = recv.shape
    tr = _tile(R, 512)

    def body(r_ref, o_ref):
        acc = r_ref[0].astype(F32)
        for j in range(1, N_CHIPS):
            acc = acc + r_ref[j].astype(F32)
        o_ref[...] = acc.astype(BF16)

    return pl.pallas_call(
        body, name=name, grid=(R // tr,), in_specs=[pl.BlockSpec((N_CHIPS, tr, C), lambda i: (0, i, 0))],
        out_specs=pl.BlockSpec((tr, C), lambda i: (i, 0)), out_shape=SDS((R, C), BF16),
        compiler_params=_cp("parallel"))(recv)


def adamw(parts, w, m, v, *, name):
    R, C = w.shape
    n = len(parts)
    tr = _tile(R, 512)
    while 2 * (n + 7) * tr * C * 4 > ELEMWISE_BLOCK_BYTES and tr % 16 == 0:
        tr //= 2
    c1 = 1.0 - ADAM_B1 ** ADAM_STEP
    c2 = 1.0 - ADAM_B2 ** ADAM_STEP

    def body(*refs):
        w_ref, m_ref, v_ref = refs[n:n + 3]
        g_ref, d_ref, nm_ref, nv_ref = refs[n + 3:]
        g = refs[0][...].astype(F32)
        for r in refs[1:n]:
            g = g + r[...].astype(F32)
        nm = ADAM_B1 * m_ref[...] + (1.0 - ADAM_B1) * g
        nv = ADAM_B2 * v_ref[...] + (1.0 - ADAM_B2) * (g * g)
        g_ref[...] = g
        nm_ref[...] = nm
        nv_ref[...] = nv
        d_ref[...] = -ADAM_LR * ((nm / c1) / (jnp.sqrt(nv / c2) + ADAM_EPS) + ADAM_WD * w_ref[...])

    blk = pl.BlockSpec((tr, C), lambda i: (i, 0))
    return pl.pallas_call(
        body, name=name, grid=(R // tr,), in_specs=[blk] * (n + 3), out_specs=[blk] * 4,
        out_shape=[SDS((R, C), F32)] * 4, compiler_params=_cp("parallel"))(*parts, w, m, v)


PACK_ROWS = 8


def _pack(parts):
    rows = []
    for a in parts:
        flat = a.reshape(-1).astype(F32)
        n = -(-flat.shape[0] // (PACK_ROWS * LANES)) * PACK_ROWS * LANES
        rows.append(jnp.pad(flat, (0, n - flat.shape[0])).reshape(-1, LANES))
    return jnp.concatenate(rows, axis=0)


def _unpack(slab, shapes):
    out, r = [], 0
    for shp in shapes:
        n = int(np.prod(shp))
        nr = -(-n // (PACK_ROWS * LANES)) * PACK_ROWS
        out.append(slab[r:r + nr].reshape(-1)[:n].reshape(shp))
        r += nr
    return out


SMALL = ["norm_mix", "gla_gate_bias_fwd", "gla_gate_bias_bwd", "gla_norm", "rel_bias", "dil_norm", "mem_norm",
         "mem_out_norm", "norm_mlp", "norm_final"]
GATE_UPS = ["gla_gate_up_fwd", "gla_gate_up_bwd"]
BIG = ["w_in", "w_mem_kv", "w_out", "w_up", "w_down"]
WEIGHTS = ["norm_mix", "w_in", "gla_gate_up_fwd", "gla_gate_bias_fwd", "gla_gate_up_bwd", "gla_gate_bias_bwd", "gla_norm",
           "rel_bias", "dil_norm", "mem_norm", "w_mem_kv", "mem_out_norm", "w_out", "norm_mlp", "w_up", "w_down",
           "norm_final"]


FWD_CARRY = {"in": [(1, "w_in")], "out": [(1, "w_out"), (1, "w_mem_kv")], "up": [(1, "w_up")], "down": [(1, "w_down")]}
FWD_CARRY_LAYER0 = {"in": ["w_out", "w_mem_kv"], "gla_f": ["w_up"], "up": ["w_down"]}


def _kernel_layout(name, g):
    if name == "w_up":
        return g
    if name == "w_in":
        full = jnp.concatenate([g[j] for j in range(N_CHIPS)], axis=1)
        return jnp.concatenate([full[:, :LR_COL], full[:, LR_COL + 2 * GLA_GATE_RANK:],
                                full[:, LR_COL:LR_COL + 2 * GLA_GATE_RANK],
                                jnp.zeros((full.shape[0], IN_PAD - IN_WIDTH), BF16)], axis=1)
    return g.reshape(-1, g.shape[-1])


def kernel(x, mem, norm_mix, w_in, gla_gate_up_fwd, gla_gate_bias_fwd, gla_gate_up_bwd, gla_gate_bias_bwd, gla_norm, rel_bias, dil_norm, mem_norm, w_mem_kv, mem_out_norm, w_out, norm_mlp, w_up, w_down, norm_final, loss_target, m_norm_mix, m_w_in, m_gla_gate_up_fwd, m_gla_gate_bias_fwd, m_gla_gate_up_bwd, m_gla_gate_bias_bwd, m_gla_norm, m_rel_bias, m_dil_norm, m_mem_norm, m_w_mem_kv, m_mem_out_norm, m_w_out, m_norm_mlp, m_w_up, m_w_down, m_norm_final, v_norm_mix, v_w_in, v_gla_gate_up_fwd, v_gla_gate_bias_fwd, v_gla_gate_up_bwd, v_gla_gate_bias_bwd, v_gla_norm, v_rel_bias, v_dil_norm, v_mem_norm, v_w_mem_kv, v_mem_out_norm, v_w_out, v_norm_mlp, v_w_up, v_w_down, v_norm_final):
    vals = dict(locals())
    W = {n: vals[n] for n in WEIGHTS}
    M1 = {n: vals["m_" + n] for n in WEIGHTS}
    V2 = {n: vals["v_" + n] for n in WEIGHTS}
    depth = w_in.shape[0]
    chip = 2 * lax.axis_index("x") + lax.axis_index("y")

    wb = {n: W[n].astype(BF16) for n in BIG + GATE_UPS}
    first = ChipExchange([wb[n] for n in ["w_in"] + GATE_UPS], scatter=False, layers=[0] + [None] * len(GATE_UPS))
    first.run("gather_first")
    gate_up = dict(zip(GATE_UPS, first.result[1:]))
    pending = {(0, "w_in"): (first, 0)}
    row = lambda a: a.reshape(1, -1)

    def fwd_carry(l):
        plan = {slot: [(l + dl, n) for dl, n in what if l + dl < depth] for slot, what in FWD_CARRY.items()}
        if l == 0:
            for slot, what in FWD_CARRY_LAYER0.items():
                plan[slot] = plan.get(slot, []) + [(0, n) for n in what]
        carry = {}
        for slot, keys in plan.items():
            if keys:
                carry[slot] = ChipExchange([wb[n] for _, n in keys], scatter=False, layers=[k for k, _ in keys])
                pending.update({key: (carry[slot], i) for i, key in enumerate(keys)})
        return carry

    class LayerWeights(dict):
        def __init__(self, l):
            super().__init__()
            self.l = l

        def __missing__(self, n):
            ex, i = pending[(self.l, n)]
            self[n] = _kernel_layout(n, ex.result[i])
            return self[n]

    def layer_weights(l):
        p = LayerWeights(l)
        full_up = lambda n: jnp.concatenate([gate_up[n][j, l] for j in range(N_CHIPS)], axis=1)
        p["up_f"] = _pad_gate_up(full_up("gla_gate_up_fwd"), 0)
        p["up_b"] = _pad_gate_up(full_up("gla_gate_up_bwd"), GLA_GATE_RANK)
        p["bias_f"], p["bias_b"] = row(gla_gate_bias_fwd[l]), row(gla_gate_bias_bwd[l])
        for n in ("norm_mix", "gla_norm", "dil_norm", "mem_norm", "mem_out_norm", "norm_mlp"):
            p[n] = row(W[n][l])
        return p

    sink = GradScatter(exchange=True)
    loss, grad_x, grads, g_final = local_step(x[0], mem[0], loss_target[0], depth, layer_weights, row(norm_final),
                                              rel_bias, fwd_carry, sink)
    sink.flush("scatter_last_grads")
    recv = sink.received()

    flat2 = lambda a: a.reshape(-1, a.shape[-1])
    part = [jnp.concatenate([sum_chips(r, name=f"sum_chips_{n}_l{l}_p{i}") for l in range(depth)
                             for i, r in enumerate(recv[(l, n)])], axis=0) for n in BIG]
    other = sibling_exchange(part, name="swap_partial_sums")
    out = {}
    for n, pa, ob in zip(BIG, part, other):
        res = adamw([pa, ob], flat2(W[n]), flat2(M1[n]), flat2(V2[n]), name=f"adamw_{n}")
        out[n] = [r.reshape(W[n].shape) for r in res]

    small_g = {}
    for n, key in (("norm_mix", "norm_mix"), ("gla_gate_bias_fwd", "bias_f"), ("gla_gate_bias_bwd", "bias_b"),
                   ("gla_norm", "gla_norm"), ("dil_norm", "dil_norm"), ("mem_norm", "mem_norm"),
                   ("mem_out_norm", "mem_out_norm"), ("norm_mlp", "norm_mlp"),
                   ("gla_gate_up_fwd", "up_f"), ("gla_gate_up_bwd", "up_b")):
        small_g[n] = jnp.stack([grads[l][key].reshape(W[n].shape[1:] if n not in GATE_UPS else (GLA_GATE_RANK, QK_W))
                                for l in range(depth)], axis=0)
    small_g["rel_bias"] = sum(grads[l]["rel_bias"] for l in range(depth))
    small_g["norm_final"] = g_final.reshape(-1)
    names = SMALL + GATE_UPS
    slab = all_reduce_small(_pack([small_g[n] for n in names] + [loss[:, 0]]), name="all_reduce_small")
    *summed, loss_sum = _unpack(slab, [small_g[n].shape for n in names] + [(1,)])
    summed = dict(zip(names, summed))
    for n in GATE_UPS:
        summed[n] = lax.dynamic_slice_in_dim(summed[n], chip * LANES, LANES, axis=2)
    names_s = SMALL + GATE_UPS
    shapes = [W[n].shape for n in names_s]
    res = adamw([_pack([summed[n] for n in names_s])], _pack([W[n] for n in names_s]), _pack([M1[n] for n in names_s]),
                _pack([V2[n] for n in names_s]), name="adamw_small")
    for n, *r in zip(names_s, *[_unpack(t, shapes) for t in res]):
        out[n] = r

    return (loss_sum.reshape(()), grad_x[None], *[out[n][0] for n in WEIGHTS], *[out[n][1] for n in WEIGHTS],
            *[out[n][2] for n in WEIGHTS], *[out[n][3] for n in WEIGHTS])
```

```python
import functools
import math

import numpy as np
import jax
import jax.numpy as jnp
from jax import lax
from jax.experimental import pallas as pl
from jax.experimental.pallas import tpu as pltpu

F32, BF16 = jnp.float32, jnp.bfloat16
SDS = jax.ShapeDtypeStruct

DEPTH = 4
GLA_HEADS, GLA_DK, GLA_DV = 4, 128, 256
GLA_GATE_RANK = 16
GLA_GATE_NORMALIZER = 16.0
GLA_CHUNK = 64
DIL_HEADS, DIL_HEAD_DIM = 4, 128
DIL_DILATIONS = (1, 4, 16)
DIL_W = 64
MEM_HEADS, MEM_HEAD_DIM = 4, 128
REL_BUCKETS, REL_MAX_DISTANCE = 32, 1024
EPS = 1e-6
NEG_INF = -1e30
QK_W, V_W, DIL_WIDTH, MEM_WIDTH = 512, 1024, 512, 512
IN_WIDTH = 5152
LR_COL = 3072
IN_PAD = 5376
C_GQ, C_GK, C_GV, C_GR, C_DQ, C_MQ, C_LR = 0, 512, 1024, 2048, 3072, 4608, 5120
IN_TILE = 1792

ADAM_LR, ADAM_B1, ADAM_B2, ADAM_EPS, ADAM_WD, ADAM_STEP = 0.001, 0.9, 0.999, 1e-08, 0.01, 10

V7X_VMEM_BYTES = 64 * 1024 * 1024
VMEM_LIMIT = V7X_VMEM_BYTES * 7 // 8
LANES = 128

TM_FWD = 512
TM_NT = 1024
TN = 2048
TK = 2048
TM_NORM_BWD = 512
TK_NORM_BWD = 2048
NORM_BWD_ROWS = 64
TM_W, TN_W, TK_W = 1024, 1024, 2048
TM_LOSS = 256
ELEMWISE_BLOCK_BYTES = 24 * 1024 * 1024

NT_DIMS = (((1,), (1,)), ((), ()))
TN_DIMS = (((0,), (0,)), ((), ()))


def _nt(a, b):
    return lax.dot_general(a, b, NT_DIMS, preferred_element_type=F32)


def _tn(a, b):
    return lax.dot_general(a, b, TN_DIMS, preferred_element_type=F32)


def _nn(a, b):
    return jnp.dot(a, b, preferred_element_type=F32)


def _cp(*sem):
    return pltpu.CompilerParams(dimension_semantics=sem, vmem_limit_bytes=VMEM_LIMIT)


def _tile(n, t):
    t = min(n, t)
    assert n % t == 0, (n, t)
    return t


MESH = pl.DeviceIdType.MESH
HBM_SPEC = pl.BlockSpec(memory_space=pltpu.HBM)
N_CHIPS = 4
N_DEV = 8


def _mesh_pos():
    return lax.axis_index("x"), lax.axis_index("y"), lax.axis_index("c")


class ChipExchange:
    def __init__(self, bufs, *, scatter, layers=None, rows=None):
        self.bufs = list(bufs)
        self.n = n = len(self.bufs)
        self.scatter = scatter
        self.layers = list(layers) if layers is not None else [None] * n
        self.rows = list(rows) if rows is not None else [None] * n
        dma = pltpu.SemaphoreType.DMA
        if scatter:
            shp = [b.shape if r is None else (b.shape[0], r[1]) + b.shape[2:] for b, r in zip(self.bufs, self.rows)]
            self.out_shape = [SDS(s, b.dtype) for s, b in zip(shp, self.bufs)]
            self.scratch = [dma((3 * n,)), dma((3 * n,)), dma((n,))]
        else:
            shp = [b.shape if l is None else b.shape[1:] for b, l in zip(self.bufs, self.layers)]
            shp = [s if r is None else (r[1],) + s[1:] for s, r in zip(shp, self.rows)]
            assert all(s[0] % 2 == 0 for s in shp), shp
            self.out_shape = [SDS((N_CHIPS,) + s, b.dtype) for s, b in zip(shp, self.bufs)]
            self.scratch = [dma((3 * n,)), dma((3 * n,)), dma((n,)), dma((3 * n,)), dma((3 * n,))]
        self.result = None

    def _plan(self, ins, outs, sems, finishing):
        x, y, c = _mesh_pos()
        me = 2 * x + y
        chips = [(1 - x, y), (x, 1 - y), (1 - x, 1 - y)]
        remote = lambda src, dst, ss, rs, k, dev: pltpu.make_async_remote_copy(
            src_ref=src, dst_ref=dst, send_sem=ss.at[k], recv_sem=rs.at[k], device_id=dev, device_id_type=MESH)
        kept, sent, landing, passed, handed = [], [], [], [], []
        for a in range(self.n):
            src = ins[a] if self.layers[a] is None else ins[a].at[self.layers[a]]
            if self.scatter:
                r = self.rows[a]
                pick = (lambda j: src.at[j]) if r is None else (lambda j: src.at[j, pl.ds(r[0], r[1])])
                kept.append(pltpu.make_async_copy(pick(me), outs[a].at[me], sems[2].at[a]))
                for j, (px, py) in enumerate(chips):
                    k, peer = 3 * a + j, 2 * px + py
                    sent.append(remote(pick(peer), outs[a].at[me], sems[0], sems[1], k, (px, py, c)))
                    if finishing:
                        landing.append(remote(pick(me), outs[a].at[peer], sems[0], sems[1], k, (px, py, c)))
                continue
            nr = self.out_shape[a].shape[1]
            half = nr // 2
            first = 0 if self.rows[a] is None else self.rows[a][0]
            mine, other = pl.ds(c * half, half), pl.ds((1 - c) * half, half)
            src_mine = src.at[pl.ds(first + c * half, half)]
            kept.append(pltpu.make_async_copy(src.at[pl.ds(first, nr)], outs[a].at[me], sems[2].at[a]))
            for j, (px, py) in enumerate(chips):
                k, peer = 3 * a + j, 2 * px + py
                sent.append(remote(src_mine, outs[a].at[me, mine], sems[0], sems[1], k, (px, py, c)))
                if finishing:
                    landing.append(remote(src_mine, outs[a].at[peer, mine], sems[0], sems[1], k, (px, py, c)))
                    passed.append(remote(outs[a].at[peer, mine], outs[a].at[peer, mine], sems[3], sems[4], k, (x, y, 1 - c)))
                    handed.append(remote(outs[a].at[peer, other], outs[a].at[peer, other], sems[3], sems[4], k,
                                         (x, y, 1 - c)))
        return kept, sent, landing, passed, handed

    def start(self, ins, outs, sems):
        kept, sent, _, _, _ = self._plan(ins, outs, sems, False)
        for cp in kept + sent:
            cp.start()

    def finish(self, ins, outs, sems):
        kept, sent, landing, passed, handed = self._plan(ins, outs, sems, True)
        for k, cp in enumerate(landing):
            cp.wait_recv()
            if passed:
                passed[k].start()
        for cp in handed:
            cp.wait_recv()
        for cp in sent + passed:
            cp.wait_send()
        for cp in kept:
            cp.wait()

    def run(self, name):
        n = self.n

        def body(*refs):
            self.start(refs[:n], refs[n:2 * n], refs[2 * n:])
            self.finish(refs[:n], refs[n:2 * n], refs[2 * n:])

        self.result = pl.pallas_call(body, name=name, in_specs=[HBM_SPEC] * n, out_specs=[HBM_SPEC] * n,
                                     out_shape=self.out_shape, scratch_shapes=self.scratch)(*self.bufs)
        return self.result


def _call(body, *, name, grid, in_specs, out_specs, out_shape, sem, args, scratch_shapes=(), comm=None):
    if comm is None:
        return pl.pallas_call(body, name=name, grid=grid, in_specs=list(in_specs), out_specs=list(out_specs),
                              out_shape=list(out_shape), scratch_shapes=list(scratch_shapes),
                              compiler_params=_cp(*sem))(*args)
    n_in, n_out, n_scr, nc = len(in_specs), len(out_shape), len(scratch_shapes), comm.n

    def wrapped(*refs):
        ins, cin = refs[:n_in], refs[n_in:n_in + nc]
        outs, cout = refs[n_in + nc:n_in + nc + n_out], refs[n_in + nc + n_out:n_in + 2 * nc + n_out]
        scr, csem = refs[n_in + 2 * nc + n_out:n_in + 2 * nc + n_out + n_scr], refs[n_in + 2 * nc + n_out + n_scr:]
        ids = [pl.program_id(d) for d in range(len(grid))]
        first = functools.reduce(jnp.logical_and, [i == 0 for i in ids])
        last = functools.reduce(jnp.logical_and, [i == g - 1 for i, g in zip(ids, grid)])

        @pl.when(first)
        def _():
            comm.start(cin, cout, csem)
        body(*ins, *outs, *scr)

        @pl.when(last)
        def _():
            comm.finish(cin, cout, csem)

    res = pl.pallas_call(
        wrapped, name=name, grid=grid, in_specs=list(in_specs) + [HBM_SPEC] * nc,
        out_specs=list(out_specs) + [HBM_SPEC] * nc, out_shape=list(out_shape) + comm.out_shape,
        scratch_shapes=list(scratch_shapes) + comm.scratch,
        compiler_params=_cp(*("arbitrary",) * len(grid)))(*args, *comm.bufs)
    comm.result = res[n_out:]
    return res[:n_out]


def _w_spec(w, rows, tn, col_of):
    if w.ndim == 2:
        tn = _tile(w.shape[1], tn)
        return w.shape[1], tn, pl.BlockSpec((rows, tn), lambda *g: (0, col_of(*g)))
    tn = _tile(w.shape[2], tn)
    per = w.shape[2] // tn
    return N_CHIPS * w.shape[2], tn, pl.BlockSpec((None, rows, tn), lambda *g: (col_of(*g) // per, 0, col_of(*g) % per))


def norm_mm(x, gain, w, *, tm, tn, name, comm=None):
    M, D = x.shape
    tm = _tile(M, tm)
    N, tn, w_spec = _w_spec(w, D, tn, lambda i, j: j)

    def body(x_ref, g_ref, w_ref, o_ref, hn_ref):
        @pl.when(pl.program_id(1) == 0)
        def _():
            xf = x_ref[...]
            r = lax.rsqrt(jnp.mean(xf * xf, axis=-1, keepdims=True) + EPS)
            hn_ref[...] = (xf * r * g_ref[...]).astype(BF16)
        o_ref[...] = _nn(hn_ref[...], w_ref[...]).astype(BF16)

    return _call(
        body, name=name, grid=(M // tm, N // tn),
        in_specs=[pl.BlockSpec((tm, D), lambda i, j: (i, 0)), pl.BlockSpec((1, D), lambda i, j: (0, 0)),
                  w_spec],
        out_specs=[pl.BlockSpec((tm, tn), lambda i, j: (i, j)), pl.BlockSpec((tm, D), lambda i, j: (i, 0))],
        out_shape=[SDS((M, N), BF16), SDS((M, D), BF16)], sem=("parallel", "arbitrary"), args=(x, gain, w), comm=comm)


def mm_res(a, w, res, *, relu2, tm, tn, tk, name, comm=None):
    M, K = a.shape
    N = w.shape[1]
    tm, tn, tk = _tile(M, tm), _tile(N, tn), _tile(K, tk)

    def body(a_ref, w_ref, r_ref, o_ref):
        @pl.when(pl.program_id(2) == 0)
        def _():
            o_ref[...] = r_ref[...]
        av = a_ref[...]
        if relu2:
            af = jnp.maximum(av.astype(F32), 0.0)
            av = (af * af).astype(BF16)
        o_ref[...] += _nn(av, w_ref[...])

    return _call(
        body, name=name, grid=(M // tm, N // tn, K // tk),
        in_specs=[pl.BlockSpec((tm, tk), lambda i, j, k: (i, k)), pl.BlockSpec((tk, tn), lambda i, j, k: (k, j)),
                  pl.BlockSpec((tm, tn), lambda i, j, k: (i, j))],
        out_specs=[pl.BlockSpec((tm, tn), lambda i, j, k: (i, j))], out_shape=[SDS((M, N), F32)],
        sem=("parallel", "parallel", "arbitrary"), args=(a, w, res), comm=comm)[0]


def mm_nt(dy, w, a, *, tm, tn, name, comm=None):
    M, K = dy.shape
    N = w.shape[0]
    tm, tn = _tile(M, tm), _tile(N, tn)
    with_a = a is not None

    def body(*refs):
        if with_a:
            dy_ref, w_ref, a_ref, o_ref = refs
        else:
            dy_ref, w_ref, o_ref = refs
        r = _nt(dy_ref[...], w_ref[...])
        if with_a:
            r = r * (2.0 * jnp.maximum(a_ref[...].astype(F32), 0.0))
        o_ref[...] = r.astype(BF16)

    in_specs = [pl.BlockSpec((tm, K), lambda i, j: (i, 0)), pl.BlockSpec((tn, K), lambda i, j: (j, 0))]
    args = [dy, w]
    if with_a:
        in_specs.append(pl.BlockSpec((tm, tn), lambda i, j: (i, j)))
        args.append(a)
    return _call(
        body, name=name, grid=(M // tm, N // tn), in_specs=in_specs,
        out_specs=[pl.BlockSpec((tm, tn), lambda i, j: (i, j))], out_shape=[SDS((M, N), BF16)],
        sem=("parallel", "arbitrary"), args=args, comm=comm)[0]


def mm_nt_normbwd(dy, w, x, gain, dx_in, *, tm, tk, name, comm=None):
    M, K = dy.shape
    D = x.shape[1]
    tm = _tile(M, tm)
    _, tk, w_spec = _w_spec(w, D, tk, lambda i, k: k)
    nk = K // tk
    rc = _tile(tm, NORM_BWD_ROWS)

    def body(dy_ref, w_ref, x_ref, g_ref, dxin_ref, dx_ref, dxb_ref, gg_ref, acc):
        i, k = pl.program_id(0), pl.program_id(1)

        @pl.when(k == 0)
        def _():
            acc[...] = jnp.zeros_like(acc)
        acc[...] += _nt(dy_ref[...], w_ref[...])

        @pl.when(k == nk - 1)
        def _():
            def rows(c, part):
                rs = pl.ds(pl.multiple_of(c * rc, rc), rc)
                dh, xf = acc[rs, :], x_ref[rs, :]
                r = lax.rsqrt(jnp.mean(xf * xf, axis=-1, keepdims=True) + EPS)
                xhat = xf * r
                u = dh * g_ref[...]
                dx = dxin_ref[rs, :] + r * (u - xhat * jnp.mean(u * xhat, axis=-1, keepdims=True))
                dx_ref[rs, :] = dx
                dxb_ref[rs, :] = dx.astype(BF16)
                return part + jnp.sum(dh * xhat, axis=0, keepdims=True)

            part = lax.fori_loop(0, tm // rc, rows, jnp.zeros((1, D), F32))

            @pl.when(i == 0)
            def _():
                gg_ref[...] = part

            @pl.when(i > 0)
            def _():
                gg_ref[...] += part

    return _call(
        body, name=name, grid=(M // tm, nk),
        in_specs=[pl.BlockSpec((tm, tk), lambda i, k: (i, k)), w_spec,
                  pl.BlockSpec((tm, D), lambda i, k: (i, 0)), pl.BlockSpec((1, D), lambda i, k: (0, 0)),
                  pl.BlockSpec((tm, D), lambda i, k: (i, 0))],
        out_specs=[pl.BlockSpec((tm, D), lambda i, k: (i, 0)),
                   pl.BlockSpec((tm, D), lambda i, k: (i, 0)),
                   pl.BlockSpec((1, D), lambda i, k: (0, 0))],
        out_shape=[SDS((M, D), F32), SDS((M, D), BF16), SDS((1, D), F32)], scratch_shapes=[pltpu.VMEM((tm, D), F32)],
        sem=("arbitrary", "arbitrary"), args=(dy, w, x, gain, dx_in), comm=comm)


def mm_tn(a, dy, *, relu2, tm, tn, tk, name, out="f32", comm=None):
    S_, Ka = a.shape
    N = dy.shape[1]
    tm, tk = _tile(Ka, tm), _tile(S_, tk)
    tn = _tile(N // N_CHIPS if out == "cols" else N, tn)
    nk = S_ // tk

    def body(a_ref, dy_ref, o_ref, acc):
        k = pl.program_id(2)

        @pl.when(k == 0)
        def _():
            acc[...] = jnp.zeros_like(acc)
        av = a_ref[...]
        if relu2:
            af = jnp.maximum(av.astype(F32), 0.0)
            av = (af * af).astype(BF16)
        acc[...] += _tn(av, dy_ref[...].astype(BF16))

        @pl.when(k == nk - 1)
        def _():
            o_ref[...] = acc[...].astype(o_ref.dtype)

    if out == "cols":
        per = N // N_CHIPS // tn
        o_spec = pl.BlockSpec((None, tm, tn), lambda i, j, k: (j // per, i, j % per))
        o_shape = SDS((N_CHIPS, Ka, N // N_CHIPS), BF16)
    else:
        o_spec = pl.BlockSpec((tm, tn), lambda i, j, k: (i, j))
        o_shape = SDS((Ka, N), F32 if out == "f32" else BF16)
    return _call(
        body, name=name, grid=(Ka // tm, N // tn, nk),
        in_specs=[pl.BlockSpec((tk, tm), lambda i, j, k: (k, i)), pl.BlockSpec((tk, tn), lambda i, j, k: (k, j))],
        out_specs=[o_spec], out_shape=[o_shape], scratch_shapes=[pltpu.VMEM((tm, tn), F32)],
        sem=("parallel", "parallel", "arbitrary"), args=(a, dy), comm=comm)[0]


def loss_head(x, gain, target, *, tm, name):
    M, D = x.shape
    tm = _tile(M, tm)

    def body(x_ref, g_ref, t_ref, dx_ref, dxb_ref, gg_ref, l_ref):
        i = pl.program_id(0)
        xf = x_ref[...]
        r = lax.rsqrt(jnp.mean(xf * xf, axis=-1, keepdims=True) + EPS)
        xhat = xf * r
        e = xhat * g_ref[...] - t_ref[...]
        lpart = 0.5 * jnp.sum(jnp.mean(e * e, axis=-1, keepdims=True), axis=0, keepdims=True)
        dy = e * (1.0 / D)
        u = dy * g_ref[...]
        dx = r * (u - xhat * jnp.mean(u * xhat, axis=-1, keepdims=True))
        dx_ref[...] = dx
        dxb_ref[...] = dx.astype(BF16)
        gpart = jnp.sum(dy * xhat, axis=0, keepdims=True)
        lrow = jnp.broadcast_to(lpart, (1, LANES))

        @pl.when(i == 0)
        def _():
            gg_ref[...] = gpart
            l_ref[...] = lrow

        @pl.when(i > 0)
        def _():
            gg_ref[...] += gpart
            l_ref[...] += lrow

    return pl.pallas_call(
        body, name=name, grid=(M // tm,),
        in_specs=[pl.BlockSpec((tm, D), lambda i: (i, 0)), pl.BlockSpec((1, D), lambda i: (0, 0)),
                  pl.BlockSpec((tm, D), lambda i: (i, 0))],
        out_specs=[pl.BlockSpec((tm, D), lambda i: (i, 0)), pl.BlockSpec((tm, D), lambda i: (i, 0)),
                   pl.BlockSpec((1, D), lambda i: (0, 0)), pl.BlockSpec((1, LANES), lambda i: (0, 0))],
        out_shape=[SDS((M, D), F32), SDS((M, D), BF16), SDS((1, D), F32), SDS((1, LANES), F32)],
        compiler_params=_cp("arbitrary"))(x, gain, target)


def _log_sigmoid(z):
    return jnp.minimum(z, 0.0) - jnp.log(1.0 + jnp.exp(-jnp.abs(z)))


def _chunk_scan(x, row, reverse):
    n = x.shape[0]
    s = 1
    while s < GLA_CHUNK:
        if reverse:
            x = x + jnp.where(row < GLA_CHUNK - s, pltpu.roll(x, n - s, 0), 0.0)
        else:
            x = x + jnp.where(row >= s, pltpu.roll(x, s, 0), 0.0)
        s *= 2
    return x


def _chunk_edge(b, reverse):
    t = b.shape[0]
    nc = t // GLA_CHUNK
    b3 = b.reshape(nc, GLA_CHUNK, b.shape[1])
    e = b3[:, 0:1, :] if reverse else b3[:, GLA_CHUNK - 1:GLA_CHUNK, :]
    return jnp.broadcast_to(e, b3.shape).reshape(b.shape), e.reshape(nc, b.shape[1])


def _gla_gates(lr, up_ref, bias_ref, cs, row, reverse):
    z = _nn(lr, up_ref[:, cs]) + bias_ref[:, cs]
    g = _log_sigmoid(z) * (1.0 / GLA_GATE_NORMALIZER)
    b = _chunk_scan(g, row, reverse)
    bl, blc = _chunk_edge(b, reverse)
    return z, b, bl, blc


def _gla_mask(reverse):
    r = lax.broadcasted_iota(jnp.int32, (GLA_CHUNK, GLA_CHUNK), 0)
    c = lax.broadcasted_iota(jnp.int32, (GLA_CHUNK, GLA_CHUNK), 1)
    return (r <= c) if reverse else (r >= c)


def gla_fwd(proj, upad, bias, *, reverse, name, comm=None):
    S_ = proj.shape[0]
    T = _tile(S_, 512)
    nt, nc = S_ // T, T // GLA_CHUNK
    H, DK, DV, C = GLA_HEADS, GLA_DK, GLA_DV, GLA_CHUNK
    tix = (lambda i: nt - 1 - i) if reverse else (lambda i: i)
    qscale = DK ** -0.5

    def body(q_ref, k_ref, v_ref, lr_ref, up_ref, b_ref, o_ref, st_ref, st_s, qd_s, ki_s, ke_s, dec_s):
        @pl.when(pl.program_id(0) == 0)
        def _():
            st_s[...] = jnp.zeros_like(st_s)
        row = lax.broadcasted_iota(jnp.int32, (T, DK), 0) % C
        mask = _gla_mask(reverse)
        lr = lr_ref[...]
        for h in range(H):
            cs = slice(h * DK, (h + 1) * DK)
            _, b, bl, _ = _gla_gates(lr, up_ref, b_ref, cs, row, reverse)
            q = q_ref[:, cs].astype(F32) * qscale
            k = k_ref[:, cs].astype(F32)
            qd_s[:, cs] = (q * jnp.exp(b)).astype(BF16)
            ki_s[:, cs] = (k * jnp.exp(-b)).astype(BF16)
            ke_s[:, cs] = (k * jnp.exp(bl - b)).astype(BF16)
            dec_s[:, cs] = jnp.exp(bl)

        def chunk(ci, carry):
            c = (nc - 1 - ci) if reverse else ci
            r0 = pl.multiple_of(c * C, C)
            s0 = pl.multiple_of(c * DV, DV)
            for h in range(H):
                cs = slice(h * DK, (h + 1) * DK)
                vs = slice(h * DV, (h + 1) * DV)
                qd, ki, ke = qd_s[pl.ds(r0, C), cs], ki_s[pl.ds(r0, C), cs], ke_s[pl.ds(r0, C), cs]
                v = v_ref[pl.ds(r0, C), vs]
                st = st_s[h]
                stb = st.astype(BF16)
                st_ref[h, pl.ds(s0, DV), :] = stb
                a = jnp.where(mask, _nt(qd, ki), 0.0).astype(BF16)
                o_ref[pl.ds(r0, C), vs] = _nn(a, v) + _nt(qd, stb)
                st_s[h] = dec_s[pl.ds(r0, 1), cs] * st + _tn(v, ke)
            return carry

        lax.fori_loop(0, nc, chunk, 0)

    return _call(
        body, name=name, grid=(nt,),
        in_specs=[pl.BlockSpec((T, QK_W), lambda i: (tix(i), C_GQ // QK_W)),
                  pl.BlockSpec((T, QK_W), lambda i: (tix(i), C_GK // QK_W)),
                  pl.BlockSpec((T, V_W), lambda i: (tix(i), C_GV // V_W)),
                  pl.BlockSpec((T, LANES), lambda i: (tix(i), C_LR // LANES)),
                  pl.BlockSpec((LANES, QK_W), lambda i: (0, 0)), pl.BlockSpec((1, QK_W), lambda i: (0, 0))],
        out_specs=[pl.BlockSpec((T, V_W), lambda i: (tix(i), 0)),
                   pl.BlockSpec((H, nc * DV, DK), lambda i: (0, tix(i), 0))],
        out_shape=[SDS((S_, V_W), F32), SDS((H, S_ // C * DV, DK), BF16)],
        scratch_shapes=[pltpu.VMEM((H, DV, DK), F32), pltpu.VMEM((T, QK_W), BF16), pltpu.VMEM((T, QK_W), BF16),
                        pltpu.VMEM((T, QK_W), BF16), pltpu.VMEM((T, QK_W), F32)],
        sem=("arbitrary",), args=(proj, proj, proj, proj, upad, bias), comm=comm)


def gla_bwd(proj, upad, bias, states, do, prev, *, reverse, name):
    S_ = proj.shape[0]
    T = _tile(S_, 512)
    nt, nc = S_ // T, T // GLA_CHUNK
    H, DK, DV, C = GLA_HEADS, GLA_DK, GLA_DV, GLA_CHUNK
    tix = (lambda i: i) if reverse else (lambda i: nt - 1 - i)
    qscale = DK ** -0.5
    with_prev = prev is not None
    NCP = max(8, nc)

    def body(*refs):
        (q_ref, k_ref, v_ref, lr_ref, up_ref, b_ref, st_ref, do_ref) = refs[:8]
        n_in = 12 if with_prev else 8
        pq_ref, pk_ref, pv_ref, plr_ref = refs[8:12] if with_prev else (None,) * 4
        dq_ref, dk_ref, dv_ref, dlr_ref, dup_ref, dbias_ref = refs[n_in:n_in + 6]
        dst_s, qd_s, ki_s, ke_s, dec_s, dqd_s, dki_s, dke_s, ddec_s, z_s, eb_s, einv_s, eend_s = refs[n_in + 6:]
        first = pl.program_id(0) == 0

        @pl.when(first)
        def _():
            dst_s[...] = jnp.zeros_like(dst_s)
            dup_ref[...] = jnp.zeros_like(dup_ref)
            dbias_ref[...] = jnp.zeros_like(dbias_ref)

        row = lax.broadcasted_iota(jnp.int32, (T, DK), 0) % C
        edge_row = 0 if reverse else C - 1
        mask = _gla_mask(reverse)
        lr = lr_ref[...]
        dlr = plr_ref[...].astype(F32) if with_prev else jnp.zeros((T, LANES), F32)

        for h in range(H):
            cs = slice(h * DK, (h + 1) * DK)
            z, b, bl, _ = _gla_gates(lr, up_ref, b_ref, cs, row, reverse)
            q = q_ref[:, cs].astype(F32) * qscale
            k = k_ref[:, cs].astype(F32)
            eb, einv, eend = jnp.exp(b), jnp.exp(-b), jnp.exp(bl - b)
            z_s[:, cs], eb_s[:, cs], einv_s[:, cs], eend_s[:, cs] = z, eb, einv, eend
            qd_s[:, cs] = (q * eb).astype(BF16)
            ki_s[:, cs] = (k * einv).astype(BF16)
            ke_s[:, cs] = (k * eend).astype(BF16)
            dec_s[:, cs] = jnp.exp(bl)

        def chunk(ci, carry):
            c = ci if reverse else (nc - 1 - ci)
            r0 = pl.multiple_of(c * C, C)
            s0 = pl.multiple_of(c * DV, DV)
            for h in range(H):
                cs = slice(h * DK, (h + 1) * DK)
                vs = slice(h * DV, (h + 1) * DV)
                qdc, kic, kec = qd_s[pl.ds(r0, C), cs], ki_s[pl.ds(r0, C), cs], ke_s[pl.ds(r0, C), cs]
                v = v_ref[pl.ds(r0, C), vs]
                doc = do_ref[pl.ds(r0, C), vs]
                st = st_ref[h, pl.ds(s0, DV), :]
                dst = dst_s[h]
                dstb = dst.astype(BF16)
                a = jnp.where(mask, _nt(qdc, kic), 0.0).astype(BF16)
                da = jnp.where(mask, _nt(doc, v), 0.0).astype(BF16)
                dvc = _tn(a, doc) + _nt(kec, dstb)
                if with_prev:
                    dvc = dvc + pv_ref[pl.ds(r0, C), vs].astype(F32)
                dv_ref[pl.ds(r0, C), vs] = dvc.astype(BF16)
                dqd_s[pl.ds(r0, C), cs] = _nn(da, kic) + _nn(doc, st)
                dki_s[pl.ds(r0, C), cs] = _tn(da, qdc)
                dke_s[pl.ds(r0, C), cs] = _nn(v, dstb)
                ddec_s[h, pl.ds(c, 1), :] = jnp.sum(dst * st.astype(F32), axis=0, keepdims=True)
                dst_s[h] = dec_s[pl.ds(r0, 1), cs] * dst + _tn(doc, qdc)
            return carry

        lax.fori_loop(0, nc, chunk, 0)

        for h in range(H):
            cs = slice(h * DK, (h + 1) * DK)
            z, eb, einv, eend = z_s[:, cs], eb_s[:, cs], einv_s[:, cs], eend_s[:, cs]
            q = q_ref[:, cs].astype(F32) * qscale
            k = k_ref[:, cs].astype(F32)
            qd, ki, ke = q * eb, k * einv, k * eend
            dec = dec_s[:, cs].reshape(nc, C, DK)[:, 0, :]
            dqd, dki, dke = dqd_s[:, cs], dki_s[:, cs], dke_s[:, cs]
            dq = dqd * eb * qscale
            dk = dki * einv + dke * eend
            if with_prev:
                dq = dq + pq_ref[:, cs].astype(F32)
                dk = dk + pk_ref[:, cs].astype(F32)
            dq_ref[:, cs] = dq.astype(BF16)
            dk_ref[:, cs] = dk.astype(BF16)
            wke = dke * ke
            db = dqd * qd - dki * ki - wke
            dbl = jnp.sum(wke.reshape(nc, C, DK), axis=1) + ddec_s[h, 0:nc, :] * dec
            dbl_b = jnp.broadcast_to(dbl.reshape(nc, 1, DK), (nc, C, DK)).reshape(T, DK)
            db = db + jnp.where(row == edge_row, dbl_b, 0.0)
            dg = _chunk_scan(db, row, not reverse)
            dz = dg * (1.0 / GLA_GATE_NORMALIZER) * (1.0 / (1.0 + jnp.exp(z)))
            dzb = dz.astype(BF16)
            dbias_ref[:, cs] += jnp.sum(dz, axis=0, keepdims=True)
            dup_ref[:, cs] += _tn(lr, dzb)
            dlr = dlr + _nt(dzb, up_ref[:, cs])
        dlr_ref[...] = dlr.astype(BF16)

    tile = lambda w, cb: pl.BlockSpec((T, w), lambda i: (tix(i), cb))
    in_specs = [tile(QK_W, C_GQ // QK_W), tile(QK_W, C_GK // QK_W), tile(V_W, C_GV // V_W), tile(LANES, C_LR // LANES),
                pl.BlockSpec((LANES, QK_W), lambda i: (0, 0)), pl.BlockSpec((1, QK_W), lambda i: (0, 0)),
                pl.BlockSpec((H, nc * DV, DK), lambda i: (0, tix(i), 0)), tile(V_W, 0)]
    args = [proj, proj, proj, proj, upad, bias, states, do]
    if with_prev:
        in_specs += [tile(QK_W, 0), tile(QK_W, 0), tile(V_W, 0), tile(LANES, 0)]
        args += list(prev)
    return pl.pallas_call(
        body, name=name, grid=(nt,), in_specs=in_specs,
        out_specs=[tile(QK_W, 0), tile(QK_W, 0), tile(V_W, 0), tile(LANES, 0),
                   pl.BlockSpec((LANES, QK_W), lambda i: (0, 0)), pl.BlockSpec((1, QK_W), lambda i: (0, 0))],
        out_shape=[SDS((S_, QK_W), BF16), SDS((S_, QK_W), BF16), SDS((S_, V_W), BF16), SDS((S_, LANES), BF16),
                   SDS((LANES, QK_W), F32), SDS((1, QK_W), F32)],
        scratch_shapes=[pltpu.VMEM((H, DV, DK), F32), pltpu.VMEM((T, QK_W), BF16), pltpu.VMEM((T, QK_W), BF16),
                        pltpu.VMEM((T, QK_W), BF16), pltpu.VMEM((T, QK_W), F32), pltpu.VMEM((T, QK_W), F32),
                        pltpu.VMEM((T, QK_W), F32), pltpu.VMEM((T, QK_W), F32), pltpu.VMEM((H, NCP, DK), F32)]
        + [pltpu.VMEM((T, QK_W), F32)] * 4,
        compiler_params=_cp("arbitrary"))(*args)


def _head_norm_fwd(o, width):
    out = []
    for h in range(o.shape[1] // width):
        oh = o[:, h * width:(h + 1) * width]
        rs = lax.rsqrt(jnp.mean(oh * oh, axis=-1, keepdims=True) + EPS)
        out.append((oh * rs, rs))
    return out


def gla_post(o_f, o_b, proj, gain, *, name):
    S_ = proj.shape[0]
    T = _tile(S_, 512)

    def body(of_ref, ob_ref, r_ref, g_ref, y_ref):
        o = of_ref[...] + ob_ref[...]
        for h, (ohat, _) in enumerate(_head_norm_fwd(o, GLA_DV)):
            vs = slice(h * GLA_DV, (h + 1) * GLA_DV)
            r = r_ref[:, vs].astype(F32)
            y_ref[:, vs] = (ohat * g_ref[:, vs] * (r / (1.0 + jnp.exp(-r)))).astype(BF16)

    blk = pl.BlockSpec((T, V_W), lambda i: (i, 0))
    return pl.pallas_call(
        body, name=name, grid=(S_ // T,),
        in_specs=[blk, blk, pl.BlockSpec((T, V_W), lambda i: (i, C_GR // V_W)), pl.BlockSpec((1, V_W), lambda i: (0, 0))],
        out_specs=blk, out_shape=SDS((S_, V_W), BF16), compiler_params=_cp("parallel"))(o_f, o_b, proj, gain)


def gla_post_bwd(dmix, o_f, o_b, proj, gain, *, name):
    S_ = proj.shape[0]
    T = _tile(S_, 512)

    def body(dy_ref, of_ref, ob_ref, r_ref, g_ref, do_ref, dr_ref, gg_ref):
        @pl.when(pl.program_id(0) == 0)
        def _():
            gg_ref[...] = jnp.zeros_like(gg_ref)
        o = of_ref[...] + ob_ref[...]
        for h, (ohat, rs) in enumerate(_head_norm_fwd(o, GLA_DV)):
            vs = slice(h * GLA_DV, (h + 1) * GLA_DV)
            r = r_ref[:, vs].astype(F32)
            dy = dy_ref[:, vs].astype(F32)
            sg = 1.0 / (1.0 + jnp.exp(-r))
            gate = r * sg
            n = ohat * g_ref[:, vs]
            dr_ref[:, vs] = (dy * n * (sg * (1.0 + r * (1.0 - sg)))).astype(BF16)
            dn = dy * gate
            u = dn * g_ref[:, vs]
            do_ref[:, vs] = (rs * (u - ohat * jnp.mean(u * ohat, axis=-1, keepdims=True))).astype(BF16)
            gg_ref[:, vs] += jnp.sum(dn * ohat, axis=0, keepdims=True)

    blk = pl.BlockSpec((T, V_W), lambda i: (i, 0))
    vec = pl.BlockSpec((1, V_W), lambda i: (0, 0))
    return pl.pallas_call(
        body, name=name, grid=(S_ // T,),
        in_specs=[blk, blk, blk, pl.BlockSpec((T, V_W), lambda i: (i, C_GR // V_W)), vec],
        out_specs=[blk, blk, vec], out_shape=[SDS((S_, V_W), BF16), SDS((S_, V_W), BF16), SDS((1, V_W), F32)],
        compiler_params=_cp("arbitrary"))(dmix, o_f, o_b, proj, gain)


def _t5_bucket_np(rel):
    half = REL_BUCKETS // 2
    max_exact = half // 2
    ret = np.where(rel > 0, half, 0)
    n = np.abs(rel)
    nf = np.maximum(n, 1).astype(np.float32)
    large = max_exact + (np.log(nf / np.float32(max_exact)) / np.float32(math.log(REL_MAX_DISTANCE / max_exact))
                         * np.float32(half - max_exact)).astype(np.int32)
    large = np.minimum(large, half - 1)
    return ret + np.where(n < max_exact, n, large)


def _rel_onehot(dilation):
    rel = np.arange(-DIL_W, DIL_W + 1)
    return np.eye(REL_BUCKETS, dtype=np.float32)[_t5_bucket_np(rel * dilation)]


def _toeplitz(vec, tq, flip):
    H = vec.shape[0]
    wlen = tq + 2 * DIL_W
    lu = wlen + tq
    if flip:
        vec = vec[:, ::-1]
    u = jnp.full((H, lu), NEG_INF, F32)
    u = lax.dynamic_update_slice(u, vec, (0, 0))
    t = jnp.tile(u, (1, tq))[:, :tq * (lu - 1)].reshape(H, tq, lu - 1)
    return t[:, :, :wlen]


def _untoeplitz(ds, tq):
    H = ds.shape[0]
    wlen = tq + 2 * DIL_W
    lu = wlen + tq
    p = jnp.pad(ds, ((0, 0), (0, 0), (0, lu - 1 - wlen))).reshape(H, tq * (lu - 1))
    p = jnp.pad(p, ((0, 0), (0, tq))).reshape(H, tq, lu)
    return jnp.sum(p, axis=1)[:, :2 * DIL_W + 1]


DIL_TQ = 1024
DIL_SUB = 128


def _dil_band(tq):
    sb = min(tq, DIL_SUB)
    return sb, sb + 2 * DIL_W


def _dil_window(p_ref, o_ref, n_ref, tq):
    return jnp.concatenate([p_ref[tq - DIL_W:, :], o_ref[...], n_ref[:DIL_W, :]], axis=0)


def _dil_specs(tq, nt, col0, ncols_per_r):
    cb = lambda h, r: r * ncols_per_r + col0 + h
    own = pl.BlockSpec((tq, LANES), lambda h, r, i: (i, cb(h, r)))
    prev = pl.BlockSpec((tq, LANES), lambda h, r, i: (jnp.maximum(i - 1, 0), cb(h, r)))
    nxt = pl.BlockSpec((tq, LANES), lambda h, r, i: (jnp.minimum(i + 1, nt - 1), cb(h, r)))
    return prev, own, nxt


def dil_fwd(qkv, bias, *, d, name, comm=None):
    l = qkv.shape[0]
    tq = _tile(l, DIL_TQ)
    nt = l // tq
    H = DIL_HEADS
    scale = DIL_HEAD_DIM ** -0.5

    sb, bw = _dil_band(tq)

    def body(q_ref, kp, ko, kn, vp, vo, vn, b_ref, o_ref, l_ref):
        i = pl.program_id(2)
        kw = _dil_window(kp, ko, kn, tq)
        vw = _dil_window(vp, vo, vn, tq)
        col = lax.broadcasted_iota(jnp.int32, (sb, bw), 1)
        for j in range(tq // sb):
            rows, win = slice(j * sb, (j + 1) * sb), slice(j * sb, j * sb + bw)
            s = _nt(q_ref[rows, :], kw[win]) * scale + b_ref[0]
            kpos = i * tq + j * sb - DIL_W + col
            s = jnp.where((kpos >= 0) & (kpos < l), s, NEG_INF)
            m = jnp.max(s, axis=-1, keepdims=True)
            p = jnp.exp(s - m)
            den = jnp.sum(p, axis=-1, keepdims=True)
            o_ref[rows, :] = (_nn(p.astype(BF16), vw[win]) / den).astype(BF16)
            l_ref[rows, :] = jnp.broadcast_to(m + jnp.log(den), (sb, LANES))

    _, q_spec, _ = _dil_specs(tq, nt, 0, 12)
    out_spec = pl.BlockSpec((tq, LANES), lambda h, r, i: (i, r * H + h))
    return _call(
        body, name=name, grid=(H, d, nt),
        in_specs=[q_spec, *_dil_specs(tq, nt, 4, 12), *_dil_specs(tq, nt, 8, 12),
                  pl.BlockSpec((1, sb, bw), lambda h, r, i: (h, 0, 0))],
        out_specs=[out_spec, out_spec], out_shape=[SDS((l, d * DIL_WIDTH), BF16), SDS((l, d * DIL_WIDTH), F32)],
        sem=("parallel", "parallel", "arbitrary"), args=(qkv, qkv, qkv, qkv, qkv, qkv, qkv, bias), comm=comm)


def dil_bwd_q(qkv, bias, do, lse, corr, *, d, name):
    l = qkv.shape[0]
    tq = _tile(l, DIL_TQ)
    nt = l // tq
    H = DIL_HEADS
    scale = DIL_HEAD_DIM ** -0.5

    sb, bw = _dil_band(tq)

    def body(q_ref, kp, ko, kn, vp, vo, vn, b_ref, do_ref, l_ref, c_ref, dq_ref, ds_ref):
        r, i = pl.program_id(1), pl.program_id(2)
        kw = _dil_window(kp, ko, kn, tq)
        vw = _dil_window(vp, vo, vn, tq)
        col = lax.broadcasted_iota(jnp.int32, (sb, bw), 1)
        ds_sum = jnp.zeros((sb, bw), F32)
        for j in range(tq // sb):
            rows, win = slice(j * sb, (j + 1) * sb), slice(j * sb, j * sb + bw)
            s = _nt(q_ref[rows, :], kw[win]) * scale + b_ref[0]
            kpos = i * tq + j * sb - DIL_W + col
            p = jnp.where((kpos >= 0) & (kpos < l), jnp.exp(s - l_ref[rows, 0:1]), 0.0)
            ds = p * (_nt(do_ref[rows, :], vw[win]) - c_ref[rows, 0:1])
            dq_ref[rows, :] = (_nn(ds.astype(BF16), kw[win]) * scale).astype(BF16)
            ds_sum = ds_sum + ds

        @pl.when((r == 0) & (i == 0))
        def _():
            ds_ref[0] = ds_sum

        @pl.when((r > 0) | (i > 0))
        def _():
            ds_ref[0] += ds_sum

    _, q_spec, _ = _dil_specs(tq, nt, 0, 12)
    t_spec = pl.BlockSpec((tq, LANES), lambda h, r, i: (i, r * H + h))
    b_spec = pl.BlockSpec((1, sb, bw), lambda h, r, i: (h, 0, 0))
    return pl.pallas_call(
        body, name=name, grid=(H, d, nt),
        in_specs=[q_spec, *_dil_specs(tq, nt, 4, 12), *_dil_specs(tq, nt, 8, 12), b_spec, t_spec, t_spec, t_spec],
        out_specs=[t_spec, b_spec], out_shape=[SDS((l, d * DIL_WIDTH), BF16), SDS((H, sb, bw), F32)],
        compiler_params=_cp("parallel", "arbitrary", "arbitrary"))(qkv, qkv, qkv, qkv, qkv, qkv, qkv, bias, do, lse, corr)


def dil_bwd_kv(qkv, bias_t, do, lse, corr, *, d, name):
    l = qkv.shape[0]
    tq = _tile(l, DIL_TQ)
    nt = l // tq
    H = DIL_HEADS
    scale = DIL_HEAD_DIM ** -0.5

    sb, bw = _dil_band(tq)

    def body(k_ref, v_ref, qp, qo, qn, dp, do_, dn, lp, lo, ln, cp, co, cn, b_ref, dk_ref, dv_ref):
        i = pl.program_id(2)
        qw = _dil_window(qp, qo, qn, tq)
        dow = _dil_window(dp, do_, dn, tq)
        lrow = _dil_window(lp, lo, ln, tq).T[0:1, :]
        crow = _dil_window(cp, co, cn, tq).T[0:1, :]
        col = lax.broadcasted_iota(jnp.int32, (sb, bw), 1)
        for j in range(tq // sb):
            rows, win = slice(j * sb, (j + 1) * sb), slice(j * sb, j * sb + bw)
            st = _nt(k_ref[rows, :], qw[win]) * scale + b_ref[0]
            qpos = i * tq + j * sb - DIL_W + col
            pt = jnp.where((qpos >= 0) & (qpos < l), jnp.exp(st - lrow[:, win]), 0.0)
            dv_ref[rows, :] = _nn(pt.astype(BF16), dow[win]).astype(BF16)
            dst = pt * (_nt(v_ref[rows, :], dow[win]) - crow[:, win])
            dk_ref[rows, :] = (_nn(dst.astype(BF16), qw[win]) * scale).astype(BF16)

    _, k_spec, _ = _dil_specs(tq, nt, 4, 12)
    _, v_spec, _ = _dil_specs(tq, nt, 8, 12)
    t_spec = pl.BlockSpec((tq, LANES), lambda h, r, i: (i, r * H + h))
    return pl.pallas_call(
        body, name=name, grid=(H, d, nt),
        in_specs=[k_spec, v_spec, *_dil_specs(tq, nt, 0, 12), *_dil_specs(tq, nt, 0, 4), *_dil_specs(tq, nt, 0, 4),
                  *_dil_specs(tq, nt, 0, 4), pl.BlockSpec((1, sb, bw), lambda h, r, i: (h, 0, 0))],
        out_specs=[t_spec, t_spec], out_shape=[SDS((l, d * DIL_WIDTH), BF16), SDS((l, d * DIL_WIDTH), BF16)],
        compiler_params=_cp("parallel", "parallel", "arbitrary"))(
            qkv, qkv, qkv, qkv, qkv, do, do, do, lse, lse, lse, corr, corr, corr, bias_t)


def _dil_weights(l1, l2, l3):
    m = jnp.maximum(jnp.maximum(l1, l2), l3)
    e = [jnp.exp(x - m) for x in (l1, l2, l3)]
    tot = e[0] + e[1] + e[2]
    return [x / tot for x in e]


def dil_combine(os_, ls_, gain, *, name):
    S_ = os_[0].shape[0]
    T = _tile(S_, 512)

    def body(o1, o2, o3, l1, l2, l3, g_ref, y_ref):
        w = _dil_weights(l1[...], l2[...], l3[...])
        o = w[0] * o1[...] + w[1] * o2[...] + w[2] * o3[...]
        for h, (ohat, _) in enumerate(_head_norm_fwd(o, DIL_HEAD_DIM)):
            cs = slice(h * DIL_HEAD_DIM, (h + 1) * DIL_HEAD_DIM)
            y_ref[:, cs] = (ohat * g_ref[:, cs]).astype(BF16)

    blk = pl.BlockSpec((T, DIL_WIDTH), lambda i: (i, 0))
    return pl.pallas_call(
        body, name=name, grid=(S_ // T,), in_specs=[blk] * 6 + [pl.BlockSpec((1, DIL_WIDTH), lambda i: (0, 0))],
        out_specs=blk, out_shape=SDS((S_, DIL_WIDTH), BF16), compiler_params=_cp("parallel"))(*os_, *ls_, gain)


def dil_combine_bwd(dmix, os_, ls_, gain, *, name):
    S_ = os_[0].shape[0]
    T = _tile(S_, 512)
    E = DIL_HEAD_DIM

    def body(dy_ref, o1, o2, o3, l1, l2, l3, g_ref, d1, d2, d3, c1, c2, c3, gg_ref):
        @pl.when(pl.program_id(0) == 0)
        def _():
            gg_ref[...] = jnp.zeros_like(gg_ref)
        w = _dil_weights(l1[...], l2[...], l3[...])
        o = w[0] * o1[...] + w[1] * o2[...] + w[2] * o3[...]
        for h, (ohat, rs) in enumerate(_head_norm_fwd(o, E)):
            cs = slice(h * E, (h + 1) * E)
            dy = dy_ref[:, cs].astype(F32)
            u = dy * g_ref[:, cs]
            do = rs * (u - ohat * jnp.mean(u * ohat, axis=-1, keepdims=True))
            gg_ref[:, cs] += jnp.sum(dy * ohat, axis=0, keepdims=True)
            tot = jnp.sum(do * o[:, cs], axis=-1, keepdims=True)
            for wr, d_ref, c_ref in zip(w, (d1, d2, d3), (c1, c2, c3)):
                d_ref[:, cs] = (wr[:, cs] * do).astype(BF16)
                c_ref[:, cs] = wr[:, cs] * tot

    blk = pl.BlockSpec((T, DIL_WIDTH), lambda i: (i, 0))
    vec = pl.BlockSpec((1, DIL_WIDTH), lambda i: (0, 0))
    return pl.pallas_call(
        body, name=name, grid=(S_ // T,),
        in_specs=[pl.BlockSpec((T, DIL_WIDTH), lambda i: (i, V_W // DIL_WIDTH))] + [blk] * 6 + [vec],
        out_specs=[blk] * 6 + [vec],
        out_shape=[SDS((S_, DIL_WIDTH), BF16)] * 3 + [SDS((S_, DIL_WIDTH), F32)] * 3 + [SDS((1, DIL_WIDTH), F32)],
        compiler_params=_cp("arbitrary"))(dmix, *os_, *ls_, gain)


def _mem_softmax(q, k, scale):
    s = _nt(q, k) * scale
    e = jnp.exp(s - jnp.max(s, axis=-1, keepdims=True))
    return e / jnp.sum(e, axis=-1, keepdims=True)


def mem_fwd(proj, kv, gain, *, name):
    S_ = proj.shape[0]
    M = kv.shape[0]
    T = _tile(S_, 1024)
    E, H = MEM_HEAD_DIM, MEM_HEADS
    scale = E ** -0.5

    def body(q_ref, k_ref, v_ref, g_ref, y_ref):
        p = _mem_softmax(q_ref[...], k_ref[...], scale)
        o = _nn(p.astype(BF16), v_ref[...])
        rs = lax.rsqrt(jnp.mean(o * o, axis=-1, keepdims=True) + EPS)
        y_ref[...] = (o * rs * g_ref[...]).astype(BF16)

    return pl.pallas_call(
        body, name=name, grid=(H, S_ // T),
        in_specs=[pl.BlockSpec((T, E), lambda h, i: (i, C_MQ // E + h)), pl.BlockSpec((M, E), lambda h, i: (0, h)),
                  pl.BlockSpec((M, E), lambda h, i: (0, H + h)), pl.BlockSpec((1, E), lambda h, i: (0, h))],
        out_specs=pl.BlockSpec((T, E), lambda h, i: (i, h)), out_shape=SDS((S_, MEM_WIDTH), BF16),
        compiler_params=_cp("parallel", "parallel"))(proj, kv, kv, gain)


def mem_bwd(dmix, proj, kv, gain, *, name):
    S_ = proj.shape[0]
    M = kv.shape[0]
    T = _tile(S_, 1024)
    E, H = MEM_HEAD_DIM, MEM_HEADS
    scale = E ** -0.5

    def body(dy_ref, q_ref, k_ref, v_ref, g_ref, dq_ref, dk_ref, dv_ref, gg_ref):
        @pl.when(pl.program_id(1) == 0)
        def _():
            dk_ref[...] = jnp.zeros_like(dk_ref)
            dv_ref[...] = jnp.zeros_like(dv_ref)
            gg_ref[...] = jnp.zeros_like(gg_ref)
        q, k, v = q_ref[...], k_ref[...], v_ref[...]
        p = _mem_softmax(q, k, scale)
        pb = p.astype(BF16)
        o = _nn(pb, v)
        rs = lax.rsqrt(jnp.mean(o * o, axis=-1, keepdims=True) + EPS)
        ohat = o * rs
        dy = dy_ref[...].astype(F32)
        u = dy * g_ref[...]
        do = (rs * (u - ohat * jnp.mean(u * ohat, axis=-1, keepdims=True))).astype(BF16)
        gg_ref[...] += jnp.sum(dy * ohat, axis=0, keepdims=True)
        dv_ref[...] += _tn(pb, do)
        dp = _nt(do, v)
        ds = (p * (dp - jnp.sum(p * dp, axis=-1, keepdims=True))).astype(BF16)
        dq_ref[...] = (_nn(ds, k) * scale).astype(BF16)
        dk_ref[...] += _tn(ds, q) * scale

    return pl.pallas_call(
        body, name=name, grid=(H, S_ // T),
        in_specs=[pl.BlockSpec((T, E), lambda h, i: (i, (V_W + DIL_WIDTH) // E + h)),
                  pl.BlockSpec((T, E), lambda h, i: (i, C_MQ // E + h)), pl.BlockSpec((M, E), lambda h, i: (0, h)),
                  pl.BlockSpec((M, E), lambda h, i: (0, H + h)), pl.BlockSpec((1, E), lambda h, i: (0, h))],
        out_specs=[pl.BlockSpec((T, E), lambda h, i: (i, h)), pl.BlockSpec((M, E), lambda h, i: (0, h)),
                   pl.BlockSpec((M, E), lambda h, i: (0, h)), pl.BlockSpec((1, E), lambda h, i: (0, h))],
        out_shape=[SDS((S_, MEM_WIDTH), BF16), SDS((M, MEM_WIDTH), F32), SDS((M, MEM_WIDTH), F32),
                   SDS((1, MEM_WIDTH), F32)],
        compiler_params=_cp("parallel", "arbitrary"))(dmix, proj, kv, kv, gain)


def _strided(a, d):
    return a if d == 1 else a.reshape(a.shape[0] // d, d * a.shape[1])


def _unstrided(a, d):
    return a if d == 1 else a.reshape(a.shape[0] * d, a.shape[1] // d)


def _dil_tables(rel_bias, seq):
    tabs = []
    for d in DIL_DILATIONS:
        sb, _ = _dil_band(min(seq // d, DIL_TQ))
        vec = jnp.sum(jnp.asarray(_rel_onehot(d))[:, :, None] * rel_bias[None], axis=1).T
        tabs.append((_toeplitz(vec, sb, False), _toeplitz(vec, sb, True)))
    return tabs


def _rel_bias_grad(ds_sums, seq):
    g = jnp.zeros((REL_BUCKETS, DIL_HEADS), F32)
    for d, ds in zip(DIL_DILATIONS, ds_sums):
        gvec = _untoeplitz(ds, _dil_band(min(seq // d, DIL_TQ))[0])
        g = g + jnp.sum(jnp.asarray(_rel_onehot(d))[:, :, None] * gvec.T[:, None, :], axis=0)
    return g


def _pad_gate_up(up, row0):
    return jnp.zeros((LANES, QK_W), BF16).at[row0:row0 + GLA_GATE_RANK].set(up.astype(BF16))


def _layer_fwd(x, mem, p, tabs, l, carry):
    tag = f"l{l}"
    proj, hn1 = norm_mm(x, p["norm_mix"], p["w_in"], tm=TM_FWD, tn=IN_TILE, name=f"in_proj_{tag}", comm=carry.get("in"))
    o_f, st_f = gla_fwd(proj, p["up_f"], p["bias_f"], reverse=False, name=f"gla_fwd_f_{tag}", comm=carry.get("gla_f"))
    o_b, st_b = gla_fwd(proj, p["up_b"], p["bias_b"], reverse=True, name=f"gla_fwd_b_{tag}", comm=carry.get("gla_b"))
    gla_out = gla_post(o_f, o_b, proj, p["gla_norm"], name=f"gla_post_{tag}")
    qkv = proj[:, C_DQ:C_MQ]
    os_, ls_, ls_strided = [], [], []
    for d, (bias, _) in zip(DIL_DILATIONS, tabs):
        o, lse = dil_fwd(_strided(qkv, d), bias, d=d, name=f"dil_fwd_d{d}_{tag}", comm=carry.get(f"dil{d}"))
        os_.append(_unstrided(o, d))
        ls_.append(_unstrided(lse, d))
        ls_strided.append(lse)
    dil_out = dil_combine(os_, ls_, p["dil_norm"], name=f"dil_combine_{tag}")
    kv, hmem = norm_mm(mem, p["mem_norm"], p["w_mem_kv"], tm=256, tn=1024, name=f"mem_kv_{tag}")
    mem_out = mem_fwd(proj, kv, p["mem_out_norm"], name=f"mem_fwd_{tag}")
    mixed = jnp.concatenate([gla_out, dil_out, mem_out], axis=1)
    x1 = mm_res(mixed, p["w_out"], x, relu2=False, tm=TM_FWD, tn=TN, tk=TK, name=f"out_proj_{tag}",
                comm=carry.get("out"))
    a, hn2 = norm_mm(x1, p["norm_mlp"], p["w_up"], tm=TM_FWD, tn=TN, name=f"up_proj_{tag}", comm=carry.get("up"))
    x2 = mm_res(a, p["w_down"], x1, relu2=True, tm=TM_FWD, tn=TN, tk=TK, name=f"down_proj_{tag}",
                comm=carry.get("down"))
    saved = dict(x=x, proj=proj, hn1=hn1, o_f=o_f, o_b=o_b, st_f=st_f, st_b=st_b, qkv=qkv, os=os_, ls=ls_,
                 ls_strided=ls_strided, kv=kv, hmem=hmem, mixed=mixed, x1=x1, a=a, hn2=hn2)
    return x2, saved


class GradScatter:
    SLOTS = {"down_x": [(1, "w_in", 0, 1)], "down_w": [(1, "w_out", 0, 1), (1, "w_mem_kv", 0, 1)],
             "up_w": [(0, "w_down", 0, 2)], "up_x": [(0, "w_down", 1, 2)],
             "in_w": [(0, "w_up", 0, 2)], "in_x": [(0, "w_up", 1, 2)]}
    LAYER0 = {"in_w": [(0, "w_out", 0, 1), (0, "w_mem_kv", 0, 1)], "in_x": [(0, "w_in", 0, 1)]}

    def __init__(self, exchange):
        self.exchange = exchange
        self.ready = {}
        self.taken = set()
        self.sent = []

    def offer(self, l, name, g):
        self.ready[(l, name)] = g

    def _take(self, keys):
        keys = [k for k in keys if k[:2] in self.ready and k not in self.taken]
        if not (keys and self.exchange):
            return None
        bufs = [self.ready[k[:2]] for k in keys]
        rows = [(k[2] * (b.shape[1] // k[3]), b.shape[1] // k[3]) for k, b in zip(keys, bufs)]
        ex = ChipExchange(bufs, scatter=True, rows=rows)
        self.taken.update(keys)
        self.sent.append((keys, ex))
        return ex

    def carrier(self, l, slot):
        plan = self.SLOTS[slot] + (self.LAYER0.get(slot, []) if l == 0 else [])
        return self._take([(l + dl, n, i, m) for dl, n, i, m in plan])

    def flush(self, name):
        pieces = {k[:2]: max([t[3] for t in self.taken if t[:2] == k[:2]] + [1]) for k in self.ready}
        ex = self._take(sorted((l, n, i, m) for (l, n), m in pieces.items() for i in range(m)))
        if ex is not None:
            ex.run(name)

    def received(self):
        got = {}
        for keys, ex in self.sent:
            for k, r in zip(keys, ex.result):
                got.setdefault(k[:2], {})[k[2]] = r
        return {k: [v[i] for i in sorted(v)] for k, v in got.items()}


def _quarters(g):
    return g.reshape(N_CHIPS, g.shape[0] // N_CHIPS, g.shape[1])


def _layer_bwd(dx2, dx2b, mem, p, tabs, s, l, sink):
    tag = f"l{l}"
    seq = dx2.shape[0]
    g = {}
    da = mm_nt(dx2b, p["w_down"], s["a"], tm=TM_NT, tn=TN, name=f"down_bwd_x_{tag}", comm=sink.carrier(l, "down_x"))
    g_down = mm_tn(s["a"], dx2b, relu2=True, tm=TM_W, tn=TN_W, tk=TK_W, out="rows", name=f"down_bwd_w_{tag}",
                   comm=sink.carrier(l, "down_w"))
    sink.offer(l, "w_down", _quarters(g_down))
    g_up = mm_tn(s["hn2"], da, relu2=False, tm=TM_W, tn=TN_W, tk=TK_W, out="cols", name=f"up_bwd_w_{tag}",
                 comm=sink.carrier(l, "up_w"))
    sink.offer(l, "w_up", g_up)
    dx1, dx1b, g["norm_mlp"] = mm_nt_normbwd(da, p["w_up"], s["x1"], p["norm_mlp"], dx2, tm=TM_NORM_BWD, tk=TK_NORM_BWD,
                                       name=f"up_bwd_x_{tag}", comm=sink.carrier(l, "up_x"))
    dmix = mm_nt(dx1b, p["w_out"], None, tm=TM_NT, tn=TN, name=f"out_bwd_x_{tag}")
    sink.offer(l, "w_out", _quarters(mm_tn(s["mixed"], dx1b, relu2=False, tm=TM_W, tn=TN_W, tk=TK_W, out="rows",
                                           name=f"out_bwd_w_{tag}")))
    proj = s["proj"]
    do, dr, g["gla_norm"] = gla_post_bwd(dmix, s["o_f"], s["o_b"], proj, p["gla_norm"], name=f"gla_post_bwd_{tag}")
    r_f = gla_bwd(proj, p["up_f"], p["bias_f"], s["st_f"], do, None, reverse=False, name=f"gla_bwd_f_{tag}")
    r_b = gla_bwd(proj, p["up_b"], p["bias_b"], s["st_b"], do, r_f[:4], reverse=True, name=f"gla_bwd_b_{tag}")
    dgq, dgk, dgv, dlr = r_b[:4]
    g["up_f"], g["bias_f"] = r_f[4][0:GLA_GATE_RANK], r_f[5]
    g["up_b"], g["bias_b"] = r_b[4][GLA_GATE_RANK:2 * GLA_GATE_RANK], r_b[5]
    *branch, g["dil_norm"] = dil_combine_bwd(dmix, s["os"], s["ls"], p["dil_norm"], name=f"dil_combine_bwd_{tag}")
    dos, corrs = branch[:3], branch[3:]
    ddq = ddk = ddv = None
    ds_sums = []
    for r, (d, (bias, bias_t)) in enumerate(zip(DIL_DILATIONS, tabs)):
        qkv_d = _strided(s["qkv"], d)
        do_d, lse_d, corr_d = _strided(dos[r], d), s["ls_strided"][r], _strided(corrs[r], d)
        dq_d, ds_sum = dil_bwd_q(qkv_d, bias, do_d, lse_d, corr_d, d=d, name=f"dil_bwd_q_d{d}_{tag}")
        dk_d, dv_d = dil_bwd_kv(qkv_d, bias_t, do_d, lse_d, corr_d, d=d, name=f"dil_bwd_kv_d{d}_{tag}")
        ds_sums.append(ds_sum)
        dq_d, dk_d, dv_d = [_unstrided(t, d).astype(F32) for t in (dq_d, dk_d, dv_d)]
        ddq, ddk, ddv = (dq_d, dk_d, dv_d) if ddq is None else (ddq + dq_d, ddk + dk_d, ddv + dv_d)
    g["rel_bias"] = _rel_bias_grad(ds_sums, seq)
    dmq, dmk, dmv, g["mem_out_norm"] = mem_bwd(dmix, proj, s["kv"], p["mem_out_norm"], name=f"mem_bwd_{tag}")
    dkv = jnp.concatenate([dmk, dmv], axis=1).astype(BF16)
    sink.offer(l, "w_mem_kv", _quarters(mm_tn(s["hmem"], dkv, relu2=False, tm=TM_W, tn=TN_W, tk=TK_W, out="rows",
                                              name=f"mem_kv_bwd_w_{tag}")))
    _, _, g["mem_norm"] = mm_nt_normbwd(dkv, p["w_mem_kv"], mem, p["mem_norm"], jnp.zeros_like(mem), tm=TM_NORM_BWD,
                                     tk=TK_NORM_BWD, name=f"mem_kv_bwd_x_{tag}")
    dproj = jnp.concatenate([dgq, dgk, dgv, dr, ddq.astype(BF16), ddk.astype(BF16), ddv.astype(BF16), dmq, dlr,
                             jnp.zeros((seq, IN_PAD - C_LR - LANES), BF16)], axis=1)
    g_in = mm_tn(s["hn1"], dproj, relu2=False, tm=TM_W, tn=IN_TILE, tk=TK_W, name=f"in_bwd_w_{tag}",
                 comm=sink.carrier(l, "in_w"))
    sink.offer(l, "w_in", _shard_w_in_grad(g_in))
    dx0, dx0b, g["norm_mix"] = mm_nt_normbwd(dproj, p["w_in"], s["x"], p["norm_mix"], dx1, tm=TM_NORM_BWD, tk=IN_TILE,
                                             name=f"in_bwd_x_{tag}", comm=sink.carrier(l, "in_x"))
    return dx0, dx0b, g


def _shard_w_in_grad(g):
    g = jnp.concatenate([g[:, :LR_COL], g[:, C_LR:C_LR + 2 * GLA_GATE_RANK], g[:, LR_COL:C_LR]], axis=1).astype(BF16)
    w = IN_WIDTH // N_CHIPS
    return jnp.stack([g[:, j * w:(j + 1) * w] for j in range(N_CHIPS)], axis=0)


def local_step(x, mem, target, depth, layer_weights, norm_final, rel_bias, fwd_carry, sink):
    tabs = _dil_tables(rel_bias, x.shape[0])
    saved, layers = [], []
    for l in range(depth):
        layers.append(layer_weights(l))
        x, s = _layer_fwd(x, mem, layers[l], tabs, l, fwd_carry(l))
        saved.append(s)
    dx, dxb, g_final, loss = loss_head(x, norm_final, target, tm=TM_LOSS, name="loss_head")
    grads = [None] * depth
    for l in reversed(range(depth)):
        dx, dxb, grads[l] = _layer_bwd(dx, dxb, mem, layers[l], tabs, saved[l], l, sink)
    return loss, dx, grads, g_final


def sibling_exchange(bufs, *, name):
    n = len(bufs)

    def body(*refs):
        ins, outs = refs[:n], refs[n:2 * n]
        send, recv = refs[2 * n:]
        x, y, c = _mesh_pos()
        cps = [pltpu.make_async_remote_copy(src_ref=ins[a], dst_ref=outs[a], send_sem=send.at[a], recv_sem=recv.at[a],
                                            device_id=(x, y, 1 - c), device_id_type=MESH) for a in range(n)]
        for cp in cps:
            cp.start()
        for cp in cps:
            cp.wait()

    return pl.pallas_call(
        body, name=name, in_specs=[HBM_SPEC] * n, out_specs=[HBM_SPEC] * n,
        out_shape=[SDS(b.shape, b.dtype) for b in bufs],
        scratch_shapes=[pltpu.SemaphoreType.DMA((n,)), pltpu.SemaphoreType.DMA((n,))])(*bufs)


def all_reduce_small(v, *, name):
    R = v.shape[0]

    def body(v_ref, o_ref, slots, send, recv):
        x, y, c = _mesh_pos()
        me = 4 * x + 2 * y + c
        flip = lambda p, f: 1 - p if f else p
        peers = [(flip(x, k & 4), flip(y, k & 2), flip(c, k & 1)) for k in range(1, N_DEV)]
        slots[pl.ds(me, 1)] = v_ref[...][None]
        cps = [pltpu.make_async_remote_copy(src_ref=v_ref, dst_ref=slots.at[me], send_sem=send.at[k], recv_sem=recv.at[k],
                                            device_id=peer, device_id_type=MESH) for k, peer in enumerate(peers)]
        for cp in cps:
            cp.start()
        for k, (px, py, pc) in enumerate(peers):
            pltpu.make_async_remote_copy(src_ref=v_ref, dst_ref=slots.at[4 * px + 2 * py + pc], send_sem=send.at[k],
                                         recv_sem=recv.at[k], device_id=(px, py, pc), device_id_type=MESH).wait_recv()
        for cp in cps:
            cp.wait_send()
        acc = slots[0]
        for s in range(1, N_DEV):
            acc = acc + slots[s]
        o_ref[...] = acc

    vm = pl.BlockSpec(memory_space=pltpu.VMEM)
    return pl.pallas_call(
        body, name=name, in_specs=[vm], out_specs=vm, out_shape=SDS(v.shape, F32),
        scratch_shapes=[pltpu.VMEM((N_DEV, R, LANES), F32), pltpu.SemaphoreType.DMA((N_DEV - 1,)),
                        pltpu.SemaphoreType.DMA((N_DEV - 1,))])(v)


def sum_chips(recv, *, name):
    _, R, C = recv.shape
    tr = _tile(R, 512)

    def body(r_ref, o_ref):
        acc = r_ref[0].astype(F32)
        for j in range(1, N_CHIPS):
            acc = acc + r_ref[j].astype(F32)
        o_ref[...] = acc.astype(BF16)

    return pl.pallas_call(
        body, name=name, grid=(R // tr,), in_specs=[pl.BlockSpec((N_CHIPS, tr, C), lambda i: (0, i, 0))],
        out_specs=pl.BlockSpec((tr, C), lambda i: (i, 0)), out_shape=SDS((R, C), BF16),
        compiler_params=_cp("parallel"))(recv)


def adamw(parts, w, m, v, *, name):
    R, C = w.shape
    n = len(parts)
    tr = _tile(R, 512)
    while 2 * (n + 7) * tr * C * 4 > ELEMWISE_BLOCK_BYTES and tr % 16 == 0:
        tr //= 2
    c1 = 1.0 - ADAM_B1 ** ADAM_STEP
    c2 = 1.0 - ADAM_B2 ** ADAM_STEP

    def body(*refs):
        w_ref, m_ref, v_ref = refs[n:n + 3]
        g_ref, d_ref, nm_ref, nv_ref = refs[n + 3:]
        g = refs[0][...].astype(F32)
        for r in refs[1:n]:
            g = g + r[...].astype(F32)
        nm = ADAM_B1 * m_ref[...] + (1.0 - ADAM_B1) * g
        nv = ADAM_B2 * v_ref[...] + (1.0 - ADAM_B2) * (g * g)
        g_ref[...] = g
        nm_ref[...] = nm
        nv_ref[...] = nv
        d_ref[...] = -ADAM_LR * ((nm / c1) / (jnp.sqrt(nv / c2) + ADAM_EPS) + ADAM_WD * w_ref[...])

    blk = pl.BlockSpec((tr, C), lambda i: (i, 0))
    return pl.pallas_call(
        body, name=name, grid=(R // tr,), in_specs=[blk] * (n + 3), out_specs=[blk] * 4,
        out_shape=[SDS((R, C), F32)] * 4, compiler_params=_cp("parallel"))(*parts, w, m, v)


PACK_ROWS = 8


def _pack(parts):
    rows = []
    for a in parts:
        flat = a.reshape(-1).astype(F32)
        n = -(-flat.shape[0] // (PACK_ROWS * LANES)) * PACK_ROWS * LANES
        rows.append(jnp.pad(flat, (0, n - flat.shape[0])).reshape(-1, LANES))
    return jnp.concatenate(rows, axis=0)


def _unpack(slab, shapes):
    out, r = [], 0
    for shp in shapes:
        n = int(np.prod(shp))
        nr = -(-n // (PACK_ROWS * LANES)) * PACK_ROWS
        out.append(slab[r:r + nr].reshape(-1)[:n].reshape(shp))
        r += nr
    return out


SMALL = ["norm_mix", "gla_gate_bias_fwd", "gla_gate_bias_bwd", "gla_norm", "rel_bias", "dil_norm", "mem_norm",
         "mem_out_norm", "norm_mlp", "norm_final"]
GATE_UPS = ["gla_gate_up_fwd", "gla_gate_up_bwd"]
BIG = ["w_in", "w_mem_kv", "w_out", "w_up", "w_down"]
WEIGHTS = ["norm_mix", "w_in", "gla_gate_up_fwd", "gla_gate_bias_fwd", "gla_gate_up_bwd", "gla_gate_bias_bwd", "gla_norm",
           "rel_bias", "dil_norm", "mem_norm", "w_mem_kv", "mem_out_norm", "w_out", "norm_mlp", "w_up", "w_down",
           "norm_final"]


FWD_CARRY = {"in": [(1, "w_in")], "out": [(1, "w_out"), (1, "w_mem_kv")], "up": [(1, "w_up")], "down": [(1, "w_down")]}
FWD_CARRY_LAYER0 = {"in": [("w_out", 0, 1), ("w_mem_kv", 0, 1)], "gla_f": [("w_up", 0, 1)], "gla_b": [("w_down", 0, 2)],
                    "dil16": [("w_down", 1, 2)]}


def _kernel_layout(name, g):
    if name == "w_up":
        return g
    if name == "w_in":
        full = jnp.concatenate([g[j] for j in range(N_CHIPS)], axis=1)
        return jnp.concatenate([full[:, :LR_COL], full[:, LR_COL + 2 * GLA_GATE_RANK:],
                                full[:, LR_COL:LR_COL + 2 * GLA_GATE_RANK],
                                jnp.zeros((full.shape[0], IN_PAD - IN_WIDTH), BF16)], axis=1)
    return g.reshape(-1, g.shape[-1])


def kernel(x, mem, norm_mix, w_in, gla_gate_up_fwd, gla_gate_bias_fwd, gla_gate_up_bwd, gla_gate_bias_bwd, gla_norm, rel_bias, dil_norm, mem_norm, w_mem_kv, mem_out_norm, w_out, norm_mlp, w_up, w_down, norm_final, loss_target, m_norm_mix, m_w_in, m_gla_gate_up_fwd, m_gla_gate_bias_fwd, m_gla_gate_up_bwd, m_gla_gate_bias_bwd, m_gla_norm, m_rel_bias, m_dil_norm, m_mem_norm, m_w_mem_kv, m_mem_out_norm, m_w_out, m_norm_mlp, m_w_up, m_w_down, m_norm_final, v_norm_mix, v_w_in, v_gla_gate_up_fwd, v_gla_gate_bias_fwd, v_gla_gate_up_bwd, v_gla_gate_bias_bwd, v_gla_norm, v_rel_bias, v_dil_norm, v_mem_norm, v_w_mem_kv, v_mem_out_norm, v_w_out, v_norm_mlp, v_w_up, v_w_down, v_norm_final):
    vals = dict(locals())
    W = {n: vals[n] for n in WEIGHTS}
    M1 = {n: vals["m_" + n] for n in WEIGHTS}
    V2 = {n: vals["v_" + n] for n in WEIGHTS}
    depth = w_in.shape[0]
    chip = 2 * lax.axis_index("x") + lax.axis_index("y")

    wb = {n: W[n].astype(BF16) for n in BIG + GATE_UPS}
    first = ChipExchange([wb[n] for n in ["w_in"] + GATE_UPS], scatter=False, layers=[0] + [None] * len(GATE_UPS))
    first.run("gather_first")
    gate_up = dict(zip(GATE_UPS, first.result[1:]))
    pending = {(0, "w_in"): [(first, 0)]}
    row = lambda a: a.reshape(1, -1)

    def fwd_carry(l):
        plan = {slot: [(l + dl, n, 0, 1) for dl, n in what if l + dl < depth] for slot, what in FWD_CARRY.items()}
        if l == 0:
            for slot, what in FWD_CARRY_LAYER0.items():
                plan[slot] = plan.get(slot, []) + [(0, n, i, m) for n, i, m in what]
        carry = {}
        for slot, keys in plan.items():
            if keys:
                shard_rows = [wb[n].shape[1] for _, n, _, _ in keys]
                carry[slot] = ChipExchange([wb[n] for _, n, _, _ in keys], scatter=False, layers=[k for k, _, _, _ in keys],
                                           rows=[(i * (r // m), r // m) for (_, _, i, m), r in zip(keys, shard_rows)])
                for place, (k, n, i, m) in enumerate(keys):
                    pending.setdefault((k, n), [None] * m)[i] = (carry[slot], place)
        return carry

    class LayerWeights(dict):
        def __init__(self, l):
            super().__init__()
            self.l = l

        def __missing__(self, n):
            pieces = [ex.result[place] for ex, place in pending[(self.l, n)]]
            self[n] = _kernel_layout(n, pieces[0] if len(pieces) == 1 else jnp.concatenate(pieces, axis=1))
            return self[n]

    def layer_weights(l):
        p = LayerWeights(l)
        full_up = lambda n: jnp.concatenate([gate_up[n][j, l] for j in range(N_CHIPS)], axis=1)
        p["up_f"] = _pad_gate_up(full_up("gla_gate_up_fwd"), 0)
        p["up_b"] = _pad_gate_up(full_up("gla_gate_up_bwd"), GLA_GATE_RANK)
        p["bias_f"], p["bias_b"] = row(gla_gate_bias_fwd[l]), row(gla_gate_bias_bwd[l])
        for n in ("norm_mix", "gla_norm", "dil_norm", "mem_norm", "mem_out_norm", "norm_mlp"):
            p[n] = row(W[n][l])
        return p

    sink = GradScatter(exchange=True)
    loss, grad_x, grads, g_final = local_step(x[0], mem[0], loss_target[0], depth, layer_weights, row(norm_final),
                                              rel_bias, fwd_carry, sink)
    sink.flush("scatter_last_grads")
    recv = sink.received()

    flat2 = lambda a: a.reshape(-1, a.shape[-1])
    part = [jnp.concatenate([sum_chips(r, name=f"sum_chips_{n}_l{l}_p{i}") for l in range(depth)
                             for i, r in enumerate(recv[(l, n)])], axis=0) for n in BIG]
    other = sibling_exchange(part, name="swap_partial_sums")
    out = {}
    for n, pa, ob in zip(BIG, part, other):
        res = adamw([pa, ob], flat2(W[n]), flat2(M1[n]), flat2(V2[n]), name=f"adamw_{n}")
        out[n] = [r.reshape(W[n].shape) for r in res]

    small_g = {}
    for n, key in (("norm_mix", "norm_mix"), ("gla_gate_bias_fwd", "bias_f"), ("gla_gate_bias_bwd", "bias_b"),
                   ("gla_norm", "gla_norm"), ("dil_norm", "dil_norm"), ("mem_norm", "mem_norm"),
                   ("mem_out_norm", "mem_out_norm"), ("norm_mlp", "norm_mlp"),
                   ("gla_gate_up_fwd", "up_f"), ("gla_gate_up_bwd", "up_b")):
        small_g[n] = jnp.stack([grads[l][key].reshape(W[n].shape[1:] if n not in GATE_UPS else (GLA_GATE_RANK, QK_W))
                                for l in range(depth)], axis=0)
    small_g["rel_bias"] = sum(grads[l]["rel_bias"] for l in range(depth))
    small_g["norm_final"] = g_final.reshape(-1)
    names = SMALL + GATE_UPS
    slab = all_reduce_small(_pack([small_g[n] for n in names] + [loss[:, 0]]), name="all_reduce_small")
    *summed, loss_sum = _unpack(slab, [small_g[n].shape for n in names] + [(1,)])
    summed = dict(zip(names, summed))
    for n in GATE_UPS:
        summed[n] = lax.dynamic_slice_in_dim(summed[n], chip * LANES, LANES, axis=2)
    names_s = SMALL + GATE_UPS
    shapes = [W[n].shape for n in names_s]
    res = adamw([_pack([summed[n] for n in names_s])], _pack([W[n] for n in names_s]), _pack([M1[n] for n in names_s]),
                _pack([V2[n] for n in names_s]), name="adamw_small")
    for n, *r in zip(names_s, *[_unpack(t, shapes) for t in res]):
        out[n] = r

    return (loss_sum.reshape(()), grad_x[None], *[out[n][0] for n in WEIGHTS], *[out[n][1] for n in WEIGHTS],
            *[out[n][2] for n in WEIGHTS], *[out[n][3] for n in WEIGHTS])
```

```python
import functools
import math

import numpy as np
import jax
import jax.numpy as jnp
from jax import lax
from jax.experimental import pallas as pl
from jax.experimental.pallas import tpu as pltpu

F32, BF16 = jnp.float32, jnp.bfloat16
SDS = jax.ShapeDtypeStruct

DEPTH = 4
GLA_HEADS, GLA_DK, GLA_DV = 4, 128, 256
GLA_GATE_RANK = 16
GLA_GATE_NORMALIZER = 16.0
GLA_CHUNK = 64
DIL_HEADS, DIL_HEAD_DIM = 4, 128
DIL_DILATIONS = (1, 4, 16)
DIL_W = 64
MEM_HEADS, MEM_HEAD_DIM = 4, 128
REL_BUCKETS, REL_MAX_DISTANCE = 32, 1024
EPS = 1e-6
NEG_INF = -1e30
QK_W, V_W, DIL_WIDTH, MEM_WIDTH = 512, 1024, 512, 512
IN_WIDTH = 5152
LR_COL = 3072
IN_PAD = 5376
C_GQ, C_GK, C_GV, C_GR, C_DQ, C_MQ, C_LR = 0, 512, 1024, 2048, 3072, 4608, 5120
IN_TILE = 1792

ADAM_LR, ADAM_B1, ADAM_B2, ADAM_EPS, ADAM_WD, ADAM_STEP = 0.001, 0.9, 0.999, 1e-08, 0.01, 10

V7X_VMEM_BYTES = 64 * 1024 * 1024
VMEM_LIMIT = V7X_VMEM_BYTES * 7 // 8
LANES = 128

TM_FWD = 512
TM_NT = 1024
TN = 2048
TK = 2048
TM_NORM_BWD = 512
TK_NORM_BWD = 2048
NORM_BWD_ROWS = 64
TM_W, TN_W, TK_W = 1024, 1024, 2048
TM_LOSS = 256
ELEMWISE_BLOCK_BYTES = 24 * 1024 * 1024

NT_DIMS = (((1,), (1,)), ((), ()))
TN_DIMS = (((0,), (0,)), ((), ()))


def _nt(a, b):
    return lax.dot_general(a, b, NT_DIMS, preferred_element_type=F32)


def _tn(a, b):
    return lax.dot_general(a, b, TN_DIMS, preferred_element_type=F32)


def _nn(a, b):
    return jnp.dot(a, b, preferred_element_type=F32)


def _cp(*sem):
    return pltpu.CompilerParams(dimension_semantics=sem, vmem_limit_bytes=VMEM_LIMIT)


def _tile(n, t):
    t = min(n, t)
    assert n % t == 0, (n, t)
    return t


MESH = pl.DeviceIdType.MESH
HBM_SPEC = pl.BlockSpec(memory_space=pltpu.HBM)
N_CHIPS = 4
N_DEV = 8


def _mesh_pos():
    return lax.axis_index("x"), lax.axis_index("y"), lax.axis_index("c")


class ChipExchange:
    def __init__(self, bufs, *, scatter, layers=None, rows=None):
        self.bufs = list(bufs)
        self.n = n = len(self.bufs)
        self.scatter = scatter
        self.layers = list(layers) if layers is not None else [None] * n
        self.rows = list(rows) if rows is not None else [None] * n
        dma = pltpu.SemaphoreType.DMA
        if scatter:
            shp = [b.shape if r is None else (b.shape[0], r[1]) + b.shape[2:] for b, r in zip(self.bufs, self.rows)]
            self.out_shape = [SDS(s, b.dtype) for s, b in zip(shp, self.bufs)]
            self.scratch = [dma((3 * n,)), dma((3 * n,)), dma((n,))]
        else:
            shp = [b.shape if l is None else b.shape[1:] for b, l in zip(self.bufs, self.layers)]
            shp = [s if r is None else (r[1],) + s[1:] for s, r in zip(shp, self.rows)]
            assert all(s[0] % 2 == 0 for s in shp), shp
            self.out_shape = [SDS((N_CHIPS,) + s, b.dtype) for s, b in zip(shp, self.bufs)]
            self.scratch = [dma((3 * n,)), dma((3 * n,)), dma((n,)), dma((3 * n,)), dma((3 * n,))]
        self.result = None

    def _plan(self, ins, outs, sems, finishing):
        x, y, c = _mesh_pos()
        me = 2 * x + y
        chips = [(1 - x, y), (x, 1 - y), (1 - x, 1 - y)]
        remote = lambda src, dst, ss, rs, k, dev: pltpu.make_async_remote_copy(
            src_ref=src, dst_ref=dst, send_sem=ss.at[k], recv_sem=rs.at[k], device_id=dev, device_id_type=MESH)
        kept, sent, landing, passed, handed = [], [], [], [], []
        for a in range(self.n):
            src = ins[a] if self.layers[a] is None else ins[a].at[self.layers[a]]
            if self.scatter:
                r = self.rows[a]
                pick = (lambda j: src.at[j]) if r is None else (lambda j: src.at[j, pl.ds(r[0], r[1])])
                kept.append(pltpu.make_async_copy(pick(me), outs[a].at[me], sems[2].at[a]))
                for j, (px, py) in enumerate(chips):
                    k, peer = 3 * a + j, 2 * px + py
                    sent.append(remote(pick(peer), outs[a].at[me], sems[0], sems[1], k, (px, py, c)))
                    if finishing:
                        landing.append(remote(pick(me), outs[a].at[peer], sems[0], sems[1], k, (px, py, c)))
                continue
            nr = self.out_shape[a].shape[1]
            half = nr // 2
            first = 0 if self.rows[a] is None else self.rows[a][0]
            mine, other = pl.ds(c * half, half), pl.ds((1 - c) * half, half)
            src_mine = src.at[pl.ds(first + c * half, half)]
            kept.append(pltpu.make_async_copy(src.at[pl.ds(first, nr)], outs[a].at[me], sems[2].at[a]))
            for j, (px, py) in enumerate(chips):
                k, peer = 3 * a + j, 2 * px + py
                sent.append(remote(src_mine, outs[a].at[me, mine], sems[0], sems[1], k, (px, py, c)))
                if finishing:
                    landing.append(remote(src_mine, outs[a].at[peer, mine], sems[0], sems[1], k, (px, py, c)))
                    passed.append(remote(outs[a].at[peer, mine], outs[a].at[peer, mine], sems[3], sems[4], k, (x, y, 1 - c)))
                    handed.append(remote(outs[a].at[peer, other], outs[a].at[peer, other], sems[3], sems[4], k,
                                         (x, y, 1 - c)))
        return kept, sent, landing, passed, handed

    def start(self, ins, outs, sems):
        kept, sent, _, _, _ = self._plan(ins, outs, sems, False)
        for cp in kept + sent:
            cp.start()

    def finish(self, ins, outs, sems):
        kept, sent, landing, passed, handed = self._plan(ins, outs, sems, True)
        for k, cp in enumerate(landing):
            cp.wait_recv()
            if passed:
                passed[k].start()
        for cp in handed:
            cp.wait_recv()
        for cp in sent + passed:
            cp.wait_send()
        for cp in kept:
            cp.wait()

    def run(self, name):
        n = self.n

        def body(*refs):
            self.start(refs[:n], refs[n:2 * n], refs[2 * n:])
            self.finish(refs[:n], refs[n:2 * n], refs[2 * n:])

        self.result = pl.pallas_call(body, name=name, in_specs=[HBM_SPEC] * n, out_specs=[HBM_SPEC] * n,
                                     out_shape=self.out_shape, scratch_shapes=self.scratch)(*self.bufs)
        return self.result


def _call(body, *, name, grid, in_specs, out_specs, out_shape, sem, args, scratch_shapes=(), comm=None):
    if comm is None:
        return pl.pallas_call(body, name=name, grid=grid, in_specs=list(in_specs), out_specs=list(out_specs),
                              out_shape=list(out_shape), scratch_shapes=list(scratch_shapes),
                              compiler_params=_cp(*sem))(*args)
    n_in, n_out, n_scr, nc = len(in_specs), len(out_shape), len(scratch_shapes), comm.n

    def wrapped(*refs):
        ins, cin = refs[:n_in], refs[n_in:n_in + nc]
        outs, cout = refs[n_in + nc:n_in + nc + n_out], refs[n_in + nc + n_out:n_in + 2 * nc + n_out]
        scr, csem = refs[n_in + 2 * nc + n_out:n_in + 2 * nc + n_out + n_scr], refs[n_in + 2 * nc + n_out + n_scr:]
        ids = [pl.program_id(d) for d in range(len(grid))]
        first = functools.reduce(jnp.logical_and, [i == 0 for i in ids])
        last = functools.reduce(jnp.logical_and, [i == g - 1 for i, g in zip(ids, grid)])

        @pl.when(first)
        def _():
            comm.start(cin, cout, csem)
        body(*ins, *outs, *scr)

        @pl.when(last)
        def _():
            comm.finish(cin, cout, csem)

    res = pl.pallas_call(
        wrapped, name=name, grid=grid, in_specs=list(in_specs) + [HBM_SPEC] * nc,
        out_specs=list(out_specs) + [HBM_SPEC] * nc, out_shape=list(out_shape) + comm.out_shape,
        scratch_shapes=list(scratch_shapes) + comm.scratch,
        compiler_params=_cp(*("arbitrary",) * len(grid)))(*args, *comm.bufs)
    comm.result = res[n_out:]
    return res[:n_out]


def _w_spec(w, rows, tn, col_of):
    if w.ndim == 2:
        tn = _tile(w.shape[1], tn)
        return w.shape[1], tn, pl.BlockSpec((rows, tn), lambda *g: (0, col_of(*g)))
    tn = _tile(w.shape[2], tn)
    per = w.shape[2] // tn
    return N_CHIPS * w.shape[2], tn, pl.BlockSpec((None, rows, tn), lambda *g: (col_of(*g) // per, 0, col_of(*g) % per))


def norm_mm(x, gain, w, *, tm, tn, name, comm=None):
    M, D = x.shape
    tm = _tile(M, tm)
    N, tn, w_spec = _w_spec(w, D, tn, lambda i, j: j)

    def body(x_ref, g_ref, w_ref, o_ref, hn_ref):
        @pl.when(pl.program_id(1) == 0)
        def _():
            xf = x_ref[...]
            r = lax.rsqrt(jnp.mean(xf * xf, axis=-1, keepdims=True) + EPS)
            hn_ref[...] = (xf * r * g_ref[...]).astype(BF16)
        o_ref[...] = _nn(hn_ref[...], w_ref[...]).astype(BF16)

    return _call(
        body, name=name, grid=(M // tm, N // tn),
        in_specs=[pl.BlockSpec((tm, D), lambda i, j: (i, 0)), pl.BlockSpec((1, D), lambda i, j: (0, 0)),
                  w_spec],
        out_specs=[pl.BlockSpec((tm, tn), lambda i, j: (i, j)), pl.BlockSpec((tm, D), lambda i, j: (i, 0))],
        out_shape=[SDS((M, N), BF16), SDS((M, D), BF16)], sem=("parallel", "arbitrary"), args=(x, gain, w), comm=comm)


def mm_res(a, w, res, *, relu2, tm, tn, tk, name, comm=None):
    M, K = a.shape
    N = w.shape[1]
    tm, tn, tk = _tile(M, tm), _tile(N, tn), _tile(K, tk)

    def body(a_ref, w_ref, r_ref, o_ref):
        @pl.when(pl.program_id(2) == 0)
        def _():
            o_ref[...] = r_ref[...]
        av = a_ref[...]
        if relu2:
            af = jnp.maximum(av.astype(F32), 0.0)
            av = (af * af).astype(BF16)
        o_ref[...] += _nn(av, w_ref[...])

    return _call(
        body, name=name, grid=(M // tm, N // tn, K // tk),
        in_specs=[pl.BlockSpec((tm, tk), lambda i, j, k: (i, k)), pl.BlockSpec((tk, tn), lambda i, j, k: (k, j)),
                  pl.BlockSpec((tm, tn), lambda i, j, k: (i, j))],
        out_specs=[pl.BlockSpec((tm, tn), lambda i, j, k: (i, j))], out_shape=[SDS((M, N), F32)],
        sem=("parallel", "parallel", "arbitrary"), args=(a, w, res), comm=comm)[0]


def mm_nt(dy, w, a, *, tm, tn, name, comm=None):
    M, K = dy.shape
    N = w.shape[0]
    tm, tn = _tile(M, tm), _tile(N, tn)
    with_a = a is not None

    def body(*refs):
        if with_a:
            dy_ref, w_ref, a_ref, o_ref = refs
        else:
            dy_ref, w_ref, o_ref = refs
        r = _nt(dy_ref[...], w_ref[...])
        if with_a:
            r = r * (2.0 * jnp.maximum(a_ref[...].astype(F32), 0.0))
        o_ref[...] = r.astype(BF16)

    in_specs = [pl.BlockSpec((tm, K), lambda i, j: (i, 0)), pl.BlockSpec((tn, K), lambda i, j: (j, 0))]
    args = [dy, w]
    if with_a:
        in_specs.append(pl.BlockSpec((tm, tn), lambda i, j: (i, j)))
        args.append(a)
    return _call(
        body, name=name, grid=(M // tm, N // tn), in_specs=in_specs,
        out_specs=[pl.BlockSpec((tm, tn), lambda i, j: (i, j))], out_shape=[SDS((M, N), BF16)],
        sem=("parallel", "arbitrary"), args=args, comm=comm)[0]


def mm_nt_normbwd(dy, w, x, gain, dx_in, *, tm, tk, name, comm=None):
    M, K = dy.shape
    D = x.shape[1]
    tm = _tile(M, tm)
    _, tk, w_spec = _w_spec(w, D, tk, lambda i, k: k)
    nk = K // tk
    rc = _tile(tm, NORM_BWD_ROWS)

    def body(dy_ref, w_ref, x_ref, g_ref, dxin_ref, dx_ref, dxb_ref, gg_ref, acc):
        i, k = pl.program_id(0), pl.program_id(1)

        @pl.when(k == 0)
        def _():
            acc[...] = jnp.zeros_like(acc)
        acc[...] += _nt(dy_ref[...], w_ref[...])

        @pl.when(k == nk - 1)
        def _():
            def rows(c, part):
                rs = pl.ds(pl.multiple_of(c * rc, rc), rc)
                dh, xf = acc[rs, :], x_ref[rs, :]
                r = lax.rsqrt(jnp.mean(xf * xf, axis=-1, keepdims=True) + EPS)
                xhat = xf * r
                u = dh * g_ref[...]
                dx = dxin_ref[rs, :] + r * (u - xhat * jnp.mean(u * xhat, axis=-1, keepdims=True))
                dx_ref[rs, :] = dx
                dxb_ref[rs, :] = dx.astype(BF16)
                return part + jnp.sum(dh * xhat, axis=0, keepdims=True)

            part = lax.fori_loop(0, tm // rc, rows, jnp.zeros((1, D), F32))

            @pl.when(i == 0)
            def _():
                gg_ref[...] = part

            @pl.when(i > 0)
            def _():
                gg_ref[...] += part

    return _call(
        body, name=name, grid=(M // tm, nk),
        in_specs=[pl.BlockSpec((tm, tk), lambda i, k: (i, k)), w_spec,
                  pl.BlockSpec((tm, D), lambda i, k: (i, 0)), pl.BlockSpec((1, D), lambda i, k: (0, 0)),
                  pl.BlockSpec((tm, D), lambda i, k: (i, 0))],
        out_specs=[pl.BlockSpec((tm, D), lambda i, k: (i, 0)),
                   pl.BlockSpec((tm, D), lambda i, k: (i, 0)),
                   pl.BlockSpec((1, D), lambda i, k: (0, 0))],
        out_shape=[SDS((M, D), F32), SDS((M, D), BF16), SDS((1, D), F32)], scratch_shapes=[pltpu.VMEM((tm, D), F32)],
        sem=("arbitrary", "arbitrary"), args=(dy, w, x, gain, dx_in), comm=comm)


def mm_tn(a, dy, *, relu2, tm, tn, tk, name, out="f32", comm=None):
    S_, Ka = a.shape
    N = dy.shape[1]
    tm, tk = _tile(Ka, tm), _tile(S_, tk)
    tn = _tile(N // N_CHIPS if out == "cols" else N, tn)
    nk = S_ // tk

    def body(a_ref, dy_ref, o_ref, acc):
        k = pl.program_id(2)

        @pl.when(k == 0)
        def _():
            acc[...] = jnp.zeros_like(acc)
        av = a_ref[...]
        if relu2:
            af = jnp.maximum(av.astype(F32), 0.0)
            av = (af * af).astype(BF16)
        acc[...] += _tn(av, dy_ref[...].astype(BF16))

        @pl.when(k == nk - 1)
        def _():
            o_ref[...] = acc[...].astype(o_ref.dtype)

    if out == "cols":
        per = N // N_CHIPS // tn
        o_spec = pl.BlockSpec((None, tm, tn), lambda i, j, k: (j // per, i, j % per))
        o_shape = SDS((N_CHIPS, Ka, N // N_CHIPS), BF16)
    else:
        o_spec = pl.BlockSpec((tm, tn), lambda i, j, k: (i, j))
        o_shape = SDS((Ka, N), F32 if out == "f32" else BF16)
    return _call(
        body, name=name, grid=(Ka // tm, N // tn, nk),
        in_specs=[pl.BlockSpec((tk, tm), lambda i, j, k: (k, i)), pl.BlockSpec((tk, tn), lambda i, j, k: (k, j))],
        out_specs=[o_spec], out_shape=[o_shape], scratch_shapes=[pltpu.VMEM((tm, tn), F32)],
        sem=("parallel", "parallel", "arbitrary"), args=(a, dy), comm=comm)[0]


def loss_head(x, gain, target, *, tm, name):
    M, D = x.shape
    tm = _tile(M, tm)

    def body(x_ref, g_ref, t_ref, dx_ref, dxb_ref, gg_ref, l_ref):
        i = pl.program_id(0)
        xf = x_ref[...]
        r = lax.rsqrt(jnp.mean(xf * xf, axis=-1, keepdims=True) + EPS)
        xhat = xf * r
        e = xhat * g_ref[...] - t_ref[...]
        lpart = 0.5 * jnp.sum(jnp.mean(e * e, axis=-1, keepdims=True), axis=0, keepdims=True)
        dy = e * (1.0 / D)
        u = dy * g_ref[...]
        dx = r * (u - xhat * jnp.mean(u * xhat, axis=-1, keepdims=True))
        dx_ref[...] = dx
        dxb_ref[...] = dx.astype(BF16)
        gpart = jnp.sum(dy * xhat, axis=0, keepdims=True)
        lrow = jnp.broadcast_to(lpart, (1, LANES))

        @pl.when(i == 0)
        def _():
            gg_ref[...] = gpart
            l_ref[...] = lrow

        @pl.when(i > 0)
        def _():
            gg_ref[...] += gpart
            l_ref[...] += lrow

    return pl.pallas_call(
        body, name=name, grid=(M // tm,),
        in_specs=[pl.BlockSpec((tm, D), lambda i: (i, 0)), pl.BlockSpec((1, D), lambda i: (0, 0)),
                  pl.BlockSpec((tm, D), lambda i: (i, 0))],
        out_specs=[pl.BlockSpec((tm, D), lambda i: (i, 0)), pl.BlockSpec((tm, D), lambda i: (i, 0)),
                   pl.BlockSpec((1, D), lambda i: (0, 0)), pl.BlockSpec((1, LANES), lambda i: (0, 0))],
        out_shape=[SDS((M, D), F32), SDS((M, D), BF16), SDS((1, D), F32), SDS((1, LANES), F32)],
        compiler_params=_cp("arbitrary"))(x, gain, target)


def _log_sigmoid(z):
    return jnp.minimum(z, 0.0) - jnp.log(1.0 + jnp.exp(-jnp.abs(z)))


def _chunk_scan(x, row, reverse):
    n = x.shape[0]
    s = 1
    while s < GLA_CHUNK:
        if reverse:
            x = x + jnp.where(row < GLA_CHUNK - s, pltpu.roll(x, n - s, 0), 0.0)
        else:
            x = x + jnp.where(row >= s, pltpu.roll(x, s, 0), 0.0)
        s *= 2
    return x


def _chunk_edge(b, reverse):
    t = b.shape[0]
    nc = t // GLA_CHUNK
    b3 = b.reshape(nc, GLA_CHUNK, b.shape[1])
    e = b3[:, 0:1, :] if reverse else b3[:, GLA_CHUNK - 1:GLA_CHUNK, :]
    return jnp.broadcast_to(e, b3.shape).reshape(b.shape), e.reshape(nc, b.shape[1])


def _gla_gates(lr, up_ref, bias_ref, cs, row, reverse):
    z = _nn(lr, up_ref[:, cs]) + bias_ref[:, cs]
    g = _log_sigmoid(z) * (1.0 / GLA_GATE_NORMALIZER)
    b = _chunk_scan(g, row, reverse)
    bl, blc = _chunk_edge(b, reverse)
    return z, b, bl, blc


def _gla_mask(reverse):
    r = lax.broadcasted_iota(jnp.int32, (GLA_CHUNK, GLA_CHUNK), 0)
    c = lax.broadcasted_iota(jnp.int32, (GLA_CHUNK, GLA_CHUNK), 1)
    return (r <= c) if reverse else (r >= c)


def gla_fwd(proj, upad, bias, *, reverse, name, comm=None):
    S_ = proj.shape[0]
    T = _tile(S_, 512)
    nt, nc = S_ // T, T // GLA_CHUNK
    H, DK, DV, C = GLA_HEADS, GLA_DK, GLA_DV, GLA_CHUNK
    tix = (lambda i: nt - 1 - i) if reverse else (lambda i: i)
    qscale = DK ** -0.5

    def body(q_ref, k_ref, v_ref, lr_ref, up_ref, b_ref, o_ref, st_ref, st_s, qd_s, ki_s, ke_s, dec_s):
        @pl.when(pl.program_id(0) == 0)
        def _():
            st_s[...] = jnp.zeros_like(st_s)
        row = lax.broadcasted_iota(jnp.int32, (T, DK), 0) % C
        mask = _gla_mask(reverse)
        lr = lr_ref[...]
        for h in range(H):
            cs = slice(h * DK, (h + 1) * DK)
            _, b, bl, _ = _gla_gates(lr, up_ref, b_ref, cs, row, reverse)
            q = q_ref[:, cs].astype(F32) * qscale
            k = k_ref[:, cs].astype(F32)
            qd_s[:, cs] = (q * jnp.exp(b)).astype(BF16)
            ki_s[:, cs] = (k * jnp.exp(-b)).astype(BF16)
            ke_s[:, cs] = (k * jnp.exp(bl - b)).astype(BF16)
            dec_s[:, cs] = jnp.exp(bl)

        def chunk(ci, carry):
            c = (nc - 1 - ci) if reverse else ci
            r0 = pl.multiple_of(c * C, C)
            s0 = pl.multiple_of(c * DV, DV)
            for h in range(H):
                cs = slice(h * DK, (h + 1) * DK)
                vs = slice(h * DV, (h + 1) * DV)
                qd, ki, ke = qd_s[pl.ds(r0, C), cs], ki_s[pl.ds(r0, C), cs], ke_s[pl.ds(r0, C), cs]
                v = v_ref[pl.ds(r0, C), vs]
                st = st_s[h]
                stb = st.astype(BF16)
                st_ref[h, pl.ds(s0, DV), :] = stb
                a = jnp.where(mask, _nt(qd, ki), 0.0).astype(BF16)
                o_ref[pl.ds(r0, C), vs] = _nn(a, v) + _nt(qd, stb)
                st_s[h] = dec_s[pl.ds(r0, 1), cs] * st + _tn(v, ke)
            return carry

        lax.fori_loop(0, nc, chunk, 0)

    return _call(
        body, name=name, grid=(nt,),
        in_specs=[pl.BlockSpec((T, QK_W), lambda i: (tix(i), C_GQ // QK_W)),
                  pl.BlockSpec((T, QK_W), lambda i: (tix(i), C_GK // QK_W)),
                  pl.BlockSpec((T, V_W), lambda i: (tix(i), C_GV // V_W)),
                  pl.BlockSpec((T, LANES), lambda i: (tix(i), C_LR // LANES)),
                  pl.BlockSpec((LANES, QK_W), lambda i: (0, 0)), pl.BlockSpec((1, QK_W), lambda i: (0, 0))],
        out_specs=[pl.BlockSpec((T, V_W), lambda i: (tix(i), 0)),
                   pl.BlockSpec((H, nc * DV, DK), lambda i: (0, tix(i), 0))],
        out_shape=[SDS((S_, V_W), F32), SDS((H, S_ // C * DV, DK), BF16)],
        scratch_shapes=[pltpu.VMEM((H, DV, DK), F32), pltpu.VMEM((T, QK_W), BF16), pltpu.VMEM((T, QK_W), BF16),
                        pltpu.VMEM((T, QK_W), BF16), pltpu.VMEM((T, QK_W), F32)],
        sem=("arbitrary",), args=(proj, proj, proj, proj, upad, bias), comm=comm)


def gla_bwd(proj, upad, bias, states, do, prev, *, reverse, name):
    S_ = proj.shape[0]
    T = _tile(S_, 512)
    nt, nc = S_ // T, T // GLA_CHUNK
    H, DK, DV, C = GLA_HEADS, GLA_DK, GLA_DV, GLA_CHUNK
    tix = (lambda i: i) if reverse else (lambda i: nt - 1 - i)
    qscale = DK ** -0.5
    with_prev = prev is not None
    NCP = max(8, nc)

    def body(*refs):
        (q_ref, k_ref, v_ref, lr_ref, up_ref, b_ref, st_ref, do_ref) = refs[:8]
        n_in = 12 if with_prev else 8
        pq_ref, pk_ref, pv_ref, plr_ref = refs[8:12] if with_prev else (None,) * 4
        dq_ref, dk_ref, dv_ref, dlr_ref, dup_ref, dbias_ref = refs[n_in:n_in + 6]
        dst_s, qd_s, ki_s, ke_s, dec_s, dqd_s, dki_s, dke_s, ddec_s, z_s, eb_s, einv_s, eend_s = refs[n_in + 6:]
        first = pl.program_id(0) == 0

        @pl.when(first)
        def _():
            dst_s[...] = jnp.zeros_like(dst_s)
            dup_ref[...] = jnp.zeros_like(dup_ref)
            dbias_ref[...] = jnp.zeros_like(dbias_ref)

        row = lax.broadcasted_iota(jnp.int32, (T, DK), 0) % C
        edge_row = 0 if reverse else C - 1
        mask = _gla_mask(reverse)
        lr = lr_ref[...]
        dlr = plr_ref[...].astype(F32) if with_prev else jnp.zeros((T, LANES), F32)

        for h in range(H):
            cs = slice(h * DK, (h + 1) * DK)
            z, b, bl, _ = _gla_gates(lr, up_ref, b_ref, cs, row, reverse)
            q = q_ref[:, cs].astype(F32) * qscale
            k = k_ref[:, cs].astype(F32)
            eb, einv, eend = jnp.exp(b), jnp.exp(-b), jnp.exp(bl - b)
            z_s[:, cs], eb_s[:, cs], einv_s[:, cs], eend_s[:, cs] = z, eb, einv, eend
            qd_s[:, cs] = (q * eb).astype(BF16)
            ki_s[:, cs] = (k * einv).astype(BF16)
            ke_s[:, cs] = (k * eend).astype(BF16)
            dec_s[:, cs] = jnp.exp(bl)

        def chunk(ci, carry):
            c = ci if reverse else (nc - 1 - ci)
            r0 = pl.multiple_of(c * C, C)
            s0 = pl.multiple_of(c * DV, DV)
            for h in range(H):
                cs = slice(h * DK, (h + 1) * DK)
                vs = slice(h * DV, (h + 1) * DV)
                qdc, kic, kec = qd_s[pl.ds(r0, C), cs], ki_s[pl.ds(r0, C), cs], ke_s[pl.ds(r0, C), cs]
                v = v_ref[pl.ds(r0, C), vs]
                doc = do_ref[pl.ds(r0, C), vs]
                st = st_ref[h, pl.ds(s0, DV), :]
                dst = dst_s[h]
                dstb = dst.astype(BF16)
                a = jnp.where(mask, _nt(qdc, kic), 0.0).astype(BF16)
                da = jnp.where(mask, _nt(doc, v), 0.0).astype(BF16)
                dvc = _tn(a, doc) + _nt(kec, dstb)
                if with_prev:
                    dvc = dvc + pv_ref[pl.ds(r0, C), vs].astype(F32)
                dv_ref[pl.ds(r0, C), vs] = dvc.astype(BF16)
                dqd_s[pl.ds(r0, C), cs] = _nn(da, kic) + _nn(doc, st)
                dki_s[pl.ds(r0, C), cs] = _tn(da, qdc)
                dke_s[pl.ds(r0, C), cs] = _nn(v, dstb)
                ddec_s[h, pl.ds(c, 1), :] = jnp.sum(dst * st.astype(F32), axis=0, keepdims=True)
                dst_s[h] = dec_s[pl.ds(r0, 1), cs] * dst + _tn(doc, qdc)
            return carry

        lax.fori_loop(0, nc, chunk, 0)

        for h in range(H):
            cs = slice(h * DK, (h + 1) * DK)
            z, eb, einv, eend = z_s[:, cs], eb_s[:, cs], einv_s[:, cs], eend_s[:, cs]
            q = q_ref[:, cs].astype(F32) * qscale
            k = k_ref[:, cs].astype(F32)
            qd, ki, ke = q * eb, k * einv, k * eend
            dec = dec_s[:, cs].reshape(nc, C, DK)[:, 0, :]
            dqd, dki, dke = dqd_s[:, cs], dki_s[:, cs], dke_s[:, cs]
            dq = dqd * eb * qscale
            dk = dki * einv + dke * eend
            if with_prev:
                dq = dq + pq_ref[:, cs].astype(F32)
                dk = dk + pk_ref[:, cs].astype(F32)
            dq_ref[:, cs] = dq.astype(BF16)
            dk_ref[:, cs] = dk.astype(BF16)
            wke = dke * ke
            db = dqd * qd - dki * ki - wke
            dbl = jnp.sum(wke.reshape(nc, C, DK), axis=1) + ddec_s[h, 0:nc, :] * dec
            dbl_b = jnp.broadcast_to(dbl.reshape(nc, 1, DK), (nc, C, DK)).reshape(T, DK)
            db = db + jnp.where(row == edge_row, dbl_b, 0.0)
            dg = _chunk_scan(db, row, not reverse)
            dz = dg * (1.0 / GLA_GATE_NORMALIZER) * (1.0 / (1.0 + jnp.exp(z)))
            dzb = dz.astype(BF16)
            dbias_ref[:, cs] += jnp.sum(dz, axis=0, keepdims=True)
            dup_ref[:, cs] += _tn(lr, dzb)
            dlr = dlr + _nt(dzb, up_ref[:, cs])
        dlr_ref[...] = dlr.astype(BF16)

    tile = lambda w, cb: pl.BlockSpec((T, w), lambda i: (tix(i), cb))
    in_specs = [tile(QK_W, C_GQ // QK_W), tile(QK_W, C_GK // QK_W), tile(V_W, C_GV // V_W), tile(LANES, C_LR // LANES),
                pl.BlockSpec((LANES, QK_W), lambda i: (0, 0)), pl.BlockSpec((1, QK_W), lambda i: (0, 0)),
                pl.BlockSpec((H, nc * DV, DK), lambda i: (0, tix(i), 0)), tile(V_W, 0)]
    args = [proj, proj, proj, proj, upad, bias, states, do]
    if with_prev:
        in_specs += [tile(QK_W, 0), tile(QK_W, 0), tile(V_W, 0), tile(LANES, 0)]
        args += list(prev)
    return pl.pallas_call(
        body, name=name, grid=(nt,), in_specs=in_specs,
        out_specs=[tile(QK_W, 0), tile(QK_W, 0), tile(V_W, 0), tile(LANES, 0),
                   pl.BlockSpec((LANES, QK_W), lambda i: (0, 0)), pl.BlockSpec((1, QK_W), lambda i: (0, 0))],
        out_shape=[SDS((S_, QK_W), BF16), SDS((S_, QK_W), BF16), SDS((S_, V_W), BF16), SDS((S_, LANES), BF16),
                   SDS((LANES, QK_W), F32), SDS((1, QK_W), F32)],
        scratch_shapes=[pltpu.VMEM((H, DV, DK), F32), pltpu.VMEM((T, QK_W), BF16), pltpu.VMEM((T, QK_W), BF16),
                        pltpu.VMEM((T, QK_W), BF16), pltpu.VMEM((T, QK_W), F32), pltpu.VMEM((T, QK_W), F32),
                        pltpu.VMEM((T, QK_W), F32), pltpu.VMEM((T, QK_W), F32), pltpu.VMEM((H, NCP, DK), F32)]
        + [pltpu.VMEM((T, QK_W), F32)] * 4,
        compiler_params=_cp("arbitrary"))(*args)


def _head_norm_fwd(o, width):
    out = []
    for h in range(o.shape[1] // width):
        oh = o[:, h * width:(h + 1) * width]
        rs = lax.rsqrt(jnp.mean(oh * oh, axis=-1, keepdims=True) + EPS)
        out.append((oh * rs, rs))
    return out


def gla_post(o_f, o_b, proj, gain, *, name):
    S_ = proj.shape[0]
    T = _tile(S_, 512)

    def body(of_ref, ob_ref, r_ref, g_ref, y_ref):
        o = of_ref[...] + ob_ref[...]
        for h, (ohat, _) in enumerate(_head_norm_fwd(o, GLA_DV)):
            vs = slice(h * GLA_DV, (h + 1) * GLA_DV)
            r = r_ref[:, vs].astype(F32)
            y_ref[:, vs] = (ohat * g_ref[:, vs] * (r / (1.0 + jnp.exp(-r)))).astype(BF16)

    blk = pl.BlockSpec((T, V_W), lambda i: (i, 0))
    return pl.pallas_call(
        body, name=name, grid=(S_ // T,),
        in_specs=[blk, blk, pl.BlockSpec((T, V_W), lambda i: (i, C_GR // V_W)), pl.BlockSpec((1, V_W), lambda i: (0, 0))],
        out_specs=blk, out_shape=SDS((S_, V_W), BF16), compiler_params=_cp("parallel"))(o_f, o_b, proj, gain)


def gla_post_bwd(dmix, o_f, o_b, proj, gain, *, name):
    S_ = proj.shape[0]
    T = _tile(S_, 512)

    def body(dy_ref, of_ref, ob_ref, r_ref, g_ref, do_ref, dr_ref, gg_ref):
        @pl.when(pl.program_id(0) == 0)
        def _():
            gg_ref[...] = jnp.zeros_like(gg_ref)
        o = of_ref[...] + ob_ref[...]
        for h, (ohat, rs) in enumerate(_head_norm_fwd(o, GLA_DV)):
            vs = slice(h * GLA_DV, (h + 1) * GLA_DV)
            r = r_ref[:, vs].astype(F32)
            dy = dy_ref[:, vs].astype(F32)
            sg = 1.0 / (1.0 + jnp.exp(-r))
            gate = r * sg
            n = ohat * g_ref[:, vs]
            dr_ref[:, vs] = (dy * n * (sg * (1.0 + r * (1.0 - sg)))).astype(BF16)
            dn = dy * gate
            u = dn * g_ref[:, vs]
            do_ref[:, vs] = (rs * (u - ohat * jnp.mean(u * ohat, axis=-1, keepdims=True))).astype(BF16)
            gg_ref[:, vs] += jnp.sum(dn * ohat, axis=0, keepdims=True)

    blk = pl.BlockSpec((T, V_W), lambda i: (i, 0))
    vec = pl.BlockSpec((1, V_W), lambda i: (0, 0))
    return pl.pallas_call(
        body, name=name, grid=(S_ // T,),
        in_specs=[blk, blk, blk, pl.BlockSpec((T, V_W), lambda i: (i, C_GR // V_W)), vec],
        out_specs=[blk, blk, vec], out_shape=[SDS((S_, V_W), BF16), SDS((S_, V_W), BF16), SDS((1, V_W), F32)],
        compiler_params=_cp("arbitrary"))(dmix, o_f, o_b, proj, gain)


def _t5_bucket_np(rel):
    half = REL_BUCKETS // 2
    max_exact = half // 2
    ret = np.where(rel > 0, half, 0)
    n = np.abs(rel)
    nf = np.maximum(n, 1).astype(np.float32)
    large = max_exact + (np.log(nf / np.float32(max_exact)) / np.float32(math.log(REL_MAX_DISTANCE / max_exact))
                         * np.float32(half - max_exact)).astype(np.int32)
    large = np.minimum(large, half - 1)
    return ret + np.where(n < max_exact, n, large)


def _rel_onehot(dilation):
    rel = np.arange(-DIL_W, DIL_W + 1)
    return np.eye(REL_BUCKETS, dtype=np.float32)[_t5_bucket_np(rel * dilation)]


def _toeplitz(vec, tq, flip):
    H = vec.shape[0]
    wlen = tq + 2 * DIL_W
    lu = wlen + tq
    if flip:
        vec = vec[:, ::-1]
    u = jnp.full((H, lu), NEG_INF, F32)
    u = lax.dynamic_update_slice(u, vec, (0, 0))
    t = jnp.tile(u, (1, tq))[:, :tq * (lu - 1)].reshape(H, tq, lu - 1)
    return t[:, :, :wlen]


def _untoeplitz(ds, tq):
    H = ds.shape[0]
    wlen = tq + 2 * DIL_W
    lu = wlen + tq
    p = jnp.pad(ds, ((0, 0), (0, 0), (0, lu - 1 - wlen))).reshape(H, tq * (lu - 1))
    p = jnp.pad(p, ((0, 0), (0, tq))).reshape(H, tq, lu)
    return jnp.sum(p, axis=1)[:, :2 * DIL_W + 1]


DIL_TQ = 1024
DIL_SUB = 128


def _dil_band(tq):
    sb = min(tq, DIL_SUB)
    return sb, sb + 2 * DIL_W


def _dil_window(p_ref, o_ref, n_ref, tq):
    return jnp.concatenate([p_ref[tq - DIL_W:, :], o_ref[...], n_ref[:DIL_W, :]], axis=0)


def _dil_specs(tq, nt, col0, ncols_per_r):
    cb = lambda h, r: r * ncols_per_r + col0 + h
    own = pl.BlockSpec((tq, LANES), lambda h, r, i: (i, cb(h, r)))
    prev = pl.BlockSpec((tq, LANES), lambda h, r, i: (jnp.maximum(i - 1, 0), cb(h, r)))
    nxt = pl.BlockSpec((tq, LANES), lambda h, r, i: (jnp.minimum(i + 1, nt - 1), cb(h, r)))
    return prev, own, nxt


def dil_fwd(qkv, bias, *, d, name, comm=None):
    l = qkv.shape[0]
    tq = _tile(l, DIL_TQ)
    nt = l // tq
    H = DIL_HEADS
    scale = DIL_HEAD_DIM ** -0.5

    sb, bw = _dil_band(tq)

    def body(q_ref, kp, ko, kn, vp, vo, vn, b_ref, o_ref, l_ref):
        i = pl.program_id(2)
        kw = _dil_window(kp, ko, kn, tq)
        vw = _dil_window(vp, vo, vn, tq)
        col = lax.broadcasted_iota(jnp.int32, (sb, bw), 1)
        for j in range(tq // sb):
            rows, win = slice(j * sb, (j + 1) * sb), slice(j * sb, j * sb + bw)
            s = _nt(q_ref[rows, :], kw[win]) * scale + b_ref[0]
            kpos = i * tq + j * sb - DIL_W + col
            s = jnp.where((kpos >= 0) & (kpos < l), s, NEG_INF)
            m = jnp.max(s, axis=-1, keepdims=True)
            p = jnp.exp(s - m)
            den = jnp.sum(p, axis=-1, keepdims=True)
            o_ref[rows, :] = (_nn(p.astype(BF16), vw[win]) / den).astype(BF16)
            l_ref[rows, :] = jnp.broadcast_to(m + jnp.log(den), (sb, LANES))

    _, q_spec, _ = _dil_specs(tq, nt, 0, 12)
    out_spec = pl.BlockSpec((tq, LANES), lambda h, r, i: (i, r * H + h))
    return _call(
        body, name=name, grid=(H, d, nt),
        in_specs=[q_spec, *_dil_specs(tq, nt, 4, 12), *_dil_specs(tq, nt, 8, 12),
                  pl.BlockSpec((1, sb, bw), lambda h, r, i: (h, 0, 0))],
        out_specs=[out_spec, out_spec], out_shape=[SDS((l, d * DIL_WIDTH), BF16), SDS((l, d * DIL_WIDTH), F32)],
        sem=("parallel", "parallel", "arbitrary"), args=(qkv, qkv, qkv, qkv, qkv, qkv, qkv, bias), comm=comm)


def dil_bwd_q(qkv, bias, do, lse, corr, *, d, name):
    l = qkv.shape[0]
    tq = _tile(l, DIL_TQ)
    nt = l // tq
    H = DIL_HEADS
    scale = DIL_HEAD_DIM ** -0.5

    sb, bw = _dil_band(tq)

    def body(q_ref, kp, ko, kn, vp, vo, vn, b_ref, do_ref, l_ref, c_ref, dq_ref, ds_ref):
        r, i = pl.program_id(1), pl.program_id(2)
        kw = _dil_window(kp, ko, kn, tq)
        vw = _dil_window(vp, vo, vn, tq)
        col = lax.broadcasted_iota(jnp.int32, (sb, bw), 1)
        ds_sum = jnp.zeros((sb, bw), F32)
        for j in range(tq // sb):
            rows, win = slice(j * sb, (j + 1) * sb), slice(j * sb, j * sb + bw)
            s = _nt(q_ref[rows, :], kw[win]) * scale + b_ref[0]
            kpos = i * tq + j * sb - DIL_W + col
            p = jnp.where((kpos >= 0) & (kpos < l), jnp.exp(s - l_ref[rows, 0:1]), 0.0)
            ds = p * (_nt(do_ref[rows, :], vw[win]) - c_ref[rows, 0:1].astype(F32))
            dq_ref[rows, :] = (_nn(ds.astype(BF16), kw[win]) * scale).astype(BF16)
            ds_sum = ds_sum + ds

        @pl.when((r == 0) & (i == 0))
        def _():
            ds_ref[0] = ds_sum

        @pl.when((r > 0) | (i > 0))
        def _():
            ds_ref[0] += ds_sum

    _, q_spec, _ = _dil_specs(tq, nt, 0, 12)
    t_spec = pl.BlockSpec((tq, LANES), lambda h, r, i: (i, r * H + h))
    b_spec = pl.BlockSpec((1, sb, bw), lambda h, r, i: (h, 0, 0))
    return pl.pallas_call(
        body, name=name, grid=(H, d, nt),
        in_specs=[q_spec, *_dil_specs(tq, nt, 4, 12), *_dil_specs(tq, nt, 8, 12), b_spec, t_spec, t_spec, t_spec],
        out_specs=[t_spec, b_spec], out_shape=[SDS((l, d * DIL_WIDTH), BF16), SDS((H, sb, bw), F32)],
        compiler_params=_cp("parallel", "arbitrary", "arbitrary"))(qkv, qkv, qkv, qkv, qkv, qkv, qkv, bias, do, lse, corr)


def dil_bwd_kv(qkv, bias_t, do, lse, corr, *, d, name):
    l = qkv.shape[0]
    tq = _tile(l, DIL_TQ)
    nt = l // tq
    H = DIL_HEADS
    scale = DIL_HEAD_DIM ** -0.5

    sb, bw = _dil_band(tq)

    def body(k_ref, v_ref, qp, qo, qn, dp, do_, dn, lp, lo, ln, cp, co, cn, b_ref, dk_ref, dv_ref):
        i = pl.program_id(2)
        qw = _dil_window(qp, qo, qn, tq)
        dow = _dil_window(dp, do_, dn, tq)
        lrow = _dil_window(lp, lo, ln, tq).T[0:1, :]
        crow = _dil_window(cp, co, cn, tq).astype(F32).T[0:1, :]
        col = lax.broadcasted_iota(jnp.int32, (sb, bw), 1)
        for j in range(tq // sb):
            rows, win = slice(j * sb, (j + 1) * sb), slice(j * sb, j * sb + bw)
            st = _nt(k_ref[rows, :], qw[win]) * scale + b_ref[0]
            qpos = i * tq + j * sb - DIL_W + col
            pt = jnp.where((qpos >= 0) & (qpos < l), jnp.exp(st - lrow[:, win]), 0.0)
            dv_ref[rows, :] = _nn(pt.astype(BF16), dow[win]).astype(BF16)
            dst = pt * (_nt(v_ref[rows, :], dow[win]) - crow[:, win])
            dk_ref[rows, :] = (_nn(dst.astype(BF16), qw[win]) * scale).astype(BF16)

    _, k_spec, _ = _dil_specs(tq, nt, 4, 12)
    _, v_spec, _ = _dil_specs(tq, nt, 8, 12)
    t_spec = pl.BlockSpec((tq, LANES), lambda h, r, i: (i, r * H + h))
    return pl.pallas_call(
        body, name=name, grid=(H, d, nt),
        in_specs=[k_spec, v_spec, *_dil_specs(tq, nt, 0, 12), *_dil_specs(tq, nt, 0, 4), *_dil_specs(tq, nt, 0, 4),
                  *_dil_specs(tq, nt, 0, 4), pl.BlockSpec((1, sb, bw), lambda h, r, i: (h, 0, 0))],
        out_specs=[t_spec, t_spec], out_shape=[SDS((l, d * DIL_WIDTH), BF16), SDS((l, d * DIL_WIDTH), BF16)],
        compiler_params=_cp("parallel", "parallel", "arbitrary"))(
            qkv, qkv, qkv, qkv, qkv, do, do, do, lse, lse, lse, corr, corr, corr, bias_t)


def _dil_weights(l1, l2, l3):
    m = jnp.maximum(jnp.maximum(l1, l2), l3)
    e = [jnp.exp(x - m) for x in (l1, l2, l3)]
    tot = e[0] + e[1] + e[2]
    return [x / tot for x in e]


def dil_combine(os_, ls_, gain, *, name):
    S_ = os_[0].shape[0]
    T = _tile(S_, 512)

    def body(o1, o2, o3, l1, l2, l3, g_ref, y_ref):
        w = _dil_weights(l1[...], l2[...], l3[...])
        o = w[0] * o1[...] + w[1] * o2[...] + w[2] * o3[...]
        for h, (ohat, _) in enumerate(_head_norm_fwd(o, DIL_HEAD_DIM)):
            cs = slice(h * DIL_HEAD_DIM, (h + 1) * DIL_HEAD_DIM)
            y_ref[:, cs] = (ohat * g_ref[:, cs]).astype(BF16)

    blk = pl.BlockSpec((T, DIL_WIDTH), lambda i: (i, 0))
    return pl.pallas_call(
        body, name=name, grid=(S_ // T,), in_specs=[blk] * 6 + [pl.BlockSpec((1, DIL_WIDTH), lambda i: (0, 0))],
        out_specs=blk, out_shape=SDS((S_, DIL_WIDTH), BF16), compiler_params=_cp("parallel"))(*os_, *ls_, gain)


def dil_combine_bwd(dmix, os_, ls_, gain, *, name):
    S_ = os_[0].shape[0]
    T = _tile(S_, 512)
    E = DIL_HEAD_DIM

    def body(dy_ref, o1, o2, o3, l1, l2, l3, g_ref, d1, d2, d3, c1, c2, c3, gg_ref):
        @pl.when(pl.program_id(0) == 0)
        def _():
            gg_ref[...] = jnp.zeros_like(gg_ref)
        w = _dil_weights(l1[...], l2[...], l3[...])
        o = w[0] * o1[...] + w[1] * o2[...] + w[2] * o3[...]
        for h, (ohat, rs) in enumerate(_head_norm_fwd(o, E)):
            cs = slice(h * E, (h + 1) * E)
            dy = dy_ref[:, cs].astype(F32)
            u = dy * g_ref[:, cs]
            do = rs * (u - ohat * jnp.mean(u * ohat, axis=-1, keepdims=True))
            gg_ref[:, cs] += jnp.sum(dy * ohat, axis=0, keepdims=True)
            tot = jnp.sum(do * o[:, cs], axis=-1, keepdims=True)
            for wr, d_ref, c_ref in zip(w, (d1, d2, d3), (c1, c2, c3)):
                d_ref[:, cs] = (wr[:, cs] * do).astype(BF16)
                c_ref[:, cs] = (wr[:, cs] * tot).astype(BF16)

    blk = pl.BlockSpec((T, DIL_WIDTH), lambda i: (i, 0))
    vec = pl.BlockSpec((1, DIL_WIDTH), lambda i: (0, 0))
    return pl.pallas_call(
        body, name=name, grid=(S_ // T,),
        in_specs=[pl.BlockSpec((T, DIL_WIDTH), lambda i: (i, V_W // DIL_WIDTH))] + [blk] * 6 + [vec],
        out_specs=[blk] * 6 + [vec],
        out_shape=[SDS((S_, DIL_WIDTH), BF16)] * 6 + [SDS((1, DIL_WIDTH), F32)],
        compiler_params=_cp("arbitrary"))(dmix, *os_, *ls_, gain)


def _mem_softmax(q, k, scale):
    s = _nt(q, k) * scale
    e = jnp.exp(s - jnp.max(s, axis=-1, keepdims=True))
    return e / jnp.sum(e, axis=-1, keepdims=True)


def mem_fwd(proj, kv, gain, *, name):
    S_ = proj.shape[0]
    M = kv.shape[0]
    T = _tile(S_, 1024)
    E, H = MEM_HEAD_DIM, MEM_HEADS
    scale = E ** -0.5

    def body(q_ref, k_ref, v_ref, g_ref, y_ref):
        p = _mem_softmax(q_ref[...], k_ref[...], scale)
        o = _nn(p.astype(BF16), v_ref[...])
        rs = lax.rsqrt(jnp.mean(o * o, axis=-1, keepdims=True) + EPS)
        y_ref[...] = (o * rs * g_ref[...]).astype(BF16)

    return pl.pallas_call(
        body, name=name, grid=(H, S_ // T),
        in_specs=[pl.BlockSpec((T, E), lambda h, i: (i, C_MQ // E + h)), pl.BlockSpec((M, E), lambda h, i: (0, h)),
                  pl.BlockSpec((M, E), lambda h, i: (0, H + h)), pl.BlockSpec((1, E), lambda h, i: (0, h))],
        out_specs=pl.BlockSpec((T, E), lambda h, i: (i, h)), out_shape=SDS((S_, MEM_WIDTH), BF16),
        compiler_params=_cp("parallel", "parallel"))(proj, kv, kv, gain)


def mem_bwd(dmix, proj, kv, gain, *, name):
    S_ = proj.shape[0]
    M = kv.shape[0]
    T = _tile(S_, 1024)
    E, H = MEM_HEAD_DIM, MEM_HEADS
    scale = E ** -0.5

    def body(dy_ref, q_ref, k_ref, v_ref, g_ref, dq_ref, dk_ref, dv_ref, gg_ref):
        @pl.when(pl.program_id(1) == 0)
        def _():
            dk_ref[...] = jnp.zeros_like(dk_ref)
            dv_ref[...] = jnp.zeros_like(dv_ref)
            gg_ref[...] = jnp.zeros_like(gg_ref)
        q, k, v = q_ref[...], k_ref[...], v_ref[...]
        p = _mem_softmax(q, k, scale)
        pb = p.astype(BF16)
        o = _nn(pb, v)
        rs = lax.rsqrt(jnp.mean(o * o, axis=-1, keepdims=True) + EPS)
        ohat = o * rs
        dy = dy_ref[...].astype(F32)
        u = dy * g_ref[...]
        do = (rs * (u - ohat * jnp.mean(u * ohat, axis=-1, keepdims=True))).astype(BF16)
        gg_ref[...] += jnp.sum(dy * ohat, axis=0, keepdims=True)
        dv_ref[...] += _tn(pb, do)
        dp = _nt(do, v)
        ds = (p * (dp - jnp.sum(p * dp, axis=-1, keepdims=True))).astype(BF16)
        dq_ref[...] = (_nn(ds, k) * scale).astype(BF16)
        dk_ref[...] += _tn(ds, q) * scale

    return pl.pallas_call(
        body, name=name, grid=(H, S_ // T),
        in_specs=[pl.BlockSpec((T, E), lambda h, i: (i, (V_W + DIL_WIDTH) // E + h)),
                  pl.BlockSpec((T, E), lambda h, i: (i, C_MQ // E + h)), pl.BlockSpec((M, E), lambda h, i: (0, h)),
                  pl.BlockSpec((M, E), lambda h, i: (0, H + h)), pl.BlockSpec((1, E), lambda h, i: (0, h))],
        out_specs=[pl.BlockSpec((T, E), lambda h, i: (i, h)), pl.BlockSpec((M, E), lambda h, i: (0, h)),
                   pl.BlockSpec((M, E), lambda h, i: (0, h)), pl.BlockSpec((1, E), lambda h, i: (0, h))],
        out_shape=[SDS((S_, MEM_WIDTH), BF16), SDS((M, MEM_WIDTH), F32), SDS((M, MEM_WIDTH), F32),
                   SDS((1, MEM_WIDTH), F32)],
        compiler_params=_cp("parallel", "arbitrary"))(dmix, proj, kv, kv, gain)


def _strided(a, d):
    return a if d == 1 else a.reshape(a.shape[0] // d, d * a.shape[1])


def _unstrided(a, d):
    return a if d == 1 else a.reshape(a.shape[0] * d, a.shape[1] // d)


def _dil_tables(rel_bias, seq):
    tabs = []
    for d in DIL_DILATIONS:
        sb, _ = _dil_band(min(seq // d, DIL_TQ))
        vec = jnp.sum(jnp.asarray(_rel_onehot(d))[:, :, None] * rel_bias[None], axis=1).T
        tabs.append((_toeplitz(vec, sb, False), _toeplitz(vec, sb, True)))
    return tabs


def _rel_bias_grad(ds_sums, seq):
    g = jnp.zeros((REL_BUCKETS, DIL_HEADS), F32)
    for d, ds in zip(DIL_DILATIONS, ds_sums):
        gvec = _untoeplitz(ds, _dil_band(min(seq // d, DIL_TQ))[0])
        g = g + jnp.sum(jnp.asarray(_rel_onehot(d))[:, :, None] * gvec.T[:, None, :], axis=0)
    return g


def _pad_gate_up(up, row0):
    return jnp.zeros((LANES, QK_W), BF16).at[row0:row0 + GLA_GATE_RANK].set(up.astype(BF16))


def _layer_fwd(x, mem, p, tabs, l, carry):
    tag = f"l{l}"
    proj, hn1 = norm_mm(x, p["norm_mix"], p["w_in"], tm=TM_FWD, tn=IN_TILE, name=f"in_proj_{tag}", comm=carry.get("in"))
    o_f, st_f = gla_fwd(proj, p["up_f"], p["bias_f"], reverse=False, name=f"gla_fwd_f_{tag}", comm=carry.get("gla_f"))
    o_b, st_b = gla_fwd(proj, p["up_b"], p["bias_b"], reverse=True, name=f"gla_fwd_b_{tag}", comm=carry.get("gla_b"))
    gla_out = gla_post(o_f, o_b, proj, p["gla_norm"], name=f"gla_post_{tag}")
    qkv = proj[:, C_DQ:C_MQ]
    os_, ls_, ls_strided = [], [], []
    for d, (bias, _) in zip(DIL_DILATIONS, tabs):
        o, lse = dil_fwd(_strided(qkv, d), bias, d=d, name=f"dil_fwd_d{d}_{tag}", comm=carry.get(f"dil{d}"))
        os_.append(_unstrided(o, d))
        ls_.append(_unstrided(lse, d))
        ls_strided.append(lse)
    dil_out = dil_combine(os_, ls_, p["dil_norm"], name=f"dil_combine_{tag}")
    kv, hmem = norm_mm(mem, p["mem_norm"], p["w_mem_kv"], tm=256, tn=1024, name=f"mem_kv_{tag}")
    mem_out = mem_fwd(proj, kv, p["mem_out_norm"], name=f"mem_fwd_{tag}")
    mixed = jnp.concatenate([gla_out, dil_out, mem_out], axis=1)
    x1 = mm_res(mixed, p["w_out"], x, relu2=False, tm=TM_FWD, tn=TN, tk=TK, name=f"out_proj_{tag}",
                comm=carry.get("out"))
    a, hn2 = norm_mm(x1, p["norm_mlp"], p["w_up"], tm=TM_FWD, tn=TN, name=f"up_proj_{tag}", comm=carry.get("up"))
    x2 = mm_res(a, p["w_down"], x1, relu2=True, tm=TM_FWD, tn=TN, tk=TK, name=f"down_proj_{tag}",
                comm=carry.get("down"))
    saved = dict(x=x, proj=proj, hn1=hn1, o_f=o_f, o_b=o_b, st_f=st_f, st_b=st_b, qkv=qkv, os=os_, ls=ls_,
                 ls_strided=ls_strided, kv=kv, hmem=hmem, mixed=mixed, x1=x1, a=a, hn2=hn2)
    return x2, saved


class GradScatter:
    SLOTS = {"down_x": [(1, "w_in", 0, 1)], "down_w": [(1, "w_out", 0, 1), (1, "w_mem_kv", 0, 1)],
             "up_w": [(0, "w_down", 0, 2)], "up_x": [(0, "w_down", 1, 2)],
             "in_w": [(0, "w_up", 0, 2)], "in_x": [(0, "w_up", 1, 2)]}
    LAYER0 = {"in_w": [(0, "w_out", 0, 1), (0, "w_mem_kv", 0, 1)], "in_x": [(0, "w_in", 0, 1)]}

    def __init__(self, exchange):
        self.exchange = exchange
        self.ready = {}
        self.taken = set()
        self.sent = []

    def offer(self, l, name, g):
        self.ready[(l, name)] = g

    def _take(self, keys):
        keys = [k for k in keys if k[:2] in self.ready and k not in self.taken]
        if not (keys and self.exchange):
            return None
        bufs = [self.ready[k[:2]] for k in keys]
        rows = [(k[2] * (b.shape[1] // k[3]), b.shape[1] // k[3]) for k, b in zip(keys, bufs)]
        ex = ChipExchange(bufs, scatter=True, rows=rows)
        self.taken.update(keys)
        self.sent.append((keys, ex))
        return ex

    def carrier(self, l, slot):
        plan = self.SLOTS[slot] + (self.LAYER0.get(slot, []) if l == 0 else [])
        return self._take([(l + dl, n, i, m) for dl, n, i, m in plan])

    def flush(self, name):
        pieces = {k[:2]: max([t[3] for t in self.taken if t[:2] == k[:2]] + [1]) for k in self.ready}
        ex = self._take(sorted((l, n, i, m) for (l, n), m in pieces.items() for i in range(m)))
        if ex is not None:
            ex.run(name)

    def received(self):
        got = {}
        for keys, ex in self.sent:
            for k, r in zip(keys, ex.result):
                got.setdefault(k[:2], {})[k[2]] = r
        return {k: [v[i] for i in sorted(v)] for k, v in got.items()}


def _quarters(g):
    return g.reshape(N_CHIPS, g.shape[0] // N_CHIPS, g.shape[1])


def _layer_bwd(dx2, dx2b, mem, p, tabs, s, l, sink):
    tag = f"l{l}"
    seq = dx2.shape[0]
    g = {}
    da = mm_nt(dx2b, p["w_down"], s["a"], tm=TM_NT, tn=TN, name=f"down_bwd_x_{tag}", comm=sink.carrier(l, "down_x"))
    g_down = mm_tn(s["a"], dx2b, relu2=True, tm=TM_W, tn=TN_W, tk=TK_W, out="rows", name=f"down_bwd_w_{tag}",
                   comm=sink.carrier(l, "down_w"))
    sink.offer(l, "w_down", _quarters(g_down))
    g_up = mm_tn(s["hn2"], da, relu2=False, tm=TM_W, tn=TN_W, tk=TK_W, out="cols", name=f"up_bwd_w_{tag}",
                 comm=sink.carrier(l, "up_w"))
    sink.offer(l, "w_up", g_up)
    dx1, dx1b, g["norm_mlp"] = mm_nt_normbwd(da, p["w_up"], s["x1"], p["norm_mlp"], dx2, tm=TM_NORM_BWD, tk=TK_NORM_BWD,
                                       name=f"up_bwd_x_{tag}", comm=sink.carrier(l, "up_x"))
    dmix = mm_nt(dx1b, p["w_out"], None, tm=TM_NT, tn=TN, name=f"out_bwd_x_{tag}")
    sink.offer(l, "w_out", _quarters(mm_tn(s["mixed"], dx1b, relu2=False, tm=TM_W, tn=TN_W, tk=TK_W, out="rows",
                                           name=f"out_bwd_w_{tag}")))
    proj = s["proj"]
    do, dr, g["gla_norm"] = gla_post_bwd(dmix, s["o_f"], s["o_b"], proj, p["gla_norm"], name=f"gla_post_bwd_{tag}")
    r_f = gla_bwd(proj, p["up_f"], p["bias_f"], s["st_f"], do, None, reverse=False, name=f"gla_bwd_f_{tag}")
    r_b = gla_bwd(proj, p["up_b"], p["bias_b"], s["st_b"], do, r_f[:4], reverse=True, name=f"gla_bwd_b_{tag}")
    dgq, dgk, dgv, dlr = r_b[:4]
    g["up_f"], g["bias_f"] = r_f[4][0:GLA_GATE_RANK], r_f[5]
    g["up_b"], g["bias_b"] = r_b[4][GLA_GATE_RANK:2 * GLA_GATE_RANK], r_b[5]
    *branch, g["dil_norm"] = dil_combine_bwd(dmix, s["os"], s["ls"], p["dil_norm"], name=f"dil_combine_bwd_{tag}")
    dos, corrs = branch[:3], branch[3:]
    ddq = ddk = ddv = None
    ds_sums = []
    for r, (d, (bias, bias_t)) in enumerate(zip(DIL_DILATIONS, tabs)):
        qkv_d = _strided(s["qkv"], d)
        do_d, lse_d, corr_d = _strided(dos[r], d), s["ls_strided"][r], _strided(corrs[r], d)
        dq_d, ds_sum = dil_bwd_q(qkv_d, bias, do_d, lse_d, corr_d, d=d, name=f"dil_bwd_q_d{d}_{tag}")
        dk_d, dv_d = dil_bwd_kv(qkv_d, bias_t, do_d, lse_d, corr_d, d=d, name=f"dil_bwd_kv_d{d}_{tag}")
        ds_sums.append(ds_sum)
        dq_d, dk_d, dv_d = [_unstrided(t, d).astype(F32) for t in (dq_d, dk_d, dv_d)]
        ddq, ddk, ddv = (dq_d, dk_d, dv_d) if ddq is None else (ddq + dq_d, ddk + dk_d, ddv + dv_d)
    g["rel_bias"] = _rel_bias_grad(ds_sums, seq)
    dmq, dmk, dmv, g["mem_out_norm"] = mem_bwd(dmix, proj, s["kv"], p["mem_out_norm"], name=f"mem_bwd_{tag}")
    dkv = jnp.concatenate([dmk, dmv], axis=1).astype(BF16)
    sink.offer(l, "w_mem_kv", _quarters(mm_tn(s["hmem"], dkv, relu2=False, tm=TM_W, tn=TN_W, tk=TK_W, out="rows",
                                              name=f"mem_kv_bwd_w_{tag}")))
    _, _, g["mem_norm"] = mm_nt_normbwd(dkv, p["w_mem_kv"], mem, p["mem_norm"], jnp.zeros_like(mem), tm=TM_NORM_BWD,
                                     tk=TK_NORM_BWD, name=f"mem_kv_bwd_x_{tag}")
    dproj = jnp.concatenate([dgq, dgk, dgv, dr, ddq.astype(BF16), ddk.astype(BF16), ddv.astype(BF16), dmq, dlr,
                             jnp.zeros((seq, IN_PAD - C_LR - LANES), BF16)], axis=1)
    g_in = mm_tn(s["hn1"], dproj, relu2=False, tm=TM_W, tn=IN_TILE, tk=TK_W, name=f"in_bwd_w_{tag}",
                 comm=sink.carrier(l, "in_w"))
    sink.offer(l, "w_in", _shard_w_in_grad(g_in))
    dx0, dx0b, g["norm_mix"] = mm_nt_normbwd(dproj, p["w_in"], s["x"], p["norm_mix"], dx1, tm=TM_NORM_BWD, tk=IN_TILE,
                                             name=f"in_bwd_x_{tag}", comm=sink.carrier(l, "in_x"))
    return dx0, dx0b, g


def _shard_w_in_grad(g):
    g = jnp.concatenate([g[:, :LR_COL], g[:, C_LR:C_LR + 2 * GLA_GATE_RANK], g[:, LR_COL:C_LR]], axis=1).astype(BF16)
    w = IN_WIDTH // N_CHIPS
    return jnp.stack([g[:, j * w:(j + 1) * w] for j in range(N_CHIPS)], axis=0)


def local_step(x, mem, target, depth, layer_weights, norm_final, rel_bias, fwd_carry, sink):
    tabs = _dil_tables(rel_bias, x.shape[0])
    saved, layers = [], []
    for l in range(depth):
        layers.append(layer_weights(l))
        x, s = _layer_fwd(x, mem, layers[l], tabs, l, fwd_carry(l))
        saved.append(s)
    dx, dxb, g_final, loss = loss_head(x, norm_final, target, tm=TM_LOSS, name="loss_head")
    grads = [None] * depth
    for l in reversed(range(depth)):
        dx, dxb, grads[l] = _layer_bwd(dx, dxb, mem, layers[l], tabs, saved[l], l, sink)
    return loss, dx, grads, g_final


def sibling_exchange(bufs, *, name):
    n = len(bufs)

    def body(*refs):
        ins, outs = refs[:n], refs[n:2 * n]
        send, recv = refs[2 * n:]
        x, y, c = _mesh_pos()
        cps = [pltpu.make_async_remote_copy(src_ref=ins[a], dst_ref=outs[a], send_sem=send.at[a], recv_sem=recv.at[a],
                                            device_id=(x, y, 1 - c), device_id_type=MESH) for a in range(n)]
        for cp in cps:
            cp.start()
        for cp in cps:
            cp.wait()

    return pl.pallas_call(
        body, name=name, in_specs=[HBM_SPEC] * n, out_specs=[HBM_SPEC] * n,
        out_shape=[SDS(b.shape, b.dtype) for b in bufs],
        scratch_shapes=[pltpu.SemaphoreType.DMA((n,)), pltpu.SemaphoreType.DMA((n,))])(*bufs)


def all_reduce_small(v, *, name):
    R = v.shape[0]

    def body(v_ref, o_ref, slots, send, recv):
        x, y, c = _mesh_pos()
        me = 4 * x + 2 * y + c
        flip = lambda p, f: 1 - p if f else p
        peers = [(flip(x, k & 4), flip(y, k & 2), flip(c, k & 1)) for k in range(1, N_DEV)]
        slots[pl.ds(me, 1)] = v_ref[...][None]
        cps = [pltpu.make_async_remote_copy(src_ref=v_ref, dst_ref=slots.at[me], send_sem=send.at[k], recv_sem=recv.at[k],
                                            device_id=peer, device_id_type=MESH) for k, peer in enumerate(peers)]
        for cp in cps:
            cp.start()
        for k, (px, py, pc) in enumerate(peers):
            pltpu.make_async_remote_copy(src_ref=v_ref, dst_ref=slots.at[4 * px + 2 * py + pc], send_sem=send.at[k],
                                         recv_sem=recv.at[k], device_id=(px, py, pc), device_id_type=MESH).wait_recv()
        for cp in cps:
            cp.wait_send()
        acc = slots[0]
        for s in range(1, N_DEV):
            acc = acc + slots[s]
        o_ref[...] = acc

    vm = pl.BlockSpec(memory_space=pltpu.VMEM)
    return pl.pallas_call(
        body, name=name, in_specs=[vm], out_specs=vm, out_shape=SDS(v.shape, F32),
        scratch_shapes=[pltpu.VMEM((N_DEV, R, LANES), F32), pltpu.SemaphoreType.DMA((N_DEV - 1,)),
                        pltpu.SemaphoreType.DMA((N_DEV - 1,))])(v)


def sum_chips(recv, *, name):
    _, R, C = recv.shape
    tr = _tile(R, 512)

    def body(r_ref, o_ref):
        acc = r_ref[0].astype(F32)
        for j in range(1, N_CHIPS):
            acc = acc + r_ref[j].astype(F32)
        o_ref[...] = acc.astype(BF16)

    return pl.pallas_call(
        body, name=name, grid=(R // tr,), in_specs=[pl.BlockSpec((N_CHIPS, tr, C), lambda i: (0, i, 0))],
        out_specs=pl.BlockSpec((tr, C), lambda i: (i, 0)), out_shape=SDS((R, C), BF16),
        compiler_params=_cp("parallel"))(recv)


def adamw(parts, w, m, v, *, name):
    R, C = w.shape
    n = len(parts)
    tr = _tile(R, 512)
    while 2 * (n + 7) * tr * C * 4 > ELEMWISE_BLOCK_BYTES and tr % 16 == 0:
        tr //= 2
    c1 = 1.0 - ADAM_B1 ** ADAM_STEP
    c2 = 1.0 - ADAM_B2 ** ADAM_STEP

    def body(*refs):
        w_ref, m_ref, v_ref = refs[n:n + 3]
        g_ref, d_ref, nm_ref, nv_ref = refs[n + 3:]
        g = refs[0][...].astype(F32)
        for r in refs[1:n]:
            g = g + r[...].astype(F32)
        nm = ADAM_B1 * m_ref[...] + (1.0 - ADAM_B1) * g
        nv = ADAM_B2 * v_ref[...] + (1.0 - ADAM_B2) * (g * g)
        g_ref[...] = g
        nm_ref[...] = nm
        nv_ref[...] = nv
        d_ref[...] = -ADAM_LR * ((nm / c1) / (jnp.sqrt(nv / c2) + ADAM_EPS) + ADAM_WD * w_ref[...])

    blk = pl.BlockSpec((tr, C), lambda i: (i, 0))
    return pl.pallas_call(
        body, name=name, grid=(R // tr,), in_specs=[blk] * (n + 3), out_specs=[blk] * 4,
        out_shape=[SDS((R, C), F32)] * 4, compiler_params=_cp("parallel"))(*parts, w, m, v)


PACK_ROWS = 8


def _pack(parts):
    rows = []
    for a in parts:
        flat = a.reshape(-1).astype(F32)
        n = -(-flat.shape[0] // (PACK_ROWS * LANES)) * PACK_ROWS * LANES
        rows.append(jnp.pad(flat, (0, n - flat.shape[0])).reshape(-1, LANES))
    return jnp.concatenate(rows, axis=0)


def _unpack(slab, shapes):
    out, r = [], 0
    for shp in shapes:
        n = int(np.prod(shp))
        nr = -(-n // (PACK_ROWS * LANES)) * PACK_ROWS
        out.append(slab[r:r + nr].reshape(-1)[:n].reshape(shp))
        r += nr
    return out


SMALL = ["norm_mix", "gla_gate_bias_fwd", "gla_gate_bias_bwd", "gla_norm", "rel_bias", "dil_norm", "mem_norm",
         "mem_out_norm", "norm_mlp", "norm_final"]
GATE_UPS = ["gla_gate_up_fwd", "gla_gate_up_bwd"]
BIG = ["w_in", "w_mem_kv", "w_out", "w_up", "w_down"]
WEIGHTS = ["norm_mix", "w_in", "gla_gate_up_fwd", "gla_gate_bias_fwd", "gla_gate_up_bwd", "gla_gate_bias_bwd", "gla_norm",
           "rel_bias", "dil_norm", "mem_norm", "w_mem_kv", "mem_out_norm", "w_out", "norm_mlp", "w_up", "w_down",
           "norm_final"]


FWD_CARRY = {"in": [(1, "w_in")], "out": [(1, "w_out"), (1, "w_mem_kv")], "up": [(1, "w_up")], "down": [(1, "w_down")]}
FWD_CARRY_LAYER0 = {"in": [("w_out", 0, 1), ("w_mem_kv", 0, 1)], "gla_f": [("w_up", 0, 2)], "gla_b": [("w_down", 0, 2)],
                    "dil4": [("w_up", 1, 2)], "dil16": [("w_down", 1, 2)]}


def _kernel_layout(name, g):
    if name == "w_up":
        return g
    if name == "w_in":
        full = jnp.concatenate([g[j] for j in range(N_CHIPS)], axis=1)
        return jnp.concatenate([full[:, :LR_COL], full[:, LR_COL + 2 * GLA_GATE_RANK:],
                                full[:, LR_COL:LR_COL + 2 * GLA_GATE_RANK],
                                jnp.zeros((full.shape[0], IN_PAD - IN_WIDTH), BF16)], axis=1)
    return g.reshape(-1, g.shape[-1])


def kernel(x, mem, norm_mix, w_in, gla_gate_up_fwd, gla_gate_bias_fwd, gla_gate_up_bwd, gla_gate_bias_bwd, gla_norm, rel_bias, dil_norm, mem_norm, w_mem_kv, mem_out_norm, w_out, norm_mlp, w_up, w_down, norm_final, loss_target, m_norm_mix, m_w_in, m_gla_gate_up_fwd, m_gla_gate_bias_fwd, m_gla_gate_up_bwd, m_gla_gate_bias_bwd, m_gla_norm, m_rel_bias, m_dil_norm, m_mem_norm, m_w_mem_kv, m_mem_out_norm, m_w_out, m_norm_mlp, m_w_up, m_w_down, m_norm_final, v_norm_mix, v_w_in, v_gla_gate_up_fwd, v_gla_gate_bias_fwd, v_gla_gate_up_bwd, v_gla_gate_bias_bwd, v_gla_norm, v_rel_bias, v_dil_norm, v_mem_norm, v_w_mem_kv, v_mem_out_norm, v_w_out, v_norm_mlp, v_w_up, v_w_down, v_norm_final):
    vals = dict(locals())
    W = {n: vals[n] for n in WEIGHTS}
    M1 = {n: vals["m_" + n] for n in WEIGHTS}
    V2 = {n: vals["v_" + n] for n in WEIGHTS}
    depth = w_in.shape[0]
    chip = 2 * lax.axis_index("x") + lax.axis_index("y")

    wb = {n: W[n].astype(BF16) for n in BIG + GATE_UPS}
    first = ChipExchange([wb[n] for n in ["w_in"] + GATE_UPS], scatter=False, layers=[0] + [None] * len(GATE_UPS))
    first.run("gather_first")
    gate_up = dict(zip(GATE_UPS, first.result[1:]))
    pending = {(0, "w_in"): [(first, 0)]}
    row = lambda a: a.reshape(1, -1)

    def fwd_carry(l):
        plan = {slot: [(l + dl, n, 0, 1) for dl, n in what if l + dl < depth] for slot, what in FWD_CARRY.items()}
        if l == 0:
            for slot, what in FWD_CARRY_LAYER0.items():
                plan[slot] = plan.get(slot, []) + [(0, n, i, m) for n, i, m in what]
        carry = {}
        for slot, keys in plan.items():
            if keys:
                shard_rows = [wb[n].shape[1] for _, n, _, _ in keys]
                carry[slot] = ChipExchange([wb[n] for _, n, _, _ in keys], scatter=False, layers=[k for k, _, _, _ in keys],
                                           rows=[(i * (r // m), r // m) for (_, _, i, m), r in zip(keys, shard_rows)])
                for place, (k, n, i, m) in enumerate(keys):
                    pending.setdefault((k, n), [None] * m)[i] = (carry[slot], place)
        return carry

    class LayerWeights(dict):
        def __init__(self, l):
            super().__init__()
            self.l = l

        def __missing__(self, n):
            pieces = [ex.result[place] for ex, place in pending[(self.l, n)]]
            self[n] = _kernel_layout(n, pieces[0] if len(pieces) == 1 else jnp.concatenate(pieces, axis=1))
            return self[n]

    def layer_weights(l):
        p = LayerWeights(l)
        full_up = lambda n: jnp.concatenate([gate_up[n][j, l] for j in range(N_CHIPS)], axis=1)
        p["up_f"] = _pad_gate_up(full_up("gla_gate_up_fwd"), 0)
        p["up_b"] = _pad_gate_up(full_up("gla_gate_up_bwd"), GLA_GATE_RANK)
        p["bias_f"], p["bias_b"] = row(gla_gate_bias_fwd[l]), row(gla_gate_bias_bwd[l])
        for n in ("norm_mix", "gla_norm", "dil_norm", "mem_norm", "mem_out_norm", "norm_mlp"):
            p[n] = row(W[n][l])
        return p

    sink = GradScatter(exchange=True)
    loss, grad_x, grads, g_final = local_step(x[0], mem[0], loss_target[0], depth, layer_weights, row(norm_final),
                                              rel_bias, fwd_carry, sink)
    sink.flush("scatter_last_grads")
    recv = sink.received()

    flat2 = lambda a: a.reshape(-1, a.shape[-1])
    part = [jnp.concatenate([sum_chips(r, name=f"sum_chips_{n}_l{l}_p{i}") for l in range(depth)
                             for i, r in enumerate(recv[(l, n)])], axis=0) for n in BIG]
    other = sibling_exchange(part, name="swap_partial_sums")
    out = {}
    for n, pa, ob in zip(BIG, part, other):
        res = adamw([pa, ob], flat2(W[n]), flat2(M1[n]), flat2(V2[n]), name=f"adamw_{n}")
        out[n] = [r.reshape(W[n].shape) for r in res]

    small_g = {}
    for n, key in (("norm_mix", "norm_mix"), ("gla_gate_bias_fwd", "bias_f"), ("gla_gate_bias_bwd", "bias_b"),
                   ("gla_norm", "gla_norm"), ("dil_norm", "dil_norm"), ("mem_norm", "mem_norm"),
                   ("mem_out_norm", "mem_out_norm"), ("norm_mlp", "norm_mlp"),
                   ("gla_gate_up_fwd", "up_f"), ("gla_gate_up_bwd", "up_b")):
        small_g[n] = jnp.stack([grads[l][key].reshape(W[n].shape[1:] if n not in GATE_UPS else (GLA_GATE_RANK, QK_W))
                                for l in range(depth)], axis=0)
    small_g["rel_bias"] = sum(grads[l]["rel_bias"] for l in range(depth))
    small_g["norm_final"] = g_final.reshape(-1)
    names = SMALL + GATE_UPS
    slab = all_reduce_small(_pack([small_g[n] for n in names] + [loss[:, 0]]), name="all_reduce_small")
    *summed, loss_sum = _unpack(slab, [small_g[n].shape for n in names] + [(1,)])
    summed = dict(zip(names, summed))
    for n in GATE_UPS:
        summed[n] = lax.dynamic_slice_in_dim(summed[n], chip * LANES, LANES, axis=2)
    names_s = SMALL + GATE_UPS
    shapes = [W[n].shape for n in names_s]
    res = adamw([_pack([summed[n] for n in names_s])], _pack([W[n] for n in names_s]), _pack([M1[n] for n in names_s]),
                _pack([V2[n] for n in names_s]), name="adamw_small")
    for n, *r in zip(names_s, *[_unpack(t, shapes) for t in res]):
        out[n] = r

    return (loss_sum.reshape(()), grad_x[None], *[out[n][0] for n in WEIGHTS], *[out[n][1] for n in WEIGHTS],
            *[out[n][2] for n in WEIGHTS], *[out[n][3] for n in WEIGHTS])
```

```python
import functools
import math

import numpy as np
import jax
import jax.numpy as jnp
from jax import lax
from jax.experimental import pallas as pl
from jax.experimental.pallas import tpu as pltpu

F32, BF16 = jnp.float32, jnp.bfloat16
SDS = jax.ShapeDtypeStruct

DEPTH = 4
GLA_HEADS, GLA_DK, GLA_DV = 4, 128, 256
GLA_GATE_RANK = 16
GLA_GATE_NORMALIZER = 16.0
GLA_CHUNK = 64
DIL_HEADS, DIL_HEAD_DIM = 4, 128
DIL_DILATIONS = (1, 4, 16)
DIL_W = 64
MEM_HEADS, MEM_HEAD_DIM = 4, 128
REL_BUCKETS, REL_MAX_DISTANCE = 32, 1024
EPS = 1e-6
NEG_INF = -1e30
QK_W, V_W, DIL_WIDTH, MEM_WIDTH = 512, 1024, 512, 512
IN_WIDTH = 5152
LR_COL = 3072
IN_PAD = 5376
C_GQ, C_GK, C_GV, C_GR, C_DQ, C_MQ, C_LR = 0, 512, 1024, 2048, 3072, 4608, 5120
IN_TILE = 1792

ADAM_LR, ADAM_B1, ADAM_B2, ADAM_EPS, ADAM_WD, ADAM_STEP = 0.001, 0.9, 0.999, 1e-08, 0.01, 10

V7X_VMEM_BYTES = 64 * 1024 * 1024
VMEM_LIMIT = V7X_VMEM_BYTES * 7 // 8
LANES = 128

TM_FWD = 512
TM_NT = 1024
TN = 2048
TK = 2048
TM_NORM_BWD = 512
TK_NORM_BWD = 2048
NORM_BWD_ROWS = 64
TM_W, TN_W, TK_W = 1024, 1024, 2048
TM_LOSS = 256
ELEMWISE_BLOCK_BYTES = 24 * 1024 * 1024

NT_DIMS = (((1,), (1,)), ((), ()))
TN_DIMS = (((0,), (0,)), ((), ()))


def _nt(a, b):
    return lax.dot_general(a, b, NT_DIMS, preferred_element_type=F32)


def _tn(a, b):
    return lax.dot_general(a, b, TN_DIMS, preferred_element_type=F32)


def _nn(a, b):
    return jnp.dot(a, b, preferred_element_type=F32)


def _cp(*sem):
    return pltpu.CompilerParams(dimension_semantics=sem, vmem_limit_bytes=VMEM_LIMIT)


def _tile(n, t):
    t = min(n, t)
    assert n % t == 0, (n, t)
    return t


MESH = pl.DeviceIdType.MESH
HBM_SPEC = pl.BlockSpec(memory_space=pltpu.HBM)
N_CHIPS = 4
N_DEV = 8


def _mesh_pos():
    return lax.axis_index("x"), lax.axis_index("y"), lax.axis_index("c")


class ChipExchange:
    def __init__(self, bufs, *, scatter, layers=None, rows=None):
        self.bufs = list(bufs)
        self.n = n = len(self.bufs)
        self.scatter = scatter
        self.layers = list(layers) if layers is not None else [None] * n
        self.rows = list(rows) if rows is not None else [None] * n
        dma = pltpu.SemaphoreType.DMA
        if scatter:
            shp = [b.shape if r is None else (b.shape[0], r[1]) + b.shape[2:] for b, r in zip(self.bufs, self.rows)]
            self.out_shape = [SDS(s, b.dtype) for s, b in zip(shp, self.bufs)]
            self.scratch = [dma((3 * n,)), dma((3 * n,)), dma((n,))]
        else:
            shp = [b.shape if l is None else b.shape[1:] for b, l in zip(self.bufs, self.layers)]
            shp = [s if r is None else (r[1],) + s[1:] for s, r in zip(shp, self.rows)]
            assert all(s[0] % 2 == 0 for s in shp), shp
            self.out_shape = [SDS((N_CHIPS,) + s, b.dtype) for s, b in zip(shp, self.bufs)]
            self.scratch = [dma((3 * n,)), dma((3 * n,)), dma((n,)), dma((3 * n,)), dma((3 * n,))]
        self.result = None

    def _plan(self, ins, outs, sems, finishing):
        x, y, c = _mesh_pos()
        me = 2 * x + y
        chips = [(1 - x, y), (x, 1 - y), (1 - x, 1 - y)]
        remote = lambda src, dst, ss, rs, k, dev: pltpu.make_async_remote_copy(
            src_ref=src, dst_ref=dst, send_sem=ss.at[k], recv_sem=rs.at[k], device_id=dev, device_id_type=MESH)
        kept, sent, landing, passed, handed = [], [], [], [], []
        for a in range(self.n):
            src = ins[a] if self.layers[a] is None else ins[a].at[self.layers[a]]
            if self.scatter:
                r = self.rows[a]
                pick = (lambda j: src.at[j]) if r is None else (lambda j: src.at[j, pl.ds(r[0], r[1])])
                kept.append(pltpu.make_async_copy(pick(me), outs[a].at[me], sems[2].at[a]))
                for j, (px, py) in enumerate(chips):
                    k, peer = 3 * a + j, 2 * px + py
                    sent.append(remote(pick(peer), outs[a].at[me], sems[0], sems[1], k, (px, py, c)))
                    if finishing:
                        landing.append(remote(pick(me), outs[a].at[peer], sems[0], sems[1], k, (px, py, c)))
                continue
            nr = self.out_shape[a].shape[1]
            half = nr // 2
            first = 0 if self.rows[a] is None else self.rows[a][0]
            mine, other = pl.ds(c * half, half), pl.ds((1 - c) * half, half)
            src_mine = src.at[pl.ds(first + c * half, half)]
            kept.append(pltpu.make_async_copy(src.at[pl.ds(first, nr)], outs[a].at[me], sems[2].at[a]))
            for j, (px, py) in enumerate(chips):
                k, peer = 3 * a + j, 2 * px + py
                sent.append(remote(src_mine, outs[a].at[me, mine], sems[0], sems[1], k, (px, py, c)))
                if finishing:
                    landing.append(remote(src_mine, outs[a].at[peer, mine], sems[0], sems[1], k, (px, py, c)))
                    passed.append(remote(outs[a].at[peer, mine], outs[a].at[peer, mine], sems[3], sems[4], k, (x, y, 1 - c)))
                    handed.append(remote(outs[a].at[peer, other], outs[a].at[peer, other], sems[3], sems[4], k,
                                         (x, y, 1 - c)))
        return kept, sent, landing, passed, handed

    def start(self, ins, outs, sems):
        kept, sent, _, _, _ = self._plan(ins, outs, sems, False)
        for cp in kept + sent:
            cp.start()

    def finish(self, ins, outs, sems):
        kept, sent, landing, passed, handed = self._plan(ins, outs, sems, True)
        for k, cp in enumerate(landing):
            cp.wait_recv()
            if passed:
                passed[k].start()
        for cp in handed:
            cp.wait_recv()
        for cp in sent + passed:
            cp.wait_send()
        for cp in kept:
            cp.wait()

    def run(self, name):
        n = self.n

        def body(*refs):
            self.start(refs[:n], refs[n:2 * n], refs[2 * n:])
            self.finish(refs[:n], refs[n:2 * n], refs[2 * n:])

        self.result = pl.pallas_call(body, name=name, in_specs=[HBM_SPEC] * n, out_specs=[HBM_SPEC] * n,
                                     out_shape=self.out_shape, scratch_shapes=self.scratch)(*self.bufs)
        return self.result


def _call(body, *, name, grid, in_specs, out_specs, out_shape, sem, args, scratch_shapes=(), comm=None):
    if comm is None:
        return pl.pallas_call(body, name=name, grid=grid, in_specs=list(in_specs), out_specs=list(out_specs),
                              out_shape=list(out_shape), scratch_shapes=list(scratch_shapes),
                              compiler_params=_cp(*sem))(*args)
    n_in, n_out, n_scr, nc = len(in_specs), len(out_shape), len(scratch_shapes), comm.n

    def wrapped(*refs):
        ins, cin = refs[:n_in], refs[n_in:n_in + nc]
        outs, cout = refs[n_in + nc:n_in + nc + n_out], refs[n_in + nc + n_out:n_in + 2 * nc + n_out]
        scr, csem = refs[n_in + 2 * nc + n_out:n_in + 2 * nc + n_out + n_scr], refs[n_in + 2 * nc + n_out + n_scr:]
        ids = [pl.program_id(d) for d in range(len(grid))]
        first = functools.reduce(jnp.logical_and, [i == 0 for i in ids])
        last = functools.reduce(jnp.logical_and, [i == g - 1 for i, g in zip(ids, grid)])

        @pl.when(first)
        def _():
            comm.start(cin, cout, csem)
        body(*ins, *outs, *scr)

        @pl.when(last)
        def _():
            comm.finish(cin, cout, csem)

    res = pl.pallas_call(
        wrapped, name=name, grid=grid, in_specs=list(in_specs) + [HBM_SPEC] * nc,
        out_specs=list(out_specs) + [HBM_SPEC] * nc, out_shape=list(out_shape) + comm.out_shape,
        scratch_shapes=list(scratch_shapes) + comm.scratch,
        compiler_params=_cp(*("arbitrary",) * len(grid)))(*args, *comm.bufs)
    comm.result = res[n_out:]
    return res[:n_out]


def _w_spec(w, rows, tn, col_of):
    if w.ndim == 2:
        tn = _tile(w.shape[1], tn)
        return w.shape[1], tn, pl.BlockSpec((rows, tn), lambda *g: (0, col_of(*g)))
    tn = _tile(w.shape[2], tn)
    per = w.shape[2] // tn
    return N_CHIPS * w.shape[2], tn, pl.BlockSpec((None, rows, tn), lambda *g: (col_of(*g) // per, 0, col_of(*g) % per))


def norm_mm(x, gain, w, *, tm, tn, name, comm=None):
    M, D = x.shape
    tm = _tile(M, tm)
    N, tn, w_spec = _w_spec(w, D, tn, lambda i, j: j)

    def body(x_ref, g_ref, w_ref, o_ref, hn_ref):
        @pl.when(pl.program_id(1) == 0)
        def _():
            xf = x_ref[...]
            r = lax.rsqrt(jnp.mean(xf * xf, axis=-1, keepdims=True) + EPS)
            hn_ref[...] = (xf * r * g_ref[...]).astype(BF16)
        o_ref[...] = _nn(hn_ref[...], w_ref[...]).astype(BF16)

    return _call(
        body, name=name, grid=(M // tm, N // tn),
        in_specs=[pl.BlockSpec((tm, D), lambda i, j: (i, 0)), pl.BlockSpec((1, D), lambda i, j: (0, 0)),
                  w_spec],
        out_specs=[pl.BlockSpec((tm, tn), lambda i, j: (i, j)), pl.BlockSpec((tm, D), lambda i, j: (i, 0))],
        out_shape=[SDS((M, N), BF16), SDS((M, D), BF16)], sem=("parallel", "arbitrary"), args=(x, gain, w), comm=comm)


def mm_res(a, w, res, *, relu2, tm, tn, tk, name, comm=None):
    M, K = a.shape
    N = w.shape[1]
    tm, tn, tk = _tile(M, tm), _tile(N, tn), _tile(K, tk)

    def body(a_ref, w_ref, r_ref, o_ref):
        @pl.when(pl.program_id(2) == 0)
        def _():
            o_ref[...] = r_ref[...]
        av = a_ref[...]
        if relu2:
            af = jnp.maximum(av.astype(F32), 0.0)
            av = (af * af).astype(BF16)
        o_ref[...] += _nn(av, w_ref[...])

    return _call(
        body, name=name, grid=(M // tm, N // tn, K // tk),
        in_specs=[pl.BlockSpec((tm, tk), lambda i, j, k: (i, k)), pl.BlockSpec((tk, tn), lambda i, j, k: (k, j)),
                  pl.BlockSpec((tm, tn), lambda i, j, k: (i, j))],
        out_specs=[pl.BlockSpec((tm, tn), lambda i, j, k: (i, j))], out_shape=[SDS((M, N), F32)],
        sem=("parallel", "parallel", "arbitrary"), args=(a, w, res), comm=comm)[0]


def mm_nt(dy, w, a, *, tm, tn, name, comm=None):
    M, K = dy.shape
    N = w.shape[0]
    tm, tn = _tile(M, tm), _tile(N, tn)
    with_a = a is not None

    def body(*refs):
        if with_a:
            dy_ref, w_ref, a_ref, o_ref = refs
        else:
            dy_ref, w_ref, o_ref = refs
        r = _nt(dy_ref[...], w_ref[...])
        if with_a:
            r = r * (2.0 * jnp.maximum(a_ref[...].astype(F32), 0.0))
        o_ref[...] = r.astype(BF16)

    in_specs = [pl.BlockSpec((tm, K), lambda i, j: (i, 0)), pl.BlockSpec((tn, K), lambda i, j: (j, 0))]
    args = [dy, w]
    if with_a:
        in_specs.append(pl.BlockSpec((tm, tn), lambda i, j: (i, j)))
        args.append(a)
    return _call(
        body, name=name, grid=(M // tm, N // tn), in_specs=in_specs,
        out_specs=[pl.BlockSpec((tm, tn), lambda i, j: (i, j))], out_shape=[SDS((M, N), BF16)],
        sem=("parallel", "arbitrary"), args=args, comm=comm)[0]


def mm_nt_normbwd(dy, w, x, gain, dx_in, *, tm, tk, name, comm=None):
    M, K = dy.shape
    D = x.shape[1]
    tm = _tile(M, tm)
    _, tk, w_spec = _w_spec(w, D, tk, lambda i, k: k)
    nk = K // tk
    rc = _tile(tm, NORM_BWD_ROWS)

    def body(dy_ref, w_ref, x_ref, g_ref, dxin_ref, dx_ref, dxb_ref, gg_ref, acc):
        i, k = pl.program_id(0), pl.program_id(1)

        @pl.when(k == 0)
        def _():
            acc[...] = jnp.zeros_like(acc)
        acc[...] += _nt(dy_ref[...], w_ref[...])

        @pl.when(k == nk - 1)
        def _():
            def rows(c, part):
                rs = pl.ds(pl.multiple_of(c * rc, rc), rc)
                dh, xf = acc[rs, :], x_ref[rs, :]
                r = lax.rsqrt(jnp.mean(xf * xf, axis=-1, keepdims=True) + EPS)
                xhat = xf * r
                u = dh * g_ref[...]
                dx = dxin_ref[rs, :] + r * (u - xhat * jnp.mean(u * xhat, axis=-1, keepdims=True))
                dx_ref[rs, :] = dx
                dxb_ref[rs, :] = dx.astype(BF16)
                return part + jnp.sum(dh * xhat, axis=0, keepdims=True)

            part = lax.fori_loop(0, tm // rc, rows, jnp.zeros((1, D), F32))

            @pl.when(i == 0)
            def _():
                gg_ref[...] = part

            @pl.when(i > 0)
            def _():
                gg_ref[...] += part

    return _call(
        body, name=name, grid=(M // tm, nk),
        in_specs=[pl.BlockSpec((tm, tk), lambda i, k: (i, k)), w_spec,
                  pl.BlockSpec((tm, D), lambda i, k: (i, 0)), pl.BlockSpec((1, D), lambda i, k: (0, 0)),
                  pl.BlockSpec((tm, D), lambda i, k: (i, 0))],
        out_specs=[pl.BlockSpec((tm, D), lambda i, k: (i, 0)),
                   pl.BlockSpec((tm, D), lambda i, k: (i, 0)),
                   pl.BlockSpec((1, D), lambda i, k: (0, 0))],
        out_shape=[SDS((M, D), F32), SDS((M, D), BF16), SDS((1, D), F32)], scratch_shapes=[pltpu.VMEM((tm, D), F32)],
        sem=("arbitrary", "arbitrary"), args=(dy, w, x, gain, dx_in), comm=comm)


def mm_tn(a, dy, *, relu2, tm, tn, tk, name, out="f32", comm=None):
    S_, Ka = a.shape
    N = dy.shape[1]
    tm, tk = _tile(Ka, tm), _tile(S_, tk)
    tn = _tile(N // N_CHIPS if out == "cols" else N, tn)
    nk = S_ // tk

    def body(a_ref, dy_ref, o_ref, acc):
        k = pl.program_id(2)

        @pl.when(k == 0)
        def _():
            acc[...] = jnp.zeros_like(acc)
        av = a_ref[...]
        if relu2:
            af = jnp.maximum(av.astype(F32), 0.0)
            av = (af * af).astype(BF16)
        acc[...] += _tn(av, dy_ref[...].astype(BF16))

        @pl.when(k == nk - 1)
        def _():
            o_ref[...] = acc[...].astype(o_ref.dtype)

    if out == "cols":
        per = N // N_CHIPS // tn
        o_spec = pl.BlockSpec((None, tm, tn), lambda i, j, k: (j // per, i, j % per))
        o_shape = SDS((N_CHIPS, Ka, N // N_CHIPS), BF16)
    else:
        o_spec = pl.BlockSpec((tm, tn), lambda i, j, k: (i, j))
        o_shape = SDS((Ka, N), F32 if out == "f32" else BF16)
    return _call(
        body, name=name, grid=(Ka // tm, N // tn, nk),
        in_specs=[pl.BlockSpec((tk, tm), lambda i, j, k: (k, i)), pl.BlockSpec((tk, tn), lambda i, j, k: (k, j))],
        out_specs=[o_spec], out_shape=[o_shape], scratch_shapes=[pltpu.VMEM((tm, tn), F32)],
        sem=("parallel", "parallel", "arbitrary"), args=(a, dy), comm=comm)[0]


def loss_head(x, gain, target, *, tm, name):
    M, D = x.shape
    tm = _tile(M, tm)

    def body(x_ref, g_ref, t_ref, dx_ref, dxb_ref, gg_ref, l_ref):
        i = pl.program_id(0)
        xf = x_ref[...]
        r = lax.rsqrt(jnp.mean(xf * xf, axis=-1, keepdims=True) + EPS)
        xhat = xf * r
        e = xhat * g_ref[...] - t_ref[...]
        lpart = 0.5 * jnp.sum(jnp.mean(e * e, axis=-1, keepdims=True), axis=0, keepdims=True)
        dy = e * (1.0 / D)
        u = dy * g_ref[...]
        dx = r * (u - xhat * jnp.mean(u * xhat, axis=-1, keepdims=True))
        dx_ref[...] = dx
        dxb_ref[...] = dx.astype(BF16)
        gpart = jnp.sum(dy * xhat, axis=0, keepdims=True)
        lrow = jnp.broadcast_to(lpart, (1, LANES))

        @pl.when(i == 0)
        def _():
            gg_ref[...] = gpart
            l_ref[...] = lrow

        @pl.when(i > 0)
        def _():
            gg_ref[...] += gpart
            l_ref[...] += lrow

    return pl.pallas_call(
        body, name=name, grid=(M // tm,),
        in_specs=[pl.BlockSpec((tm, D), lambda i: (i, 0)), pl.BlockSpec((1, D), lambda i: (0, 0)),
                  pl.BlockSpec((tm, D), lambda i: (i, 0))],
        out_specs=[pl.BlockSpec((tm, D), lambda i: (i, 0)), pl.BlockSpec((tm, D), lambda i: (i, 0)),
                   pl.BlockSpec((1, D), lambda i: (0, 0)), pl.BlockSpec((1, LANES), lambda i: (0, 0))],
        out_shape=[SDS((M, D), F32), SDS((M, D), BF16), SDS((1, D), F32), SDS((1, LANES), F32)],
        compiler_params=_cp("arbitrary"))(x, gain, target)


def _log_sigmoid(z):
    return jnp.minimum(z, 0.0) - jnp.log(1.0 + jnp.exp(-jnp.abs(z)))


def _chunk_scan(x, row, reverse):
    n = x.shape[0]
    s = 1
    while s < GLA_CHUNK:
        if reverse:
            x = x + jnp.where(row < GLA_CHUNK - s, pltpu.roll(x, n - s, 0), 0.0)
        else:
            x = x + jnp.where(row >= s, pltpu.roll(x, s, 0), 0.0)
        s *= 2
    return x


def _chunk_edge(b, reverse):
    t = b.shape[0]
    nc = t // GLA_CHUNK
    b3 = b.reshape(nc, GLA_CHUNK, b.shape[1])
    e = b3[:, 0:1, :] if reverse else b3[:, GLA_CHUNK - 1:GLA_CHUNK, :]
    return jnp.broadcast_to(e, b3.shape).reshape(b.shape), e.reshape(nc, b.shape[1])


def _gla_gates(lr, up_ref, bias_ref, cs, row, reverse):
    z = _nn(lr, up_ref[:, cs]) + bias_ref[:, cs]
    g = _log_sigmoid(z) * (1.0 / GLA_GATE_NORMALIZER)
    b = _chunk_scan(g, row, reverse)
    bl, blc = _chunk_edge(b, reverse)
    return z, b, bl, blc


def _gla_mask(reverse):
    r = lax.broadcasted_iota(jnp.int32, (GLA_CHUNK, GLA_CHUNK), 0)
    c = lax.broadcasted_iota(jnp.int32, (GLA_CHUNK, GLA_CHUNK), 1)
    return (r <= c) if reverse else (r >= c)


def gla_fwd(proj, upad, bias, *, reverse, name, comm=None):
    S_ = proj.shape[0]
    T = _tile(S_, 512)
    nt, nc = S_ // T, T // GLA_CHUNK
    H, DK, DV, C = GLA_HEADS, GLA_DK, GLA_DV, GLA_CHUNK
    tix = (lambda i: nt - 1 - i) if reverse else (lambda i: i)
    qscale = DK ** -0.5

    def body(q_ref, k_ref, v_ref, lr_ref, up_ref, b_ref, o_ref, st_ref, st_s, qd_s, ki_s, ke_s, dec_s):
        @pl.when(pl.program_id(0) == 0)
        def _():
            st_s[...] = jnp.zeros_like(st_s)
        row = lax.broadcasted_iota(jnp.int32, (T, DK), 0) % C
        mask = _gla_mask(reverse)
        lr = lr_ref[...]
        for h in range(H):
            cs = slice(h * DK, (h + 1) * DK)
            _, b, bl, _ = _gla_gates(lr, up_ref, b_ref, cs, row, reverse)
            q = q_ref[:, cs].astype(F32) * qscale
            k = k_ref[:, cs].astype(F32)
            qd_s[:, cs] = (q * jnp.exp(b)).astype(BF16)
            ki_s[:, cs] = (k * jnp.exp(-b)).astype(BF16)
            ke_s[:, cs] = (k * jnp.exp(bl - b)).astype(BF16)
            dec_s[:, cs] = jnp.exp(bl)

        def chunk(ci, carry):
            c = (nc - 1 - ci) if reverse else ci
            r0 = pl.multiple_of(c * C, C)
            s0 = pl.multiple_of(c * DV, DV)
            for h in range(H):
                cs = slice(h * DK, (h + 1) * DK)
                vs = slice(h * DV, (h + 1) * DV)
                qd, ki, ke = qd_s[pl.ds(r0, C), cs], ki_s[pl.ds(r0, C), cs], ke_s[pl.ds(r0, C), cs]
                v = v_ref[pl.ds(r0, C), vs]
                st = st_s[h]
                stb = st.astype(BF16)
                st_ref[h, pl.ds(s0, DV), :] = stb
                a = jnp.where(mask, _nt(qd, ki), 0.0).astype(BF16)
                o_ref[pl.ds(r0, C), vs] = (_nn(a, v) + _nt(qd, stb)).astype(BF16)
                st_s[h] = dec_s[pl.ds(r0, 1), cs] * st + _tn(v, ke)
            return carry

        lax.fori_loop(0, nc, chunk, 0)

    return _call(
        body, name=name, grid=(nt,),
        in_specs=[pl.BlockSpec((T, QK_W), lambda i: (tix(i), C_GQ // QK_W)),
                  pl.BlockSpec((T, QK_W), lambda i: (tix(i), C_GK // QK_W)),
                  pl.BlockSpec((T, V_W), lambda i: (tix(i), C_GV // V_W)),
                  pl.BlockSpec((T, LANES), lambda i: (tix(i), C_LR // LANES)),
                  pl.BlockSpec((LANES, QK_W), lambda i: (0, 0)), pl.BlockSpec((1, QK_W), lambda i: (0, 0))],
        out_specs=[pl.BlockSpec((T, V_W), lambda i: (tix(i), 0)),
                   pl.BlockSpec((H, nc * DV, DK), lambda i: (0, tix(i), 0))],
        out_shape=[SDS((S_, V_W), BF16), SDS((H, S_ // C * DV, DK), BF16)],
        scratch_shapes=[pltpu.VMEM((H, DV, DK), F32), pltpu.VMEM((T, QK_W), BF16), pltpu.VMEM((T, QK_W), BF16),
                        pltpu.VMEM((T, QK_W), BF16), pltpu.VMEM((T, QK_W), F32)],
        sem=("arbitrary",), args=(proj, proj, proj, proj, upad, bias), comm=comm)


def gla_bwd(proj, upad, bias, states, do, prev, *, reverse, name):
    S_ = proj.shape[0]
    T = _tile(S_, 512)
    nt, nc = S_ // T, T // GLA_CHUNK
    H, DK, DV, C = GLA_HEADS, GLA_DK, GLA_DV, GLA_CHUNK
    tix = (lambda i: i) if reverse else (lambda i: nt - 1 - i)
    qscale = DK ** -0.5
    with_prev = prev is not None
    NCP = max(8, nc)

    def body(*refs):
        (q_ref, k_ref, v_ref, lr_ref, up_ref, b_ref, st_ref, do_ref) = refs[:8]
        n_in = 12 if with_prev else 8
        pq_ref, pk_ref, pv_ref, plr_ref = refs[8:12] if with_prev else (None,) * 4
        dq_ref, dk_ref, dv_ref, dlr_ref, dup_ref, dbias_ref = refs[n_in:n_in + 6]
        dst_s, qd_s, ki_s, ke_s, dec_s, dqd_s, dki_s, dke_s, ddec_s, z_s, eb_s, einv_s, eend_s = refs[n_in + 6:]
        first = pl.program_id(0) == 0

        @pl.when(first)
        def _():
            dst_s[...] = jnp.zeros_like(dst_s)
            dup_ref[...] = jnp.zeros_like(dup_ref)
            dbias_ref[...] = jnp.zeros_like(dbias_ref)

        row = lax.broadcasted_iota(jnp.int32, (T, DK), 0) % C
        edge_row = 0 if reverse else C - 1
        mask = _gla_mask(reverse)
        lr = lr_ref[...]
        dlr = plr_ref[...].astype(F32) if with_prev else jnp.zeros((T, LANES), F32)

        for h in range(H):
            cs = slice(h * DK, (h + 1) * DK)
            z, b, bl, _ = _gla_gates(lr, up_ref, b_ref, cs, row, reverse)
            q = q_ref[:, cs].astype(F32) * qscale
            k = k_ref[:, cs].astype(F32)
            eb, einv, eend = jnp.exp(b), jnp.exp(-b), jnp.exp(bl - b)
            z_s[:, cs], eb_s[:, cs], einv_s[:, cs], eend_s[:, cs] = z, eb, einv, eend
            qd_s[:, cs] = (q * eb).astype(BF16)
            ki_s[:, cs] = (k * einv).astype(BF16)
            ke_s[:, cs] = (k * eend).astype(BF16)
            dec_s[:, cs] = jnp.exp(bl)

        def chunk(ci, carry):
            c = ci if reverse else (nc - 1 - ci)
            r0 = pl.multiple_of(c * C, C)
            s0 = pl.multiple_of(c * DV, DV)
            for h in range(H):
                cs = slice(h * DK, (h + 1) * DK)
                vs = slice(h * DV, (h + 1) * DV)
                qdc, kic, kec = qd_s[pl.ds(r0, C), cs], ki_s[pl.ds(r0, C), cs], ke_s[pl.ds(r0, C), cs]
                v = v_ref[pl.ds(r0, C), vs]
                doc = do_ref[pl.ds(r0, C), vs]
                st = st_ref[h, pl.ds(s0, DV), :]
                dst = dst_s[h]
                dstb = dst.astype(BF16)
                a = jnp.where(mask, _nt(qdc, kic), 0.0).astype(BF16)
                da = jnp.where(mask, _nt(doc, v), 0.0).astype(BF16)
                dvc = _tn(a, doc) + _nt(kec, dstb)
                if with_prev:
                    dvc = dvc + pv_ref[pl.ds(r0, C), vs].astype(F32)
                dv_ref[pl.ds(r0, C), vs] = dvc.astype(BF16)
                dqd_s[pl.ds(r0, C), cs] = _nn(da, kic) + _nn(doc, st)
                dki_s[pl.ds(r0, C), cs] = _tn(da, qdc)
                dke_s[pl.ds(r0, C), cs] = _nn(v, dstb)
                ddec_s[h, pl.ds(c, 1), :] = jnp.sum(dst * st.astype(F32), axis=0, keepdims=True)
                dst_s[h] = dec_s[pl.ds(r0, 1), cs] * dst + _tn(doc, qdc)
            return carry

        lax.fori_loop(0, nc, chunk, 0)

        for h in range(H):
            cs = slice(h * DK, (h + 1) * DK)
            z, eb, einv, eend = z_s[:, cs], eb_s[:, cs], einv_s[:, cs], eend_s[:, cs]
            q = q_ref[:, cs].astype(F32) * qscale
            k = k_ref[:, cs].astype(F32)
            qd, ki, ke = q * eb, k * einv, k * eend
            dec = dec_s[:, cs].reshape(nc, C, DK)[:, 0, :]
            dqd, dki, dke = dqd_s[:, cs], dki_s[:, cs], dke_s[:, cs]
            dq = dqd * eb * qscale
            dk = dki * einv + dke * eend
            if with_prev:
                dq = dq + pq_ref[:, cs].astype(F32)
                dk = dk + pk_ref[:, cs].astype(F32)
            dq_ref[:, cs] = dq.astype(BF16)
            dk_ref[:, cs] = dk.astype(BF16)
            wke = dke * ke
            db = dqd * qd - dki * ki - wke
            dbl = jnp.sum(wke.reshape(nc, C, DK), axis=1) + ddec_s[h, 0:nc, :] * dec
            dbl_b = jnp.broadcast_to(dbl.reshape(nc, 1, DK), (nc, C, DK)).reshape(T, DK)
            db = db + jnp.where(row == edge_row, dbl_b, 0.0)
            dg = _chunk_scan(db, row, not reverse)
            dz = dg * (1.0 / GLA_GATE_NORMALIZER) * (1.0 / (1.0 + jnp.exp(z)))
            dzb = dz.astype(BF16)
            dbias_ref[:, cs] += jnp.sum(dz, axis=0, keepdims=True)
            dup_ref[:, cs] += _tn(lr, dzb)
            dlr = dlr + _nt(dzb, up_ref[:, cs])
        dlr_ref[...] = dlr.astype(BF16)

    tile = lambda w, cb: pl.BlockSpec((T, w), lambda i: (tix(i), cb))
    in_specs = [tile(QK_W, C_GQ // QK_W), tile(QK_W, C_GK // QK_W), tile(V_W, C_GV // V_W), tile(LANES, C_LR // LANES),
                pl.BlockSpec((LANES, QK_W), lambda i: (0, 0)), pl.BlockSpec((1, QK_W), lambda i: (0, 0)),
                pl.BlockSpec((H, nc * DV, DK), lambda i: (0, tix(i), 0)), tile(V_W, 0)]
    args = [proj, proj, proj, proj, upad, bias, states, do]
    if with_prev:
        in_specs += [tile(QK_W, 0), tile(QK_W, 0), tile(V_W, 0), tile(LANES, 0)]
        args += list(prev)
    return pl.pallas_call(
        body, name=name, grid=(nt,), in_specs=in_specs,
        out_specs=[tile(QK_W, 0), tile(QK_W, 0), tile(V_W, 0), tile(LANES, 0),
                   pl.BlockSpec((LANES, QK_W), lambda i: (0, 0)), pl.BlockSpec((1, QK_W), lambda i: (0, 0))],
        out_shape=[SDS((S_, QK_W), BF16), SDS((S_, QK_W), BF16), SDS((S_, V_W), BF16), SDS((S_, LANES), BF16),
                   SDS((LANES, QK_W), F32), SDS((1, QK_W), F32)],
        scratch_shapes=[pltpu.VMEM((H, DV, DK), F32), pltpu.VMEM((T, QK_W), BF16), pltpu.VMEM((T, QK_W), BF16),
                        pltpu.VMEM((T, QK_W), BF16), pltpu.VMEM((T, QK_W), F32), pltpu.VMEM((T, QK_W), F32),
                        pltpu.VMEM((T, QK_W), F32), pltpu.VMEM((T, QK_W), F32), pltpu.VMEM((H, NCP, DK), F32)]
        + [pltpu.VMEM((T, QK_W), F32)] * 4,
        compiler_params=_cp("arbitrary"))(*args)


def _head_norm_fwd(o, width):
    out = []
    for h in range(o.shape[1] // width):
        oh = o[:, h * width:(h + 1) * width]
        rs = lax.rsqrt(jnp.mean(oh * oh, axis=-1, keepdims=True) + EPS)
        out.append((oh * rs, rs))
    return out


def gla_post(o_f, o_b, proj, gain, *, name):
    S_ = proj.shape[0]
    T = _tile(S_, 512)

    def body(of_ref, ob_ref, r_ref, g_ref, y_ref):
        o = of_ref[...].astype(F32) + ob_ref[...].astype(F32)
        for h, (ohat, _) in enumerate(_head_norm_fwd(o, GLA_DV)):
            vs = slice(h * GLA_DV, (h + 1) * GLA_DV)
            r = r_ref[:, vs].astype(F32)
            y_ref[:, vs] = (ohat * g_ref[:, vs] * (r / (1.0 + jnp.exp(-r)))).astype(BF16)

    blk = pl.BlockSpec((T, V_W), lambda i: (i, 0))
    return pl.pallas_call(
        body, name=name, grid=(S_ // T,),
        in_specs=[blk, blk, pl.BlockSpec((T, V_W), lambda i: (i, C_GR // V_W)), pl.BlockSpec((1, V_W), lambda i: (0, 0))],
        out_specs=blk, out_shape=SDS((S_, V_W), BF16), compiler_params=_cp("parallel"))(o_f, o_b, proj, gain)


def gla_post_bwd(dmix, o_f, o_b, proj, gain, *, name):
    S_ = proj.shape[0]
    T = _tile(S_, 512)

    def body(dy_ref, of_ref, ob_ref, r_ref, g_ref, do_ref, dr_ref, gg_ref):
        @pl.when(pl.program_id(0) == 0)
        def _():
            gg_ref[...] = jnp.zeros_like(gg_ref)
        o = of_ref[...].astype(F32) + ob_ref[...].astype(F32)
        for h, (ohat, rs) in enumerate(_head_norm_fwd(o, GLA_DV)):
            vs = slice(h * GLA_DV, (h + 1) * GLA_DV)
            r = r_ref[:, vs].astype(F32)
            dy = dy_ref[:, vs].astype(F32)
            sg = 1.0 / (1.0 + jnp.exp(-r))
            gate = r * sg
            n = ohat * g_ref[:, vs]
            dr_ref[:, vs] = (dy * n * (sg * (1.0 + r * (1.0 - sg)))).astype(BF16)
            dn = dy * gate
            u = dn * g_ref[:, vs]
            do_ref[:, vs] = (rs * (u - ohat * jnp.mean(u * ohat, axis=-1, keepdims=True))).astype(BF16)
            gg_ref[:, vs] += jnp.sum(dn * ohat, axis=0, keepdims=True)

    blk = pl.BlockSpec((T, V_W), lambda i: (i, 0))
    vec = pl.BlockSpec((1, V_W), lambda i: (0, 0))
    return pl.pallas_call(
        body, name=name, grid=(S_ // T,),
        in_specs=[blk, blk, blk, pl.BlockSpec((T, V_W), lambda i: (i, C_GR // V_W)), vec],
        out_specs=[blk, blk, vec], out_shape=[SDS((S_, V_W), BF16), SDS((S_, V_W), BF16), SDS((1, V_W), F32)],
        compiler_params=_cp("arbitrary"))(dmix, o_f, o_b, proj, gain)


def _t5_bucket_np(rel):
    half = REL_BUCKETS // 2
    max_exact = half // 2
    ret = np.where(rel > 0, half, 0)
    n = np.abs(rel)
    nf = np.maximum(n, 1).astype(np.float32)
    large = max_exact + (np.log(nf / np.float32(max_exact)) / np.float32(math.log(REL_MAX_DISTANCE / max_exact))
                         * np.float32(half - max_exact)).astype(np.int32)
    large = np.minimum(large, half - 1)
    return ret + np.where(n < max_exact, n, large)


def _rel_onehot(dilation):
    rel = np.arange(-DIL_W, DIL_W + 1)
    return np.eye(REL_BUCKETS, dtype=np.float32)[_t5_bucket_np(rel * dilation)]


def _toeplitz(vec, tq, flip):
    H = vec.shape[0]
    wlen = tq + 2 * DIL_W
    lu = wlen + tq
    if flip:
        vec = vec[:, ::-1]
    u = jnp.full((H, lu), NEG_INF, F32)
    u = lax.dynamic_update_slice(u, vec, (0, 0))
    t = jnp.tile(u, (1, tq))[:, :tq * (lu - 1)].reshape(H, tq, lu - 1)
    return t[:, :, :wlen]


def _untoeplitz(ds, tq):
    H = ds.shape[0]
    wlen = tq + 2 * DIL_W
    lu = wlen + tq
    p = jnp.pad(ds, ((0, 0), (0, 0), (0, lu - 1 - wlen))).reshape(H, tq * (lu - 1))
    p = jnp.pad(p, ((0, 0), (0, tq))).reshape(H, tq, lu)
    return jnp.sum(p, axis=1)[:, :2 * DIL_W + 1]


DIL_TQ = 1024
DIL_SUB = 128


def _dil_band(tq):
    sb = min(tq, DIL_SUB)
    return sb, sb + 2 * DIL_W


def _dil_window(p_ref, o_ref, n_ref, tq):
    return jnp.concatenate([p_ref[tq - DIL_W:, :], o_ref[...], n_ref[:DIL_W, :]], axis=0)


def _dil_specs(tq, nt, col0, ncols_per_r):
    cb = lambda h, r: r * ncols_per_r + col0 + h
    own = pl.BlockSpec((tq, LANES), lambda h, r, i: (i, cb(h, r)))
    prev = pl.BlockSpec((tq, LANES), lambda h, r, i: (jnp.maximum(i - 1, 0), cb(h, r)))
    nxt = pl.BlockSpec((tq, LANES), lambda h, r, i: (jnp.minimum(i + 1, nt - 1), cb(h, r)))
    return prev, own, nxt


def dil_fwd(qkv, bias, *, d, name, comm=None):
    l = qkv.shape[0]
    tq = _tile(l, DIL_TQ)
    nt = l // tq
    H = DIL_HEADS
    scale = DIL_HEAD_DIM ** -0.5

    sb, bw = _dil_band(tq)

    def body(q_ref, kp, ko, kn, vp, vo, vn, b_ref, o_ref, l_ref):
        i = pl.program_id(2)
        kw = _dil_window(kp, ko, kn, tq)
        vw = _dil_window(vp, vo, vn, tq)
        col = lax.broadcasted_iota(jnp.int32, (sb, bw), 1)
        for j in range(tq // sb):
            rows, win = slice(j * sb, (j + 1) * sb), slice(j * sb, j * sb + bw)
            s = _nt(q_ref[rows, :], kw[win]) * scale + b_ref[0]
            kpos = i * tq + j * sb - DIL_W + col
            s = jnp.where((kpos >= 0) & (kpos < l), s, NEG_INF)
            m = jnp.max(s, axis=-1, keepdims=True)
            p = jnp.exp(s - m)
            den = jnp.sum(p, axis=-1, keepdims=True)
            o_ref[rows, :] = (_nn(p.astype(BF16), vw[win]) / den).astype(BF16)
            l_ref[rows, :] = jnp.broadcast_to(m + jnp.log(den), (sb, LANES))

    _, q_spec, _ = _dil_specs(tq, nt, 0, 12)
    out_spec = pl.BlockSpec((tq, LANES), lambda h, r, i: (i, r * H + h))
    return _call(
        body, name=name, grid=(H, d, nt),
        in_specs=[q_spec, *_dil_specs(tq, nt, 4, 12), *_dil_specs(tq, nt, 8, 12),
                  pl.BlockSpec((1, sb, bw), lambda h, r, i: (h, 0, 0))],
        out_specs=[out_spec, out_spec], out_shape=[SDS((l, d * DIL_WIDTH), BF16), SDS((l, d * DIL_WIDTH), F32)],
        sem=("parallel", "parallel", "arbitrary"), args=(qkv, qkv, qkv, qkv, qkv, qkv, qkv, bias), comm=comm)


def dil_bwd_q(qkv, bias, do, lse, corr, *, d, name):
    l = qkv.shape[0]
    tq = _tile(l, DIL_TQ)
    nt = l // tq
    H = DIL_HEADS
    scale = DIL_HEAD_DIM ** -0.5

    sb, bw = _dil_band(tq)

    def body(q_ref, kp, ko, kn, vp, vo, vn, b_ref, do_ref, l_ref, c_ref, dq_ref, ds_ref):
        r, i = pl.program_id(1), pl.program_id(2)
        kw = _dil_window(kp, ko, kn, tq)
        vw = _dil_window(vp, vo, vn, tq)
        col = lax.broadcasted_iota(jnp.int32, (sb, bw), 1)
        ds_sum = jnp.zeros((sb, bw), F32)
        for j in range(tq // sb):
            rows, win = slice(j * sb, (j + 1) * sb), slice(j * sb, j * sb + bw)
            s = _nt(q_ref[rows, :], kw[win]) * scale + b_ref[0]
            kpos = i * tq + j * sb - DIL_W + col
            p = jnp.where((kpos >= 0) & (kpos < l), jnp.exp(s - l_ref[rows, 0:1]), 0.0)
            ds = p * (_nt(do_ref[rows, :], vw[win]) - c_ref[rows, 0:1].astype(F32))
            dq_ref[rows, :] = (_nn(ds.astype(BF16), kw[win]) * scale).astype(BF16)
            ds_sum = ds_sum + ds

        @pl.when((r == 0) & (i == 0))
        def _():
            ds_ref[0] = ds_sum

        @pl.when((r > 0) | (i > 0))
        def _():
            ds_ref[0] += ds_sum

    _, q_spec, _ = _dil_specs(tq, nt, 0, 12)
    t_spec = pl.BlockSpec((tq, LANES), lambda h, r, i: (i, r * H + h))
    b_spec = pl.BlockSpec((1, sb, bw), lambda h, r, i: (h, 0, 0))
    return pl.pallas_call(
        body, name=name, grid=(H, d, nt),
        in_specs=[q_spec, *_dil_specs(tq, nt, 4, 12), *_dil_specs(tq, nt, 8, 12), b_spec, t_spec, t_spec, t_spec],
        out_specs=[t_spec, b_spec], out_shape=[SDS((l, d * DIL_WIDTH), BF16), SDS((H, sb, bw), F32)],
        compiler_params=_cp("parallel", "arbitrary", "arbitrary"))(qkv, qkv, qkv, qkv, qkv, qkv, qkv, bias, do, lse, corr)


def dil_bwd_kv(qkv, bias_t, do, lse, corr, *, d, name):
    l = qkv.shape[0]
    tq = _tile(l, DIL_TQ)
    nt = l // tq
    H = DIL_HEADS
    scale = DIL_HEAD_DIM ** -0.5

    sb, bw = _dil_band(tq)

    def body(k_ref, v_ref, qp, qo, qn, dp, do_, dn, lp, lo, ln, cp, co, cn, b_ref, dk_ref, dv_ref):
        i = pl.program_id(2)
        qw = _dil_window(qp, qo, qn, tq)
        dow = _dil_window(dp, do_, dn, tq)
        lrow = _dil_window(lp, lo, ln, tq).T[0:1, :]
        crow = _dil_window(cp, co, cn, tq).astype(F32).T[0:1, :]
        col = lax.broadcasted_iota(jnp.int32, (sb, bw), 1)
        for j in range(tq // sb):
            rows, win = slice(j * sb, (j + 1) * sb), slice(j * sb, j * sb + bw)
            st = _nt(k_ref[rows, :], qw[win]) * scale + b_ref[0]
            qpos = i * tq + j * sb - DIL_W + col
            pt = jnp.where((qpos >= 0) & (qpos < l), jnp.exp(st - lrow[:, win]), 0.0)
            dv_ref[rows, :] = _nn(pt.astype(BF16), dow[win]).astype(BF16)
            dst = pt * (_nt(v_ref[rows, :], dow[win]) - crow[:, win])
            dk_ref[rows, :] = (_nn(dst.astype(BF16), qw[win]) * scale).astype(BF16)

    _, k_spec, _ = _dil_specs(tq, nt, 4, 12)
    _, v_spec, _ = _dil_specs(tq, nt, 8, 12)
    t_spec = pl.BlockSpec((tq, LANES), lambda h, r, i: (i, r * H + h))
    return pl.pallas_call(
        body, name=name, grid=(H, d, nt),
        in_specs=[k_spec, v_spec, *_dil_specs(tq, nt, 0, 12), *_dil_specs(tq, nt, 0, 4), *_dil_specs(tq, nt, 0, 4),
                  *_dil_specs(tq, nt, 0, 4), pl.BlockSpec((1, sb, bw), lambda h, r, i: (h, 0, 0))],
        out_specs=[t_spec, t_spec], out_shape=[SDS((l, d * DIL_WIDTH), BF16), SDS((l, d * DIL_WIDTH), BF16)],
        compiler_params=_cp("parallel", "parallel", "arbitrary"))(
            qkv, qkv, qkv, qkv, qkv, do, do, do, lse, lse, lse, corr, corr, corr, bias_t)


def _dil_weights(l1, l2, l3):
    m = jnp.maximum(jnp.maximum(l1, l2), l3)
    e = [jnp.exp(x - m) for x in (l1, l2, l3)]
    tot = e[0] + e[1] + e[2]
    return [x / tot for x in e]


def dil_combine(os_, ls_, gain, *, name):
    S_ = os_[0].shape[0]
    T = _tile(S_, 512)

    def body(o1, o2, o3, l1, l2, l3, g_ref, y_ref):
        w = _dil_weights(l1[...], l2[...], l3[...])
        o = w[0] * o1[...] + w[1] * o2[...] + w[2] * o3[...]
        for h, (ohat, _) in enumerate(_head_norm_fwd(o, DIL_HEAD_DIM)):
            cs = slice(h * DIL_HEAD_DIM, (h + 1) * DIL_HEAD_DIM)
            y_ref[:, cs] = (ohat * g_ref[:, cs]).astype(BF16)

    blk = pl.BlockSpec((T, DIL_WIDTH), lambda i: (i, 0))
    return pl.pallas_call(
        body, name=name, grid=(S_ // T,), in_specs=[blk] * 6 + [pl.BlockSpec((1, DIL_WIDTH), lambda i: (0, 0))],
        out_specs=blk, out_shape=SDS((S_, DIL_WIDTH), BF16), compiler_params=_cp("parallel"))(*os_, *ls_, gain)


def dil_combine_bwd(dmix, os_, ls_, gain, *, name):
    S_ = os_[0].shape[0]
    T = _tile(S_, 512)
    E = DIL_HEAD_DIM

    def body(dy_ref, o1, o2, o3, l1, l2, l3, g_ref, d1, d2, d3, c1, c2, c3, gg_ref):
        @pl.when(pl.program_id(0) == 0)
        def _():
            gg_ref[...] = jnp.zeros_like(gg_ref)
        w = _dil_weights(l1[...], l2[...], l3[...])
        o = w[0] * o1[...] + w[1] * o2[...] + w[2] * o3[...]
        for h, (ohat, rs) in enumerate(_head_norm_fwd(o, E)):
            cs = slice(h * E, (h + 1) * E)
            dy = dy_ref[:, cs].astype(F32)
            u = dy * g_ref[:, cs]
            do = rs * (u - ohat * jnp.mean(u * ohat, axis=-1, keepdims=True))
            gg_ref[:, cs] += jnp.sum(dy * ohat, axis=0, keepdims=True)
            tot = jnp.sum(do * o[:, cs], axis=-1, keepdims=True)
            for wr, d_ref, c_ref in zip(w, (d1, d2, d3), (c1, c2, c3)):
                d_ref[:, cs] = (wr[:, cs] * do).astype(BF16)
                c_ref[:, cs] = (wr[:, cs] * tot).astype(BF16)

    blk = pl.BlockSpec((T, DIL_WIDTH), lambda i: (i, 0))
    vec = pl.BlockSpec((1, DIL_WIDTH), lambda i: (0, 0))
    return pl.pallas_call(
        body, name=name, grid=(S_ // T,),
        in_specs=[pl.BlockSpec((T, DIL_WIDTH), lambda i: (i, V_W // DIL_WIDTH))] + [blk] * 6 + [vec],
        out_specs=[blk] * 6 + [vec],
        out_shape=[SDS((S_, DIL_WIDTH), BF16)] * 6 + [SDS((1, DIL_WIDTH), F32)],
        compiler_params=_cp("arbitrary"))(dmix, *os_, *ls_, gain)


def _mem_softmax(q, k, scale):
    s = _nt(q, k) * scale
    e = jnp.exp(s - jnp.max(s, axis=-1, keepdims=True))
    return e / jnp.sum(e, axis=-1, keepdims=True)


def mem_fwd(proj, kv, gain, *, name):
    S_ = proj.shape[0]
    M = kv.shape[0]
    T = _tile(S_, 1024)
    E, H = MEM_HEAD_DIM, MEM_HEADS
    scale = E ** -0.5

    def body(q_ref, k_ref, v_ref, g_ref, y_ref):
        p = _mem_softmax(q_ref[...], k_ref[...], scale)
        o = _nn(p.astype(BF16), v_ref[...])
        rs = lax.rsqrt(jnp.mean(o * o, axis=-1, keepdims=True) + EPS)
        y_ref[...] = (o * rs * g_ref[...]).astype(BF16)

    return pl.pallas_call(
        body, name=name, grid=(H, S_ // T),
        in_specs=[pl.BlockSpec((T, E), lambda h, i: (i, C_MQ // E + h)), pl.BlockSpec((M, E), lambda h, i: (0, h)),
                  pl.BlockSpec((M, E), lambda h, i: (0, H + h)), pl.BlockSpec((1, E), lambda h, i: (0, h))],
        out_specs=pl.BlockSpec((T, E), lambda h, i: (i, h)), out_shape=SDS((S_, MEM_WIDTH), BF16),
        compiler_params=_cp("parallel", "parallel"))(proj, kv, kv, gain)


def mem_bwd(dmix, proj, kv, gain, *, name):
    S_ = proj.shape[0]
    M = kv.shape[0]
    T = _tile(S_, 1024)
    E, H = MEM_HEAD_DIM, MEM_HEADS
    scale = E ** -0.5

    def body(dy_ref, q_ref, k_ref, v_ref, g_ref, dq_ref, dk_ref, dv_ref, gg_ref):
        @pl.when(pl.program_id(1) == 0)
        def _():
            dk_ref[...] = jnp.zeros_like(dk_ref)
            dv_ref[...] = jnp.zeros_like(dv_ref)
            gg_ref[...] = jnp.zeros_like(gg_ref)
        q, k, v = q_ref[...], k_ref[...], v_ref[...]
        p = _mem_softmax(q, k, scale)
        pb = p.astype(BF16)
        o = _nn(pb, v)
        rs = lax.rsqrt(jnp.mean(o * o, axis=-1, keepdims=True) + EPS)
        ohat = o * rs
        dy = dy_ref[...].astype(F32)
        u = dy * g_ref[...]
        do = (rs * (u - ohat * jnp.mean(u * ohat, axis=-1, keepdims=True))).astype(BF16)
        gg_ref[...] += jnp.sum(dy * ohat, axis=0, keepdims=True)
        dv_ref[...] += _tn(pb, do)
        dp = _nt(do, v)
        ds = (p * (dp - jnp.sum(p * dp, axis=-1, keepdims=True))).astype(BF16)
        dq_ref[...] = (_nn(ds, k) * scale).astype(BF16)
        dk_ref[...] += _tn(ds, q) * scale

    return pl.pallas_call(
        body, name=name, grid=(H, S_ // T),
        in_specs=[pl.BlockSpec((T, E), lambda h, i: (i, (V_W + DIL_WIDTH) // E + h)),
                  pl.BlockSpec((T, E), lambda h, i: (i, C_MQ // E + h)), pl.BlockSpec((M, E), lambda h, i: (0, h)),
                  pl.BlockSpec((M, E), lambda h, i: (0, H + h)), pl.BlockSpec((1, E), lambda h, i: (0, h))],
        out_specs=[pl.BlockSpec((T, E), lambda h, i: (i, h)), pl.BlockSpec((M, E), lambda h, i: (0, h)),
                   pl.BlockSpec((M, E), lambda h, i: (0, h)), pl.BlockSpec((1, E), lambda h, i: (0, h))],
        out_shape=[SDS((S_, MEM_WIDTH), BF16), SDS((M, MEM_WIDTH), F32), SDS((M, MEM_WIDTH), F32),
                   SDS((1, MEM_WIDTH), F32)],
        compiler_params=_cp("parallel", "arbitrary"))(dmix, proj, kv, kv, gain)


def _strided(a, d):
    return a if d == 1 else a.reshape(a.shape[0] // d, d * a.shape[1])


def _unstrided(a, d):
    return a if d == 1 else a.reshape(a.shape[0] * d, a.shape[1] // d)


def _dil_tables(rel_bias, seq):
    tabs = []
    for d in DIL_DILATIONS:
        sb, _ = _dil_band(min(seq // d, DIL_TQ))
        vec = jnp.sum(jnp.asarray(_rel_onehot(d))[:, :, None] * rel_bias[None], axis=1).T
        tabs.append((_toeplitz(vec, sb, False), _toeplitz(vec, sb, True)))
    return tabs


def _rel_bias_grad(ds_sums, seq):
    g = jnp.zeros((REL_BUCKETS, DIL_HEADS), F32)
    for d, ds in zip(DIL_DILATIONS, ds_sums):
        gvec = _untoeplitz(ds, _dil_band(min(seq // d, DIL_TQ))[0])
        g = g + jnp.sum(jnp.asarray(_rel_onehot(d))[:, :, None] * gvec.T[:, None, :], axis=0)
    return g


def _pad_gate_up(up, row0):
    return jnp.zeros((LANES, QK_W), BF16).at[row0:row0 + GLA_GATE_RANK].set(up.astype(BF16))


def _layer_fwd(x, mem, p, tabs, l, carry):
    tag = f"l{l}"
    proj, hn1 = norm_mm(x, p["norm_mix"], p["w_in"], tm=TM_FWD, tn=IN_TILE, name=f"in_proj_{tag}", comm=carry.get("in"))
    o_f, st_f = gla_fwd(proj, p["up_f"], p["bias_f"], reverse=False, name=f"gla_fwd_f_{tag}", comm=carry.get("gla_f"))
    o_b, st_b = gla_fwd(proj, p["up_b"], p["bias_b"], reverse=True, name=f"gla_fwd_b_{tag}", comm=carry.get("gla_b"))
    gla_out = gla_post(o_f, o_b, proj, p["gla_norm"], name=f"gla_post_{tag}")
    qkv = proj[:, C_DQ:C_MQ]
    os_, ls_, ls_strided = [], [], []
    for d, (bias, _) in zip(DIL_DILATIONS, tabs):
        o, lse = dil_fwd(_strided(qkv, d), bias, d=d, name=f"dil_fwd_d{d}_{tag}", comm=carry.get(f"dil{d}"))
        os_.append(_unstrided(o, d))
        ls_.append(_unstrided(lse, d))
        ls_strided.append(lse)
    dil_out = dil_combine(os_, ls_, p["dil_norm"], name=f"dil_combine_{tag}")
    kv, hmem = norm_mm(mem, p["mem_norm"], p["w_mem_kv"], tm=256, tn=1024, name=f"mem_kv_{tag}")
    mem_out = mem_fwd(proj, kv, p["mem_out_norm"], name=f"mem_fwd_{tag}")
    mixed = jnp.concatenate([gla_out, dil_out, mem_out], axis=1)
    x1 = mm_res(mixed, p["w_out"], x, relu2=False, tm=TM_FWD, tn=TN, tk=TK, name=f"out_proj_{tag}",
                comm=carry.get("out"))
    a, hn2 = norm_mm(x1, p["norm_mlp"], p["w_up"], tm=TM_FWD, tn=TN, name=f"up_proj_{tag}", comm=carry.get("up"))
    x2 = mm_res(a, p["w_down"], x1, relu2=True, tm=TM_FWD, tn=TN, tk=TK, name=f"down_proj_{tag}",
                comm=carry.get("down"))
    saved = dict(x=x, proj=proj, hn1=hn1, o_f=o_f, o_b=o_b, st_f=st_f, st_b=st_b, qkv=qkv, os=os_, ls=ls_,
                 ls_strided=ls_strided, kv=kv, hmem=hmem, mixed=mixed, x1=x1, a=a, hn2=hn2)
    return x2, saved


class GradScatter:
    SLOTS = {"down_x": [(1, "w_in", 0, 1)], "down_w": [(1, "w_out", 0, 1), (1, "w_mem_kv", 0, 1)],
             "up_w": [(0, "w_down", 0, 2)], "up_x": [(0, "w_down", 1, 2)],
             "in_w": [(0, "w_up", 0, 2)], "in_x": [(0, "w_up", 1, 2)]}
    LAYER0 = {"in_w": [(0, "w_out", 0, 1), (0, "w_mem_kv", 0, 1)], "in_x": [(0, "w_in", 0, 1)]}

    def __init__(self, exchange):
        self.exchange = exchange
        self.ready = {}
        self.taken = set()
        self.sent = []

    def offer(self, l, name, g):
        self.ready[(l, name)] = g

    def _take(self, keys):
        keys = [k for k in keys if k[:2] in self.ready and k not in self.taken]
        if not (keys and self.exchange):
            return None
        bufs = [self.ready[k[:2]] for k in keys]
        rows = [(k[2] * (b.shape[1] // k[3]), b.shape[1] // k[3]) for k, b in zip(keys, bufs)]
        ex = ChipExchange(bufs, scatter=True, rows=rows)
        self.taken.update(keys)
        self.sent.append((keys, ex))
        return ex

    def carrier(self, l, slot):
        plan = self.SLOTS[slot] + (self.LAYER0.get(slot, []) if l == 0 else [])
        return self._take([(l + dl, n, i, m) for dl, n, i, m in plan])

    def flush(self, name):
        pieces = {k[:2]: max([t[3] for t in self.taken if t[:2] == k[:2]] + [1]) for k in self.ready}
        ex = self._take(sorted((l, n, i, m) for (l, n), m in pieces.items() for i in range(m)))
        if ex is not None:
            ex.run(name)

    def received(self):
        got = {}
        for keys, ex in self.sent:
            for k, r in zip(keys, ex.result):
                got.setdefault(k[:2], {})[k[2]] = r
        return {k: [v[i] for i in sorted(v)] for k, v in got.items()}


def _quarters(g):
    return g.reshape(N_CHIPS, g.shape[0] // N_CHIPS, g.shape[1])


def _layer_bwd(dx2, dx2b, mem, p, tabs, s, l, sink):
    tag = f"l{l}"
    seq = dx2.shape[0]
    g = {}
    da = mm_nt(dx2b, p["w_down"], s["a"], tm=TM_NT, tn=TN, name=f"down_bwd_x_{tag}", comm=sink.carrier(l, "down_x"))
    g_down = mm_tn(s["a"], dx2b, relu2=True, tm=TM_W, tn=TN_W, tk=TK_W, out="rows", name=f"down_bwd_w_{tag}",
                   comm=sink.carrier(l, "down_w"))
    sink.offer(l, "w_down", _quarters(g_down))
    g_up = mm_tn(s["hn2"], da, relu2=False, tm=TM_W, tn=TN_W, tk=TK_W, out="cols", name=f"up_bwd_w_{tag}",
                 comm=sink.carrier(l, "up_w"))
    sink.offer(l, "w_up", g_up)
    dx1, dx1b, g["norm_mlp"] = mm_nt_normbwd(da, p["w_up"], s["x1"], p["norm_mlp"], dx2, tm=TM_NORM_BWD, tk=TK_NORM_BWD,
                                       name=f"up_bwd_x_{tag}", comm=sink.carrier(l, "up_x"))
    dmix = mm_nt(dx1b, p["w_out"], None, tm=TM_NT, tn=TN, name=f"out_bwd_x_{tag}")
    sink.offer(l, "w_out", _quarters(mm_tn(s["mixed"], dx1b, relu2=False, tm=TM_W, tn=TN_W, tk=TK_W, out="rows",
                                           name=f"out_bwd_w_{tag}")))
    proj = s["proj"]
    do, dr, g["gla_norm"] = gla_post_bwd(dmix, s["o_f"], s["o_b"], proj, p["gla_norm"], name=f"gla_post_bwd_{tag}")
    r_f = gla_bwd(proj, p["up_f"], p["bias_f"], s["st_f"], do, None, reverse=False, name=f"gla_bwd_f_{tag}")
    r_b = gla_bwd(proj, p["up_b"], p["bias_b"], s["st_b"], do, r_f[:4], reverse=True, name=f"gla_bwd_b_{tag}")
    dgq, dgk, dgv, dlr = r_b[:4]
    g["up_f"], g["bias_f"] = r_f[4][0:GLA_GATE_RANK], r_f[5]
    g["up_b"], g["bias_b"] = r_b[4][GLA_GATE_RANK:2 * GLA_GATE_RANK], r_b[5]
    *branch, g["dil_norm"] = dil_combine_bwd(dmix, s["os"], s["ls"], p["dil_norm"], name=f"dil_combine_bwd_{tag}")
    dos, corrs = branch[:3], branch[3:]
    ddq = ddk = ddv = None
    ds_sums = []
    for r, (d, (bias, bias_t)) in enumerate(zip(DIL_DILATIONS, tabs)):
        qkv_d = _strided(s["qkv"], d)
        do_d, lse_d, corr_d = _strided(dos[r], d), s["ls_strided"][r], _strided(corrs[r], d)
        dq_d, ds_sum = dil_bwd_q(qkv_d, bias, do_d, lse_d, corr_d, d=d, name=f"dil_bwd_q_d{d}_{tag}")
        dk_d, dv_d = dil_bwd_kv(qkv_d, bias_t, do_d, lse_d, corr_d, d=d, name=f"dil_bwd_kv_d{d}_{tag}")
        ds_sums.append(ds_sum)
        dq_d, dk_d, dv_d = [_unstrided(t, d).astype(F32) for t in (dq_d, dk_d, dv_d)]
        ddq, ddk, ddv = (dq_d, dk_d, dv_d) if ddq is None else (ddq + dq_d, ddk + dk_d, ddv + dv_d)
    g["rel_bias"] = _rel_bias_grad(ds_sums, seq)
    dmq, dmk, dmv, g["mem_out_norm"] = mem_bwd(dmix, proj, s["kv"], p["mem_out_norm"], name=f"mem_bwd_{tag}")
    dkv = jnp.concatenate([dmk, dmv], axis=1).astype(BF16)
    sink.offer(l, "w_mem_kv", _quarters(mm_tn(s["hmem"], dkv, relu2=False, tm=TM_W, tn=TN_W, tk=TK_W, out="rows",
                                              name=f"mem_kv_bwd_w_{tag}")))
    _, _, g["mem_norm"] = mm_nt_normbwd(dkv, p["w_mem_kv"], mem, p["mem_norm"], jnp.zeros_like(mem), tm=TM_NORM_BWD,
                                     tk=TK_NORM_BWD, name=f"mem_kv_bwd_x_{tag}")
    dproj = jnp.concatenate([dgq, dgk, dgv, dr, ddq.astype(BF16), ddk.astype(BF16), ddv.astype(BF16), dmq, dlr,
                             jnp.zeros((seq, IN_PAD - C_LR - LANES), BF16)], axis=1)
    g_in = mm_tn(s["hn1"], dproj, relu2=False, tm=TM_W, tn=IN_TILE, tk=TK_W, name=f"in_bwd_w_{tag}",
                 comm=sink.carrier(l, "in_w"))
    sink.offer(l, "w_in", _shard_w_in_grad(g_in))
    dx0, dx0b, g["norm_mix"] = mm_nt_normbwd(dproj, p["w_in"], s["x"], p["norm_mix"], dx1, tm=TM_NORM_BWD, tk=IN_TILE,
                                             name=f"in_bwd_x_{tag}", comm=sink.carrier(l, "in_x"))
    return dx0, dx0b, g


def _shard_w_in_grad(g):
    g = jnp.concatenate([g[:, :LR_COL], g[:, C_LR:C_LR + 2 * GLA_GATE_RANK], g[:, LR_COL:C_LR]], axis=1).astype(BF16)
    w = IN_WIDTH // N_CHIPS
    return jnp.stack([g[:, j * w:(j + 1) * w] for j in range(N_CHIPS)], axis=0)


def local_step(x, mem, target, depth, layer_weights, norm_final, rel_bias, fwd_carry, sink):
    tabs = _dil_tables(rel_bias, x.shape[0])
    saved, layers = [], []
    for l in range(depth):
        layers.append(layer_weights(l))
        x, s = _layer_fwd(x, mem, layers[l], tabs, l, fwd_carry(l))
        saved.append(s)
    dx, dxb, g_final, loss = loss_head(x, norm_final, target, tm=TM_LOSS, name="loss_head")
    grads = [None] * depth
    for l in reversed(range(depth)):
        dx, dxb, grads[l] = _layer_bwd(dx, dxb, mem, layers[l], tabs, saved[l], l, sink)
    return loss, dx, grads, g_final


def sibling_exchange(bufs, *, name):
    n = len(bufs)

    def body(*refs):
        ins, outs = refs[:n], refs[n:2 * n]
        send, recv = refs[2 * n:]
        x, y, c = _mesh_pos()
        cps = [pltpu.make_async_remote_copy(src_ref=ins[a], dst_ref=outs[a], send_sem=send.at[a], recv_sem=recv.at[a],
                                            device_id=(x, y, 1 - c), device_id_type=MESH) for a in range(n)]
        for cp in cps:
            cp.start()
        for cp in cps:
            cp.wait()

    return pl.pallas_call(
        body, name=name, in_specs=[HBM_SPEC] * n, out_specs=[HBM_SPEC] * n,
        out_shape=[SDS(b.shape, b.dtype) for b in bufs],
        scratch_shapes=[pltpu.SemaphoreType.DMA((n,)), pltpu.SemaphoreType.DMA((n,))])(*bufs)


def all_reduce_small(v, *, name):
    R = v.shape[0]

    def body(v_ref, o_ref, slots, send, recv):
        x, y, c = _mesh_pos()
        me = 4 * x + 2 * y + c
        flip = lambda p, f: 1 - p if f else p
        peers = [(flip(x, k & 4), flip(y, k & 2), flip(c, k & 1)) for k in range(1, N_DEV)]
        slots[pl.ds(me, 1)] = v_ref[...][None]
        cps = [pltpu.make_async_remote_copy(src_ref=v_ref, dst_ref=slots.at[me], send_sem=send.at[k], recv_sem=recv.at[k],
                                            device_id=peer, device_id_type=MESH) for k, peer in enumerate(peers)]
        for cp in cps:
            cp.start()
        for k, (px, py, pc) in enumerate(peers):
            pltpu.make_async_remote_copy(src_ref=v_ref, dst_ref=slots.at[4 * px + 2 * py + pc], send_sem=send.at[k],
                                         recv_sem=recv.at[k], device_id=(px, py, pc), device_id_type=MESH).wait_recv()
        for cp in cps:
            cp.wait_send()
        acc = slots[0]
        for s in range(1, N_DEV):
            acc = acc + slots[s]
        o_ref[...] = acc

    vm = pl.BlockSpec(memory_space=pltpu.VMEM)
    return pl.pallas_call(
        body, name=name, in_specs=[vm], out_specs=vm, out_shape=SDS(v.shape, F32),
        scratch_shapes=[pltpu.VMEM((N_DEV, R, LANES), F32), pltpu.SemaphoreType.DMA((N_DEV - 1,)),
                        pltpu.SemaphoreType.DMA((N_DEV - 1,))])(v)


def sum_chips(recv, *, name):
    _, R, C = recv.shape
    tr = _tile(R, 512)

    def body(r_ref, o_ref):
        acc = r_ref[0].astype(F32)
        for j in range(1, N_CHIPS):
            acc = acc + r_ref[j].astype(F32)
        o_ref[...] = acc.astype(BF16)

    return pl.pallas_call(
        body, name=name, grid=(R // tr,), in_specs=[pl.BlockSpec((N_CHIPS, tr, C), lambda i: (0, i, 0))],
        out_specs=pl.BlockSpec((tr, C), lambda i: (i, 0)), out_shape=SDS((R, C), BF16),
        compiler_params=_cp("parallel"))(recv)


def adamw(parts, w, m, v, *, name):
    R, C = w.shape
    n = len(parts)
    tr = _tile(R, 512)
    while 2 * (n + 7) * tr * C * 4 > ELEMWISE_BLOCK_BYTES and tr % 16 == 0:
        tr //= 2
    c1 = 1.0 - ADAM_B1 ** ADAM_STEP
    c2 = 1.0 - ADAM_B2 ** ADAM_STEP

    def body(*refs):
        w_ref, m_ref, v_ref = refs[n:n + 3]
        g_ref, d_ref, nm_ref, nv_ref = refs[n + 3:]
        g = refs[0][...].astype(F32)
        for r in refs[1:n]:
            g = g + r[...].astype(F32)
        nm = ADAM_B1 * m_ref[...] + (1.0 - ADAM_B1) * g
        nv = ADAM_B2 * v_ref[...] + (1.0 - ADAM_B2) * (g * g)
        g_ref[...] = g
        nm_ref[...] = nm
        nv_ref[...] = nv
        d_ref[...] = -ADAM_LR * ((nm / c1) / (jnp.sqrt(nv / c2) + ADAM_EPS) + ADAM_WD * w_ref[...])

    blk = pl.BlockSpec((tr, C), lambda i: (i, 0))
    return pl.pallas_call(
        body, name=name, grid=(R // tr,), in_specs=[blk] * (n + 3), out_specs=[blk] * 4,
        out_shape=[SDS((R, C), F32)] * 4, compiler_params=_cp("parallel"))(*parts, w, m, v)


PACK_ROWS = 8


def _pack(parts):
    rows = []
    for a in parts:
        flat = a.reshape(-1).astype(F32)
        n = -(-flat.shape[0] // (PACK_ROWS * LANES)) * PACK_ROWS * LANES
        rows.append(jnp.pad(flat, (0, n - flat.shape[0])).reshape(-1, LANES))
    return jnp.concatenate(rows, axis=0)


def _unpack(slab, shapes):
    out, r = [], 0
    for shp in shapes:
        n = int(np.prod(shp))
        nr = -(-n // (PACK_ROWS * LANES)) * PACK_ROWS
        out.append(slab[r:r + nr].reshape(-1)[:n].reshape(shp))
        r += nr
    return out


SMALL = ["norm_mix", "gla_gate_bias_fwd", "gla_gate_bias_bwd", "gla_norm", "rel_bias", "dil_norm", "mem_norm",
         "mem_out_norm", "norm_mlp", "norm_final"]
GATE_UPS = ["gla_gate_up_fwd", "gla_gate_up_bwd"]
BIG = ["w_in", "w_mem_kv", "w_out", "w_up", "w_down"]
WEIGHTS = ["norm_mix", "w_in", "gla_gate_up_fwd", "gla_gate_bias_fwd", "gla_gate_up_bwd", "gla_gate_bias_bwd", "gla_norm",
           "rel_bias", "dil_norm", "mem_norm", "w_mem_kv", "mem_out_norm", "w_out", "norm_mlp", "w_up", "w_down",
           "norm_final"]


FWD_CARRY = {"in": [(1, "w_in")], "out": [(1, "w_out"), (1, "w_mem_kv")], "up": [(1, "w_up")], "down": [(1, "w_down")]}
FWD_CARRY_LAYER0 = {"in": [("w_out", 0, 1), ("w_mem_kv", 0, 1)], "gla_f": [("w_up", 0, 2)], "gla_b": [("w_down", 0, 2)],
                    "dil4": [("w_up", 1, 2)], "dil16": [("w_down", 1, 2)]}


def _kernel_layout(name, g):
    if name == "w_up":
        return g
    if name == "w_in":
        full = jnp.concatenate([g[j] for j in range(N_CHIPS)], axis=1)
        return jnp.concatenate([full[:, :LR_COL], full[:, LR_COL + 2 * GLA_GATE_RANK:],
                                full[:, LR_COL:LR_COL + 2 * GLA_GATE_RANK],
                                jnp.zeros((full.shape[0], IN_PAD - IN_WIDTH), BF16)], axis=1)
    return g.reshape(-1, g.shape[-1])


def kernel(x, mem, norm_mix, w_in, gla_gate_up_fwd, gla_gate_bias_fwd, gla_gate_up_bwd, gla_gate_bias_bwd, gla_norm, rel_bias, dil_norm, mem_norm, w_mem_kv, mem_out_norm, w_out, norm_mlp, w_up, w_down, norm_final, loss_target, m_norm_mix, m_w_in, m_gla_gate_up_fwd, m_gla_gate_bias_fwd, m_gla_gate_up_bwd, m_gla_gate_bias_bwd, m_gla_norm, m_rel_bias, m_dil_norm, m_mem_norm, m_w_mem_kv, m_mem_out_norm, m_w_out, m_norm_mlp, m_w_up, m_w_down, m_norm_final, v_norm_mix, v_w_in, v_gla_gate_up_fwd, v_gla_gate_bias_fwd, v_gla_gate_up_bwd, v_gla_gate_bias_bwd, v_gla_norm, v_rel_bias, v_dil_norm, v_mem_norm, v_w_mem_kv, v_mem_out_norm, v_w_out, v_norm_mlp, v_w_up, v_w_down, v_norm_final):
    vals = dict(locals())
    W = {n: vals[n] for n in WEIGHTS}
    M1 = {n: vals["m_" + n] for n in WEIGHTS}
    V2 = {n: vals["v_" + n] for n in WEIGHTS}
    depth = w_in.shape[0]
    chip = 2 * lax.axis_index("x") + lax.axis_index("y")

    wb = {n: W[n].astype(BF16) for n in BIG + GATE_UPS}
    first = ChipExchange([wb[n] for n in ["w_in"] + GATE_UPS], scatter=False, layers=[0] + [None] * len(GATE_UPS))
    first.run("gather_first")
    gate_up = dict(zip(GATE_UPS, first.result[1:]))
    pending = {(0, "w_in"): [(first, 0)]}
    row = lambda a: a.reshape(1, -1)

    def fwd_carry(l):
        plan = {slot: [(l + dl, n, 0, 1) for dl, n in what if l + dl < depth] for slot, what in FWD_CARRY.items()}
        if l == 0:
            for slot, what in FWD_CARRY_LAYER0.items():
                plan[slot] = plan.get(slot, []) + [(0, n, i, m) for n, i, m in what]
        carry = {}
        for slot, keys in plan.items():
            if keys:
                shard_rows = [wb[n].shape[1] for _, n, _, _ in keys]
                carry[slot] = ChipExchange([wb[n] for _, n, _, _ in keys], scatter=False, layers=[k for k, _, _, _ in keys],
                                           rows=[(i * (r // m), r // m) for (_, _, i, m), r in zip(keys, shard_rows)])
                for place, (k, n, i, m) in enumerate(keys):
                    pending.setdefault((k, n), [None] * m)[i] = (carry[slot], place)
        return carry

    class LayerWeights(dict):
        def __init__(self, l):
            super().__init__()
            self.l = l

        def __missing__(self, n):
            pieces = [ex.result[place] for ex, place in pending[(self.l, n)]]
            self[n] = _kernel_layout(n, pieces[0] if len(pieces) == 1 else jnp.concatenate(pieces, axis=1))
            return self[n]

    def layer_weights(l):
        p = LayerWeights(l)
        full_up = lambda n: jnp.concatenate([gate_up[n][j, l] for j in range(N_CHIPS)], axis=1)
        p["up_f"] = _pad_gate_up(full_up("gla_gate_up_fwd"), 0)
        p["up_b"] = _pad_gate_up(full_up("gla_gate_up_bwd"), GLA_GATE_RANK)
        p["bias_f"], p["bias_b"] = row(gla_gate_bias_fwd[l]), row(gla_gate_bias_bwd[l])
        for n in ("norm_mix", "gla_norm", "dil_norm", "mem_norm", "mem_out_norm", "norm_mlp"):
            p[n] = row(W[n][l])
        return p

    sink = GradScatter(exchange=True)
    loss, grad_x, grads, g_final = local_step(x[0], mem[0], loss_target[0], depth, layer_weights, row(norm_final),
                                              rel_bias, fwd_carry, sink)
    sink.flush("scatter_last_grads")
    recv = sink.received()

    flat2 = lambda a: a.reshape(-1, a.shape[-1])
    part = [jnp.concatenate([sum_chips(r, name=f"sum_chips_{n}_l{l}_p{i}") for l in range(depth)
                             for i, r in enumerate(recv[(l, n)])], axis=0) for n in BIG]
    other = sibling_exchange(part, name="swap_partial_sums")
    out = {}
    for n, pa, ob in zip(BIG, part, other):
        res = adamw([pa, ob], flat2(W[n]), flat2(M1[n]), flat2(V2[n]), name=f"adamw_{n}")
        out[n] = [r.reshape(W[n].shape) for r in res]

    small_g = {}
    for n, key in (("norm_mix", "norm_mix"), ("gla_gate_bias_fwd", "bias_f"), ("gla_gate_bias_bwd", "bias_b"),
                   ("gla_norm", "gla_norm"), ("dil_norm", "dil_norm"), ("mem_norm", "mem_norm"),
                   ("mem_out_norm", "mem_out_norm"), ("norm_mlp", "norm_mlp"),
                   ("gla_gate_up_fwd", "up_f"), ("gla_gate_up_bwd", "up_b")):
        small_g[n] = jnp.stack([grads[l][key].reshape(W[n].shape[1:] if n not in GATE_UPS else (GLA_GATE_RANK, QK_W))
                                for l in range(depth)], axis=0)
    small_g["rel_bias"] = sum(grads[l]["rel_bias"] for l in range(depth))
    small_g["norm_final"] = g_final.reshape(-1)
    names = SMALL + GATE_UPS
    slab = all_reduce_small(_pack([small_g[n] for n in names] + [loss[:, 0]]), name="all_reduce_small")
    *summed, loss_sum = _unpack(slab, [small_g[n].shape for n in names] + [(1,)])
    summed = dict(zip(names, summed))
    for n in GATE_UPS:
        summed[n] = lax.dynamic_slice_in_dim(summed[n], chip * LANES, LANES, axis=2)
    names_s = SMALL + GATE_UPS
    shapes = [W[n].shape for n in names_s]
    res = adamw([_pack([summed[n] for n in names_s])], _pack([W[n] for n in names_s]), _pack([M1[n] for n in names_s]),
                _pack([V2[n] for n in names_s]), name="adamw_small")
    for n, *r in zip(names_s, *[_unpack(t, shapes) for t in res]):
        out[n] = r

    return (loss_sum.reshape(()), grad_x[None], *[out[n][0] for n in WEIGHTS], *[out[n][1] for n in WEIGHTS],
            *[out[n][2] for n in WEIGHTS], *[out[n][3] for n in WEIGHTS])
```
